```python
import math
import jax
import jax.numpy as jnp
from jax import lax
import numpy as np

D_MODEL = 1024
BATCH = 1
SEQ = 16384
DEPTH = 4
DEC_BATCH = 8
DEC_SEQ = 8192
PAST_LEN = 128

N_EVEN = (DEPTH + 1) // 2
N_ODD = DEPTH // 2
N_MEM = 256
EPS = 1e-6
GN_EPS = 1e-5
CHUNK = 128

MLSTM_HEADS = 4
MLSTM_WIDTH = D_MODEL
MLSTM_HD = MLSTM_WIDTH // MLSTM_HEADS
N_GATE_COLS = 4 * MLSTM_HEADS

HYENA_WIDTH = D_MODEL
HYENA_EMB = 33
HYENA_BANDS = (HYENA_EMB - 1) // 2
HYENA_HIDDEN = 64
HYENA_SHIFT = 0.05
HYENA_FAST_PCT = 0.3
HYENA_SLOW_PCT = 1.5
HYENA_TARGET = 1e-2

RET_HEADS = 4
RET_QK = D_MODEL
RET_V = 2 * D_MODEL
RET_HDK = RET_QK // RET_HEADS
RET_HDV = RET_V // RET_HEADS
ROPE_BASE = 10000.0

CA_HEADS = 4
CA_HD = D_MODEL // CA_HEADS

SHORT_CONV = 3
EVEN_COLS = 5 * MLSTM_WIDTH + N_GATE_COLS + 4 * HYENA_WIDTH
ODD_COLS = 2 * RET_QK + 2 * RET_V

kernel_name = 'hybrid_mlstm_hyena_retention_encoder'


def _rmsnorm(x, g):
    xf = x.astype(jnp.float32)
    y = xf * lax.rsqrt(jnp.mean(xf * xf, axis=-1, keepdims=True) + EPS)
    return (y * g.astype(jnp.float32)).astype(x.dtype)


def _heads(x, n):
    b, l, c = x.shape
    return x.reshape(b, l, n, c // n).transpose(0, 2, 1, 3)


def _merge(x):
    b, h, l, d = x.shape
    return x.transpose(0, 2, 1, 3).reshape(b, l, h * d)


def _flip(a):
    return jnp.flip(a, axis=2)


def _head_layernorm(o, g):
    mu = jnp.mean(o, axis=-1, keepdims=True)
    var = jnp.mean(jnp.square(o - mu), axis=-1, keepdims=True)
    return _merge((o - mu) * lax.rsqrt(var + GN_EPS)) * g.astype(jnp.float32)


def _short_conv(x, w, b):
    w = w.astype(jnp.float32)
    xp = jnp.pad(x, ((0, 0), (1, 1), (0, 0)))
    return xp[:, :-2] * w[0] + xp[:, 1:-1] * w[1] + xp[:, 2:] * w[2] + b.astype(jnp.float32)


def _rope(x):
    l, d = x.shape[2], x.shape[3]
    inv = ROPE_BASE ** (-jnp.arange(0, d, 2, dtype=jnp.float32) / d)
    ang = jnp.arange(l, dtype=jnp.float32)[:, None] * inv[None, :]
    c, s = jnp.cos(ang), jnp.sin(ang)
    x1, x2 = jnp.split(x, 2, axis=-1)
    return jnp.concatenate([x1 * c - x2 * s, x1 * s + x2 * c], axis=-1)


def _to_chunks(a):
    b, h, l = a.shape[:3]
    a = a.reshape(b, h, l // CHUNK, CHUNK, *a.shape[3:])
    return jnp.moveaxis(a, 2, 0)


def _from_chunks(a):
    nc, b, h, t = a.shape[:4]
    return jnp.moveaxis(a, 0, 2).reshape(b, h, nc * t, *a.shape[4:])


def _mlstm_scan(q, k, v, ig, lf):
    b, h, l, dh = q.shape
    causal = jnp.tril(jnp.ones((CHUNK, CHUNK), dtype=bool))

    def step(carry, inp):
        c_st, n_st, m_st = carry
        qq, kk, vv, ii, ff = inp
        bcum = jnp.cumsum(ff, axis=-1)
        dmat = bcum[..., :, None] - bcum[..., None, :] + ii[..., None, :]
        dmat = jnp.where(causal, dmat, -jnp.inf)
        inter = bcum + m_st[..., None]
        m_t = jnp.maximum(jnp.max(dmat, axis=-1), inter)
        s = jnp.einsum('bhtd,bhsd->bhts', qq, kk) * jnp.exp(dmat - m_t[..., None])
        sc = jnp.exp(inter - m_t)
        num = jnp.einsum('bhts,bhsv->bhtv', s, vv) + sc[..., None] * jnp.einsum('bhvk,bhtk->bhtv', c_st, qq)
        den = jnp.sum(s, axis=-1) + sc * jnp.einsum('bhk,bhtk->bht', n_st, qq)
        out = num / jnp.maximum(jnp.abs(den), jnp.exp(-m_t))[..., None]
        btot = bcum[..., -1]
        gk = btot[..., None] - bcum + ii
        m_new = jnp.maximum(btot + m_st, jnp.max(gk, axis=-1))
        dec = jnp.exp(btot + m_st - m_new)
        wk = jnp.exp(gk - m_new[..., None])
        c_new = dec[..., None, None] * c_st + jnp.einsum('bhs,bhsv,bhsk->bhvk', wk, vv, kk)
        n_new = dec[..., None] * n_st + jnp.einsum('bhs,bhsk->bhk', wk, kk)
        return (c_new, n_new, m_new), out

    init = (jnp.zeros((b, h, dh, dh), jnp.float32), jnp.zeros((b, h, dh), jnp.float32),
            jnp.zeros((b, h), jnp.float32))
    _, hs = lax.scan(step, init, (_to_chunks(q), _to_chunks(k), _to_chunks(v), _to_chunks(ig), _to_chunks(lf)))
    return _from_chunks(hs)


def _retention_scan(q, k, v, log_gamma):
    b, h, l, dk = q.shape
    dv = v.shape[-1]
    pos = jnp.arange(CHUNK, dtype=jnp.float32)
    rel = pos[:, None] - pos[None, :]
    lg = log_gamma[:, None, None]
    dmask = jnp.where(rel >= 0, jnp.exp(lg * jnp.maximum(rel, 0.0)), 0.0)
    q_dec = jnp.exp(lg * (pos[:, None] + 1.0))
    k_dec = jnp.exp(lg * (CHUNK - 1.0 - pos[:, None]))
    c_dec = jnp.exp(lg * CHUNK)

    def step(r_st, inp):
        qq, kk, vv = inp
        s = jnp.einsum('bhtd,bhsd->bhts', qq, kk) * dmask
        out = jnp.einsum('bhts,bhsv->bhtv', s, vv) + jnp.einsum('bhtd,bhdv->bhtv', qq * q_dec, r_st)
        r_new = c_dec * r_st + jnp.einsum('bhsd,bhsv->bhdv', kk * k_dec, vv)
        return r_new, out

    init = jnp.zeros((b, h, dk, dv), jnp.float32)
    _, os_ = lax.scan(step, init, (_to_chunks(q), _to_chunks(k), _to_chunks(v)))
    return _from_chunks(os_)


def _hyena_filter(l, w1, b1, f1, w2, b2, f2, w3, delta):
    f32 = jnp.float32
    t = jnp.linspace(0.0, 1.0, l, dtype=f32)[:, None]
    j = jnp.arange(l, dtype=f32)[:, None]
    bands = jnp.linspace(1e-4, HYENA_BANDS - 1, HYENA_BANDS, dtype=f32)[None, :]
    ang = (2.0 * math.pi / l) * bands * j
    feats = jnp.concatenate([t, jnp.cos(ang), -jnp.sin(ang)], axis=-1)
    z = jnp.sin(f1.astype(f32) * (feats @ w1.astype(f32) + b1.astype(f32)))
    z = jnp.sin(f2.astype(f32) * (z @ w2.astype(f32) + b2.astype(f32)))
    hk = z @ w3.astype(f32)
    hk = hk * (jnp.exp(-t * jnp.abs(delta.astype(f32))) + HYENA_SHIFT)
    h_fwd, h_bwd = jnp.split(hk, 2, axis=-1)
    kern = jnp.concatenate([h_fwd, jnp.zeros_like(h_fwd[:1]), jnp.flip(h_bwd[1:], axis=0)], axis=0)
    return kern / jnp.sum(jnp.abs(kern), axis=0, keepdims=True)


def _fft_conv(u, kern):
    l = u.shape[1]
    uf = jnp.fft.rfft(u, n=2 * l, axis=1)
    kf = jnp.fft.rfft(kern, n=2 * l, axis=0)
    return jnp.fft.irfft(uf * kf[None], n=2 * l, axis=1)[:, :l]


def _even_mixer(h, p, i):
    f32 = jnp.float32
    b, l, _ = h.shape
    w, nh = MLSTM_WIDTH, MLSTM_HEADS
    proj = jnp.matmul(h, p['even_w_in'][i]).astype(f32)
    qk, v_a, o_a, z_a, gates, hy, z_b = jnp.split(
        proj, [2 * w, 3 * w, 4 * w, 5 * w, 5 * w + N_GATE_COLS, 5 * w + N_GATE_COLS + 3 * HYENA_WIDTH], axis=-1)
    qk = jax.nn.silu(_short_conv(qk, p['mlstm_conv_w'][i], p['mlstm_conv_b'][i]))
    q, k = jnp.split(qk, 2, axis=-1)
    q = _heads(q, nh)
    k = _heads(k, nh) * MLSTM_HD ** -0.5
    v = _heads(v_a, nh)
    g = gates.reshape(b, l, 2, 2, nh) + p['mlstm_gate_bias'][i].astype(f32)
    g = jnp.moveaxis(g, 1, -1)
    ig_f, lf_f = g[:, 0, 0], jax.nn.log_sigmoid(g[:, 0, 1])
    ig_b, lf_b = g[:, 1, 0], jax.nn.log_sigmoid(g[:, 1, 1])
    h_f = _mlstm_scan(q, k, v, ig_f, lf_f)
    h_b = _flip(_mlstm_scan(_flip(q), _flip(k), _flip(v), _flip(ig_b), _flip(lf_b)))
    y_a = _head_layernorm(h_f + h_b, p['mlstm_norm_g'][i]) * jax.nn.sigmoid(o_a) * jax.nn.silu(z_a)
    hy = _short_conv(hy, p['hyena_conv_w'][i], p['hyena_conv_b'][i])
    x0, x1, vh = jnp.split(hy, 3, axis=-1)
    kern = _hyena_filter(l, p['hyena_w1'][i], p['hyena_b1'][i], p['hyena_freq1'][i], p['hyena_w2'][i],
                         p['hyena_b2'][i], p['hyena_freq2'][i], p['hyena_w3'][i], p['hyena_delta'][i])
    t = x1 * vh
    y_b = x0 * (_fft_conv(t, kern) + p['hyena_skip'][i].astype(f32) * t)
    y_b = y_b * jax.nn.silu(z_b)
    return jnp.concatenate([y_a, y_b], axis=-1) @ p['even_w_out'][i].astype(f32)


def _odd_mixer(h, p, i):
    f32 = jnp.float32
    proj = jnp.matmul(h, p['odd_w_in'][i]).astype(f32)
    q, k, v, g = jnp.split(proj, [RET_QK, 2 * RET_QK, 2 * RET_QK + RET_V], axis=-1)
    q = _rope(_heads(q, RET_HEADS)) * RET_HDK ** -0.5
    k = _rope(_heads(k, RET_HEADS))
    v = _heads(v, RET_HEADS)
    lg = jax.nn.log_sigmoid(p['ret_decay_logit'][i].astype(f32))
    o_f = _retention_scan(q, k, v, lg[0])
    o_b = _flip(_retention_scan(_flip(q), _flip(k), _flip(v), lg[1]))
    o = _head_layernorm(o_f + o_b, p['ret_norm_g'][i])
    return (jax.nn.silu(g) * o) @ p['odd_w_out'][i].astype(f32)


def _cross_attn(h, m, wq, wkv, wo):
    f32 = jnp.float32
    q = _heads(jnp.matmul(h, wq), CA_HEADS).astype(f32)
    k, v = jnp.split(jnp.matmul(m, wkv), 2, axis=-1)
    k = _heads(k, CA_HEADS).astype(f32)
    v = _heads(v, CA_HEADS).astype(f32)
    s = jnp.einsum('bhtd,bhmd->bhtm', q, k) * CA_HD ** -0.5
    a = jax.nn.softmax(s, axis=-1)
    o = jnp.einsum('bhtm,bhmd->bhtd', a, v)
    return _merge(o) @ wo.astype(f32)


def _trunk(x, mem, p):
    for layer in range(DEPTH):
        i = layer // 2
        hn = _rmsnorm(x, p['norm_mix_g'][layer])
        mix = _even_mixer(hn, p, i) if layer % 2 == 0 else _odd_mixer(hn, p, i)
        x = x + mix.astype(x.dtype)
        hq = _rmsnorm(x, p['norm_ca_g'][layer])
        mn = _rmsnorm(mem, p['norm_mem_g'][layer])
        x = x + _cross_attn(hq, mn, p['ca_wq'][layer], p['ca_wkv'][layer], p['ca_wo'][layer]).astype(x.dtype)
    return _rmsnorm(x, p['norm_final_g'])


def setup_inputs(seed: int = 0) -> dict:
    key = jax.random.key(seed)
    it = iter(jax.random.split(key, 48))
    f32 = jnp.float32

    def nrm(shape, scale):
        return jax.random.normal(next(it), shape, f32) * scale

    def gain(shape):
        return 1.0 + nrm(shape, 0.01)

    d, w, nh = D_MODEL, MLSTM_WIDTH, MLSTM_HEADS
    x_prompt = nrm((BATCH, SEQ, d), 1.0)
    x_sample = nrm((DEC_BATCH, DEC_SEQ, d), 1.0)
    mem_prompt = nrm((BATCH, N_MEM, d), 1.0)
    mem_sample = nrm((DEC_BATCH, N_MEM, d), 1.0)
    norm_mix_g = gain((DEPTH, d))
    norm_ca_g = gain((DEPTH, d))
    norm_mem_g = gain((DEPTH, d))
    norm_final_g = gain((d,))
    even_w_in = nrm((N_EVEN, d, EVEN_COLS), d ** -0.5)
    mlstm_conv_w = nrm((N_EVEN, SHORT_CONV, 2 * w), SHORT_CONV ** -0.5)
    mlstm_conv_b = nrm((N_EVEN, 2 * w), 0.01)
    ib = nrm((N_EVEN, 2, 1, nh), 0.1)
    fb = jnp.broadcast_to(jnp.linspace(3.0, 6.0, nh, dtype=f32), (N_EVEN, 2, 1, nh)) + nrm((N_EVEN, 2, 1, nh), 0.01)
    mlstm_gate_bias = jnp.concatenate([ib, fb], axis=2)
    mlstm_norm_g = gain((N_EVEN, w))
    hyena_conv_w = nrm((N_EVEN, SHORT_CONV, 3 * HYENA_WIDTH), SHORT_CONV ** -0.5)
    hyena_conv_b = nrm((N_EVEN, 3 * HYENA_WIDTH), 0.01)
    hyena_w1 = nrm((N_EVEN, HYENA_EMB, HYENA_HIDDEN), HYENA_EMB ** -0.5)
    hyena_b1 = nrm((N_EVEN, HYENA_HIDDEN), 0.01)
    hyena_freq1 = gain((N_EVEN, HYENA_HIDDEN))
    hyena_w2 = nrm((N_EVEN, HYENA_HIDDEN, HYENA_HIDDEN), HYENA_HIDDEN ** -0.5)
    hyena_b2 = nrm((N_EVEN, HYENA_HIDDEN), 0.01)
    hyena_freq2 = gain((N_EVEN, HYENA_HIDDEN))
    hyena_w3 = nrm((N_EVEN, HYENA_HIDDEN, 2 * HYENA_WIDTH), HYENA_HIDDEN ** -0.5)
    min_decay = math.log(HYENA_TARGET) / HYENA_SLOW_PCT
    max_decay = math.log(HYENA_TARGET) / HYENA_FAST_PCT
    base = jnp.tile(jnp.linspace(min_decay, max_decay, HYENA_WIDTH, dtype=f32), 2)
    hyena_delta = base[None, :] + nrm((N_EVEN, 2 * HYENA_WIDTH), 0.01)
    hyena_skip = nrm((N_EVEN, HYENA_WIDTH), 1.0)
    even_w_out = nrm((N_EVEN, w + HYENA_WIDTH, d), (w + HYENA_WIDTH) ** -0.5)
    odd_w_in = nrm((N_ODD, d, ODD_COLS), d ** -0.5)
    logit = np.log(2.0 ** (5.0 + np.arange(RET_HEADS)) - 1.0).astype(np.float32)
    ret_decay_logit = jnp.broadcast_to(jnp.asarray(logit), (N_ODD, 2, RET_HEADS)) + nrm((N_ODD, 2, RET_HEADS), 0.01)
    ret_norm_g = gain((N_ODD, RET_V))
    odd_w_out = nrm((N_ODD, RET_V, d), RET_V ** -0.5)
    ca_wq = nrm((DEPTH, d, d), d ** -0.5)
    ca_wkv = nrm((DEPTH, d, 2 * d), d ** -0.5)
    ca_wo = nrm((DEPTH, d, d), d ** -0.5)
    return {'x_prompt': x_prompt, 'x_sample': x_sample, 'mem_prompt': mem_prompt, 'mem_sample': mem_sample,
            'norm_mix_g': norm_mix_g, 'norm_ca_g': norm_ca_g, 'norm_mem_g': norm_mem_g, 'norm_final_g': norm_final_g,
            'even_w_in': even_w_in, 'mlstm_conv_w': mlstm_conv_w, 'mlstm_conv_b': mlstm_conv_b,
            'mlstm_gate_bias': mlstm_gate_bias, 'mlstm_norm_g': mlstm_norm_g,
            'hyena_conv_w': hyena_conv_w, 'hyena_conv_b': hyena_conv_b, 'hyena_w1': hyena_w1, 'hyena_b1': hyena_b1,
            'hyena_freq1': hyena_freq1, 'hyena_w2': hyena_w2, 'hyena_b2': hyena_b2, 'hyena_freq2': hyena_freq2,
            'hyena_w3': hyena_w3, 'hyena_delta': hyena_delta, 'hyena_skip': hyena_skip, 'even_w_out': even_w_out,
            'odd_w_in': odd_w_in, 'ret_decay_logit': ret_decay_logit, 'ret_norm_g': ret_norm_g, 'odd_w_out': odd_w_out,
            'ca_wq': ca_wq, 'ca_wkv': ca_wkv, 'ca_wo': ca_wo}


def reference(x_prompt, x_sample, mem_prompt, mem_sample, norm_mix_g, norm_ca_g, norm_mem_g, norm_final_g,
              even_w_in, mlstm_conv_w, mlstm_conv_b, mlstm_gate_bias, mlstm_norm_g,
              hyena_conv_w, hyena_conv_b, hyena_w1, hyena_b1, hyena_freq1, hyena_w2, hyena_b2, hyena_freq2,
              hyena_w3, hyena_delta, hyena_skip, even_w_out,
              odd_w_in, ret_decay_logit, ret_norm_g, odd_w_out, ca_wq, ca_wkv, ca_wo):
    p = {'norm_mix_g': norm_mix_g, 'norm_ca_g': norm_ca_g, 'norm_mem_g': norm_mem_g, 'norm_final_g': norm_final_g,
         'even_w_in': even_w_in, 'mlstm_conv_w': mlstm_conv_w, 'mlstm_conv_b': mlstm_conv_b,
         'mlstm_gate_bias': mlstm_gate_bias, 'mlstm_norm_g': mlstm_norm_g,
         'hyena_conv_w': hyena_conv_w, 'hyena_conv_b': hyena_conv_b, 'hyena_w1': hyena_w1, 'hyena_b1': hyena_b1,
         'hyena_freq1': hyena_freq1, 'hyena_w2': hyena_w2, 'hyena_b2': hyena_b2, 'hyena_freq2': hyena_freq2,
         'hyena_w3': hyena_w3, 'hyena_delta': hyena_delta, 'hyena_skip': hyena_skip, 'even_w_out': even_w_out,
         'odd_w_in': odd_w_in, 'ret_decay_logit': ret_decay_logit, 'ret_norm_g': ret_norm_g, 'odd_w_out': odd_w_out,
         'ca_wq': ca_wq, 'ca_wkv': ca_wkv, 'ca_wo': ca_wo}
    y_prompt = _trunk(x_prompt, mem_prompt, p)
    y_sample = _trunk(x_sample, mem_sample, p)
    return (y_prompt, y_sample)
```

```python
import functools
import math

import numpy as np
import jax
import jax.numpy as jnp
from jax import lax
from jax.experimental import pallas as pl
from jax.experimental.pallas import tpu as pltpu

F32 = jnp.float32
BF16 = jnp.bfloat16

D_MODEL = 1024
EPS = 1e-6
GN_EPS = 1e-5
CHUNK = 128

MLSTM_HEADS = 4
MLSTM_HD = D_MODEL // MLSTM_HEADS
N_GATE_COLS = 4 * MLSTM_HEADS

HYENA_WIDTH = D_MODEL
HYENA_EMB = 33
HYENA_BANDS = (HYENA_EMB - 1) // 2
HYENA_SHIFT = 0.05
FFT_N2 = 256

RET_HEADS = 4
RET_HDK = D_MODEL // RET_HEADS
RET_HDV = 2 * D_MODEL // RET_HEADS
ROPE_BASE = 10000.0

CA_HEADS = 4
CA_HD = D_MODEL // CA_HEADS

LANES = 128
BF16_SUBLANES = 16
VMEM_LIMIT = 56 * 1024 * 1024

HI = lax.Precision.HIGHEST


def _params(*sem):
    return pltpu.CompilerParams(dimension_semantics=sem, vmem_limit_bytes=VMEM_LIMIT)


def _silu(x):
    return x * (1.0 / (1.0 + jnp.exp(-x)))


def _sigmoid(x):
    return 1.0 / (1.0 + jnp.exp(-x))


def _log_sigmoid(x):
    return jnp.minimum(x, 0.0) - jnp.log(1.0 + jnp.exp(-jnp.abs(x)))


def _rms_matmul_kernel(x_ref, g_ref, w_ref, o_ref, xn_ref):
    @pl.when(pl.program_id(1) == 0)
    def _():
        x = x_ref[...]
        y = x * lax.rsqrt(jnp.mean(x * x, axis=-1, keepdims=True) + EPS) * g_ref[...]
        xn_ref[...] = y.astype(BF16)

    o_ref[...] = jnp.dot(xn_ref[...], w_ref[...], preferred_element_type=F32).astype(o_ref.dtype)


def rms_matmul(x, g, w, out_dtype, tm, tn):
    m, d = x.shape
    n = w.shape[1]
    return pl.pallas_call(
        _rms_matmul_kernel,
        grid=(m // tm, n // tn),
        in_specs=[pl.BlockSpec((tm, d), lambda i, j: (i, 0)),
                  pl.BlockSpec((1, d), lambda i, j: (0, 0)),
                  pl.BlockSpec((d, tn), lambda i, j: (0, j))],
        out_specs=pl.BlockSpec((tm, tn), lambda i, j: (i, j)),
        out_shape=jax.ShapeDtypeStruct((m, n), out_dtype),
        scratch_shapes=[pltpu.VMEM((tm, d), BF16)],
        compiler_params=_params("parallel", "arbitrary"),
        name="rms_matmul",
    )(x, g.reshape(1, d).astype(F32), w)


def _conv3(x, prev_row, next_row, w, b):
    t = x.shape[0]
    row = lax.broadcasted_iota(jnp.int32, x.shape, 0)
    xm = jnp.where(row == 0, prev_row, pltpu.roll(x, 1, axis=0))
    xp = jnp.where(row == t - 1, next_row, pltpu.roll(x, t - 1, axis=0))
    return xm * w[0:1, :] + x * w[1:2, :] + xp * w[2:3, :] + b


def _halo_rows(prev_ref, next_ref, i, n_i):
    prev_row = prev_ref[0, BF16_SUBLANES - 1:BF16_SUBLANES, :].astype(F32)
    next_row = next_ref[0, 0:1, :].astype(F32)
    prev_row = jnp.where(i == 0, 0.0, prev_row)
    next_row = jnp.where(i == n_i - 1, 0.0, next_row)
    return prev_row, next_row


def _halo_specs(tl, tc, l, col_block):
    r = tl // BF16_SUBLANES
    last = l // BF16_SUBLANES - 1
    main = pl.BlockSpec((1, tl, tc), lambda b, i, c: (b, i, col_block(c)))
    prev = pl.BlockSpec((1, BF16_SUBLANES, tc),
                        lambda b, i, c: (b, jnp.maximum(i * r - 1, 0), col_block(c)))
    nxt = pl.BlockSpec((1, BF16_SUBLANES, tc),
                       lambda b, i, c: (b, jnp.minimum((i + 1) * r, last), col_block(c)))
    return main, prev, nxt


def _qk_conv_kernel(x_ref, p_ref, n_ref, w_ref, b_ref, o_ref):
    i = pl.program_id(1)
    prev_row, next_row = _halo_rows(p_ref, n_ref, i, pl.num_programs(1))
    y = _conv3(x_ref[0].astype(F32), prev_row, next_row, w_ref[...], b_ref[...])
    o_ref[0] = _silu(y).astype(o_ref.dtype)


def qk_conv(proj, w, b, tl=512, tc=512):
    bsz, l, _ = proj.shape
    c = w.shape[1]
    tl = min(tl, l)
    main, prev, nxt = _halo_specs(tl, tc, l, lambda cb: cb)
    return pl.pallas_call(
        _qk_conv_kernel,
        grid=(bsz, l // tl, c // tc),
        in_specs=[main, prev, nxt,
                  pl.BlockSpec((3, tc), lambda b_, i, cb: (0, cb)),
                  pl.BlockSpec((1, tc), lambda b_, i, cb: (0, cb))],
        out_specs=pl.BlockSpec((1, tl, tc), lambda b_, i, cb: (b_, i, cb)),
        out_shape=jax.ShapeDtypeStruct((bsz, l, c), BF16),
        compiler_params=_params("parallel", "parallel", "parallel"),
        name="qk_conv",
    )(proj, proj, proj, w.astype(F32), b.reshape(1, c).astype(F32))


def _hyena_pre_kernel(x0_ref, x0p_ref, x0n_ref, x1_ref, x1p_ref, x1n_ref, v_ref, vp_ref, vn_ref,
                      w_ref, b_ref, x0_out, t_out):
    i = pl.program_id(1)
    n_i = pl.num_programs(1)
    outs = []
    for k, (m_ref, p_ref, n_ref) in enumerate(((x0_ref, x0p_ref, x0n_ref), (x1_ref, x1p_ref, x1n_ref),
                                               (v_ref, vp_ref, vn_ref))):
        prev_row, next_row = _halo_rows(p_ref, n_ref, i, n_i)
        outs.append(_conv3(m_ref[0].astype(F32), prev_row, next_row, w_ref[k], b_ref[k]))
    x0_out[0] = outs[0].astype(x0_out.dtype)
    t_out[0] = outs[1] * outs[2]


def hyena_pre(proj, col0, w, b, tl=512, tc=512):
    bsz, l, _ = proj.shape
    wd = HYENA_WIDTH
    tl = min(tl, l)
    specs = []
    for k in range(3):
        base = (col0 + k * wd) // tc
        specs.extend(_halo_specs(tl, tc, l, lambda cb, base=base: base + cb))
    w3 = w.astype(F32).reshape(3, 3, wd).transpose(1, 0, 2)
    b3 = b.astype(F32).reshape(3, 1, wd)
    out_spec = pl.BlockSpec((1, tl, tc), lambda b_, i, cb: (b_, i, cb))
    return pl.pallas_call(
        _hyena_pre_kernel,
        grid=(bsz, l // tl, wd // tc),
        in_specs=specs + [pl.BlockSpec((3, 3, tc), lambda b_, i, cb: (0, 0, cb)),
                          pl.BlockSpec((3, 1, tc), lambda b_, i, cb: (0, 0, cb))],
        out_specs=[out_spec, out_spec],
        out_shape=[jax.ShapeDtypeStruct((bsz, l, wd), BF16), jax.ShapeDtypeStruct((bsz, l, wd), F32)],
        compiler_params=_params("parallel", "parallel", "parallel"),
        name="hyena_pre",
    )(*([proj] * 9), w3, b3)


def _seg_scan(x, op, identity, reverse):
    t = x.shape[0]
    r = lax.broadcasted_iota(jnp.int32, x.shape, 0) % CHUNK
    k = 1
    while k < CHUNK:
        if reverse:
            shifted = pltpu.roll(x, t - k, axis=0)
            valid = r < CHUNK - k
        else:
            shifted = pltpu.roll(x, k, axis=0)
            valid = r >= k
        x = op(x, jnp.where(valid, shifted, identity))
        k *= 2
    return x


def _gate_prep_kernel(ig_ref, fg_ref, bi_ref, bf_ref, bcum_ref, a_ref, amax_ref):
    ig = ig_ref[0] + bi_ref[...]
    lf = _log_sigmoid(fg_ref[0] + bf_ref[...])
    lane = lax.broadcasted_iota(jnp.int32, ig.shape, 1)
    fwd = lane < MLSTM_HEADS
    bcum = jnp.where(fwd, _seg_scan(lf, jnp.add, 0.0, False), _seg_scan(lf, jnp.add, 0.0, True))
    a = ig - bcum
    amax = jnp.where(fwd, _seg_scan(a, jnp.maximum, -jnp.inf, False),
                     _seg_scan(a, jnp.maximum, -jnp.inf, True))
    bcum_ref[0] = bcum
    a_ref[0] = a
    amax_ref[0] = amax


def gate_prep(ig, fg, bias_i, bias_f, tl=1024):
    bsz, l, _ = ig.shape
    tl = min(tl, l)
    spec = pl.BlockSpec((1, tl, LANES), lambda b, i: (b, i, 0))
    vec = pl.BlockSpec((1, LANES), lambda b, i: (0, 0))
    shp = jax.ShapeDtypeStruct((bsz, l, LANES), F32)
    return pl.pallas_call(
        _gate_prep_kernel,
        grid=(bsz, l // tl),
        in_specs=[spec, spec, vec, vec],
        out_specs=[spec, spec, spec],
        out_shape=[shp, shp, shp],
        compiler_params=_params("parallel", "parallel"),
        name="gate_prep",
    )(ig, fg, bias_i, bias_f)


def _mlstm_kernel(qf_ref, kf_ref, vf_ref, cf_ref, rf_ref, qb_ref, kb_ref, vb_ref, cb_ref, rb_ref,
                  hf_ref, hb_ref, c_st, n_st, m_st):
    @pl.when(pl.program_id(1) == 0)
    def _():
        c_st[...] = jnp.zeros_like(c_st)
        n_st[...] = jnp.zeros_like(n_st)
        m_st[...] = jnp.zeros_like(m_st)

    nh, hd = MLSTM_HEADS, MLSTM_HD
    t_idx = lax.broadcasted_iota(jnp.int32, (CHUNK, CHUNK), 0)
    s_idx = lax.broadcasted_iota(jnp.int32, (CHUNK, CHUNK), 1)
    scale = hd ** -0.5
    dirs = ((qf_ref, kf_ref, vf_ref, cf_ref, rf_ref, hf_ref, False),
            (qb_ref, kb_ref, vb_ref, cb_ref, rb_ref, hb_ref, True))
    for d, (q_ref, k_ref, v_ref, col_ref, row_ref, o_ref, reverse) in enumerate(dirs):
        mask = (s_idx >= t_idx) if reverse else (s_idx <= t_idx)
        last = 0 if reverse else CHUNK - 1
        for h in range(nh):
            j = d * nh + h
            hs = slice(h * hd, (h + 1) * hd)
            q = q_ref[0, :, hs]
            k = k_ref[0, :, hs] * scale
            v = v_ref[0, :, hs]
            bc = col_ref[0, :, j:j + 1]
            a_c = col_ref[0, :, 2 * nh + j:2 * nh + j + 1]
            amax = col_ref[0, :, 4 * nh + j:4 * nh + j + 1]
            a_r = row_ref[0, j:j + 1, :]
            m_prev = m_st[j, 0:1, 0:1]
            mt = jnp.maximum(amax, m_prev)
            dmat = jnp.exp(jnp.where(mask, a_r - mt, -jnp.inf))
            s = lax.dot_general(q, k, (((1,), (1,)), ((), ())), preferred_element_type=F32) * dmat
            sc = jnp.exp(m_prev - mt)
            inter = jnp.dot(q, c_st[j].astype(BF16), preferred_element_type=F32)
            num = jnp.dot(s.astype(BF16), v, preferred_element_type=F32) + sc * inter
            qn = jnp.sum(q.astype(F32) * n_st[j, 0:1, :], axis=-1, keepdims=True)
            den = jnp.sum(s, axis=-1, keepdims=True) + sc * qn
            inv = 1.0 / jnp.maximum(jnp.abs(den), jnp.exp(-(bc + mt)))
            o_ref[0, :, hs] = (num * inv).astype(o_ref.dtype)

            m_last = mt[last:last + 1, :]
            btot = bc[last:last + 1, :]
            dec = jnp.exp(m_prev - m_last)
            kw = jnp.exp(a_c - m_last) * k.astype(F32)
            upd = lax.dot_general(kw.astype(BF16), v, (((0,), (0,)), ((), ())),
                                  preferred_element_type=F32)
            c_st[j] = dec * c_st[j] + upd
            n_new = dec * n_st[j, 0:1, :] + jnp.sum(kw, axis=0, keepdims=True)
            n_st[j] = jnp.broadcast_to(n_new, n_st.shape[1:])
            m_st[j] = jnp.broadcast_to(btot + m_last, m_st.shape[1:])


def mlstm_scan(qk, proj, v_col, cols, rows, out_dtype):
    bsz, l, _ = qk.shape
    w = D_MODEL
    nc = l // CHUNK
    vb = v_col // w
    fwd = lambda cb: (lambda b, c: (b, c, cb))
    bwd = lambda cb: (lambda b, c: (b, nc - 1 - c, cb))
    blk = lambda im: pl.BlockSpec((1, CHUNK, w), im)
    ncol = cols.shape[-1]
    in_specs = []
    for mk in (fwd, bwd):
        in_specs += [blk(mk(0)), blk(mk(1)), blk(mk(vb)),
                     pl.BlockSpec((1, CHUNK, ncol), mk(0)),
                     pl.BlockSpec((1, 2 * MLSTM_HEADS, CHUNK),
                                  (lambda b, c: (b, 0, c)) if mk is fwd else (lambda b, c: (b, 0, nc - 1 - c)))]
    nst = 2 * MLSTM_HEADS
    shp = jax.ShapeDtypeStruct((bsz, l, w), out_dtype)
    return pl.pallas_call(
        _mlstm_kernel,
        grid=(bsz, nc),
        in_specs=in_specs,
        out_specs=[blk(fwd(0)), blk(bwd(0))],
        out_shape=[shp, shp],
        scratch_shapes=[pltpu.VMEM((nst, MLSTM_HD, MLSTM_HD), F32),
                        pltpu.VMEM((nst, 8, MLSTM_HD), F32),
                        pltpu.VMEM((nst, 8, LANES), F32)],
        compiler_params=_params("parallel", "arbitrary"),
        name="mlstm_scan",
    )(qk, qk, proj, cols, rows, qk, qk, proj, cols, rows)


def _rope_table_kernel(inv_ref, cos_ref, sin_ref):
    tl = cos_ref.shape[0]
    pos = (lax.broadcasted_iota(jnp.int32, cos_ref.shape, 0) + pl.program_id(0) * tl).astype(F32)
    ang = pos * inv_ref[...]
    cos_ref[...] = jnp.cos(ang)
    sin_ref[...] = jnp.sin(ang)


def rope_tables(l, tl=1024):
    half = RET_HDK // 2
    tl = min(tl, l)
    inv = (ROPE_BASE ** (-np.arange(0, RET_HDK, 2, dtype=np.float32) / RET_HDK)).astype(np.float32)
    spec = pl.BlockSpec((tl, half), lambda i: (i, 0))
    shp = jax.ShapeDtypeStruct((l, half), F32)
    return pl.pallas_call(
        _rope_table_kernel,
        grid=(l // tl,),
        in_specs=[pl.BlockSpec((1, half), lambda i: (0, 0))],
        out_specs=[spec, spec],
        out_shape=[shp, shp],
        compiler_params=_params("parallel"),
        name="rope_tables",
    )(jnp.asarray(inv).reshape(1, half))


def _rope(x, cos, sin):
    half = x.shape[-1] // 2
    x1, x2 = x[:, :half], x[:, half:]
    return jnp.concatenate([x1 * cos - x2 * sin, x1 * sin + x2 * cos], axis=-1)


def _retention_kernel(qf_ref, kf_ref, vf_ref, cosf_ref, sinf_ref, qb_ref, kb_ref, vb_ref, cosb_ref, sinb_ref,
                      logit_ref, of_ref, ob_ref, r_st):
    @pl.when(pl.program_id(1) == 0)
    def _():
        r_st[...] = jnp.zeros_like(r_st)

    nh, dk, dv = RET_HEADS, RET_HDK, RET_HDV
    t_idx = lax.broadcasted_iota(jnp.int32, (CHUNK, CHUNK), 0)
    s_idx = lax.broadcasted_iota(jnp.int32, (CHUNK, CHUNK), 1)
    pos = lax.broadcasted_iota(jnp.int32, (CHUNK, 1), 0).astype(F32)
    dirs = ((qf_ref, kf_ref, vf_ref, cosf_ref, sinf_ref, of_ref, False),
            (qb_ref, kb_ref, vb_ref, cosb_ref, sinb_ref, ob_ref, True))
    for d, (q_ref, k_ref, v_ref, cos_ref, sin_ref, o_ref, reverse) in enumerate(dirs):
        cos = cos_ref[...]
        sin = sin_ref[...]
        rel = ((s_idx - t_idx) if reverse else (t_idx - s_idx)).astype(F32)
        for h in range(nh):
            j = d * nh + h
            lg = _log_sigmoid(logit_ref[j])
            lg_r = lg[0:1, :]
            lg_s = lg[0:1, 0:1]
            dmask = jnp.where(rel >= 0, jnp.exp(lg_r * jnp.maximum(rel, 0.0)), 0.0)
            if reverse:
                q_dec = jnp.exp(lg_s * (CHUNK - pos))
                k_dec = jnp.exp(lg_s * pos)
            else:
                q_dec = jnp.exp(lg_s * (pos + 1.0))
                k_dec = jnp.exp(lg_s * (CHUNK - 1.0 - pos))
            c_dec = jnp.exp(lg_s * CHUNK)
            q = _rope(q_ref[0, :, h * dk:(h + 1) * dk].astype(F32), cos, sin) * dk ** -0.5
            k = _rope(k_ref[0, :, h * dk:(h + 1) * dk].astype(F32), cos, sin)
            v = v_ref[0, :, h * dv:(h + 1) * dv]
            s = lax.dot_general(q.astype(BF16), k.astype(BF16), (((1,), (1,)), ((), ())),
                                preferred_element_type=F32) * dmask
            out = jnp.dot(s.astype(BF16), v, preferred_element_type=F32)
            out = out + jnp.dot((q * q_dec).astype(BF16), r_st[j].astype(BF16), preferred_element_type=F32)
            o_ref[0, :, h * dv:(h + 1) * dv] = out.astype(o_ref.dtype)
            upd = lax.dot_general((k * k_dec).astype(BF16), v, (((0,), (0,)), ((), ())),
                                  preferred_element_type=F32)
            r_st[j] = c_dec * r_st[j] + upd


def retention_scan(proj, logit_b, cos, sin, out_dtype):
    bsz, l, _ = proj.shape
    nc = l // CHUNK
    qw, vw = RET_HEADS * RET_HDK, RET_HEADS * RET_HDV
    half = RET_HDK // 2
    in_specs = []
    for rev in (False, True):
        ci = (lambda c: nc - 1 - c) if rev else (lambda c: c)
        in_specs += [pl.BlockSpec((1, CHUNK, qw), lambda b, c, ci=ci: (b, ci(c), 0)),
                     pl.BlockSpec((1, CHUNK, qw), lambda b, c, ci=ci: (b, ci(c), 1)),
                     pl.BlockSpec((1, CHUNK, vw), lambda b, c, ci=ci: (b, ci(c), 1)),
                     pl.BlockSpec((CHUNK, half), lambda b, c, ci=ci: (ci(c), 0)),
                     pl.BlockSpec((CHUNK, half), lambda b, c, ci=ci: (ci(c), 0))]
    in_specs.append(pl.BlockSpec((2 * RET_HEADS, 8, LANES), lambda b, c: (0, 0, 0)))
    shp = jax.ShapeDtypeStruct((bsz, l, vw), out_dtype)
    return pl.pallas_call(
        _retention_kernel,
        grid=(bsz, nc),
        in_specs=in_specs,
        out_specs=[pl.BlockSpec((1, CHUNK, vw), lambda b, c: (b, c, 0)),
                   pl.BlockSpec((1, CHUNK, vw), lambda b, c: (b, nc - 1 - c, 0))],
        out_shape=[shp, shp],
        scratch_shapes=[pltpu.VMEM((2 * RET_HEADS, RET_HDK, RET_HDV), F32)],
        compiler_params=_params("parallel", "arbitrary"),
        name="retention_scan",
    )(proj, proj, proj, cos, sin, proj, proj, proj, cos, sin, logit_b)


def _hyena_filter_kernel(bands_ref, w1t_ref, w1c_ref, w1s_ref, b1_ref, f1_ref, w2_ref, b2_ref, f2_ref,
                         w3_ref, delta_ref, kern_ref, asum_ref, *, l):
    i = pl.program_id(0)
    tr = kern_ref.shape[0]
    n = lax.broadcasted_iota(jnp.int32, (tr, 1), 0) + i * tr
    p = jnp.where(n < l, n, 2 * l - n).astype(F32)
    tt = p / (l - 1.0)
    ang = (2.0 * math.pi / l) * bands_ref[...] * p
    pre = tt * w1t_ref[...]
    pre = pre + jnp.dot(jnp.cos(ang), w1c_ref[...], preferred_element_type=F32, precision=HI)
    pre = pre + jnp.dot(-jnp.sin(ang), w1s_ref[...], preferred_element_type=F32, precision=HI)
    z = jnp.sin(f1_ref[...] * (pre + b1_ref[...]))
    z = jnp.sin(f2_ref[...] * (jnp.dot(z, w2_ref[...], preferred_element_type=F32, precision=HI) + b2_ref[...]))
    hk = jnp.dot(z, w3_ref[...], preferred_element_type=F32, precision=HI)
    hk = hk * (jnp.exp(-tt * jnp.abs(delta_ref[...])) + HYENA_SHIFT)
    hk = jnp.where(n == l, 0.0, hk)
    kern_ref[...] = hk

    @pl.when(i == 0)
    def _():
        asum_ref[...] = jnp.zeros_like(asum_ref)

    asum_ref[...] += jnp.broadcast_to(jnp.sum(jnp.abs(hk), axis=0, keepdims=True), asum_ref.shape)


def hyena_filter(l, w1, b1, f1, w2, b2, f2, w3, delta, tr=512):
    wd = HYENA_WIDTH
    hid = w1.shape[1]
    tr = min(tr, l)

    def pad(a, rows, cols):
        a = a.astype(F32)
        return jnp.pad(a, ((0, rows - a.shape[0]), (0, cols - a.shape[1])))

    bands = np.zeros((1, LANES), np.float32)
    bands[0, :HYENA_BANDS] = np.linspace(1e-4, HYENA_BANDS - 1, HYENA_BANDS, dtype=np.float32)
    w1t = pad(w1[0:1], 1, LANES)
    w1c = pad(w1[1:1 + HYENA_BANDS], LANES, LANES)
    w1s = pad(w1[1 + HYENA_BANDS:], LANES, LANES)
    vec = lambda a: pad(a.reshape(1, hid), 1, LANES)
    w2p = pad(w2, LANES, LANES)
    w3p = pad(w3, LANES, 2 * wd)
    full = lambda shape: pl.BlockSpec(shape, lambda i: (0, 0))
    half_sel = lambda i: (0, (i * tr) // l)
    return pl.pallas_call(
        functools.partial(_hyena_filter_kernel, l=l),
        grid=(2 * l // tr,),
        in_specs=[full((1, LANES)), full((1, LANES)), full((LANES, LANES)), full((LANES, LANES)),
                  full((1, LANES)), full((1, LANES)), full((LANES, LANES)), full((1, LANES)), full((1, LANES)),
                  pl.BlockSpec((LANES, wd), half_sel), pl.BlockSpec((1, wd), half_sel)],
        out_specs=[pl.BlockSpec((tr, wd), lambda i: (i, 0)), pl.BlockSpec((8, wd), lambda i: (0, 0))],
        out_shape=[jax.ShapeDtypeStruct((2 * l, wd), F32), jax.ShapeDtypeStruct((8, wd), F32)],
        compiler_params=_params("arbitrary"),
        name="hyena_filter",
    )(jnp.asarray(bands), w1t, w1c, w1s, vec(b1), vec(f1), w2p, vec(b2), vec(f2), w3p,
      delta.reshape(1, 2 * wd).astype(F32))


def _dft_consts(n1, n2):
    n = n1 * n2
    k1 = np.arange(n1)
    f1 = np.exp(-2j * np.pi * np.outer(k1, k1) / n1)
    k2 = np.arange(n2)
    f2 = np.exp(-2j * np.pi * np.outer(k2, k2) / n2)
    tw = np.exp(-2j * np.pi * np.outer(k1, k2) / n)
    g = f2[None, :, :] * tw[:, None, :]
    ginv = np.conj(np.transpose(g, (0, 2, 1))) / n
    f1inv = np.conj(f1.T)
    return f1, g, ginv, f1inv


def _stack(c):
    return np.concatenate([c.real, c.imag], axis=-2).astype(np.float32)


def _cdot(fs, xr, xi, rows):
    p = jnp.dot(fs, xr, preferred_element_type=F32, precision=HI)
    if xi is None:
        return p[:rows], p[rows:]
    q = jnp.dot(fs, xi, preferred_element_type=F32, precision=HI)
    return p[:rows] - q[rows:], q[:rows] + p[rows:]


def _fft_s1_kernel(*refs, n1, complex_in):
    if complex_in:
        fs_ref, xr_ref, xi_ref, ar_ref, ai_ref = refs
        xi = xi_ref[0]
    else:
        fs_ref, xr_ref, ar_ref, ai_ref = refs
        xi = None
    ar, ai = _cdot(fs_ref[...], xr_ref[0], xi, n1)
    ar_ref[0] = ar
    ai_ref[0] = ai


def fft_stage1(x, n1, rows_in, pairs, complex_in, tc=4096):
    c = x.shape[-1]
    cols = FFT_N2 * c
    xv = x.reshape(x.shape[0], rows_in, cols)
    f1 = _dft_consts(n1, FFT_N2)[0][:, :rows_in]
    fs = jnp.asarray(_stack(f1))
    in_specs = [pl.BlockSpec((2 * n1, rows_in), lambda p, j: (0, 0)),
                pl.BlockSpec((1, rows_in, tc), lambda p, j: (p, 0, j))]
    args = [fs, xv]
    if complex_in:
        in_specs.append(pl.BlockSpec((1, rows_in, tc), lambda p, j: (p + pairs, 0, j)))
        args.append(xv)
    out_spec = pl.BlockSpec((1, n1, tc), lambda p, j: (p, 0, j))
    shp = jax.ShapeDtypeStruct((pairs, n1, cols), F32)
    return pl.pallas_call(
        functools.partial(_fft_s1_kernel, n1=n1, complex_in=complex_in),
        grid=(pairs, cols // tc),
        in_specs=in_specs,
        out_specs=[out_spec, out_spec],
        out_shape=[shp, shp],
        compiler_params=_params("parallel", "parallel"),
        name="fft_stage1",
    )(*args)


def _fft_mid_kernel(*refs, conv):
    n2 = FFT_N2
    if conv:
        gs_ref, gis_ref, ar_ref, ai_ref, kr_ref, ki_ref, br_ref, bi_ref = refs
    else:
        gs_ref, ar_ref, ai_ref, br_ref, bi_ref = refs
    xr, xi = _cdot(gs_ref[0], ar_ref[0, 0], ai_ref[0, 0], n2)
    if conv:
        kr, ki = kr_ref[0], ki_ref[0]
        yr = xr * kr - xi * ki
        yi = xr * ki + xi * kr
        xr, xi = _cdot(gis_ref[0], yr, yi, n2)
    br_ref[0, 0] = xr
    bi_ref[0, 0] = xi


def fft_mid(ar, ai, n1, c, kf=None, ct=512):
    n2 = FFT_N2
    pairs = ar.shape[0]
    _, g, ginv, _ = _dft_consts(n1, n2)
    a4 = lambda a: a.reshape(pairs, n1, n2, c)
    mat = pl.BlockSpec((1, 2 * n2, n2), lambda k, cb, p: (k, 0, 0))
    dat = pl.BlockSpec((1, 1, n2, ct), lambda k, cb, p: (p, k, 0, cb))
    in_specs = [mat]
    args = [jnp.asarray(_stack(g))]
    if kf is not None:
        in_specs.append(mat)
        args.append(jnp.asarray(_stack(ginv)))
    in_specs += [dat, dat]
    args += [a4(ar), a4(ai)]
    if kf is not None:
        fil = pl.BlockSpec((1, n2, ct), lambda k, cb, p: (k, 0, cb))
        in_specs += [fil, fil]
        args += [kf[0], kf[1]]
    shp = jax.ShapeDtypeStruct((pairs, n1, n2, c), F32)
    br, bi = pl.pallas_call(
        functools.partial(_fft_mid_kernel, conv=kf is not None),
        grid=(n1, c // ct, pairs),
        in_specs=in_specs,
        out_specs=[dat, dat],
        out_shape=[shp, shp],
        compiler_params=_params("parallel", "parallel", "arbitrary"),
        name="fft_mid",
    )(*args)
    return br, bi


def _fft_s1inv_kernel(fs_ref, br_ref, bi_ref, y_ref, *, rows_out, complex_out):
    fs = fs_ref[...]
    p = jnp.dot(fs, br_ref[0], preferred_element_type=F32, precision=HI)
    q = jnp.dot(fs, bi_ref[0], preferred_element_type=F32, precision=HI)
    y_ref[0, 0] = p[:rows_out] - q[rows_out:]
    if complex_out:
        y_ref[1, 0] = q[:rows_out] + p[rows_out:]


def fft_stage1_inv(br, bi, n1, rows_out, c, complex_out, tc=4096):
    pairs = br.shape[0]
    cols = FFT_N2 * c
    f1inv = _dft_consts(n1, FFT_N2)[3][:rows_out, :]
    parts = 2 if complex_out else 1
    dat = pl.BlockSpec((1, n1, tc), lambda p, j: (p, 0, j))
    y = pl.pallas_call(
        functools.partial(_fft_s1inv_kernel, rows_out=rows_out, complex_out=complex_out),
        grid=(pairs, cols // tc),
        in_specs=[pl.BlockSpec((2 * rows_out, n1), lambda p, j: (0, 0)), dat, dat],
        out_specs=pl.BlockSpec((parts, 1, rows_out, tc), lambda p, j: (0, p, 0, j)),
        out_shape=jax.ShapeDtypeStruct((parts, pairs, rows_out, cols), F32),
        compiler_params=_params("parallel", "parallel"),
        name="fft_stage1_inv",
    )(jnp.asarray(_stack(f1inv)), br.reshape(pairs, n1, cols), bi.reshape(pairs, n1, cols))
    return y.reshape(parts * pairs, rows_out * FFT_N2, c)


def hyena_long_conv(t, kern):
    bsz, l, c = t.shape
    n1 = 2 * l // FFT_N2
    kr, ki = fft_stage1(kern[None], n1, n1, 1, False)
    kr, ki = fft_mid(kr, ki, n1, c)
    kf = (kr.reshape(n1, FFT_N2, c), ki.reshape(n1, FFT_N2, c))
    complex_in = bsz % 2 == 0
    pairs = bsz // 2 if complex_in else bsz
    ar, ai = fft_stage1(t, n1, n1 // 2, pairs, complex_in)
    br, bi = fft_mid(ar, ai, n1, c, kf)
    return fft_stage1_inv(br, bi, n1, n1 // 2, c, complex_in)


def _head_ln(x, nheads, g):
    hd = x.shape[-1] // nheads
    outs = []
    for h in range(nheads):
        seg = x[:, h * hd:(h + 1) * hd]
        mu = jnp.mean(seg, axis=-1, keepdims=True)
        cen = seg - mu
        var = jnp.mean(cen * cen, axis=-1, keepdims=True)
        outs.append(cen * lax.rsqrt(var + GN_EPS))
    return jnp.concatenate(outs, axis=-1) * g


def _even_out_kernel(x_ref, hf_ref, hb_ref, o_ref, za_ref, x0_ref, y_ref, t_ref, zb_ref,
                     g_ref, skip_ref, asum_ref, w_ref, out_ref):
    wd = D_MODEL
    ln = _head_ln(hf_ref[...].astype(F32) + hb_ref[...].astype(F32), MLSTM_HEADS, g_ref[...])
    ya = ln * _sigmoid(o_ref[...].astype(F32)) * _silu(za_ref[...].astype(F32))
    t = t_ref[...]
    conv = y_ref[...] * (1.0 / asum_ref[0:1, :])
    yb = x0_ref[...].astype(F32) * (conv + skip_ref[...] * t) * _silu(zb_ref[...].astype(F32))
    mix = jnp.dot(ya.astype(BF16), w_ref[0:wd, :], preferred_element_type=F32)
    mix = mix + jnp.dot(yb.astype(BF16), w_ref[wd:2 * wd, :], preferred_element_type=F32)
    out_ref[...] = x_ref[...] + mix


def even_out(x, hf, hb, proj, x0, y, t, g, skip, asum, w_out, tm=256):
    m, d = x.shape
    row = lambda cb: pl.BlockSpec((tm, d), lambda i, cb=cb: (i, cb))
    vec = pl.BlockSpec((1, d), lambda i: (0, 0))
    return pl.pallas_call(
        _even_out_kernel,
        grid=(m // tm,),
        in_specs=[row(0), row(0), row(0), row(3), row(4), row(0), row(0), row(0), row(8),
                  vec, vec, pl.BlockSpec((8, d), lambda i: (0, 0)),
                  pl.BlockSpec((2 * d, d), lambda i: (0, 0))],
        out_specs=row(0),
        out_shape=jax.ShapeDtypeStruct((m, d), F32),
        compiler_params=_params("parallel"),
        name="even_out",
    )(x, hf, hb, proj, proj, x0, y, t, proj, g.reshape(1, d).astype(F32), skip.reshape(1, d).astype(F32),
      asum, w_out)


def _odd_out_kernel(x_ref, of_ref, ob_ref, gate_ref, g_ref, w_ref, out_ref):
    o = _head_ln(of_ref[...].astype(F32) + ob_ref[...].astype(F32), RET_HEADS, g_ref[...])
    y = _silu(gate_ref[...].astype(F32)) * o
    out_ref[...] = x_ref[...] + jnp.dot(y.astype(BF16), w_ref[...], preferred_element_type=F32)


def odd_out(x, of, ob, proj, g, w_out, tm=256):
    m, d = x.shape
    vw = RET_HEADS * RET_HDV
    wide = lambda cb: pl.BlockSpec((tm, vw), lambda i, cb=cb: (i, cb))
    return pl.pallas_call(
        _odd_out_kernel,
        grid=(m // tm,),
        in_specs=[pl.BlockSpec((tm, d), lambda i: (i, 0)), wide(0), wide(0), wide(2),
                  pl.BlockSpec((1, vw), lambda i: (0, 0)), pl.BlockSpec((vw, d), lambda i: (0, 0))],
        out_specs=pl.BlockSpec((tm, d), lambda i: (i, 0)),
        out_shape=jax.ShapeDtypeStruct((m, d), F32),
        compiler_params=_params("parallel"),
        name="odd_out",
    )(x, of, ob, proj, g.reshape(1, vw).astype(F32), w_out)


def _cross_attn_kernel(x_ref, kv_ref, gq_ref, wq_ref, wo_ref, gf_ref, out_ref, *, final_norm):
    d = D_MODEL
    x = x_ref[0]
    xn = x * lax.rsqrt(jnp.mean(x * x, axis=-1, keepdims=True) + EPS) * gq_ref[...]
    q = jnp.dot(xn.astype(BF16), wq_ref[...], preferred_element_type=F32).astype(BF16)
    outs = []
    for h in range(CA_HEADS):
        hs = slice(h * CA_HD, (h + 1) * CA_HD)
        k = kv_ref[0, :, hs]
        v = kv_ref[0, :, d + h * CA_HD:d + (h + 1) * CA_HD]
        s = lax.dot_general(q[:, hs], k, (((1,), (1,)), ((), ())), preferred_element_type=F32) * CA_HD ** -0.5
        p = jnp.exp(s - jnp.max(s, axis=-1, keepdims=True))
        o = jnp.dot(p.astype(BF16), v, preferred_element_type=F32)
        outs.append(o * (1.0 / jnp.sum(p, axis=-1, keepdims=True)))
    o = jnp.concatenate(outs, axis=-1).astype(BF16)
    y = x + jnp.dot(o, wo_ref[...], preferred_element_type=F32)
    if final_norm:
        y = y * lax.rsqrt(jnp.mean(y * y, axis=-1, keepdims=True) + EPS) * gf_ref[...]
    out_ref[0] = y


def cross_attn(x, kv, gq, wq, wo, gf, final_norm, tm=512):
    bsz, l, d = x.shape
    nm = kv.shape[1]
    tm = min(tm, l)
    vec = pl.BlockSpec((1, d), lambda b, i: (0, 0))
    mat = pl.BlockSpec((d, d), lambda b, i: (0, 0))
    return pl.pallas_call(
        functools.partial(_cross_attn_kernel, final_norm=final_norm),
        grid=(bsz, l // tm),
        in_specs=[pl.BlockSpec((1, tm, d), lambda b, i: (b, i, 0)),
                  pl.BlockSpec((1, nm, 2 * d), lambda b, i: (b, 0, 0)),
                  vec, mat, mat, vec],
        out_specs=pl.BlockSpec((1, tm, d), lambda b, i: (b, i, 0)),
        out_shape=jax.ShapeDtypeStruct((bsz, l, d), F32),
        compiler_params=_params("parallel", "parallel"),
        name="cross_attn",
    )(x, kv, gq.reshape(1, d).astype(F32), wq, wo, gf.reshape(1, d).astype(F32))


def _even_mixer(x, p, i):
    bsz, l, d = x.shape
    w = D_MODEL
    nh = MLSTM_HEADS
    m = bsz * l
    g_mix = p['norm_mix_g_layer']
    w_in = p['even_w_in'][i]
    gate0 = 5 * w
    w_main = jnp.concatenate([w_in[:, :gate0], w_in[:, gate0 + N_GATE_COLS:]], axis=1).astype(BF16)
    w_gate = jnp.pad(w_in[:, gate0:gate0 + N_GATE_COLS], ((0, 0), (0, LANES - N_GATE_COLS))).astype(BF16)
    xf = x.reshape(m, d)
    proj = rms_matmul(xf, g_mix, w_main, BF16, tm=512, tn=1024)
    gates = rms_matmul(xf, g_mix, w_gate, F32, tm=512, tn=LANES)
    proj3 = proj.reshape(bsz, l, 9 * w)

    gates = gates.reshape(bsz, l, LANES)[..., :N_GATE_COLS].reshape(bsz, l, 2, 2, nh)
    bias = p['mlstm_gate_bias'][i].astype(F32)
    lane_pad = lambda a: jnp.pad(a, [(0, 0)] * (a.ndim - 1) + [(0, LANES - 2 * nh)])
    ig = lane_pad(gates[:, :, :, 0, :].reshape(bsz, l, 2 * nh))
    fg = lane_pad(gates[:, :, :, 1, :].reshape(bsz, l, 2 * nh))
    bias_i = lane_pad(bias[:, 0, :].reshape(1, 2 * nh))
    bias_f = lane_pad(bias[:, 1, :].reshape(1, 2 * nh))
    bcum, a, amax = gate_prep(ig, fg, bias_i, bias_f)
    k8 = 2 * nh
    cols = jnp.concatenate([bcum[..., :k8], a[..., :k8], amax[..., :k8],
                            jnp.zeros((bsz, l, k8), F32)], axis=-1)
    rows = jnp.swapaxes(a[..., :k8], 1, 2)

    qk = qk_conv(proj3, p['mlstm_conv_w'][i], p['mlstm_conv_b'][i])
    hf, hb = mlstm_scan(qk, proj3, 2 * w, cols, rows, F32)

    x0, t = hyena_pre(proj3, 5 * w, p['hyena_conv_w'][i], p['hyena_conv_b'][i])
    kern, asum = hyena_filter(l, p['hyena_w1'][i], p['hyena_b1'][i], p['hyena_freq1'][i], p['hyena_w2'][i],
                              p['hyena_b2'][i], p['hyena_freq2'][i], p['hyena_w3'][i], p['hyena_delta'][i])
    y = hyena_long_conv(t, kern)

    out = even_out(xf, hf.reshape(m, w), hb.reshape(m, w), proj, x0.reshape(m, w), y.reshape(m, w),
                   t.reshape(m, w), p['mlstm_norm_g'][i], p['hyena_skip'][i], asum,
                   p['even_w_out'][i].astype(BF16))
    return out.reshape(bsz, l, d)


def _odd_mixer(x, p, i, cos, sin):
    bsz, l, d = x.shape
    m = bsz * l
    xf = x.reshape(m, d)
    proj = rms_matmul(xf, p['norm_mix_g_layer'], p['odd_w_in'][i].astype(BF16), BF16, tm=512, tn=1024)
    logit = p['ret_decay_logit'][i].astype(F32).reshape(2 * RET_HEADS, 1, 1)
    logit_b = jnp.broadcast_to(logit, (2 * RET_HEADS, 8, LANES))
    of, ob = retention_scan(proj.reshape(bsz, l, -1), logit_b, cos, sin, F32)
    vw = RET_HEADS * RET_HDV
    out = odd_out(xf, of.reshape(m, vw), ob.reshape(m, vw), proj, p['ret_norm_g'][i],
                  p['odd_w_out'][i].astype(BF16))
    return out.reshape(bsz, l, d)


def _trunk(x, mem, p, cos, sin):
    depth = p['norm_mix_g'].shape[0]
    bsz, nm, d = mem.shape
    for layer in range(depth):
        i = layer // 2
        p['norm_mix_g_layer'] = p['norm_mix_g'][layer]
        x = _even_mixer(x, p, i) if layer % 2 == 0 else _odd_mixer(x, p, i, cos, sin)
        kv = rms_matmul(mem.reshape(bsz * nm, d), p['norm_mem_g'][layer], p['ca_wkv'][layer].astype(BF16),
                        BF16, tm=nm, tn=1024).reshape(bsz, nm, 2 * d)
        x = cross_attn(x, kv, p['norm_ca_g'][layer], p['ca_wq'][layer].astype(BF16),
                       p['ca_wo'][layer].astype(BF16), p['norm_final_g'], layer == depth - 1)
    return x


def kernel(x_prompt, x_sample, mem_prompt, mem_sample, norm_mix_g, norm_ca_g, norm_mem_g, norm_final_g, even_w_in, mlstm_conv_w, mlstm_conv_b, mlstm_gate_bias, mlstm_norm_g, hyena_conv_w, hyena_conv_b, hyena_w1, hyena_b1, hyena_freq1, hyena_w2, hyena_b2, hyena_freq2, hyena_w3, hyena_delta, hyena_skip, even_w_out, odd_w_in, ret_decay_logit, ret_norm_g, odd_w_out, ca_wq, ca_wkv, ca_wo):
    p = {'norm_mix_g': norm_mix_g, 'norm_ca_g': norm_ca_g, 'norm_mem_g': norm_mem_g, 'norm_final_g': norm_final_g,
         'even_w_in': even_w_in, 'mlstm_conv_w': mlstm_conv_w, 'mlstm_conv_b': mlstm_conv_b,
         'mlstm_gate_bias': mlstm_gate_bias, 'mlstm_norm_g': mlstm_norm_g,
         'hyena_conv_w': hyena_conv_w, 'hyena_conv_b': hyena_conv_b, 'hyena_w1': hyena_w1, 'hyena_b1': hyena_b1,
         'hyena_freq1': hyena_freq1, 'hyena_w2': hyena_w2, 'hyena_b2': hyena_b2, 'hyena_freq2': hyena_freq2,
         'hyena_w3': hyena_w3, 'hyena_delta': hyena_delta, 'hyena_skip': hyena_skip, 'even_w_out': even_w_out,
         'odd_w_in': odd_w_in, 'ret_decay_logit': ret_decay_logit, 'ret_norm_g': ret_norm_g, 'odd_w_out': odd_w_out,
         'ca_wq': ca_wq, 'ca_wkv': ca_wkv, 'ca_wo': ca_wo}
    l_max = max(x_prompt.shape[1], x_sample.shape[1])
    cos, sin = rope_tables(l_max)
    y_prompt = _trunk(x_prompt, mem_prompt, dict(p), cos, sin)
    y_sample = _trunk(x_sample, mem_sample, dict(p), cos, sin)
    return (y_prompt, y_sample)
```

```python
import functools
import math

import numpy as np
import jax
import jax.numpy as jnp
from jax import lax
from jax.experimental import pallas as pl
from jax.experimental.pallas import tpu as pltpu

F32 = jnp.float32
BF16 = jnp.bfloat16

D_MODEL = 1024
EPS = 1e-6
GN_EPS = 1e-5
CHUNK = 128

MLSTM_HEADS = 4
MLSTM_HD = D_MODEL // MLSTM_HEADS
N_GATE_COLS = 4 * MLSTM_HEADS

HYENA_WIDTH = D_MODEL
HYENA_EMB = 33
HYENA_BANDS = (HYENA_EMB - 1) // 2
HYENA_SHIFT = 0.05
FFT_N2 = 256

RET_HEADS = 4
RET_HDK = D_MODEL // RET_HEADS
RET_HDV = 2 * D_MODEL // RET_HEADS
ROPE_BASE = 10000.0

CA_HEADS = 4
CA_HD = D_MODEL // CA_HEADS

LANES = 128
BF16_SUBLANES = 16
VMEM_LIMIT = 56 * 1024 * 1024

HI = lax.Precision.HIGHEST


def _params(*sem):
    return pltpu.CompilerParams(dimension_semantics=sem, vmem_limit_bytes=VMEM_LIMIT)


def _silu(x):
    return x * (1.0 / (1.0 + jnp.exp(-x)))


def _sigmoid(x):
    return 1.0 / (1.0 + jnp.exp(-x))


def _log_sigmoid(x):
    return jnp.minimum(x, 0.0) - jnp.log(1.0 + jnp.exp(-jnp.abs(x)))


def _rms_matmul_kernel(x_ref, g_ref, *refs, tn):
    nw = len(refs) // 2
    x = x_ref[...]
    xn = (x * lax.rsqrt(jnp.mean(x * x, axis=-1, keepdims=True) + EPS) * g_ref[...]).astype(BF16)
    for w_ref, o_ref in zip(refs[:nw], refs[nw:]):
        n = o_ref.shape[1]
        step = min(tn, n)
        for j in range(n // step):
            cols = slice(j * step, (j + 1) * step)
            o_ref[:, cols] = jnp.dot(xn, w_ref[:, cols], preferred_element_type=F32).astype(o_ref.dtype)


def rms_matmul(x, g, ws, out_dtypes, tm, tn=1024):
    m, d = x.shape
    w_specs = [pl.BlockSpec(w.shape, lambda i: (0, 0), pipeline_mode=pl.Buffered(1)) for w in ws]
    return pl.pallas_call(
        functools.partial(_rms_matmul_kernel, tn=tn),
        grid=(m // tm,),
        in_specs=[pl.BlockSpec((tm, d), lambda i: (i, 0)), pl.BlockSpec((1, d), lambda i: (0, 0))] + w_specs,
        out_specs=[pl.BlockSpec((tm, w.shape[1]), lambda i: (i, 0)) for w in ws],
        out_shape=[jax.ShapeDtypeStruct((m, w.shape[1]), dt) for w, dt in zip(ws, out_dtypes)],
        compiler_params=_params("parallel"),
        name="rms_matmul",
    )(x, g.reshape(1, d).astype(F32), *ws)


def _conv3(x, prev_row, next_row, w, b):
    t = x.shape[0]
    row = lax.broadcasted_iota(jnp.int32, x.shape, 0)
    xm = jnp.where(row == 0, prev_row, pltpu.roll(x, 1, axis=0))
    xp = jnp.where(row == t - 1, next_row, pltpu.roll(x, t - 1, axis=0))
    return xm * w[0:1, :] + x * w[1:2, :] + xp * w[2:3, :] + b


def _halo_rows(prev_ref, next_ref, i, n_i):
    prev_row = prev_ref[0, BF16_SUBLANES - 1:BF16_SUBLANES, :].astype(F32)
    next_row = next_ref[0, 0:1, :].astype(F32)
    prev_row = jnp.where(i == 0, 0.0, prev_row)
    next_row = jnp.where(i == n_i - 1, 0.0, next_row)
    return prev_row, next_row


def _halo_specs(tl, tc, l, col_block):
    r = tl // BF16_SUBLANES
    last = l // BF16_SUBLANES - 1
    main = pl.BlockSpec((1, tl, tc), lambda b, i, c: (b, i, col_block(c)))
    prev = pl.BlockSpec((1, BF16_SUBLANES, tc),
                        lambda b, i, c: (b, jnp.maximum(i * r - 1, 0), col_block(c)))
    nxt = pl.BlockSpec((1, BF16_SUBLANES, tc),
                       lambda b, i, c: (b, jnp.minimum((i + 1) * r, last), col_block(c)))
    return main, prev, nxt


def _qk_conv_kernel(x_ref, p_ref, n_ref, w_ref, b_ref, o_ref):
    i = pl.program_id(1)
    prev_row, next_row = _halo_rows(p_ref, n_ref, i, pl.num_programs(1))
    y = _conv3(x_ref[0].astype(F32), prev_row, next_row, w_ref[...], b_ref[...])
    o_ref[0] = _silu(y).astype(o_ref.dtype)


def qk_conv(proj, w, b, tl=512, tc=512):
    bsz, l, _ = proj.shape
    c = w.shape[1]
    tl = min(tl, l)
    main, prev, nxt = _halo_specs(tl, tc, l, lambda cb: cb)
    return pl.pallas_call(
        _qk_conv_kernel,
        grid=(bsz, l // tl, c // tc),
        in_specs=[main, prev, nxt,
                  pl.BlockSpec((3, tc), lambda b_, i, cb: (0, cb)),
                  pl.BlockSpec((1, tc), lambda b_, i, cb: (0, cb))],
        out_specs=pl.BlockSpec((1, tl, tc), lambda b_, i, cb: (b_, i, cb)),
        out_shape=jax.ShapeDtypeStruct((bsz, l, c), BF16),
        compiler_params=_params("parallel", "parallel", "parallel"),
        name="qk_conv",
    )(proj, proj, proj, w.astype(F32), b.reshape(1, c).astype(F32))


def _hyena_pre_kernel(x0_ref, x0p_ref, x0n_ref, x1_ref, x1p_ref, x1n_ref, v_ref, vp_ref, vn_ref,
                      w_ref, b_ref, x0_out, t_out):
    i = pl.program_id(1)
    n_i = pl.num_programs(1)
    outs = []
    for k, (m_ref, p_ref, n_ref) in enumerate(((x0_ref, x0p_ref, x0n_ref), (x1_ref, x1p_ref, x1n_ref),
                                               (v_ref, vp_ref, vn_ref))):
        prev_row, next_row = _halo_rows(p_ref, n_ref, i, n_i)
        outs.append(_conv3(m_ref[0].astype(F32), prev_row, next_row, w_ref[k], b_ref[k]))
    x0_out[0] = outs[0].astype(x0_out.dtype)
    t_out[0] = outs[1] * outs[2]


def hyena_pre(proj, col0, w, b, tl=512, tc=512):
    bsz, l, _ = proj.shape
    wd = HYENA_WIDTH
    tl = min(tl, l)
    specs = []
    for k in range(3):
        base = (col0 + k * wd) // tc
        specs.extend(_halo_specs(tl, tc, l, lambda cb, base=base: base + cb))
    w3 = w.astype(F32).reshape(3, 3, wd).transpose(1, 0, 2)
    b3 = b.astype(F32).reshape(3, 1, wd)
    out_spec = pl.BlockSpec((1, tl, tc), lambda b_, i, cb: (b_, i, cb))
    return pl.pallas_call(
        _hyena_pre_kernel,
        grid=(bsz, l // tl, wd // tc),
        in_specs=specs + [pl.BlockSpec((3, 3, tc), lambda b_, i, cb: (0, 0, cb)),
                          pl.BlockSpec((3, 1, tc), lambda b_, i, cb: (0, 0, cb))],
        out_specs=[out_spec, out_spec],
        out_shape=[jax.ShapeDtypeStruct((bsz, l, wd), BF16), jax.ShapeDtypeStruct((bsz, l, wd), F32)],
        compiler_params=_params("parallel", "parallel", "parallel"),
        name="hyena_pre",
    )(*([proj] * 9), w3, b3)


def _seg_scan(x, op, identity, reverse):
    t = x.shape[0]
    r = lax.broadcasted_iota(jnp.int32, x.shape, 0) % CHUNK
    k = 1
    while k < CHUNK:
        if reverse:
            shifted = pltpu.roll(x, t - k, axis=0)
            valid = r < CHUNK - k
        else:
            shifted = pltpu.roll(x, k, axis=0)
            valid = r >= k
        x = op(x, jnp.where(valid, shifted, identity))
        k *= 2
    return x


def _gate_prep_kernel(ig_ref, fg_ref, bi_ref, bf_ref, bcum_ref, a_ref, amax_ref):
    ig = ig_ref[0] + bi_ref[...]
    lf = _log_sigmoid(fg_ref[0] + bf_ref[...])
    lane = lax.broadcasted_iota(jnp.int32, ig.shape, 1)
    fwd = lane < MLSTM_HEADS
    bcum = jnp.where(fwd, _seg_scan(lf, jnp.add, 0.0, False), _seg_scan(lf, jnp.add, 0.0, True))
    a = ig - bcum
    amax = jnp.where(fwd, _seg_scan(a, jnp.maximum, -jnp.inf, False),
                     _seg_scan(a, jnp.maximum, -jnp.inf, True))
    bcum_ref[0] = bcum
    a_ref[0] = a
    amax_ref[0] = amax


def gate_prep(ig, fg, bias_i, bias_f, tl=1024):
    bsz, l, _ = ig.shape
    tl = min(tl, l)
    spec = pl.BlockSpec((1, tl, LANES), lambda b, i: (b, i, 0))
    vec = pl.BlockSpec((1, LANES), lambda b, i: (0, 0))
    shp = jax.ShapeDtypeStruct((bsz, l, LANES), F32)
    return pl.pallas_call(
        _gate_prep_kernel,
        grid=(bsz, l // tl),
        in_specs=[spec, spec, vec, vec],
        out_specs=[spec, spec, spec],
        out_shape=[shp, shp, shp],
        compiler_params=_params("parallel", "parallel"),
        name="gate_prep",
    )(ig, fg, bias_i, bias_f)


def _mlstm_kernel(qf_ref, kf_ref, vf_ref, cf_ref, rf_ref, qb_ref, kb_ref, vb_ref, cb_ref, rb_ref,
                  hf_ref, hb_ref, c_st, n_st, m_st):
    @pl.when(pl.program_id(1) == 0)
    def _():
        c_st[...] = jnp.zeros_like(c_st)
        n_st[...] = jnp.zeros_like(n_st)
        m_st[...] = jnp.zeros_like(m_st)

    nh, hd = MLSTM_HEADS, MLSTM_HD
    t_idx = lax.broadcasted_iota(jnp.int32, (CHUNK, CHUNK), 0)
    s_idx = lax.broadcasted_iota(jnp.int32, (CHUNK, CHUNK), 1)
    scale = hd ** -0.5
    dirs = ((qf_ref, kf_ref, vf_ref, cf_ref, rf_ref, hf_ref, False),
            (qb_ref, kb_ref, vb_ref, cb_ref, rb_ref, hb_ref, True))
    for d, (q_ref, k_ref, v_ref, col_ref, row_ref, o_ref, reverse) in enumerate(dirs):
        mask = (s_idx >= t_idx) if reverse else (s_idx <= t_idx)
        last = 0 if reverse else CHUNK - 1
        for h in range(nh):
            j = d * nh + h
            hs = slice(h * hd, (h + 1) * hd)
            q = q_ref[0, :, hs]
            k = k_ref[0, :, hs] * scale
            v = v_ref[0, :, hs]
            bc = col_ref[0, :, j:j + 1]
            a_c = col_ref[0, :, 2 * nh + j:2 * nh + j + 1]
            amax = col_ref[0, :, 4 * nh + j:4 * nh + j + 1]
            a_r = row_ref[0, j:j + 1, :]
            m_prev = m_st[j, 0:1, 0:1]
            mt = jnp.maximum(amax, m_prev)
            dmat = jnp.exp(jnp.where(mask, a_r - mt, -jnp.inf))
            s = lax.dot_general(q, k, (((1,), (1,)), ((), ())), preferred_element_type=F32) * dmat
            sc = jnp.exp(m_prev - mt)
            inter = jnp.dot(q, c_st[j].astype(BF16), preferred_element_type=F32)
            num = jnp.dot(s.astype(BF16), v, preferred_element_type=F32) + sc * inter
            qn = jnp.sum(q.astype(F32) * n_st[j, 0:1, :], axis=-1, keepdims=True)
            den = jnp.sum(s, axis=-1, keepdims=True) + sc * qn
            inv = 1.0 / jnp.maximum(jnp.abs(den), jnp.exp(-(bc + mt)))
            o_ref[0, :, hs] = (num * inv).astype(o_ref.dtype)

            m_last = mt[last:last + 1, :]
            btot = bc[last:last + 1, :]
            dec = jnp.exp(m_prev - m_last)
            kw = jnp.exp(a_c - m_last) * k.astype(F32)
            upd = lax.dot_general(kw.astype(BF16), v, (((0,), (0,)), ((), ())),
                                  preferred_element_type=F32)
            c_st[j] = dec * c_st[j] + upd
            n_new = dec * n_st[j, 0:1, :] + jnp.sum(kw, axis=0, keepdims=True)
            n_st[j] = jnp.broadcast_to(n_new, n_st.shape[1:])
            m_st[j] = jnp.broadcast_to(btot + m_last, m_st.shape[1:])


def mlstm_scan(qk, proj, v_col, cols, rows, out_dtype):
    bsz, l, _ = qk.shape
    w = D_MODEL
    nc = l // CHUNK
    vb = v_col // w
    fwd = lambda cb: (lambda b, c: (b, c, cb))
    bwd = lambda cb: (lambda b, c: (b, nc - 1 - c, cb))
    blk = lambda im: pl.BlockSpec((1, CHUNK, w), im)
    ncol = cols.shape[-1]
    in_specs = []
    for mk in (fwd, bwd):
        in_specs += [blk(mk(0)), blk(mk(1)), blk(mk(vb)),
                     pl.BlockSpec((1, CHUNK, ncol), mk(0)),
                     pl.BlockSpec((1, 2 * MLSTM_HEADS, CHUNK),
                                  (lambda b, c: (b, 0, c)) if mk is fwd else (lambda b, c: (b, 0, nc - 1 - c)))]
    nst = 2 * MLSTM_HEADS
    shp = jax.ShapeDtypeStruct((bsz, l, w), out_dtype)
    return pl.pallas_call(
        _mlstm_kernel,
        grid=(bsz, nc),
        in_specs=in_specs,
        out_specs=[blk(fwd(0)), blk(bwd(0))],
        out_shape=[shp, shp],
        scratch_shapes=[pltpu.VMEM((nst, MLSTM_HD, MLSTM_HD), F32),
                        pltpu.VMEM((nst, 8, MLSTM_HD), F32),
                        pltpu.VMEM((nst, 8, LANES), F32)],
        compiler_params=_params("parallel", "arbitrary"),
        name="mlstm_scan",
    )(qk, qk, proj, cols, rows, qk, qk, proj, cols, rows)


def _rope_table_kernel(inv_ref, cos_ref, sin_ref):
    tl = cos_ref.shape[0]
    pos = (lax.broadcasted_iota(jnp.int32, cos_ref.shape, 0) + pl.program_id(0) * tl).astype(F32)
    ang = pos * inv_ref[...]
    cos_ref[...] = jnp.cos(ang)
    sin_ref[...] = jnp.sin(ang)


def rope_tables(l, tl=1024):
    half = RET_HDK // 2
    tl = min(tl, l)
    inv = (ROPE_BASE ** (-np.arange(0, RET_HDK, 2, dtype=np.float32) / RET_HDK)).astype(np.float32)
    spec = pl.BlockSpec((tl, half), lambda i: (i, 0))
    shp = jax.ShapeDtypeStruct((l, half), F32)
    return pl.pallas_call(
        _rope_table_kernel,
        grid=(l // tl,),
        in_specs=[pl.BlockSpec((1, half), lambda i: (0, 0))],
        out_specs=[spec, spec],
        out_shape=[shp, shp],
        compiler_params=_params("parallel"),
        name="rope_tables",
    )(jnp.asarray(inv).reshape(1, half))


def _rope(x, cos, sin):
    half = x.shape[-1] // 2
    x1, x2 = x[:, :half], x[:, half:]
    return jnp.concatenate([x1 * cos - x2 * sin, x1 * sin + x2 * cos], axis=-1)


def _retention_kernel(qf_ref, kf_ref, vf_ref, cosf_ref, sinf_ref, qb_ref, kb_ref, vb_ref, cosb_ref, sinb_ref,
                      logit_ref, of_ref, ob_ref, r_st):
    @pl.when(pl.program_id(1) == 0)
    def _():
        r_st[...] = jnp.zeros_like(r_st)

    nh, dk, dv = RET_HEADS, RET_HDK, RET_HDV
    t_idx = lax.broadcasted_iota(jnp.int32, (CHUNK, CHUNK), 0)
    s_idx = lax.broadcasted_iota(jnp.int32, (CHUNK, CHUNK), 1)
    pos = lax.broadcasted_iota(jnp.int32, (CHUNK, 1), 0).astype(F32)
    dirs = ((qf_ref, kf_ref, vf_ref, cosf_ref, sinf_ref, of_ref, False),
            (qb_ref, kb_ref, vb_ref, cosb_ref, sinb_ref, ob_ref, True))
    for d, (q_ref, k_ref, v_ref, cos_ref, sin_ref, o_ref, reverse) in enumerate(dirs):
        cos = cos_ref[...]
        sin = sin_ref[...]
        rel = ((s_idx - t_idx) if reverse else (t_idx - s_idx)).astype(F32)
        for h in range(nh):
            j = d * nh + h
            lg = _log_sigmoid(logit_ref[j])
            lg_r = lg[0:1, :]
            lg_s = lg[0:1, 0:1]
            dmask = jnp.where(rel >= 0, jnp.exp(lg_r * jnp.maximum(rel, 0.0)), 0.0)
            if reverse:
                q_dec = jnp.exp(lg_s * (CHUNK - pos))
                k_dec = jnp.exp(lg_s * pos)
            else:
                q_dec = jnp.exp(lg_s * (pos + 1.0))
                k_dec = jnp.exp(lg_s * (CHUNK - 1.0 - pos))
            c_dec = jnp.exp(lg_s * CHUNK)
            q = _rope(q_ref[0, :, h * dk:(h + 1) * dk].astype(F32), cos, sin) * dk ** -0.5
            k = _rope(k_ref[0, :, h * dk:(h + 1) * dk].astype(F32), cos, sin)
            v = v_ref[0, :, h * dv:(h + 1) * dv]
            s = lax.dot_general(q.astype(BF16), k.astype(BF16), (((1,), (1,)), ((), ())),
                                preferred_element_type=F32) * dmask
            out = jnp.dot(s.astype(BF16), v, preferred_element_type=F32)
            out = out + jnp.dot((q * q_dec).astype(BF16), r_st[j].astype(BF16), preferred_element_type=F32)
            o_ref[0, :, h * dv:(h + 1) * dv] = out.astype(o_ref.dtype)
            upd = lax.dot_general((k * k_dec).astype(BF16), v, (((0,), (0,)), ((), ())),
                                  preferred_element_type=F32)
            r_st[j] = c_dec * r_st[j] + upd


def retention_scan(proj, logit_b, cos, sin, out_dtype):
    bsz, l, _ = proj.shape
    nc = l // CHUNK
    qw, vw = RET_HEADS * RET_HDK, RET_HEADS * RET_HDV
    half = RET_HDK // 2
    in_specs = []
    for rev in (False, True):
        ci = (lambda c: nc - 1 - c) if rev else (lambda c: c)
        in_specs += [pl.BlockSpec((1, CHUNK, qw), lambda b, c, ci=ci: (b, ci(c), 0)),
                     pl.BlockSpec((1, CHUNK, qw), lambda b, c, ci=ci: (b, ci(c), 1)),
                     pl.BlockSpec((1, CHUNK, vw), lambda b, c, ci=ci: (b, ci(c), 1)),
                     pl.BlockSpec((CHUNK, half), lambda b, c, ci=ci: (ci(c), 0)),
                     pl.BlockSpec((CHUNK, half), lambda b, c, ci=ci: (ci(c), 0))]
    in_specs.append(pl.BlockSpec((2 * RET_HEADS, 8, LANES), lambda b, c: (0, 0, 0)))
    shp = jax.ShapeDtypeStruct((bsz, l, vw), out_dtype)
    return pl.pallas_call(
        _retention_kernel,
        grid=(bsz, nc),
        in_specs=in_specs,
        out_specs=[pl.BlockSpec((1, CHUNK, vw), lambda b, c: (b, c, 0)),
                   pl.BlockSpec((1, CHUNK, vw), lambda b, c: (b, nc - 1 - c, 0))],
        out_shape=[shp, shp],
        scratch_shapes=[pltpu.VMEM((2 * RET_HEADS, RET_HDK, RET_HDV), F32)],
        compiler_params=_params("parallel", "arbitrary"),
        name="retention_scan",
    )(proj, proj, proj, cos, sin, proj, proj, proj, cos, sin, logit_b)


def _hyena_filter_kernel(bands_ref, w1t_ref, w1c_ref, w1s_ref, b1_ref, f1_ref, w2_ref, b2_ref, f2_ref,
                         w3_ref, delta_ref, kern_ref, asum_ref, *, l):
    i = pl.program_id(0)
    tr = kern_ref.shape[0]
    n = lax.broadcasted_iota(jnp.int32, (tr, 1), 0) + i * tr
    p = jnp.where(n < l, n, 2 * l - n).astype(F32)
    tt = p / (l - 1.0)
    ang = (2.0 * math.pi / l) * bands_ref[...] * p
    pre = tt * w1t_ref[...]
    pre = pre + jnp.dot(jnp.cos(ang), w1c_ref[...], preferred_element_type=F32, precision=HI)
    pre = pre + jnp.dot(-jnp.sin(ang), w1s_ref[...], preferred_element_type=F32, precision=HI)
    z = jnp.sin(f1_ref[...] * (pre + b1_ref[...]))
    z = jnp.sin(f2_ref[...] * (jnp.dot(z, w2_ref[...], preferred_element_type=F32, precision=HI) + b2_ref[...]))
    hk = jnp.dot(z, w3_ref[...], preferred_element_type=F32, precision=HI)
    hk = hk * (jnp.exp(-tt * jnp.abs(delta_ref[...])) + HYENA_SHIFT)
    hk = jnp.where(n == l, 0.0, hk)
    kern_ref[...] = hk

    @pl.when(i == 0)
    def _():
        asum_ref[...] = jnp.zeros_like(asum_ref)

    asum_ref[...] += jnp.broadcast_to(jnp.sum(jnp.abs(hk), axis=0, keepdims=True), asum_ref.shape)


def hyena_filter(l, w1, b1, f1, w2, b2, f2, w3, delta, tr=512):
    wd = HYENA_WIDTH
    hid = w1.shape[1]
    tr = min(tr, l)

    def pad(a, rows, cols):
        a = a.astype(F32)
        return jnp.pad(a, ((0, rows - a.shape[0]), (0, cols - a.shape[1])))

    bands = np.zeros((1, LANES), np.float32)
    bands[0, :HYENA_BANDS] = np.linspace(1e-4, HYENA_BANDS - 1, HYENA_BANDS, dtype=np.float32)
    w1t = pad(w1[0:1], 1, LANES)
    w1c = pad(w1[1:1 + HYENA_BANDS], LANES, LANES)
    w1s = pad(w1[1 + HYENA_BANDS:], LANES, LANES)
    vec = lambda a: pad(a.reshape(1, hid), 1, LANES)
    w2p = pad(w2, LANES, LANES)
    w3p = pad(w3, LANES, 2 * wd)
    full = lambda shape: pl.BlockSpec(shape, lambda i: (0, 0))
    half_sel = lambda i: (0, (i * tr) // l)
    return pl.pallas_call(
        functools.partial(_hyena_filter_kernel, l=l),
        grid=(2 * l // tr,),
        in_specs=[full((1, LANES)), full((1, LANES)), full((LANES, LANES)), full((LANES, LANES)),
                  full((1, LANES)), full((1, LANES)), full((LANES, LANES)), full((1, LANES)), full((1, LANES)),
                  pl.BlockSpec((LANES, wd), half_sel), pl.BlockSpec((1, wd), half_sel)],
        out_specs=[pl.BlockSpec((tr, wd), lambda i: (i, 0)), pl.BlockSpec((8, wd), lambda i: (0, 0))],
        out_shape=[jax.ShapeDtypeStruct((2 * l, wd), F32), jax.ShapeDtypeStruct((8, wd), F32)],
        compiler_params=_params("arbitrary"),
        name="hyena_filter",
    )(jnp.asarray(bands), w1t, w1c, w1s, vec(b1), vec(f1), w2p, vec(b2), vec(f2), w3p,
      delta.reshape(1, 2 * wd).astype(F32))


def _dft_consts(n1, n2):
    n = n1 * n2
    k1 = np.arange(n1)
    f1 = np.exp(-2j * np.pi * np.outer(k1, k1) / n1)
    k2 = np.arange(n2)
    f2 = np.exp(-2j * np.pi * np.outer(k2, k2) / n2)
    tw = np.exp(-2j * np.pi * np.outer(k1, k2) / n)
    g = f2[None, :, :] * tw[:, None, :]
    ginv = np.conj(np.transpose(g, (0, 2, 1))) / n
    f1inv = np.conj(f1.T)
    return f1, g, ginv, f1inv


def _stack(c):
    return jnp.asarray(np.concatenate([c.real, c.imag], axis=-2).astype(np.float32)).astype(BF16)


def _cdot(fs, xr, xi, rows):
    p = jnp.dot(fs, xr, preferred_element_type=F32)
    if xi is None:
        return p[:rows], p[rows:]
    q = jnp.dot(fs, xi, preferred_element_type=F32)
    return p[:rows] - q[rows:], q[:rows] + p[rows:]


def _fft_s1_kernel(*refs, n1, complex_in):
    if complex_in:
        fs_ref, xr_ref, xi_ref, ar_ref, ai_ref = refs
        xi = xi_ref[0].astype(BF16)
    else:
        fs_ref, xr_ref, ar_ref, ai_ref = refs
        xi = None
    ar, ai = _cdot(fs_ref[...], xr_ref[0].astype(BF16), xi, n1)
    ar_ref[0] = ar.astype(BF16)
    ai_ref[0] = ai.astype(BF16)


def fft_stage1(x, n1, rows_in, pairs, complex_in, tc=4096):
    c = x.shape[-1]
    cols = FFT_N2 * c
    xv = x.reshape(x.shape[0], rows_in, cols)
    f1 = _dft_consts(n1, FFT_N2)[0][:, :rows_in]
    fs = _stack(f1)
    in_specs = [pl.BlockSpec((2 * n1, rows_in), lambda p, j: (0, 0)),
                pl.BlockSpec((1, rows_in, tc), lambda p, j: (p, 0, j))]
    args = [fs, xv]
    if complex_in:
        in_specs.append(pl.BlockSpec((1, rows_in, tc), lambda p, j: (p + pairs, 0, j)))
        args.append(xv)
    out_spec = pl.BlockSpec((1, n1, tc), lambda p, j: (p, 0, j))
    shp = jax.ShapeDtypeStruct((pairs, n1, cols), BF16)
    return pl.pallas_call(
        functools.partial(_fft_s1_kernel, n1=n1, complex_in=complex_in),
        grid=(pairs, cols // tc),
        in_specs=in_specs,
        out_specs=[out_spec, out_spec],
        out_shape=[shp, shp],
        compiler_params=_params("parallel", "parallel"),
        name="fft_stage1",
    )(*args)


def _fft_mid_kernel(*refs, conv):
    n2 = FFT_N2
    if conv:
        gs_ref, gis_ref, ar_ref, ai_ref, kr_ref, ki_ref, br_ref, bi_ref = refs
    else:
        gs_ref, ar_ref, ai_ref, br_ref, bi_ref = refs
    xr, xi = _cdot(gs_ref[0], ar_ref[0, 0], ai_ref[0, 0], n2)
    if conv:
        kr, ki = kr_ref[0].astype(F32), ki_ref[0].astype(F32)
        yr = xr * kr - xi * ki
        yi = xr * ki + xi * kr
        xr, xi = _cdot(gis_ref[0], yr.astype(BF16), yi.astype(BF16), n2)
    br_ref[0, 0] = xr.astype(BF16)
    bi_ref[0, 0] = xi.astype(BF16)


def fft_mid(ar, ai, n1, c, kf=None, ct=1024):
    n2 = FFT_N2
    pairs = ar.shape[0]
    _, g, ginv, _ = _dft_consts(n1, n2)
    a4 = lambda a: a.reshape(pairs, n1, n2, c)
    mat = pl.BlockSpec((1, 2 * n2, n2), lambda k, cb, p: (k, 0, 0))
    dat = pl.BlockSpec((1, 1, n2, ct), lambda k, cb, p: (p, k, 0, cb))
    in_specs = [mat]
    args = [_stack(g)]
    if kf is not None:
        in_specs.append(mat)
        args.append(_stack(ginv))
    in_specs += [dat, dat]
    args += [a4(ar), a4(ai)]
    if kf is not None:
        fil = pl.BlockSpec((1, n2, ct), lambda k, cb, p: (k, 0, cb))
        in_specs += [fil, fil]
        args += [kf[0], kf[1]]
    shp = jax.ShapeDtypeStruct((pairs, n1, n2, c), BF16)
    br, bi = pl.pallas_call(
        functools.partial(_fft_mid_kernel, conv=kf is not None),
        grid=(n1, c // ct, pairs),
        in_specs=in_specs,
        out_specs=[dat, dat],
        out_shape=[shp, shp],
        compiler_params=_params("parallel", "parallel", "arbitrary"),
        name="fft_mid",
    )(*args)
    return br, bi


def _fft_s1inv_kernel(fs_ref, br_ref, bi_ref, y_ref, *, rows_out, complex_out):
    fs = fs_ref[...]
    p = jnp.dot(fs, br_ref[0], preferred_element_type=F32)
    q = jnp.dot(fs, bi_ref[0], preferred_element_type=F32)
    y_ref[0, 0] = p[:rows_out] - q[rows_out:]
    if complex_out:
        y_ref[1, 0] = q[:rows_out] + p[rows_out:]


def fft_stage1_inv(br, bi, n1, rows_out, c, complex_out, tc=4096):
    pairs = br.shape[0]
    cols = FFT_N2 * c
    f1inv = _dft_consts(n1, FFT_N2)[3][:rows_out, :]
    parts = 2 if complex_out else 1
    dat = pl.BlockSpec((1, n1, tc), lambda p, j: (p, 0, j))
    y = pl.pallas_call(
        functools.partial(_fft_s1inv_kernel, rows_out=rows_out, complex_out=complex_out),
        grid=(pairs, cols // tc),
        in_specs=[pl.BlockSpec((2 * rows_out, n1), lambda p, j: (0, 0)), dat, dat],
        out_specs=pl.BlockSpec((parts, 1, rows_out, tc), lambda p, j: (0, p, 0, j)),
        out_shape=jax.ShapeDtypeStruct((parts, pairs, rows_out, cols), F32),
        compiler_params=_params("parallel", "parallel"),
        name="fft_stage1_inv",
    )(_stack(f1inv), br.reshape(pairs, n1, cols), bi.reshape(pairs, n1, cols))
    return y.reshape(parts * pairs, rows_out * FFT_N2, c)


def hyena_long_conv(t, kern):
    bsz, l, c = t.shape
    n1 = 2 * l // FFT_N2
    kr, ki = fft_stage1(kern[None], n1, n1, 1, False)
    kr, ki = fft_mid(kr, ki, n1, c)
    kf = (kr.reshape(n1, FFT_N2, c), ki.reshape(n1, FFT_N2, c))
    complex_in = bsz % 2 == 0
    pairs = bsz // 2 if complex_in else bsz
    ar, ai = fft_stage1(t, n1, n1 // 2, pairs, complex_in)
    br, bi = fft_mid(ar, ai, n1, c, kf)
    return fft_stage1_inv(br, bi, n1, n1 // 2, c, complex_in)


def _head_ln(x, nheads, g):
    hd = x.shape[-1] // nheads
    outs = []
    for h in range(nheads):
        seg = x[:, h * hd:(h + 1) * hd]
        mu = jnp.mean(seg, axis=-1, keepdims=True)
        cen = seg - mu
        var = jnp.mean(cen * cen, axis=-1, keepdims=True)
        outs.append(cen * lax.rsqrt(var + GN_EPS))
    return jnp.concatenate(outs, axis=-1) * g


def _even_out_kernel(x_ref, hf_ref, hb_ref, o_ref, za_ref, x0_ref, y_ref, t_ref, zb_ref,
                     g_ref, skip_ref, asum_ref, w_ref, out_ref):
    wd = D_MODEL
    ln = _head_ln(hf_ref[...].astype(F32) + hb_ref[...].astype(F32), MLSTM_HEADS, g_ref[...])
    ya = ln * _sigmoid(o_ref[...].astype(F32)) * _silu(za_ref[...].astype(F32))
    t = t_ref[...]
    conv = y_ref[...] * (1.0 / asum_ref[0:1, :])
    yb = x0_ref[...].astype(F32) * (conv + skip_ref[...] * t) * _silu(zb_ref[...].astype(F32))
    mix = jnp.dot(ya.astype(BF16), w_ref[0:wd, :], preferred_element_type=F32)
    mix = mix + jnp.dot(yb.astype(BF16), w_ref[wd:2 * wd, :], preferred_element_type=F32)
    out_ref[...] = x_ref[...] + mix


def even_out(x, hf, hb, proj, x0, y, t, g, skip, asum, w_out, tm=256):
    m, d = x.shape
    row = lambda cb: pl.BlockSpec((tm, d), lambda i, cb=cb: (i, cb))
    vec = pl.BlockSpec((1, d), lambda i: (0, 0))
    return pl.pallas_call(
        _even_out_kernel,
        grid=(m // tm,),
        in_specs=[row(0), row(0), row(0), row(3), row(4), row(0), row(0), row(0), row(8),
                  vec, vec, pl.BlockSpec((8, d), lambda i: (0, 0)),
                  pl.BlockSpec((2 * d, d), lambda i: (0, 0))],
        out_specs=row(0),
        out_shape=jax.ShapeDtypeStruct((m, d), F32),
        compiler_params=_params("parallel"),
        name="even_out",
    )(x, hf, hb, proj, proj, x0, y, t, proj, g.reshape(1, d).astype(F32), skip.reshape(1, d).astype(F32),
      asum, w_out)


def _odd_out_kernel(x_ref, of_ref, ob_ref, gate_ref, g_ref, w_ref, out_ref):
    o = _head_ln(of_ref[...].astype(F32) + ob_ref[...].astype(F32), RET_HEADS, g_ref[...])
    y = _silu(gate_ref[...].astype(F32)) * o
    out_ref[...] = x_ref[...] + jnp.dot(y.astype(BF16), w_ref[...], preferred_element_type=F32)


def odd_out(x, of, ob, proj, g, w_out, tm=256):
    m, d = x.shape
    vw = RET_HEADS * RET_HDV
    wide = lambda cb: pl.BlockSpec((tm, vw), lambda i, cb=cb: (i, cb))
    return pl.pallas_call(
        _odd_out_kernel,
        grid=(m // tm,),
        in_specs=[pl.BlockSpec((tm, d), lambda i: (i, 0)), wide(0), wide(0), wide(2),
                  pl.BlockSpec((1, vw), lambda i: (0, 0)), pl.BlockSpec((vw, d), lambda i: (0, 0))],
        out_specs=pl.BlockSpec((tm, d), lambda i: (i, 0)),
        out_shape=jax.ShapeDtypeStruct((m, d), F32),
        compiler_params=_params("parallel"),
        name="odd_out",
    )(x, of, ob, proj, g.reshape(1, vw).astype(F32), w_out)


def _cross_attn_kernel(x_ref, kv_ref, gq_ref, wq_ref, wo_ref, gf_ref, out_ref, *, final_norm):
    d = D_MODEL
    x = x_ref[0]
    xn = x * lax.rsqrt(jnp.mean(x * x, axis=-1, keepdims=True) + EPS) * gq_ref[...]
    q = jnp.dot(xn.astype(BF16), wq_ref[...], preferred_element_type=F32).astype(BF16)
    outs = []
    for h in range(CA_HEADS):
        hs = slice(h * CA_HD, (h + 1) * CA_HD)
        k = kv_ref[0, :, hs]
        v = kv_ref[0, :, d + h * CA_HD:d + (h + 1) * CA_HD]
        s = lax.dot_general(q[:, hs], k, (((1,), (1,)), ((), ())), preferred_element_type=F32) * CA_HD ** -0.5
        p = jnp.exp(s - jnp.max(s, axis=-1, keepdims=True))
        o = jnp.dot(p.astype(BF16), v, preferred_element_type=F32)
        outs.append(o * (1.0 / jnp.sum(p, axis=-1, keepdims=True)))
    o = jnp.concatenate(outs, axis=-1).astype(BF16)
    y = x + jnp.dot(o, wo_ref[...], preferred_element_type=F32)
    if final_norm:
        y = y * lax.rsqrt(jnp.mean(y * y, axis=-1, keepdims=True) + EPS) * gf_ref[...]
    out_ref[0] = y


def cross_attn(x, kv, gq, wq, wo, gf, final_norm, tm=512):
    bsz, l, d = x.shape
    nm = kv.shape[1]
    tm = min(tm, l)
    vec = pl.BlockSpec((1, d), lambda b, i: (0, 0))
    mat = pl.BlockSpec((d, d), lambda b, i: (0, 0))
    return pl.pallas_call(
        functools.partial(_cross_attn_kernel, final_norm=final_norm),
        grid=(bsz, l // tm),
        in_specs=[pl.BlockSpec((1, tm, d), lambda b, i: (b, i, 0)),
                  pl.BlockSpec((1, nm, 2 * d), lambda b, i: (b, 0, 0)),
                  vec, mat, mat, vec],
        out_specs=pl.BlockSpec((1, tm, d), lambda b, i: (b, i, 0)),
        out_shape=jax.ShapeDtypeStruct((bsz, l, d), F32),
        compiler_params=_params("parallel", "parallel"),
        name="cross_attn",
    )(x, kv, gq.reshape(1, d).astype(F32), wq, wo, gf.reshape(1, d).astype(F32))


def _even_mixer(x, p, i):
    bsz, l, d = x.shape
    w = D_MODEL
    nh = MLSTM_HEADS
    m = bsz * l
    g_mix = p['norm_mix_g_layer']
    w_in = p['even_w_in'][i]
    gate0 = 5 * w
    w_main = jnp.concatenate([w_in[:, :gate0], w_in[:, gate0 + N_GATE_COLS:]], axis=1).astype(BF16)
    w_gate = jnp.pad(w_in[:, gate0:gate0 + N_GATE_COLS], ((0, 0), (0, LANES - N_GATE_COLS))).astype(BF16)
    xf = x.reshape(m, d)
    proj, gates = rms_matmul(xf, g_mix, [w_main, w_gate], [BF16, F32], tm=512)
    proj3 = proj.reshape(bsz, l, 9 * w)

    gates = gates.reshape(bsz, l, LANES)[..., :N_GATE_COLS].reshape(bsz, l, 2, 2, nh)
    bias = p['mlstm_gate_bias'][i].astype(F32)
    lane_pad = lambda a: jnp.pad(a, [(0, 0)] * (a.ndim - 1) + [(0, LANES - 2 * nh)])
    ig = lane_pad(gates[:, :, :, 0, :].reshape(bsz, l, 2 * nh))
    fg = lane_pad(gates[:, :, :, 1, :].reshape(bsz, l, 2 * nh))
    bias_i = lane_pad(bias[:, 0, :].reshape(1, 2 * nh))
    bias_f = lane_pad(bias[:, 1, :].reshape(1, 2 * nh))
    bcum, a, amax = gate_prep(ig, fg, bias_i, bias_f)
    k8 = 2 * nh
    cols = jnp.concatenate([bcum[..., :k8], a[..., :k8], amax[..., :k8],
                            jnp.zeros((bsz, l, k8), F32)], axis=-1)
    rows = jnp.swapaxes(a[..., :k8], 1, 2)

    qk = qk_conv(proj3, p['mlstm_conv_w'][i], p['mlstm_conv_b'][i])
    hf, hb = mlstm_scan(qk, proj3, 2 * w, cols, rows, F32)

    x0, t = hyena_pre(proj3, 5 * w, p['hyena_conv_w'][i], p['hyena_conv_b'][i])
    kern, asum = hyena_filter(l, p['hyena_w1'][i], p['hyena_b1'][i], p['hyena_freq1'][i], p['hyena_w2'][i],
                              p['hyena_b2'][i], p['hyena_freq2'][i], p['hyena_w3'][i], p['hyena_delta'][i])
    y = hyena_long_conv(t, kern)

    out = even_out(xf, hf.reshape(m, w), hb.reshape(m, w), proj, x0.reshape(m, w), y.reshape(m, w),
                   t.reshape(m, w), p['mlstm_norm_g'][i], p['hyena_skip'][i], asum,
                   p['even_w_out'][i].astype(BF16))
    return out.reshape(bsz, l, d)


def _odd_mixer(x, p, i, cos, sin):
    bsz, l, d = x.shape
    m = bsz * l
    xf = x.reshape(m, d)
    proj, = rms_matmul(xf, p['norm_mix_g_layer'], [p['odd_w_in'][i].astype(BF16)], [BF16], tm=512)
    logit = p['ret_decay_logit'][i].astype(F32).reshape(2 * RET_HEADS, 1, 1)
    logit_b = jnp.broadcast_to(logit, (2 * RET_HEADS, 8, LANES))
    of, ob = retention_scan(proj.reshape(bsz, l, -1), logit_b, cos, sin, F32)
    vw = RET_HEADS * RET_HDV
    out = odd_out(xf, of.reshape(m, vw), ob.reshape(m, vw), proj, p['ret_norm_g'][i],
                  p['odd_w_out'][i].astype(BF16))
    return out.reshape(bsz, l, d)


def _trunk(x, mem, p, cos, sin):
    depth = p['norm_mix_g'].shape[0]
    bsz, nm, d = mem.shape
    for layer in range(depth):
        i = layer // 2
        p['norm_mix_g_layer'] = p['norm_mix_g'][layer]
        x = _even_mixer(x, p, i) if layer % 2 == 0 else _odd_mixer(x, p, i, cos, sin)
        kv, = rms_matmul(mem.reshape(bsz * nm, d), p['norm_mem_g'][layer], [p['ca_wkv'][layer].astype(BF16)],
                         [BF16], tm=nm)
        kv = kv.reshape(bsz, nm, 2 * d)
        x = cross_attn(x, kv, p['norm_ca_g'][layer], p['ca_wq'][layer].astype(BF16),
                       p['ca_wo'][layer].astype(BF16), p['norm_final_g'], layer == depth - 1)
    return x


def kernel(x_prompt, x_sample, mem_prompt, mem_sample, norm_mix_g, norm_ca_g, norm_mem_g, norm_final_g, even_w_in, mlstm_conv_w, mlstm_conv_b, mlstm_gate_bias, mlstm_norm_g, hyena_conv_w, hyena_conv_b, hyena_w1, hyena_b1, hyena_freq1, hyena_w2, hyena_b2, hyena_freq2, hyena_w3, hyena_delta, hyena_skip, even_w_out, odd_w_in, ret_decay_logit, ret_norm_g, odd_w_out, ca_wq, ca_wkv, ca_wo):
    p = {'norm_mix_g': norm_mix_g, 'norm_ca_g': norm_ca_g, 'norm_mem_g': norm_mem_g, 'norm_final_g': norm_final_g,
         'even_w_in': even_w_in, 'mlstm_conv_w': mlstm_conv_w, 'mlstm_conv_b': mlstm_conv_b,
         'mlstm_gate_bias': mlstm_gate_bias, 'mlstm_norm_g': mlstm_norm_g,
         'hyena_conv_w': hyena_conv_w, 'hyena_conv_b': hyena_conv_b, 'hyena_w1': hyena_w1, 'hyena_b1': hyena_b1,
         'hyena_freq1': hyena_freq1, 'hyena_w2': hyena_w2, 'hyena_b2': hyena_b2, 'hyena_freq2': hyena_freq2,
         'hyena_w3': hyena_w3, 'hyena_delta': hyena_delta, 'hyena_skip': hyena_skip, 'even_w_out': even_w_out,
         'odd_w_in': odd_w_in, 'ret_decay_logit': ret_decay_logit, 'ret_norm_g': ret_norm_g, 'odd_w_out': odd_w_out,
         'ca_wq': ca_wq, 'ca_wkv': ca_wkv, 'ca_wo': ca_wo}
    l_max = max(x_prompt.shape[1], x_sample.shape[1])
    cos, sin = rope_tables(l_max)
    y_prompt = _trunk(x_prompt, mem_prompt, dict(p), cos, sin)
    y_sample = _trunk(x_sample, mem_sample, dict(p), cos, sin)
    return (y_prompt, y_sample)
```

```python
import functools
import math

import numpy as np
import jax
import jax.numpy as jnp
from jax import lax
from jax.experimental import pallas as pl
from jax.experimental.pallas import tpu as pltpu

F32 = jnp.float32
BF16 = jnp.bfloat16

D_MODEL = 1024
EPS = 1e-6
GN_EPS = 1e-5
CHUNK = 128

MLSTM_HEADS = 4
MLSTM_HD = D_MODEL // MLSTM_HEADS
N_GATE_COLS = 4 * MLSTM_HEADS

HYENA_WIDTH = D_MODEL
HYENA_EMB = 33
HYENA_BANDS = (HYENA_EMB - 1) // 2
HYENA_SHIFT = 0.05
FFT_N2 = 256

RET_HEADS = 4
RET_HDK = D_MODEL // RET_HEADS
RET_HDV = 2 * D_MODEL // RET_HEADS
ROPE_BASE = 10000.0

CA_HEADS = 4
CA_HD = D_MODEL // CA_HEADS

LANES = 128
BF16_SUBLANES = 16
VMEM_LIMIT = 56 * 1024 * 1024

HI = lax.Precision.HIGHEST


def _params(*sem):
    return pltpu.CompilerParams(dimension_semantics=sem, vmem_limit_bytes=VMEM_LIMIT)


def _silu(x):
    return x * (1.0 / (1.0 + jnp.exp(-x)))


def _sigmoid(x):
    return 1.0 / (1.0 + jnp.exp(-x))


def _log_sigmoid(x):
    return jnp.minimum(x, 0.0) - jnp.log(1.0 + jnp.exp(-jnp.abs(x)))


def _rms_matmul_kernel(x_ref, g_ref, *refs, tn):
    nw = len(refs) // 2
    x = x_ref[...]
    xn = (x * lax.rsqrt(jnp.mean(x * x, axis=-1, keepdims=True) + EPS) * g_ref[...]).astype(BF16)
    for w_ref, o_ref in zip(refs[:nw], refs[nw:]):
        n = o_ref.shape[1]
        step = min(tn, n)
        for j in range(n // step):
            cols = slice(j * step, (j + 1) * step)
            o_ref[:, cols] = jnp.dot(xn, w_ref[:, cols], preferred_element_type=F32).astype(o_ref.dtype)


def rms_matmul(x, g, ws, out_dtypes, tm, tn=1024):
    m, d = x.shape
    w_specs = [pl.BlockSpec(w.shape, lambda i: (0, 0), pipeline_mode=pl.Buffered(1)) for w in ws]
    return pl.pallas_call(
        functools.partial(_rms_matmul_kernel, tn=tn),
        grid=(m // tm,),
        in_specs=[pl.BlockSpec((tm, d), lambda i: (i, 0)), pl.BlockSpec((1, d), lambda i: (0, 0))] + w_specs,
        out_specs=[pl.BlockSpec((tm, w.shape[1]), lambda i: (i, 0)) for w in ws],
        out_shape=[jax.ShapeDtypeStruct((m, w.shape[1]), dt) for w, dt in zip(ws, out_dtypes)],
        compiler_params=_params("parallel"),
        name="rms_matmul",
    )(x, g.reshape(1, d).astype(F32), *ws)


def _conv3(x, prev_row, next_row, w, b):
    t = x.shape[0]
    row = lax.broadcasted_iota(jnp.int32, x.shape, 0)
    xm = jnp.where(row == 0, prev_row, pltpu.roll(x, 1, axis=0))
    xp = jnp.where(row == t - 1, next_row, pltpu.roll(x, t - 1, axis=0))
    return xm * w[0:1, :] + x * w[1:2, :] + xp * w[2:3, :] + b


def _halo_rows(prev_ref, next_ref, i, n_i):
    prev_row = prev_ref[0, BF16_SUBLANES - 1:BF16_SUBLANES, :].astype(F32)
    next_row = next_ref[0, 0:1, :].astype(F32)
    prev_row = jnp.where(i == 0, 0.0, prev_row)
    next_row = jnp.where(i == n_i - 1, 0.0, next_row)
    return prev_row, next_row


def _halo_specs(tl, tc, l, col_block):
    r = tl // BF16_SUBLANES
    last = l // BF16_SUBLANES - 1
    main = pl.BlockSpec((1, tl, tc), lambda b, i, c: (b, i, col_block(c)))
    prev = pl.BlockSpec((1, BF16_SUBLANES, tc),
                        lambda b, i, c: (b, jnp.maximum(i * r - 1, 0), col_block(c)))
    nxt = pl.BlockSpec((1, BF16_SUBLANES, tc),
                       lambda b, i, c: (b, jnp.minimum((i + 1) * r, last), col_block(c)))
    return main, prev, nxt


def _qk_conv_kernel(x_ref, p_ref, n_ref, w_ref, b_ref, o_ref):
    i = pl.program_id(1)
    prev_row, next_row = _halo_rows(p_ref, n_ref, i, pl.num_programs(1))
    y = _conv3(x_ref[0].astype(F32), prev_row, next_row, w_ref[...], b_ref[...])
    o_ref[0] = _silu(y).astype(o_ref.dtype)


def qk_conv(proj, w, b, tl=512, tc=512):
    bsz, l, _ = proj.shape
    c = w.shape[1]
    tl = min(tl, l)
    main, prev, nxt = _halo_specs(tl, tc, l, lambda cb: cb)
    return pl.pallas_call(
        _qk_conv_kernel,
        grid=(bsz, l // tl, c // tc),
        in_specs=[main, prev, nxt,
                  pl.BlockSpec((3, tc), lambda b_, i, cb: (0, cb)),
                  pl.BlockSpec((1, tc), lambda b_, i, cb: (0, cb))],
        out_specs=pl.BlockSpec((1, tl, tc), lambda b_, i, cb: (b_, i, cb)),
        out_shape=jax.ShapeDtypeStruct((bsz, l, c), BF16),
        compiler_params=_params("parallel", "parallel", "parallel"),
        name="qk_conv",
    )(proj, proj, proj, w.astype(F32), b.reshape(1, c).astype(F32))


def _hyena_pre_kernel(x0_ref, x0p_ref, x0n_ref, x1_ref, x1p_ref, x1n_ref, v_ref, vp_ref, vn_ref,
                      w_ref, b_ref, x0_out, t_out):
    i = pl.program_id(1)
    n_i = pl.num_programs(1)
    outs = []
    for k, (m_ref, p_ref, n_ref) in enumerate(((x0_ref, x0p_ref, x0n_ref), (x1_ref, x1p_ref, x1n_ref),
                                               (v_ref, vp_ref, vn_ref))):
        prev_row, next_row = _halo_rows(p_ref, n_ref, i, n_i)
        outs.append(_conv3(m_ref[0].astype(F32), prev_row, next_row, w_ref[k], b_ref[k]))
    x0_out[0] = outs[0].astype(x0_out.dtype)
    t_out[0] = outs[1] * outs[2]


def hyena_pre(proj, col0, w, b, tl=512, tc=512):
    bsz, l, _ = proj.shape
    wd = HYENA_WIDTH
    tl = min(tl, l)
    specs = []
    for k in range(3):
        base = (col0 + k * wd) // tc
        specs.extend(_halo_specs(tl, tc, l, lambda cb, base=base: base + cb))
    w3 = w.astype(F32).reshape(3, 3, wd).transpose(1, 0, 2)
    b3 = b.astype(F32).reshape(3, 1, wd)
    out_spec = pl.BlockSpec((1, tl, tc), lambda b_, i, cb: (b_, i, cb))
    return pl.pallas_call(
        _hyena_pre_kernel,
        grid=(bsz, l // tl, wd // tc),
        in_specs=specs + [pl.BlockSpec((3, 3, tc), lambda b_, i, cb: (0, 0, cb)),
                          pl.BlockSpec((3, 1, tc), lambda b_, i, cb: (0, 0, cb))],
        out_specs=[out_spec, out_spec],
        out_shape=[jax.ShapeDtypeStruct((bsz, l, wd), BF16), jax.ShapeDtypeStruct((bsz, l, wd), F32)],
        compiler_params=_params("parallel", "parallel", "parallel"),
        name="hyena_pre",
    )(*([proj] * 9), w3, b3)


def _seg_scan(x, op, identity, reverse):
    t = x.shape[0]
    r = lax.broadcasted_iota(jnp.int32, x.shape, 0) % CHUNK
    k = 1
    while k < CHUNK:
        if reverse:
            shifted = pltpu.roll(x, t - k, axis=0)
            valid = r < CHUNK - k
        else:
            shifted = pltpu.roll(x, k, axis=0)
            valid = r >= k
        x = op(x, jnp.where(valid, shifted, identity))
        k *= 2
    return x


GATE_A_LANE = 16
GATE_AMAX_LANE = 32


def _gate_lane(d, h):
    return d * 2 * MLSTM_HEADS + h


def _gate_prep_kernel(g_ref, bias_ref, out_ref):
    nh = MLSTM_HEADS
    g = g_ref[0] + bias_ref[...]
    lane = lax.broadcasted_iota(jnp.int32, g.shape, 1)
    live = (lane < 4 * nh) & (lane % (2 * nh) < nh)
    fwd = lane < 2 * nh
    lf = pltpu.roll(_log_sigmoid(g), LANES - nh, axis=1)
    bcum = jnp.where(fwd, _seg_scan(lf, jnp.add, 0.0, False), _seg_scan(lf, jnp.add, 0.0, True))
    a = g - bcum
    amax = jnp.where(fwd, _seg_scan(a, jnp.maximum, -jnp.inf, False),
                     _seg_scan(a, jnp.maximum, -jnp.inf, True))
    keep = lambda v: jnp.where(live, v, 0.0)
    out_ref[0] = (keep(bcum) + pltpu.roll(keep(a), GATE_A_LANE, axis=1)
                  + pltpu.roll(keep(amax), GATE_AMAX_LANE, axis=1))


def gate_prep(gates, bias, tl=1024):
    bsz, l, _ = gates.shape
    tl = min(tl, l)
    spec = pl.BlockSpec((1, tl, LANES), lambda b, i: (b, i, 0))
    return pl.pallas_call(
        _gate_prep_kernel,
        grid=(bsz, l // tl),
        in_specs=[spec, pl.BlockSpec((1, LANES), lambda b, i: (0, 0))],
        out_specs=spec,
        out_shape=jax.ShapeDtypeStruct((bsz, l, LANES), F32),
        compiler_params=_params("parallel", "parallel"),
        name="gate_prep",
    )(gates, bias)


def _mlstm_kernel(qf_ref, kf_ref, vf_ref, cf_ref, rf_ref, qb_ref, kb_ref, vb_ref, cb_ref, rb_ref,
                  hf_ref, hb_ref, c_st, cb_st, m_st):
    @pl.when(pl.program_id(1) == 0)
    def _():
        c_st[...] = jnp.zeros_like(c_st)
        cb_st[...] = jnp.zeros_like(cb_st)
        m_st[...] = jnp.zeros_like(m_st)

    nh, hd = MLSTM_HEADS, MLSTM_HD
    t_idx = lax.broadcasted_iota(jnp.int32, (CHUNK, CHUNK), 0)
    s_idx = lax.broadcasted_iota(jnp.int32, (CHUNK, CHUNK), 1)
    scale = hd ** -0.5
    ones_blk = jnp.ones((CHUNK, LANES), BF16)
    rep = lambda x, n: jnp.concatenate([x] * n, axis=1)
    lane_dense = lambda col: jnp.broadcast_to(col, (CHUNK, LANES))
    dirs = ((qf_ref, kf_ref, vf_ref, cf_ref, rf_ref, hf_ref, False),
            (qb_ref, kb_ref, vb_ref, cb_ref, rb_ref, hb_ref, True))
    for d, (q_ref, k_ref, v_ref, col_ref, row_ref, o_ref, reverse) in enumerate(dirs):
        mask = (s_idx >= t_idx) if reverse else (s_idx <= t_idx)
        last = 0 if reverse else CHUNK - 1
        for h in range(nh):
            j = d * nh + h
            hs = slice(h * hd, (h + 1) * hd)
            q = q_ref[0, :, hs]
            k = k_ref[0, :, hs] * scale
            v_aug = jnp.concatenate([v_ref[0, :, hs], ones_blk], axis=1)
            gl = _gate_lane(d, h)
            bc = lane_dense(col_ref[0, :, gl:gl + 1])
            a_c = lane_dense(col_ref[0, :, GATE_A_LANE + gl:GATE_A_LANE + gl + 1])
            amax = lane_dense(col_ref[0, :, GATE_AMAX_LANE + gl:GATE_AMAX_LANE + gl + 1])
            a_r = row_ref[0, gl:gl + 1, :]
            dmat = jnp.exp(jnp.where(mask, a_r - amax, -jnp.inf))
            s = lax.dot_general(q, k, (((1,), (1,)), ((), ())), preferred_element_type=F32) * dmat
            nd_l = jnp.dot(s.astype(BF16), v_aug, preferred_element_type=F32)
            a_last = amax[last:last + 1, :]
            btot = bc[last:last + 1, :]
            kw = rep(jnp.exp(a_c - a_last), hd // LANES) * k.astype(F32)
            upd = lax.dot_general(kw.astype(BF16), v_aug, (((0,), (0,)), ((), ())),
                                  preferred_element_type=F32)

            m_prev = m_st[j, 0:1, :]
            mt = jnp.maximum(amax, m_prev)
            f_l = jnp.exp(amax - mt)
            sc = jnp.exp(m_prev - mt)
            nd_c = jnp.dot(q, cb_st[j], preferred_element_type=F32)
            den = f_l * nd_l[:, hd:] + sc * nd_c[:, hd:]
            inv = 1.0 / jnp.maximum(jnp.abs(den), jnp.exp(-(bc + mt)))
            num = rep(f_l, hd // LANES) * nd_l[:, :hd] + rep(sc, hd // LANES) * nd_c[:, :hd]
            o_ref[0, :, hs] = (num * rep(inv, hd // LANES)).astype(o_ref.dtype)

            m_last = jnp.maximum(a_last, m_prev)
            dec = jnp.exp(m_prev - m_last)
            f_u = jnp.exp(a_last - m_last)
            wide = (hd + LANES) // LANES
            c_new = rep(dec, wide) * c_st[j] + rep(f_u, wide) * upd
            c_st[j] = c_new
            cb_st[j] = c_new.astype(BF16)
            m_st[j] = jnp.broadcast_to(btot + m_last, m_st.shape[1:])


def mlstm_scan(qk, proj, v_col, cols, rows, out_dtype):
    bsz, l, _ = qk.shape
    w = D_MODEL
    nc = l // CHUNK
    vb = v_col // w
    fwd = lambda cb: (lambda b, c: (b, c, cb))
    bwd = lambda cb: (lambda b, c: (b, nc - 1 - c, cb))
    blk = lambda im: pl.BlockSpec((1, CHUNK, w), im)
    ncol = cols.shape[-1]
    in_specs = []
    for mk in (fwd, bwd):
        in_specs += [blk(mk(0)), blk(mk(1)), blk(mk(vb)),
                     pl.BlockSpec((1, CHUNK, ncol), mk(0)),
                     pl.BlockSpec((1, rows.shape[1], CHUNK),
                                  (lambda b, c: (b, 0, c)) if mk is fwd else (lambda b, c: (b, 0, nc - 1 - c)))]
    nst = 2 * MLSTM_HEADS
    shp = jax.ShapeDtypeStruct((bsz, l, w), out_dtype)
    return pl.pallas_call(
        _mlstm_kernel,
        grid=(bsz, nc),
        in_specs=in_specs,
        out_specs=[blk(fwd(0)), blk(bwd(0))],
        out_shape=[shp, shp],
        scratch_shapes=[pltpu.VMEM((nst, MLSTM_HD, MLSTM_HD + LANES), F32),
                        pltpu.VMEM((nst, MLSTM_HD, MLSTM_HD + LANES), BF16),
                        pltpu.VMEM((nst, 8, LANES), F32)],
        compiler_params=_params("parallel", "arbitrary"),
        name="mlstm_scan",
    )(qk, qk, proj, cols, rows, qk, qk, proj, cols, rows)


def _rope_table_kernel(inv_ref, cos_ref, sin_ref):
    tl = cos_ref.shape[0]
    pos = (lax.broadcasted_iota(jnp.int32, cos_ref.shape, 0) + pl.program_id(0) * tl).astype(F32)
    ang = pos * inv_ref[...]
    cos_ref[...] = jnp.cos(ang)
    sin_ref[...] = jnp.sin(ang)


def rope_tables(l, tl=1024):
    half = RET_HDK // 2
    tl = min(tl, l)
    inv = (ROPE_BASE ** (-np.arange(0, RET_HDK, 2, dtype=np.float32) / RET_HDK)).astype(np.float32)
    spec = pl.BlockSpec((tl, half), lambda i: (i, 0))
    shp = jax.ShapeDtypeStruct((l, half), F32)
    return pl.pallas_call(
        _rope_table_kernel,
        grid=(l // tl,),
        in_specs=[pl.BlockSpec((1, half), lambda i: (0, 0))],
        out_specs=[spec, spec],
        out_shape=[shp, shp],
        compiler_params=_params("parallel"),
        name="rope_tables",
    )(jnp.asarray(inv).reshape(1, half))


def _rope(x, cos, sin):
    half = x.shape[-1] // 2
    x1, x2 = x[:, :half], x[:, half:]
    return jnp.concatenate([x1 * cos - x2 * sin, x1 * sin + x2 * cos], axis=-1)


def _retention_kernel(qf_ref, kf_ref, vf_ref, cosf_ref, sinf_ref, qb_ref, kb_ref, vb_ref, cosb_ref, sinb_ref,
                      logit_ref, of_ref, ob_ref, r_st, rb_st):
    @pl.when(pl.program_id(1) == 0)
    def _():
        r_st[...] = jnp.zeros_like(r_st)
        rb_st[...] = jnp.zeros_like(rb_st)

    nh, dk, dv = RET_HEADS, RET_HDK, RET_HDV
    t_idx = lax.broadcasted_iota(jnp.int32, (CHUNK, CHUNK), 0)
    s_idx = lax.broadcasted_iota(jnp.int32, (CHUNK, CHUNK), 1)
    pos = lax.broadcasted_iota(jnp.int32, (CHUNK, LANES), 0).astype(F32)
    rep = lambda x, n: jnp.concatenate([x] * n, axis=1)
    dirs = ((qf_ref, kf_ref, vf_ref, cosf_ref, sinf_ref, of_ref, False),
            (qb_ref, kb_ref, vb_ref, cosb_ref, sinb_ref, ob_ref, True))
    for d, (q_ref, k_ref, v_ref, cos_ref, sin_ref, o_ref, reverse) in enumerate(dirs):
        cos = cos_ref[...]
        sin = sin_ref[...]
        rel = ((s_idx - t_idx) if reverse else (t_idx - s_idx)).astype(F32)
        for h in range(nh):
            j = d * nh + h
            lg = _log_sigmoid(logit_ref[j])[0:1, :]
            dmask = jnp.where(rel >= 0, jnp.exp(lg * jnp.maximum(rel, 0.0)), 0.0)
            if reverse:
                q_dec = jnp.exp(lg * (CHUNK - pos))
                k_dec = jnp.exp(lg * pos)
            else:
                q_dec = jnp.exp(lg * (pos + 1.0))
                k_dec = jnp.exp(lg * (CHUNK - 1.0 - pos))
            c_dec = jnp.exp(lg * CHUNK)
            q = _rope(q_ref[0, :, h * dk:(h + 1) * dk].astype(F32), cos, sin) * dk ** -0.5
            k = _rope(k_ref[0, :, h * dk:(h + 1) * dk].astype(F32), cos, sin)
            v = v_ref[0, :, h * dv:(h + 1) * dv]
            s = lax.dot_general(q.astype(BF16), k.astype(BF16), (((1,), (1,)), ((), ())),
                                preferred_element_type=F32) * dmask
            out = jnp.dot(s.astype(BF16), v, preferred_element_type=F32)
            out = out + jnp.dot((q * rep(q_dec, dk // LANES)).astype(BF16), rb_st[j],
                                preferred_element_type=F32)
            o_ref[0, :, h * dv:(h + 1) * dv] = out.astype(o_ref.dtype)
            upd = lax.dot_general((k * rep(k_dec, dk // LANES)).astype(BF16), v, (((0,), (0,)), ((), ())),
                                  preferred_element_type=F32)
            r_new = rep(c_dec, dv // LANES) * r_st[j] + upd
            r_st[j] = r_new
            rb_st[j] = r_new.astype(BF16)


def retention_scan(proj, logit_b, cos, sin, out_dtype):
    bsz, l, _ = proj.shape
    nc = l // CHUNK
    qw, vw = RET_HEADS * RET_HDK, RET_HEADS * RET_HDV
    half = RET_HDK // 2
    in_specs = []
    for rev in (False, True):
        ci = (lambda c: nc - 1 - c) if rev else (lambda c: c)
        in_specs += [pl.BlockSpec((1, CHUNK, qw), lambda b, c, ci=ci: (b, ci(c), 0)),
                     pl.BlockSpec((1, CHUNK, qw), lambda b, c, ci=ci: (b, ci(c), 1)),
                     pl.BlockSpec((1, CHUNK, vw), lambda b, c, ci=ci: (b, ci(c), 1)),
                     pl.BlockSpec((CHUNK, half), lambda b, c, ci=ci: (ci(c), 0)),
                     pl.BlockSpec((CHUNK, half), lambda b, c, ci=ci: (ci(c), 0))]
    in_specs.append(pl.BlockSpec((2 * RET_HEADS, 8, LANES), lambda b, c: (0, 0, 0)))
    shp = jax.ShapeDtypeStruct((bsz, l, vw), out_dtype)
    return pl.pallas_call(
        _retention_kernel,
        grid=(bsz, nc),
        in_specs=in_specs,
        out_specs=[pl.BlockSpec((1, CHUNK, vw), lambda b, c: (b, c, 0)),
                   pl.BlockSpec((1, CHUNK, vw), lambda b, c: (b, nc - 1 - c, 0))],
        out_shape=[shp, shp],
        scratch_shapes=[pltpu.VMEM((2 * RET_HEADS, RET_HDK, RET_HDV), F32),
                        pltpu.VMEM((2 * RET_HEADS, RET_HDK, RET_HDV), BF16)],
        compiler_params=_params("parallel", "arbitrary"),
        name="retention_scan",
    )(proj, proj, proj, cos, sin, proj, proj, proj, cos, sin, logit_b)


def _hyena_filter_kernel(bands_ref, w1t_ref, w1c_ref, w1s_ref, b1_ref, f1_ref, w2_ref, b2_ref, f2_ref,
                         w3_ref, delta_ref, kern_ref, asum_ref, *, l):
    i = pl.program_id(0)
    tr = kern_ref.shape[0]
    n = lax.broadcasted_iota(jnp.int32, (tr, 1), 0) + i * tr
    p = jnp.where(n < l, n, 2 * l - n).astype(F32)
    tt = p / (l - 1.0)
    ang = (2.0 * math.pi / l) * bands_ref[...] * p
    pre = tt * w1t_ref[...]
    pre = pre + jnp.dot(jnp.cos(ang), w1c_ref[...], preferred_element_type=F32, precision=HI)
    pre = pre + jnp.dot(-jnp.sin(ang), w1s_ref[...], preferred_element_type=F32, precision=HI)
    z = jnp.sin(f1_ref[...] * (pre + b1_ref[...]))
    z = jnp.sin(f2_ref[...] * (jnp.dot(z, w2_ref[...], preferred_element_type=F32, precision=HI) + b2_ref[...]))
    hk = jnp.dot(z, w3_ref[...], preferred_element_type=F32, precision=HI)
    hk = hk * (jnp.exp(-tt * jnp.abs(delta_ref[...])) + HYENA_SHIFT)
    hk = jnp.where(n == l, 0.0, hk)
    kern_ref[...] = hk

    @pl.when(i == 0)
    def _():
        asum_ref[...] = jnp.zeros_like(asum_ref)

    asum_ref[...] += jnp.broadcast_to(jnp.sum(jnp.abs(hk), axis=0, keepdims=True), asum_ref.shape)


def hyena_filter(l, w1, b1, f1, w2, b2, f2, w3, delta, tr=512):
    wd = HYENA_WIDTH
    hid = w1.shape[1]
    tr = min(tr, l)

    def pad(a, rows, cols):
        a = a.astype(F32)
        return jnp.pad(a, ((0, rows - a.shape[0]), (0, cols - a.shape[1])))

    bands = np.zeros((1, LANES), np.float32)
    bands[0, :HYENA_BANDS] = np.linspace(1e-4, HYENA_BANDS - 1, HYENA_BANDS, dtype=np.float32)
    w1t = pad(w1[0:1], 1, LANES)
    w1c = pad(w1[1:1 + HYENA_BANDS], LANES, LANES)
    w1s = pad(w1[1 + HYENA_BANDS:], LANES, LANES)
    vec = lambda a: pad(a.reshape(1, hid), 1, LANES)
    w2p = pad(w2, LANES, LANES)
    w3p = pad(w3, LANES, 2 * wd)
    full = lambda shape: pl.BlockSpec(shape, lambda i: (0, 0))
    half_sel = lambda i: (0, (i * tr) // l)
    return pl.pallas_call(
        functools.partial(_hyena_filter_kernel, l=l),
        grid=(2 * l // tr,),
        in_specs=[full((1, LANES)), full((1, LANES)), full((LANES, LANES)), full((LANES, LANES)),
                  full((1, LANES)), full((1, LANES)), full((LANES, LANES)), full((1, LANES)), full((1, LANES)),
                  pl.BlockSpec((LANES, wd), half_sel), pl.BlockSpec((1, wd), half_sel)],
        out_specs=[pl.BlockSpec((tr, wd), lambda i: (i, 0)), pl.BlockSpec((8, wd), lambda i: (0, 0))],
        out_shape=[jax.ShapeDtypeStruct((2 * l, wd), F32), jax.ShapeDtypeStruct((8, wd), F32)],
        compiler_params=_params("arbitrary"),
        name="hyena_filter",
    )(jnp.asarray(bands), w1t, w1c, w1s, vec(b1), vec(f1), w2p, vec(b2), vec(f2), w3p,
      delta.reshape(1, 2 * wd).astype(F32))


def _dft_consts(n1, n2):
    n = n1 * n2
    k1 = np.arange(n1)
    f1 = np.exp(-2j * np.pi * np.outer(k1, k1) / n1)
    k2 = np.arange(n2)
    f2 = np.exp(-2j * np.pi * np.outer(k2, k2) / n2)
    tw = np.exp(-2j * np.pi * np.outer(k1, k2) / n)
    g = f2[None, :, :] * tw[:, None, :]
    ginv = np.conj(np.transpose(g, (0, 2, 1))) / n
    f1inv = np.conj(f1.T)
    return f1, g, ginv, f1inv


def _stack(c):
    return jnp.asarray(np.concatenate([c.real, c.imag], axis=-2).astype(np.float32)).astype(BF16)


def _cdot(fs, xr, xi, rows):
    p = jnp.dot(fs, xr, preferred_element_type=F32)
    if xi is None:
        return p[:rows], p[rows:]
    q = jnp.dot(fs, xi, preferred_element_type=F32)
    return p[:rows] - q[rows:], q[:rows] + p[rows:]


def _fft_s1_kernel(*refs, n1, complex_in):
    if complex_in:
        fs_ref, xr_ref, xi_ref, ar_ref, ai_ref = refs
    else:
        fs_ref, xr_ref, ar_ref, ai_ref = refs
    fs = fs_ref[...]
    for s in range(xr_ref.shape[2]):
        xi = xi_ref[0, :, s, :].astype(BF16) if complex_in else None
        ar, ai = _cdot(fs, xr_ref[0, :, s, :].astype(BF16), xi, n1)
        ar_ref[0, :, s, :] = ar
        ai_ref[0, :, s, :] = ai


def fft_stage1(x, n1, rows_in, pairs, complex_in, s_blk=8):
    c = x.shape[-1]
    n2 = FFT_N2
    xv = x.reshape(x.shape[0], rows_in, n2, c)
    f1 = _dft_consts(n1, n2)[0][:, :rows_in]
    fs = _stack(f1)
    in_specs = [pl.BlockSpec((2 * n1, rows_in), lambda p, j: (0, 0)),
                pl.BlockSpec((1, rows_in, s_blk, c), lambda p, j: (p, 0, j, 0))]
    args = [fs, xv]
    if complex_in:
        in_specs.append(pl.BlockSpec((1, rows_in, s_blk, c), lambda p, j: (p + pairs, 0, j, 0)))
        args.append(xv)
    out_spec = pl.BlockSpec((1, n1, s_blk, c), lambda p, j: (p, 0, j, 0))
    shp = jax.ShapeDtypeStruct((pairs, n1, n2, c), F32)
    return pl.pallas_call(
        functools.partial(_fft_s1_kernel, n1=n1, complex_in=complex_in),
        grid=(pairs, n2 // s_blk),
        in_specs=in_specs,
        out_specs=[out_spec, out_spec],
        out_shape=[shp, shp],
        compiler_params=_params("parallel", "parallel"),
        name="fft_stage1",
    )(*args)


def _fft_mid_kernel(*refs, conv):
    n2 = FFT_N2
    if conv:
        gs_ref, gis_ref, ar_ref, ai_ref, kr_ref, ki_ref, br_ref, bi_ref = refs
    else:
        gs_ref, ar_ref, ai_ref, br_ref, bi_ref = refs
    xr, xi = _cdot(gs_ref[0], ar_ref[0, 0].astype(BF16), ai_ref[0, 0].astype(BF16), n2)
    if conv:
        kr, ki = kr_ref[0].astype(F32), ki_ref[0].astype(F32)
        yr = xr * kr - xi * ki
        yi = xr * ki + xi * kr
        xr, xi = _cdot(gis_ref[0], yr.astype(BF16), yi.astype(BF16), n2)
    br_ref[0, 0] = xr.astype(br_ref.dtype)
    bi_ref[0, 0] = xi.astype(bi_ref.dtype)


def fft_mid(ar, ai, n1, c, kf=None, ct=1024):
    n2 = FFT_N2
    pairs = ar.shape[0]
    _, g, ginv, _ = _dft_consts(n1, n2)
    a4 = lambda a: a
    mat = pl.BlockSpec((1, 2 * n2, n2), lambda k, cb, p: (k, 0, 0))
    dat = pl.BlockSpec((1, 1, n2, ct), lambda k, cb, p: (p, k, 0, cb))
    in_specs = [mat]
    args = [_stack(g)]
    if kf is not None:
        in_specs.append(mat)
        args.append(_stack(ginv))
    in_specs += [dat, dat]
    args += [a4(ar), a4(ai)]
    if kf is not None:
        fil = pl.BlockSpec((1, n2, ct), lambda k, cb, p: (k, 0, cb))
        in_specs += [fil, fil]
        args += [kf[0], kf[1]]
    shp = jax.ShapeDtypeStruct((pairs, n1, n2, c), BF16 if kf is None else F32)
    br, bi = pl.pallas_call(
        functools.partial(_fft_mid_kernel, conv=kf is not None),
        grid=(n1, c // ct, pairs),
        in_specs=in_specs,
        out_specs=[dat, dat],
        out_shape=[shp, shp],
        compiler_params=_params("parallel", "parallel", "arbitrary"),
        name="fft_mid",
    )(*args)
    return br, bi


def _fft_s1inv_kernel(fs_ref, br_ref, bi_ref, y_ref, *, rows_out, complex_out):
    fs = fs_ref[...]
    for s in range(br_ref.shape[2]):
        p = jnp.dot(fs, br_ref[0, :, s, :].astype(BF16), preferred_element_type=F32)
        q = jnp.dot(fs, bi_ref[0, :, s, :].astype(BF16), preferred_element_type=F32)
        y_ref[0, 0, :, s, :] = p[:rows_out] - q[rows_out:]
        if complex_out:
            y_ref[1, 0, :, s, :] = q[:rows_out] + p[rows_out:]


def fft_stage1_inv(br, bi, n1, rows_out, c, complex_out, s_blk=8):
    pairs = br.shape[0]
    n2 = FFT_N2
    f1inv = _dft_consts(n1, n2)[3][:rows_out, :]
    parts = 2 if complex_out else 1
    dat = pl.BlockSpec((1, n1, s_blk, c), lambda p, j: (p, 0, j, 0))
    y = pl.pallas_call(
        functools.partial(_fft_s1inv_kernel, rows_out=rows_out, complex_out=complex_out),
        grid=(pairs, n2 // s_blk),
        in_specs=[pl.BlockSpec((2 * rows_out, n1), lambda p, j: (0, 0)), dat, dat],
        out_specs=pl.BlockSpec((parts, 1, rows_out, s_blk, c), lambda p, j: (0, p, 0, j, 0)),
        out_shape=jax.ShapeDtypeStruct((parts, pairs, rows_out, n2, c), F32),
        compiler_params=_params("parallel", "parallel"),
        name="fft_stage1_inv",
    )(_stack(f1inv), br, bi)
    return y.reshape(parts * pairs, rows_out * n2, c)


def hyena_long_conv(t, kern):
    bsz, l, c = t.shape
    n1 = 2 * l // FFT_N2
    kr, ki = fft_stage1(kern[None], n1, n1, 1, False)
    kr, ki = fft_mid(kr, ki, n1, c)
    kf = (kr.reshape(n1, FFT_N2, c), ki.reshape(n1, FFT_N2, c))
    complex_in = bsz % 2 == 0
    pairs = bsz // 2 if complex_in else bsz
    ar, ai = fft_stage1(t, n1, n1 // 2, pairs, complex_in)
    br, bi = fft_mid(ar, ai, n1, c, kf)
    return fft_stage1_inv(br, bi, n1, n1 // 2, c, complex_in)


def _head_ln(x, nheads, g):
    hd = x.shape[-1] // nheads
    outs = []
    for h in range(nheads):
        seg = x[:, h * hd:(h + 1) * hd]
        mu = jnp.mean(seg, axis=-1, keepdims=True)
        cen = seg - mu
        var = jnp.mean(cen * cen, axis=-1, keepdims=True)
        outs.append(cen * lax.rsqrt(var + GN_EPS))
    return jnp.concatenate(outs, axis=-1) * g


def _even_out_kernel(x_ref, hf_ref, hb_ref, o_ref, za_ref, x0_ref, y_ref, t_ref, zb_ref,
                     g_ref, skip_ref, asum_ref, w_ref, out_ref):
    wd = D_MODEL
    ln = _head_ln(hf_ref[...].astype(F32) + hb_ref[...].astype(F32), MLSTM_HEADS, g_ref[...])
    ya = ln * _sigmoid(o_ref[...].astype(F32)) * _silu(za_ref[...].astype(F32))
    t = t_ref[...]
    conv = y_ref[...] * (1.0 / asum_ref[0:1, :])
    yb = x0_ref[...].astype(F32) * (conv + skip_ref[...] * t) * _silu(zb_ref[...].astype(F32))
    mix = jnp.dot(ya.astype(BF16), w_ref[0:wd, :], preferred_element_type=F32)
    mix = mix + jnp.dot(yb.astype(BF16), w_ref[wd:2 * wd, :], preferred_element_type=F32)
    out_ref[...] = x_ref[...] + mix


def even_out(x, hf, hb, proj, x0, y, t, g, skip, asum, w_out, tm=256):
    m, d = x.shape
    row = lambda cb: pl.BlockSpec((tm, d), lambda i, cb=cb: (i, cb))
    vec = pl.BlockSpec((1, d), lambda i: (0, 0))
    return pl.pallas_call(
        _even_out_kernel,
        grid=(m // tm,),
        in_specs=[row(0), row(0), row(0), row(3), row(4), row(0), row(0), row(0), row(8),
                  vec, vec, pl.BlockSpec((8, d), lambda i: (0, 0)),
                  pl.BlockSpec((2 * d, d), lambda i: (0, 0))],
        out_specs=row(0),
        out_shape=jax.ShapeDtypeStruct((m, d), F32),
        compiler_params=_params("parallel"),
        name="even_out",
    )(x, hf, hb, proj, proj, x0, y, t, proj, g.reshape(1, d).astype(F32), skip.reshape(1, d).astype(F32),
      asum, w_out)


def _odd_out_kernel(x_ref, of_ref, ob_ref, gate_ref, g_ref, w_ref, out_ref):
    o = _head_ln(of_ref[...].astype(F32) + ob_ref[...].astype(F32), RET_HEADS, g_ref[...])
    y = _silu(gate_ref[...].astype(F32)) * o
    out_ref[...] = x_ref[...] + jnp.dot(y.astype(BF16), w_ref[...], preferred_element_type=F32)


def odd_out(x, of, ob, proj, g, w_out, tm=256):
    m, d = x.shape
    vw = RET_HEADS * RET_HDV
    wide = lambda cb: pl.BlockSpec((tm, vw), lambda i, cb=cb: (i, cb))
    return pl.pallas_call(
        _odd_out_kernel,
        grid=(m // tm,),
        in_specs=[pl.BlockSpec((tm, d), lambda i: (i, 0)), wide(0), wide(0), wide(2),
                  pl.BlockSpec((1, vw), lambda i: (0, 0)), pl.BlockSpec((vw, d), lambda i: (0, 0))],
        out_specs=pl.BlockSpec((tm, d), lambda i: (i, 0)),
        out_shape=jax.ShapeDtypeStruct((m, d), F32),
        compiler_params=_params("parallel"),
        name="odd_out",
    )(x, of, ob, proj, g.reshape(1, vw).astype(F32), w_out)


def _cross_attn_kernel(x_ref, kv_ref, gq_ref, wq_ref, wo_ref, gf_ref, out_ref, *, final_norm):
    d = D_MODEL
    x = x_ref[0]
    xn = x * lax.rsqrt(jnp.mean(x * x, axis=-1, keepdims=True) + EPS) * gq_ref[...]
    q = jnp.dot(xn.astype(BF16), wq_ref[...], preferred_element_type=F32).astype(BF16)
    outs = []
    for h in range(CA_HEADS):
        hs = slice(h * CA_HD, (h + 1) * CA_HD)
        k = kv_ref[0, :, hs]
        v = kv_ref[0, :, d + h * CA_HD:d + (h + 1) * CA_HD]
        s = lax.dot_general(q[:, hs], k, (((1,), (1,)), ((), ())), preferred_element_type=F32) * CA_HD ** -0.5
        p = jnp.exp(s - jnp.max(s, axis=-1, keepdims=True))
        o = jnp.dot(p.astype(BF16), v, preferred_element_type=F32)
        outs.append(o * (1.0 / jnp.sum(p, axis=-1, keepdims=True)))
    o = jnp.concatenate(outs, axis=-1).astype(BF16)
    y = x + jnp.dot(o, wo_ref[...], preferred_element_type=F32)
    if final_norm:
        y = y * lax.rsqrt(jnp.mean(y * y, axis=-1, keepdims=True) + EPS) * gf_ref[...]
    out_ref[0] = y


def cross_attn(x, kv, gq, wq, wo, gf, final_norm, tm=512):
    bsz, l, d = x.shape
    nm = kv.shape[1]
    tm = min(tm, l)
    vec = pl.BlockSpec((1, d), lambda b, i: (0, 0))
    mat = pl.BlockSpec((d, d), lambda b, i: (0, 0))
    return pl.pallas_call(
        functools.partial(_cross_attn_kernel, final_norm=final_norm),
        grid=(bsz, l // tm),
        in_specs=[pl.BlockSpec((1, tm, d), lambda b, i: (b, i, 0)),
                  pl.BlockSpec((1, nm, 2 * d), lambda b, i: (b, 0, 0)),
                  vec, mat, mat, vec],
        out_specs=pl.BlockSpec((1, tm, d), lambda b, i: (b, i, 0)),
        out_shape=jax.ShapeDtypeStruct((bsz, l, d), F32),
        compiler_params=_params("parallel", "parallel"),
        name="cross_attn",
    )(x, kv, gq.reshape(1, d).astype(F32), wq, wo, gf.reshape(1, d).astype(F32))


def _even_mixer(x, p, i):
    bsz, l, d = x.shape
    w = D_MODEL
    nh = MLSTM_HEADS
    m = bsz * l
    g_mix = p['norm_mix_g_layer']
    w_in = p['even_w_in'][i]
    gate0 = 5 * w
    w_main = jnp.concatenate([w_in[:, :gate0], w_in[:, gate0 + N_GATE_COLS:]], axis=1).astype(BF16)
    w_gate = jnp.pad(w_in[:, gate0:gate0 + N_GATE_COLS], ((0, 0), (0, LANES - N_GATE_COLS))).astype(BF16)
    xf = x.reshape(m, d)
    proj, gates = rms_matmul(xf, g_mix, [w_main, w_gate], [BF16, F32], tm=512)
    proj3 = proj.reshape(bsz, l, 9 * w)

    bias = jnp.pad(p['mlstm_gate_bias'][i].astype(F32).reshape(1, N_GATE_COLS),
                   ((0, 0), (0, LANES - N_GATE_COLS)))
    cols = gate_prep(gates.reshape(bsz, l, LANES), bias)
    rows = jnp.swapaxes(cols[..., GATE_A_LANE:GATE_A_LANE + N_GATE_COLS], 1, 2)

    qk = qk_conv(proj3, p['mlstm_conv_w'][i], p['mlstm_conv_b'][i])
    hf, hb = mlstm_scan(qk, proj3, 2 * w, cols, rows, BF16)

    x0, t = hyena_pre(proj3, 5 * w, p['hyena_conv_w'][i], p['hyena_conv_b'][i])
    kern, asum = hyena_filter(l, p['hyena_w1'][i], p['hyena_b1'][i], p['hyena_freq1'][i], p['hyena_w2'][i],
                              p['hyena_b2'][i], p['hyena_freq2'][i], p['hyena_w3'][i], p['hyena_delta'][i])
    y = hyena_long_conv(t, kern)

    out = even_out(xf, hf.reshape(m, w), hb.reshape(m, w), proj, x0.reshape(m, w), y.reshape(m, w),
                   t.reshape(m, w), p['mlstm_norm_g'][i], p['hyena_skip'][i], asum,
                   p['even_w_out'][i].astype(BF16))
    return out.reshape(bsz, l, d)


def _odd_mixer(x, p, i, cos, sin):
    bsz, l, d = x.shape
    m = bsz * l
    xf = x.reshape(m, d)
    proj, = rms_matmul(xf, p['norm_mix_g_layer'], [p['odd_w_in'][i].astype(BF16)], [BF16], tm=512)
    logit = p['ret_decay_logit'][i].astype(F32).reshape(2 * RET_HEADS, 1, 1)
    logit_b = jnp.broadcast_to(logit, (2 * RET_HEADS, 8, LANES))
    of, ob = retention_scan(proj.reshape(bsz, l, -1), logit_b, cos, sin, BF16)
    vw = RET_HEADS * RET_HDV
    out = odd_out(xf, of.reshape(m, vw), ob.reshape(m, vw), proj, p['ret_norm_g'][i],
                  p['odd_w_out'][i].astype(BF16))
    return out.reshape(bsz, l, d)


def _trunk(x, mem, p, cos, sin):
    depth = p['norm_mix_g'].shape[0]
    bsz, nm, d = mem.shape
    for layer in range(depth):
        i = layer // 2
        p['norm_mix_g_layer'] = p['norm_mix_g'][layer]
        x = _even_mixer(x, p, i) if layer % 2 == 0 else _odd_mixer(x, p, i, cos, sin)
        kv, = rms_matmul(mem.reshape(bsz * nm, d), p['norm_mem_g'][layer], [p['ca_wkv'][layer].astype(BF16)],
                         [BF16], tm=nm)
        kv = kv.reshape(bsz, nm, 2 * d)
        x = cross_attn(x, kv, p['norm_ca_g'][layer], p['ca_wq'][layer].astype(BF16),
                       p['ca_wo'][layer].astype(BF16), p['norm_final_g'], layer == depth - 1)
    return x


def kernel(x_prompt, x_sample, mem_prompt, mem_sample, norm_mix_g, norm_ca_g, norm_mem_g, norm_final_g, even_w_in, mlstm_conv_w, mlstm_conv_b, mlstm_gate_bias, mlstm_norm_g, hyena_conv_w, hyena_conv_b, hyena_w1, hyena_b1, hyena_freq1, hyena_w2, hyena_b2, hyena_freq2, hyena_w3, hyena_delta, hyena_skip, even_w_out, odd_w_in, ret_decay_logit, ret_norm_g, odd_w_out, ca_wq, ca_wkv, ca_wo):
    p = {'norm_mix_g': norm_mix_g, 'norm_ca_g': norm_ca_g, 'norm_mem_g': norm_mem_g, 'norm_final_g': norm_final_g,
         'even_w_in': even_w_in, 'mlstm_conv_w': mlstm_conv_w, 'mlstm_conv_b': mlstm_conv_b,
         'mlstm_gate_bias': mlstm_gate_bias, 'mlstm_norm_g': mlstm_norm_g,
         'hyena_conv_w': hyena_conv_w, 'hyena_conv_b': hyena_conv_b, 'hyena_w1': hyena_w1, 'hyena_b1': hyena_b1,
         'hyena_freq1': hyena_freq1, 'hyena_w2': hyena_w2, 'hyena_b2': hyena_b2, 'hyena_freq2': hyena_freq2,
         'hyena_w3': hyena_w3, 'hyena_delta': hyena_delta, 'hyena_skip': hyena_skip, 'even_w_out': even_w_out,
         'odd_w_in': odd_w_in, 'ret_decay_logit': ret_decay_logit, 'ret_norm_g': ret_norm_g, 'odd_w_out': odd_w_out,
         'ca_wq': ca_wq, 'ca_wkv': ca_wkv, 'ca_wo': ca_wo}
    l_max = max(x_prompt.shape[1], x_sample.shape[1])
    cos, sin = rope_tables(l_max)
    y_prompt = _trunk(x_prompt, mem_prompt, dict(p), cos, sin)
    y_sample = _trunk(x_sample, mem_sample, dict(p), cos, sin)
    return (y_prompt, y_sample)
```

```python
import functools
import math

import numpy as np
import jax
import jax.numpy as jnp
from jax import lax
from jax.experimental import pallas as pl
from jax.experimental.pallas import tpu as pltpu

F32 = jnp.float32
BF16 = jnp.bfloat16

D_MODEL = 1024
EPS = 1e-6
GN_EPS = 1e-5
CHUNK = 128

MLSTM_HEADS = 4
MLSTM_HD = D_MODEL // MLSTM_HEADS
N_GATE_COLS = 4 * MLSTM_HEADS

HYENA_WIDTH = D_MODEL
HYENA_EMB = 33
HYENA_BANDS = (HYENA_EMB - 1) // 2
HYENA_SHIFT = 0.05
FFT_N2 = 256

RET_HEADS = 4
RET_HDK = D_MODEL // RET_HEADS
RET_HDV = 2 * D_MODEL // RET_HEADS
ROPE_BASE = 10000.0

CA_HEADS = 4
CA_HD = D_MODEL // CA_HEADS

LANES = 128
BF16_SUBLANES = 16
VMEM_LIMIT = 56 * 1024 * 1024

HI = lax.Precision.HIGHEST


def _params(*sem):
    return pltpu.CompilerParams(dimension_semantics=sem, vmem_limit_bytes=VMEM_LIMIT)


def _silu(x):
    return x * (1.0 / (1.0 + jnp.exp(-x)))


def _sigmoid(x):
    return 1.0 / (1.0 + jnp.exp(-x))


def _log_sigmoid(x):
    return jnp.minimum(x, 0.0) - jnp.log(1.0 + jnp.exp(-jnp.abs(x)))


def _rms_matmul_kernel(x_ref, g_ref, *refs, tn):
    nw = len(refs) // 2
    x = x_ref[...]
    xn = (x * lax.rsqrt(jnp.mean(x * x, axis=-1, keepdims=True) + EPS) * g_ref[...]).astype(BF16)
    for w_ref, o_ref in zip(refs[:nw], refs[nw:]):
        n = o_ref.shape[1]
        step = min(tn, n)
        for j in range(n // step):
            cols = slice(j * step, (j + 1) * step)
            o_ref[:, cols] = jnp.dot(xn, w_ref[:, cols], preferred_element_type=F32).astype(o_ref.dtype)


def rms_matmul(x, g, ws, out_dtypes, tm, tn=1024):
    m, d = x.shape
    w_specs = [pl.BlockSpec(w.shape, lambda i: (0, 0), pipeline_mode=pl.Buffered(1)) for w in ws]
    return pl.pallas_call(
        functools.partial(_rms_matmul_kernel, tn=tn),
        grid=(m // tm,),
        in_specs=[pl.BlockSpec((tm, d), lambda i: (i, 0)), pl.BlockSpec((1, d), lambda i: (0, 0))] + w_specs,
        out_specs=[pl.BlockSpec((tm, w.shape[1]), lambda i: (i, 0)) for w in ws],
        out_shape=[jax.ShapeDtypeStruct((m, w.shape[1]), dt) for w, dt in zip(ws, out_dtypes)],
        compiler_params=_params("parallel"),
        name="rms_matmul",
    )(x, g.reshape(1, d).astype(F32), *ws)


def _conv3(x, prev_row, next_row, w, b):
    t = x.shape[0]
    row = lax.broadcasted_iota(jnp.int32, x.shape, 0)
    xm = jnp.where(row == 0, prev_row, pltpu.roll(x, 1, axis=0))
    xp = jnp.where(row == t - 1, next_row, pltpu.roll(x, t - 1, axis=0))
    return xm * w[0:1, :] + x * w[1:2, :] + xp * w[2:3, :] + b


def _halo_rows(prev_ref, next_ref, i, n_i):
    prev_row = prev_ref[0, BF16_SUBLANES - 1:BF16_SUBLANES, :].astype(F32)
    next_row = next_ref[0, 0:1, :].astype(F32)
    prev_row = jnp.where(i == 0, 0.0, prev_row)
    next_row = jnp.where(i == n_i - 1, 0.0, next_row)
    return prev_row, next_row


def _halo_specs(tl, tc, l, col_block):
    r = tl // BF16_SUBLANES
    last = l // BF16_SUBLANES - 1
    main = pl.BlockSpec((1, tl, tc), lambda b, i, c: (b, i, col_block(c)))
    prev = pl.BlockSpec((1, BF16_SUBLANES, tc),
                        lambda b, i, c: (b, jnp.maximum(i * r - 1, 0), col_block(c)))
    nxt = pl.BlockSpec((1, BF16_SUBLANES, tc),
                       lambda b, i, c: (b, jnp.minimum((i + 1) * r, last), col_block(c)))
    return main, prev, nxt


def _qk_conv_kernel(x_ref, p_ref, n_ref, w_ref, b_ref, o_ref):
    i = pl.program_id(1)
    prev_row, next_row = _halo_rows(p_ref, n_ref, i, pl.num_programs(1))
    y = _conv3(x_ref[0].astype(F32), prev_row, next_row, w_ref[...], b_ref[...])
    o_ref[0] = _silu(y).astype(o_ref.dtype)


def qk_conv(proj, w, b, tl=512, tc=512):
    bsz, l, _ = proj.shape
    c = w.shape[1]
    tl = min(tl, l)
    main, prev, nxt = _halo_specs(tl, tc, l, lambda cb: cb)
    return pl.pallas_call(
        _qk_conv_kernel,
        grid=(bsz, l // tl, c // tc),
        in_specs=[main, prev, nxt,
                  pl.BlockSpec((3, tc), lambda b_, i, cb: (0, cb)),
                  pl.BlockSpec((1, tc), lambda b_, i, cb: (0, cb))],
        out_specs=pl.BlockSpec((1, tl, tc), lambda b_, i, cb: (b_, i, cb)),
        out_shape=jax.ShapeDtypeStruct((bsz, l, c), BF16),
        compiler_params=_params("parallel", "parallel", "parallel"),
        name="qk_conv",
    )(proj, proj, proj, w.astype(F32), b.reshape(1, c).astype(F32))


def _hyena_pre_kernel(x0_ref, x0p_ref, x0n_ref, x1_ref, x1p_ref, x1n_ref, v_ref, vp_ref, vn_ref,
                      w_ref, b_ref, x0_out, t_out):
    i = pl.program_id(1)
    n_i = pl.num_programs(1)
    outs = []
    for k, (m_ref, p_ref, n_ref) in enumerate(((x0_ref, x0p_ref, x0n_ref), (x1_ref, x1p_ref, x1n_ref),
                                               (v_ref, vp_ref, vn_ref))):
        prev_row, next_row = _halo_rows(p_ref, n_ref, i, n_i)
        outs.append(_conv3(m_ref[0].astype(F32), prev_row, next_row, w_ref[k], b_ref[k]))
    x0_out[0] = outs[0].astype(x0_out.dtype)
    t_out[0] = outs[1] * outs[2]


def hyena_pre(proj, col0, w, b, tl=512, tc=512):
    bsz, l, _ = proj.shape
    wd = HYENA_WIDTH
    tl = min(tl, l)
    specs = []
    for k in range(3):
        base = (col0 + k * wd) // tc
        specs.extend(_halo_specs(tl, tc, l, lambda cb, base=base: base + cb))
    w3 = w.astype(F32).reshape(3, 3, wd).transpose(1, 0, 2)
    b3 = b.astype(F32).reshape(3, 1, wd)
    out_spec = pl.BlockSpec((1, tl, tc), lambda b_, i, cb: (b_, i, cb))
    return pl.pallas_call(
        _hyena_pre_kernel,
        grid=(bsz, l // tl, wd // tc),
        in_specs=specs + [pl.BlockSpec((3, 3, tc), lambda b_, i, cb: (0, 0, cb)),
                          pl.BlockSpec((3, 1, tc), lambda b_, i, cb: (0, 0, cb))],
        out_specs=[out_spec, out_spec],
        out_shape=[jax.ShapeDtypeStruct((bsz, l, wd), BF16), jax.ShapeDtypeStruct((bsz, l, wd), F32)],
        compiler_params=_params("parallel", "parallel", "parallel"),
        name="hyena_pre",
    )(*([proj] * 9), w3, b3)


def _seg_scan(x, op, identity, reverse):
    t = x.shape[0]
    r = lax.broadcasted_iota(jnp.int32, x.shape, 0) % CHUNK
    k = 1
    while k < CHUNK:
        if reverse:
            shifted = pltpu.roll(x, t - k, axis=0)
            valid = r < CHUNK - k
        else:
            shifted = pltpu.roll(x, k, axis=0)
            valid = r >= k
        x = op(x, jnp.where(valid, shifted, identity))
        k *= 2
    return x


GATE_A_LANE = 16
GATE_AMAX_LANE = 32


def _gate_lane(d, h):
    return d * 2 * MLSTM_HEADS + h


def _gate_prep_kernel(g_ref, bias_ref, out_ref):
    nh = MLSTM_HEADS
    g = g_ref[0] + bias_ref[...]
    lane = lax.broadcasted_iota(jnp.int32, g.shape, 1)
    live = (lane < 4 * nh) & (lane % (2 * nh) < nh)
    fwd = lane < 2 * nh
    lf = pltpu.roll(_log_sigmoid(g), LANES - nh, axis=1)
    bcum = jnp.where(fwd, _seg_scan(lf, jnp.add, 0.0, False), _seg_scan(lf, jnp.add, 0.0, True))
    a = g - bcum
    amax = jnp.where(fwd, _seg_scan(a, jnp.maximum, -jnp.inf, False),
                     _seg_scan(a, jnp.maximum, -jnp.inf, True))
    keep = lambda v: jnp.where(live, v, 0.0)
    out_ref[0] = (keep(bcum) + pltpu.roll(keep(a), GATE_A_LANE, axis=1)
                  + pltpu.roll(keep(amax), GATE_AMAX_LANE, axis=1))


def gate_prep(gates, bias, tl=1024):
    bsz, l, _ = gates.shape
    tl = min(tl, l)
    spec = pl.BlockSpec((1, tl, LANES), lambda b, i: (b, i, 0))
    return pl.pallas_call(
        _gate_prep_kernel,
        grid=(bsz, l // tl),
        in_specs=[spec, pl.BlockSpec((1, LANES), lambda b, i: (0, 0))],
        out_specs=spec,
        out_shape=jax.ShapeDtypeStruct((bsz, l, LANES), F32),
        compiler_params=_params("parallel", "parallel"),
        name="gate_prep",
    )(gates, bias)


def _mlstm_kernel(qf_ref, kf_ref, vf_ref, cf_ref, rf_ref, qb_ref, kb_ref, vb_ref, cb_ref, rb_ref,
                  hf_ref, hb_ref, c_st, cb_st, m_st):
    @pl.when(pl.program_id(1) == 0)
    def _():
        c_st[...] = jnp.zeros_like(c_st)
        cb_st[...] = jnp.zeros_like(cb_st)
        m_st[...] = jnp.zeros_like(m_st)

    nh, hd = MLSTM_HEADS, MLSTM_HD
    t_idx = lax.broadcasted_iota(jnp.int32, (CHUNK, CHUNK), 0)
    s_idx = lax.broadcasted_iota(jnp.int32, (CHUNK, CHUNK), 1)
    scale = hd ** -0.5
    ones_blk = jnp.ones((CHUNK, LANES), BF16)
    rep = lambda x, n: jnp.concatenate([x] * n, axis=1)
    lane_dense = lambda col: jnp.broadcast_to(col, (CHUNK, LANES))
    dirs = ((qf_ref, kf_ref, vf_ref, cf_ref, rf_ref, hf_ref, False),
            (qb_ref, kb_ref, vb_ref, cb_ref, rb_ref, hb_ref, True))
    for d, (q_ref, k_ref, v_ref, col_ref, row_ref, o_ref, reverse) in enumerate(dirs):
        mask = (s_idx >= t_idx) if reverse else (s_idx <= t_idx)
        last = 0 if reverse else CHUNK - 1
        for h in range(nh):
            j = d * nh + h
            hs = slice(h * hd, (h + 1) * hd)
            q = q_ref[0, :, hs]
            k = k_ref[0, :, hs] * scale
            v_aug = jnp.concatenate([v_ref[0, :, hs], ones_blk], axis=1)
            gl = _gate_lane(d, h)
            bc = lane_dense(col_ref[0, :, gl:gl + 1])
            a_c = lane_dense(col_ref[0, :, GATE_A_LANE + gl:GATE_A_LANE + gl + 1])
            amax = lane_dense(col_ref[0, :, GATE_AMAX_LANE + gl:GATE_AMAX_LANE + gl + 1])
            a_r = row_ref[0, gl:gl + 1, :]
            dmat = jnp.exp(jnp.where(mask, a_r - amax, -jnp.inf))
            s = lax.dot_general(q, k, (((1,), (1,)), ((), ())), preferred_element_type=F32) * dmat
            nd_l = jnp.dot(s.astype(BF16), v_aug, preferred_element_type=F32)
            a_last = amax[last:last + 1, :]
            btot = bc[last:last + 1, :]
            kw = rep(jnp.exp(a_c - a_last), hd // LANES) * k.astype(F32)
            upd = lax.dot_general(kw.astype(BF16), v_aug, (((0,), (0,)), ((), ())),
                                  preferred_element_type=F32)

            m_prev = m_st[j, 0:1, :]
            mt = jnp.maximum(amax, m_prev)
            f_l = jnp.exp(amax - mt)
            sc = jnp.exp(m_prev - mt)
            nd_c = jnp.dot(q, cb_st[j], preferred_element_type=F32)
            den = f_l * nd_l[:, hd:] + sc * nd_c[:, hd:]
            inv = 1.0 / jnp.maximum(jnp.abs(den), jnp.exp(-(bc + mt)))
            num = rep(f_l, hd // LANES) * nd_l[:, :hd] + rep(sc, hd // LANES) * nd_c[:, :hd]
            o_ref[0, :, hs] = (num * rep(inv, hd // LANES)).astype(o_ref.dtype)

            m_last = jnp.maximum(a_last, m_prev)
            dec = jnp.exp(m_prev - m_last)
            f_u = jnp.exp(a_last - m_last)
            wide = (hd + LANES) // LANES
            c_new = rep(dec, wide) * c_st[j] + rep(f_u, wide) * upd
            c_st[j] = c_new
            cb_st[j] = c_new.astype(BF16)
            m_st[j] = jnp.broadcast_to(btot + m_last, m_st.shape[1:])


def mlstm_scan(qk, proj, v_col, cols, rows, out_dtype):
    bsz, l, _ = qk.shape
    w = D_MODEL
    nc = l // CHUNK
    vb = v_col // w
    fwd = lambda cb: (lambda b, c: (b, c, cb))
    bwd = lambda cb: (lambda b, c: (b, nc - 1 - c, cb))
    blk = lambda im: pl.BlockSpec((1, CHUNK, w), im)
    ncol = cols.shape[-1]
    in_specs = []
    for mk in (fwd, bwd):
        in_specs += [blk(mk(0)), blk(mk(1)), blk(mk(vb)),
                     pl.BlockSpec((1, CHUNK, ncol), mk(0)),
                     pl.BlockSpec((1, rows.shape[1], CHUNK),
                                  (lambda b, c: (b, 0, c)) if mk is fwd else (lambda b, c: (b, 0, nc - 1 - c)))]
    nst = 2 * MLSTM_HEADS
    shp = jax.ShapeDtypeStruct((bsz, l, w), out_dtype)
    return pl.pallas_call(
        _mlstm_kernel,
        grid=(bsz, nc),
        in_specs=in_specs,
        out_specs=[blk(fwd(0)), blk(bwd(0))],
        out_shape=[shp, shp],
        scratch_shapes=[pltpu.VMEM((nst, MLSTM_HD, MLSTM_HD + LANES), F32),
                        pltpu.VMEM((nst, MLSTM_HD, MLSTM_HD + LANES), BF16),
                        pltpu.VMEM((nst, 8, LANES), F32)],
        compiler_params=_params("parallel", "arbitrary"),
        name="mlstm_scan",
    )(qk, qk, proj, cols, rows, qk, qk, proj, cols, rows)


def _rope_table_kernel(inv_ref, cos_ref, sin_ref):
    tl = cos_ref.shape[0]
    pos = (lax.broadcasted_iota(jnp.int32, cos_ref.shape, 0) + pl.program_id(0) * tl).astype(F32)
    ang = pos * inv_ref[...]
    cos_ref[...] = jnp.cos(ang)
    sin_ref[...] = jnp.sin(ang)


def rope_tables(l, tl=1024):
    half = RET_HDK // 2
    tl = min(tl, l)
    inv = (ROPE_BASE ** (-np.arange(0, RET_HDK, 2, dtype=np.float32) / RET_HDK)).astype(np.float32)
    spec = pl.BlockSpec((tl, half), lambda i: (i, 0))
    shp = jax.ShapeDtypeStruct((l, half), F32)
    return pl.pallas_call(
        _rope_table_kernel,
        grid=(l // tl,),
        in_specs=[pl.BlockSpec((1, half), lambda i: (0, 0))],
        out_specs=[spec, spec],
        out_shape=[shp, shp],
        compiler_params=_params("parallel"),
        name="rope_tables",
    )(jnp.asarray(inv).reshape(1, half))


def _rope(x, cos, sin):
    half = x.shape[-1] // 2
    x1, x2 = x[:, :half], x[:, half:]
    return jnp.concatenate([x1 * cos - x2 * sin, x1 * sin + x2 * cos], axis=-1)


def _retention_kernel(qf_ref, kf_ref, vf_ref, cosf_ref, sinf_ref, qb_ref, kb_ref, vb_ref, cosb_ref, sinb_ref,
                      logit_ref, of_ref, ob_ref, r_st, rb_st):
    @pl.when(pl.program_id(1) == 0)
    def _():
        r_st[...] = jnp.zeros_like(r_st)
        rb_st[...] = jnp.zeros_like(rb_st)

    nh, dk, dv = RET_HEADS, RET_HDK, RET_HDV
    t_idx = lax.broadcasted_iota(jnp.int32, (CHUNK, CHUNK), 0)
    s_idx = lax.broadcasted_iota(jnp.int32, (CHUNK, CHUNK), 1)
    pos = lax.broadcasted_iota(jnp.int32, (CHUNK, LANES), 0).astype(F32)
    rep = lambda x, n: jnp.concatenate([x] * n, axis=1)
    dirs = ((qf_ref, kf_ref, vf_ref, cosf_ref, sinf_ref, of_ref, False),
            (qb_ref, kb_ref, vb_ref, cosb_ref, sinb_ref, ob_ref, True))
    for d, (q_ref, k_ref, v_ref, cos_ref, sin_ref, o_ref, reverse) in enumerate(dirs):
        cos = cos_ref[...]
        sin = sin_ref[...]
        rel = ((s_idx - t_idx) if reverse else (t_idx - s_idx)).astype(F32)
        for h in range(nh):
            j = d * nh + h
            lg = _log_sigmoid(logit_ref[j])[0:1, :]
            dmask = jnp.where(rel >= 0, jnp.exp(lg * jnp.maximum(rel, 0.0)), 0.0)
            if reverse:
                q_dec = jnp.exp(lg * (CHUNK - pos))
                k_dec = jnp.exp(lg * pos)
            else:
                q_dec = jnp.exp(lg * (pos + 1.0))
                k_dec = jnp.exp(lg * (CHUNK - 1.0 - pos))
            c_dec = jnp.exp(lg * CHUNK)
            q = _rope(q_ref[0, :, h * dk:(h + 1) * dk].astype(F32), cos, sin) * dk ** -0.5
            k = _rope(k_ref[0, :, h * dk:(h + 1) * dk].astype(F32), cos, sin)
            v = v_ref[0, :, h * dv:(h + 1) * dv]
            s = lax.dot_general(q.astype(BF16), k.astype(BF16), (((1,), (1,)), ((), ())),
                                preferred_element_type=F32) * dmask
            out = jnp.dot(s.astype(BF16), v, preferred_element_type=F32)
            out = out + jnp.dot((q * rep(q_dec, dk // LANES)).astype(BF16), rb_st[j],
                                preferred_element_type=F32)
            o_ref[0, :, h * dv:(h + 1) * dv] = out.astype(o_ref.dtype)
            upd = lax.dot_general((k * rep(k_dec, dk // LANES)).astype(BF16), v, (((0,), (0,)), ((), ())),
                                  preferred_element_type=F32)
            r_new = rep(c_dec, dv // LANES) * r_st[j] + upd
            r_st[j] = r_new
            rb_st[j] = r_new.astype(BF16)


def retention_scan(proj, logit_b, cos, sin, out_dtype):
    bsz, l, _ = proj.shape
    nc = l // CHUNK
    qw, vw = RET_HEADS * RET_HDK, RET_HEADS * RET_HDV
    half = RET_HDK // 2
    in_specs = []
    for rev in (False, True):
        ci = (lambda c: nc - 1 - c) if rev else (lambda c: c)
        in_specs += [pl.BlockSpec((1, CHUNK, qw), lambda b, c, ci=ci: (b, ci(c), 0)),
                     pl.BlockSpec((1, CHUNK, qw), lambda b, c, ci=ci: (b, ci(c), 1)),
                     pl.BlockSpec((1, CHUNK, vw), lambda b, c, ci=ci: (b, ci(c), 1)),
                     pl.BlockSpec((CHUNK, half), lambda b, c, ci=ci: (ci(c), 0)),
                     pl.BlockSpec((CHUNK, half), lambda b, c, ci=ci: (ci(c), 0))]
    in_specs.append(pl.BlockSpec((2 * RET_HEADS, 8, LANES), lambda b, c: (0, 0, 0)))
    shp = jax.ShapeDtypeStruct((bsz, l, vw), out_dtype)
    return pl.pallas_call(
        _retention_kernel,
        grid=(bsz, nc),
        in_specs=in_specs,
        out_specs=[pl.BlockSpec((1, CHUNK, vw), lambda b, c: (b, c, 0)),
                   pl.BlockSpec((1, CHUNK, vw), lambda b, c: (b, nc - 1 - c, 0))],
        out_shape=[shp, shp],
        scratch_shapes=[pltpu.VMEM((2 * RET_HEADS, RET_HDK, RET_HDV), F32),
                        pltpu.VMEM((2 * RET_HEADS, RET_HDK, RET_HDV), BF16)],
        compiler_params=_params("parallel", "arbitrary"),
        name="retention_scan",
    )(proj, proj, proj, cos, sin, proj, proj, proj, cos, sin, logit_b)


def _hyena_filter_kernel(bands_ref, w1t_ref, w1c_ref, w1s_ref, b1_ref, f1_ref, w2_ref, b2_ref, f2_ref,
                         w3_ref, delta_ref, kern_ref, asum_ref, *, l):
    i = pl.program_id(0)
    tr = kern_ref.shape[0]
    n = lax.broadcasted_iota(jnp.int32, (tr, 1), 0) + i * tr
    p = jnp.where(n < l, n, 2 * l - n).astype(F32)
    tt = p / (l - 1.0)
    ang = (2.0 * math.pi / l) * bands_ref[...] * p
    pre = tt * w1t_ref[...]
    pre = pre + jnp.dot(jnp.cos(ang), w1c_ref[...], preferred_element_type=F32, precision=HI)
    pre = pre + jnp.dot(-jnp.sin(ang), w1s_ref[...], preferred_element_type=F32, precision=HI)
    z = jnp.sin(f1_ref[...] * (pre + b1_ref[...]))
    z = jnp.sin(f2_ref[...] * (jnp.dot(z, w2_ref[...], preferred_element_type=F32, precision=HI) + b2_ref[...]))
    hk = jnp.dot(z, w3_ref[...], preferred_element_type=F32, precision=HI)
    hk = hk * (jnp.exp(-tt * jnp.abs(delta_ref[...])) + HYENA_SHIFT)
    hk = jnp.where(n == l, 0.0, hk)
    kern_ref[...] = hk

    @pl.when(i == 0)
    def _():
        asum_ref[...] = jnp.zeros_like(asum_ref)

    asum_ref[...] += jnp.broadcast_to(jnp.sum(jnp.abs(hk), axis=0, keepdims=True), asum_ref.shape)


def hyena_filter(l, w1, b1, f1, w2, b2, f2, w3, delta, tr=512):
    wd = HYENA_WIDTH
    hid = w1.shape[1]
    tr = min(tr, l)

    def pad(a, rows, cols):
        a = a.astype(F32)
        return jnp.pad(a, ((0, rows - a.shape[0]), (0, cols - a.shape[1])))

    bands = np.zeros((1, LANES), np.float32)
    bands[0, :HYENA_BANDS] = np.linspace(1e-4, HYENA_BANDS - 1, HYENA_BANDS, dtype=np.float32)
    w1t = pad(w1[0:1], 1, LANES)
    w1c = pad(w1[1:1 + HYENA_BANDS], LANES, LANES)
    w1s = pad(w1[1 + HYENA_BANDS:], LANES, LANES)
    vec = lambda a: pad(a.reshape(1, hid), 1, LANES)
    w2p = pad(w2, LANES, LANES)
    w3p = pad(w3, LANES, 2 * wd)
    full = lambda shape: pl.BlockSpec(shape, lambda i: (0, 0))
    half_sel = lambda i: (0, (i * tr) // l)
    return pl.pallas_call(
        functools.partial(_hyena_filter_kernel, l=l),
        grid=(2 * l // tr,),
        in_specs=[full((1, LANES)), full((1, LANES)), full((LANES, LANES)), full((LANES, LANES)),
                  full((1, LANES)), full((1, LANES)), full((LANES, LANES)), full((1, LANES)), full((1, LANES)),
                  pl.BlockSpec((LANES, wd), half_sel), pl.BlockSpec((1, wd), half_sel)],
        out_specs=[pl.BlockSpec((tr, wd), lambda i: (i, 0)), pl.BlockSpec((8, wd), lambda i: (0, 0))],
        out_shape=[jax.ShapeDtypeStruct((2 * l, wd), F32), jax.ShapeDtypeStruct((8, wd), F32)],
        compiler_params=_params("arbitrary"),
        name="hyena_filter",
    )(jnp.asarray(bands), w1t, w1c, w1s, vec(b1), vec(f1), w2p, vec(b2), vec(f2), w3p,
      delta.reshape(1, 2 * wd).astype(F32))


def _dft_consts(n1, n2):
    n = n1 * n2
    k1 = np.arange(n1)
    f1 = np.exp(-2j * np.pi * np.outer(k1, k1) / n1)
    k2 = np.arange(n2)
    f2 = np.exp(-2j * np.pi * np.outer(k2, k2) / n2)
    tw = np.exp(-2j * np.pi * np.outer(k1, k2) / n)
    g = f2[None, :, :] * tw[:, None, :]
    ginv = np.conj(np.transpose(g, (0, 2, 1))) / n
    f1inv = np.conj(f1.T)
    return f1, g, ginv, f1inv


def _stack(c):
    return jnp.asarray(np.concatenate([c.real, c.imag], axis=-2).astype(np.float32)).astype(BF16)


def _cdot(fs, xr, xi, rows):
    p = jnp.dot(fs, xr, preferred_element_type=F32)
    if xi is None:
        return p[:rows], p[rows:]
    q = jnp.dot(fs, xi, preferred_element_type=F32)
    return p[:rows] - q[rows:], q[:rows] + p[rows:]


FFT_ROW_CHUNK = 512


def _kron_eye(f, s_blk):
    k = np.kron(f, np.eye(s_blk))
    to = lambda a: jnp.asarray(a.astype(np.float32)).astype(BF16)
    return to(k.real), to(k.imag)


def _strided_stage_kernel(*refs, has_imag_in, n_out):
    mr_ref, mi_ref, xr_ref = refs[:3]
    xi_ref = refs[3] if has_imag_in else None
    n_in = 4 if has_imag_in else 3
    if len(refs[n_in].shape) == 5:
        outs = [refs[n_in].at[part, 0] for part in range(n_out)]
    else:
        outs = [r.at[0] for r in refs[n_in:]]
    flat = lambda r: r.reshape(r.shape[0] * r.shape[1], r.shape[2])
    xr = flat(xr_ref[0]).astype(BF16)
    xi = flat(xi_ref[0]).astype(BF16) if has_imag_in else None
    rows = mr_ref.shape[0]
    rc = min(FFT_ROW_CHUNK, rows)
    s_blk = xr_ref.shape[2]
    for c0 in range(0, rows, rc):
        mr = mr_ref[c0:c0 + rc, :]
        mi = mi_ref[c0:c0 + rc, :]
        re = jnp.dot(mr, xr, preferred_element_type=F32)
        im = jnp.dot(mi, xr, preferred_element_type=F32) if n_out == 2 else None
        if has_imag_in:
            re = re - jnp.dot(mi, xi, preferred_element_type=F32)
            if n_out == 2:
                im = im + jnp.dot(mr, xi, preferred_element_type=F32)
        k0, k1 = c0 // s_blk, (c0 + rc) // s_blk
        for o_ref, val in zip(outs, (re, im)):
            o_ref[k0:k1] = val.reshape(rc // s_blk, s_blk, val.shape[1]).astype(o_ref.dtype)


def fft_stage1(x, n1, rows_in, pairs, complex_in, s_blk, out_dtype):
    c = x.shape[-1]
    n2 = FFT_N2
    xv = x.reshape(x.shape[0], rows_in, n2, c)
    mr, mi = _kron_eye(_dft_consts(n1, n2)[0][:, :rows_in], s_blk)
    mat = pl.BlockSpec(mr.shape, lambda p, j: (0, 0), pipeline_mode=pl.Buffered(1))
    in_specs = [mat, mat, pl.BlockSpec((1, rows_in, s_blk, c), lambda p, j: (p, 0, j, 0))]
    args = [mr, mi, xv]
    if complex_in:
        in_specs.append(pl.BlockSpec((1, rows_in, s_blk, c), lambda p, j: (p + pairs, 0, j, 0)))
        args.append(xv)
    out_spec = pl.BlockSpec((1, n1, s_blk, c), lambda p, j: (p, 0, j, 0))
    shp = jax.ShapeDtypeStruct((pairs, n1, n2, c), out_dtype)
    return pl.pallas_call(
        functools.partial(_strided_stage_kernel, has_imag_in=complex_in, n_out=2),
        grid=(pairs, n2 // s_blk),
        in_specs=in_specs,
        out_specs=[out_spec, out_spec],
        out_shape=[shp, shp],
        compiler_params=_params("parallel", "parallel"),
        name="fft_stage1",
    )(*args)


def _fft_mid_kernel(*refs, conv):
    n2 = FFT_N2
    if conv:
        gs_ref, gis_ref, ar_ref, ai_ref, kr_ref, ki_ref, br_ref, bi_ref = refs
    else:
        gs_ref, ar_ref, ai_ref, br_ref, bi_ref = refs
    xr, xi = _cdot(gs_ref[0], ar_ref[0, 0].astype(BF16), ai_ref[0, 0].astype(BF16), n2)
    if conv:
        kr, ki = kr_ref[0].astype(F32), ki_ref[0].astype(F32)
        yr = xr * kr - xi * ki
        yi = xr * ki + xi * kr
        xr, xi = _cdot(gis_ref[0], yr.astype(BF16), yi.astype(BF16), n2)
    br_ref[0, 0] = xr.astype(br_ref.dtype)
    bi_ref[0, 0] = xi.astype(bi_ref.dtype)


def fft_mid(ar, ai, n1, c, kf=None, ct=1024):
    n2 = FFT_N2
    pairs = ar.shape[0]
    _, g, ginv, _ = _dft_consts(n1, n2)
    a4 = lambda a: a
    mat = pl.BlockSpec((1, 2 * n2, n2), lambda k, cb, p: (k, 0, 0))
    dat = pl.BlockSpec((1, 1, n2, ct), lambda k, cb, p: (p, k, 0, cb))
    in_specs = [mat]
    args = [_stack(g)]
    if kf is not None:
        in_specs.append(mat)
        args.append(_stack(ginv))
    in_specs += [dat, dat]
    args += [a4(ar), a4(ai)]
    if kf is not None:
        fil = pl.BlockSpec((1, n2, ct), lambda k, cb, p: (k, 0, cb))
        in_specs += [fil, fil]
        args += [kf[0], kf[1]]
    shp = jax.ShapeDtypeStruct((pairs, n1, n2, c), BF16)
    br, bi = pl.pallas_call(
        functools.partial(_fft_mid_kernel, conv=kf is not None),
        grid=(n1, c // ct, pairs),
        in_specs=in_specs,
        out_specs=[dat, dat],
        out_shape=[shp, shp],
        compiler_params=_params("parallel", "parallel", "arbitrary"),
        name="fft_mid",
    )(*args)
    return br, bi


def fft_stage1_inv(br, bi, n1, rows_out, c, complex_out, s_blk):
    pairs = br.shape[0]
    n2 = FFT_N2
    mr, mi = _kron_eye(_dft_consts(n1, n2)[3][:rows_out, :], s_blk)
    parts = 2 if complex_out else 1
    mat = pl.BlockSpec(mr.shape, lambda p, j: (0, 0), pipeline_mode=pl.Buffered(1))
    dat = pl.BlockSpec((1, n1, s_blk, c), lambda p, j: (p, 0, j, 0))
    y = pl.pallas_call(
        functools.partial(_strided_stage_kernel, has_imag_in=True, n_out=parts),
        grid=(pairs, n2 // s_blk),
        in_specs=[mat, mat, dat, dat],
        out_specs=pl.BlockSpec((parts, 1, rows_out, s_blk, c), lambda p, j: (0, p, 0, j, 0)),
        out_shape=jax.ShapeDtypeStruct((parts, pairs, rows_out, n2, c), F32),
        compiler_params=_params("parallel", "parallel"),
        name="fft_stage1_inv",
    )(mr, mi, br, bi)
    return y.reshape(parts * pairs, rows_out * n2, c)


def hyena_long_conv(t, kern):
    bsz, l, c = t.shape
    n1 = 2 * l // FFT_N2
    kr, ki = fft_stage1(kern[None], n1, n1, 1, False, 8, F32)
    kr, ki = fft_mid(kr, ki, n1, c)
    kf = (kr.reshape(n1, FFT_N2, c), ki.reshape(n1, FFT_N2, c))
    complex_in = bsz % 2 == 0
    pairs = bsz // 2 if complex_in else bsz
    ar, ai = fft_stage1(t, n1, n1 // 2, pairs, complex_in, BF16_SUBLANES, BF16)
    br, bi = fft_mid(ar, ai, n1, c, kf)
    return fft_stage1_inv(br, bi, n1, n1 // 2, c, complex_in, BF16_SUBLANES)


def _head_ln(x, nheads, g):
    hd = x.shape[-1] // nheads
    outs = []
    for h in range(nheads):
        seg = x[:, h * hd:(h + 1) * hd]
        mu = jnp.mean(seg, axis=-1, keepdims=True)
        cen = seg - mu
        var = jnp.mean(cen * cen, axis=-1, keepdims=True)
        outs.append(cen * lax.rsqrt(var + GN_EPS))
    return jnp.concatenate(outs, axis=-1) * g


def _even_out_kernel(x_ref, hf_ref, hb_ref, o_ref, za_ref, x0_ref, y_ref, t_ref, zb_ref,
                     g_ref, skip_ref, asum_ref, w_ref, out_ref):
    wd = D_MODEL
    ln = _head_ln(hf_ref[...].astype(F32) + hb_ref[...].astype(F32), MLSTM_HEADS, g_ref[...])
    ya = ln * _sigmoid(o_ref[...].astype(F32)) * _silu(za_ref[...].astype(F32))
    t = t_ref[...]
    conv = y_ref[...] * (1.0 / asum_ref[0:1, :])
    yb = x0_ref[...].astype(F32) * (conv + skip_ref[...] * t) * _silu(zb_ref[...].astype(F32))
    mix = jnp.dot(ya.astype(BF16), w_ref[0:wd, :], preferred_element_type=F32)
    mix = mix + jnp.dot(yb.astype(BF16), w_ref[wd:2 * wd, :], preferred_element_type=F32)
    out_ref[...] = x_ref[...] + mix


def even_out(x, hf, hb, proj, x0, y, t, g, skip, asum, w_out, tm=512):
    m, d = x.shape
    row = lambda cb: pl.BlockSpec((tm, d), lambda i, cb=cb: (i, cb))
    vec = pl.BlockSpec((1, d), lambda i: (0, 0))
    return pl.pallas_call(
        _even_out_kernel,
        grid=(m // tm,),
        in_specs=[row(0), row(0), row(0), row(3), row(4), row(0), row(0), row(0), row(8),
                  vec, vec, pl.BlockSpec((8, d), lambda i: (0, 0)),
                  pl.BlockSpec((2 * d, d), lambda i: (0, 0), pipeline_mode=pl.Buffered(1))],
        out_specs=row(0),
        out_shape=jax.ShapeDtypeStruct((m, d), F32),
        compiler_params=_params("parallel"),
        name="even_out",
    )(x, hf, hb, proj, proj, x0, y, t, proj, g.reshape(1, d).astype(F32), skip.reshape(1, d).astype(F32),
      asum, w_out)


def _odd_out_kernel(x_ref, of_ref, ob_ref, gate_ref, g_ref, w_ref, out_ref):
    o = _head_ln(of_ref[...].astype(F32) + ob_ref[...].astype(F32), RET_HEADS, g_ref[...])
    y = _silu(gate_ref[...].astype(F32)) * o
    out_ref[...] = x_ref[...] + jnp.dot(y.astype(BF16), w_ref[...], preferred_element_type=F32)


def odd_out(x, of, ob, proj, g, w_out, tm=512):
    m, d = x.shape
    vw = RET_HEADS * RET_HDV
    wide = lambda cb: pl.BlockSpec((tm, vw), lambda i, cb=cb: (i, cb))
    return pl.pallas_call(
        _odd_out_kernel,
        grid=(m // tm,),
        in_specs=[pl.BlockSpec((tm, d), lambda i: (i, 0)), wide(0), wide(0), wide(2),
                  pl.BlockSpec((1, vw), lambda i: (0, 0)),
                  pl.BlockSpec((vw, d), lambda i: (0, 0), pipeline_mode=pl.Buffered(1))],
        out_specs=pl.BlockSpec((tm, d), lambda i: (i, 0)),
        out_shape=jax.ShapeDtypeStruct((m, d), F32),
        compiler_params=_params("parallel"),
        name="odd_out",
    )(x, of, ob, proj, g.reshape(1, vw).astype(F32), w_out)


def _cross_attn_kernel(x_ref, kv_ref, gq_ref, wq_ref, wo_ref, gf_ref, out_ref, *, final_norm):
    d = D_MODEL
    x = x_ref[0]
    xn = x * lax.rsqrt(jnp.mean(x * x, axis=-1, keepdims=True) + EPS) * gq_ref[...]
    q = jnp.dot(xn.astype(BF16), wq_ref[...], preferred_element_type=F32).astype(BF16)
    outs = []
    for h in range(CA_HEADS):
        hs = slice(h * CA_HD, (h + 1) * CA_HD)
        k = kv_ref[0, :, hs]
        v = kv_ref[0, :, d + h * CA_HD:d + (h + 1) * CA_HD]
        s = lax.dot_general(q[:, hs], k, (((1,), (1,)), ((), ())), preferred_element_type=F32) * CA_HD ** -0.5
        p = jnp.exp(s - jnp.max(s, axis=-1, keepdims=True))
        o = jnp.dot(p.astype(BF16), v, preferred_element_type=F32)
        outs.append(o * (1.0 / jnp.sum(p, axis=-1, keepdims=True)))
    o = jnp.concatenate(outs, axis=-1).astype(BF16)
    y = x + jnp.dot(o, wo_ref[...], preferred_element_type=F32)
    if final_norm:
        y = y * lax.rsqrt(jnp.mean(y * y, axis=-1, keepdims=True) + EPS) * gf_ref[...]
    out_ref[0] = y


def cross_attn(x, kv, gq, wq, wo, gf, final_norm, tm=512):
    bsz, l, d = x.shape
    nm = kv.shape[1]
    tm = min(tm, l)
    vec = pl.BlockSpec((1, d), lambda b, i: (0, 0))
    mat = pl.BlockSpec((d, d), lambda b, i: (0, 0), pipeline_mode=pl.Buffered(1))
    return pl.pallas_call(
        functools.partial(_cross_attn_kernel, final_norm=final_norm),
        grid=(bsz, l // tm),
        in_specs=[pl.BlockSpec((1, tm, d), lambda b, i: (b, i, 0)),
                  pl.BlockSpec((1, nm, 2 * d), lambda b, i: (b, 0, 0)),
                  vec, mat, mat, vec],
        out_specs=pl.BlockSpec((1, tm, d), lambda b, i: (b, i, 0)),
        out_shape=jax.ShapeDtypeStruct((bsz, l, d), F32),
        compiler_params=_params("parallel", "parallel"),
        name="cross_attn",
    )(x, kv, gq.reshape(1, d).astype(F32), wq, wo, gf.reshape(1, d).astype(F32))


def _even_mixer(x, p, i):
    bsz, l, d = x.shape
    w = D_MODEL
    nh = MLSTM_HEADS
    m = bsz * l
    g_mix = p['norm_mix_g_layer']
    w_in = p['even_w_in'][i]
    gate0 = 5 * w
    w_main = jnp.concatenate([w_in[:, :gate0], w_in[:, gate0 + N_GATE_COLS:]], axis=1).astype(BF16)
    w_gate = jnp.pad(w_in[:, gate0:gate0 + N_GATE_COLS], ((0, 0), (0, LANES - N_GATE_COLS))).astype(BF16)
    xf = x.reshape(m, d)
    proj, gates = rms_matmul(xf, g_mix, [w_main, w_gate], [BF16, F32], tm=512)
    proj3 = proj.reshape(bsz, l, 9 * w)

    bias = jnp.pad(p['mlstm_gate_bias'][i].astype(F32).reshape(1, N_GATE_COLS),
                   ((0, 0), (0, LANES - N_GATE_COLS)))
    cols = gate_prep(gates.reshape(bsz, l, LANES), bias)
    rows = jnp.swapaxes(cols[..., GATE_A_LANE:GATE_A_LANE + N_GATE_COLS], 1, 2)

    qk = qk_conv(proj3, p['mlstm_conv_w'][i], p['mlstm_conv_b'][i])
    hf, hb = mlstm_scan(qk, proj3, 2 * w, cols, rows, BF16)

    x0, t = hyena_pre(proj3, 5 * w, p['hyena_conv_w'][i], p['hyena_conv_b'][i])
    kern, asum = hyena_filter(l, p['hyena_w1'][i], p['hyena_b1'][i], p['hyena_freq1'][i], p['hyena_w2'][i],
                              p['hyena_b2'][i], p['hyena_freq2'][i], p['hyena_w3'][i], p['hyena_delta'][i])
    y = hyena_long_conv(t, kern)

    out = even_out(xf, hf.reshape(m, w), hb.reshape(m, w), proj, x0.reshape(m, w), y.reshape(m, w),
                   t.reshape(m, w), p['mlstm_norm_g'][i], p['hyena_skip'][i], asum,
                   p['even_w_out'][i].astype(BF16))
    return out.reshape(bsz, l, d)


def _odd_mixer(x, p, i, cos, sin):
    bsz, l, d = x.shape
    m = bsz * l
    xf = x.reshape(m, d)
    proj, = rms_matmul(xf, p['norm_mix_g_layer'], [p['odd_w_in'][i].astype(BF16)], [BF16], tm=512)
    logit = p['ret_decay_logit'][i].astype(F32).reshape(2 * RET_HEADS, 1, 1)
    logit_b = jnp.broadcast_to(logit, (2 * RET_HEADS, 8, LANES))
    of, ob = retention_scan(proj.reshape(bsz, l, -1), logit_b, cos, sin, BF16)
    vw = RET_HEADS * RET_HDV
    out = odd_out(xf, of.reshape(m, vw), ob.reshape(m, vw), proj, p['ret_norm_g'][i],
                  p['odd_w_out'][i].astype(BF16))
    return out.reshape(bsz, l, d)


def _trunk(x, mem, p, cos, sin):
    depth = p['norm_mix_g'].shape[0]
    bsz, nm, d = mem.shape
    for layer in range(depth):
        i = layer // 2
        p['norm_mix_g_layer'] = p['norm_mix_g'][layer]
        x = _even_mixer(x, p, i) if layer % 2 == 0 else _odd_mixer(x, p, i, cos, sin)
        kv, = rms_matmul(mem.reshape(bsz * nm, d), p['norm_mem_g'][layer], [p['ca_wkv'][layer].astype(BF16)],
                         [BF16], tm=nm)
        kv = kv.reshape(bsz, nm, 2 * d)
        x = cross_attn(x, kv, p['norm_ca_g'][layer], p['ca_wq'][layer].astype(BF16),
                       p['ca_wo'][layer].astype(BF16), p['norm_final_g'], layer == depth - 1)
    return x


def kernel(x_prompt, x_sample, mem_prompt, mem_sample, norm_mix_g, norm_ca_g, norm_mem_g, norm_final_g, even_w_in, mlstm_conv_w, mlstm_conv_b, mlstm_gate_bias, mlstm_norm_g, hyena_conv_w, hyena_conv_b, hyena_w1, hyena_b1, hyena_freq1, hyena_w2, hyena_b2, hyena_freq2, hyena_w3, hyena_delta, hyena_skip, even_w_out, odd_w_in, ret_decay_logit, ret_norm_g, odd_w_out, ca_wq, ca_wkv, ca_wo):
    p = {'norm_mix_g': norm_mix_g, 'norm_ca_g': norm_ca_g, 'norm_mem_g': norm_mem_g, 'norm_final_g': norm_final_g,
         'even_w_in': even_w_in, 'mlstm_conv_w': mlstm_conv_w, 'mlstm_conv_b': mlstm_conv_b,
         'mlstm_gate_bias': mlstm_gate_bias, 'mlstm_norm_g': mlstm_norm_g,
         'hyena_conv_w': hyena_conv_w, 'hyena_conv_b': hyena_conv_b, 'hyena_w1': hyena_w1, 'hyena_b1': hyena_b1,
         'hyena_freq1': hyena_freq1, 'hyena_w2': hyena_w2, 'hyena_b2': hyena_b2, 'hyena_freq2': hyena_freq2,
         'hyena_w3': hyena_w3, 'hyena_delta': hyena_delta, 'hyena_skip': hyena_skip, 'even_w_out': even_w_out,
         'odd_w_in': odd_w_in, 'ret_decay_logit': ret_decay_logit, 'ret_norm_g': ret_norm_g, 'odd_w_out': odd_w_out,
         'ca_wq': ca_wq, 'ca_wkv': ca_wkv, 'ca_wo': ca_wo}
    l_max = max(x_prompt.shape[1], x_sample.shape[1])
    cos, sin = rope_tables(l_max)
    y_prompt = _trunk(x_prompt, mem_prompt, dict(p), cos, sin)
    y_sample = _trunk(x_sample, mem_sample, dict(p), cos, sin)
    return (y_prompt, y_sample)
```

```python
import functools
import math

import numpy as np
import jax
import jax.numpy as jnp
from jax import lax
from jax.experimental import pallas as pl
from jax.experimental.pallas import tpu as pltpu

F32 = jnp.float32
BF16 = jnp.bfloat16

D_MODEL = 1024
EPS = 1e-6
GN_EPS = 1e-5
CHUNK = 256

MLSTM_HEADS = 4
MLSTM_HD = D_MODEL // MLSTM_HEADS
N_GATE_COLS = 4 * MLSTM_HEADS

HYENA_WIDTH = D_MODEL
HYENA_EMB = 33
HYENA_BANDS = (HYENA_EMB - 1) // 2
HYENA_SHIFT = 0.05
FFT_N2 = 256

RET_HEADS = 4
RET_HDK = D_MODEL // RET_HEADS
RET_HDV = 2 * D_MODEL // RET_HEADS
ROPE_BASE = 10000.0

CA_HEADS = 4
CA_HD = D_MODEL // CA_HEADS

LANES = 128
BF16_SUBLANES = 16
VMEM_LIMIT = 56 * 1024 * 1024

HI = lax.Precision.HIGHEST


def _params(*sem):
    return pltpu.CompilerParams(dimension_semantics=sem, vmem_limit_bytes=VMEM_LIMIT)


def _silu(x):
    return x * (1.0 / (1.0 + jnp.exp(-x)))


def _sigmoid(x):
    return 1.0 / (1.0 + jnp.exp(-x))


def _log_sigmoid(x):
    return jnp.minimum(x, 0.0) - jnp.log(1.0 + jnp.exp(-jnp.abs(x)))


def _rms_matmul_kernel(x_ref, g_ref, *refs, tn):
    nw = len(refs) // 2
    x = x_ref[...]
    xn = (x * lax.rsqrt(jnp.mean(x * x, axis=-1, keepdims=True) + EPS) * g_ref[...]).astype(BF16)
    for w_ref, o_ref in zip(refs[:nw], refs[nw:]):
        n = o_ref.shape[1]
        step = min(tn, n)
        for j in range(n // step):
            cols = slice(j * step, (j + 1) * step)
            o_ref[:, cols] = jnp.dot(xn, w_ref[:, cols], preferred_element_type=F32).astype(o_ref.dtype)


def rms_matmul(x, g, ws, out_dtypes, tm, tn=1024):
    m, d = x.shape
    w_specs = [pl.BlockSpec(w.shape, lambda i: (0, 0), pipeline_mode=pl.Buffered(1)) for w in ws]
    return pl.pallas_call(
        functools.partial(_rms_matmul_kernel, tn=tn),
        grid=(m // tm,),
        in_specs=[pl.BlockSpec((tm, d), lambda i: (i, 0)), pl.BlockSpec((1, d), lambda i: (0, 0))] + w_specs,
        out_specs=[pl.BlockSpec((tm, w.shape[1]), lambda i: (i, 0)) for w in ws],
        out_shape=[jax.ShapeDtypeStruct((m, w.shape[1]), dt) for w, dt in zip(ws, out_dtypes)],
        compiler_params=_params("parallel"),
        name="rms_matmul",
    )(x, g.reshape(1, d).astype(F32), *ws)


F32_SUBLANES = 8


def _even_proj_kernel(x_ref, xp_ref, xn_ref, g_ref, wc_ref, wr_ref, wg_ref, cw_ref, cb_ref,
                      qk_ref, x0_ref, t_ref, rest_ref, gates_ref, *, tiles_per_seq, tn):
    i = pl.program_id(0)
    tm, d = x_ref.shape
    w, h = D_MODEL, F32_SUBLANES
    first = (i % tiles_per_seq) == 0
    last = (i % tiles_per_seq) == tiles_per_seq - 1
    g = g_ref[...]

    def norm(x):
        return x * lax.rsqrt(jnp.mean(x * x, axis=-1, keepdims=True) + EPS) * g

    xm = norm(x_ref[...])
    xp = jnp.where(first, 0.0, norm(xp_ref[...]))
    xn = jnp.where(last, 0.0, norm(xn_ref[...]))
    x_ext = jnp.concatenate([xp, xm, xn], axis=0).astype(BF16)
    xb = xm.astype(BF16)

    def conv(c0):
        cols = slice(c0, c0 + tn)
        r = jnp.dot(x_ext, wc_ref[:, cols], preferred_element_type=F32)
        t = r.shape[0]
        y = (pltpu.roll(r, 1, axis=0) * cw_ref[0:1, cols] + r * cw_ref[1:2, cols]
             + pltpu.roll(r, t - 1, axis=0) * cw_ref[2:3, cols] + cb_ref[:, cols])
        return y[h:h + tm]

    for j in range(2 * w // tn):
        qk_ref[:, j * tn:(j + 1) * tn] = _silu(conv(j * tn)).astype(qk_ref.dtype)
    for j in range(w // tn):
        cols = slice(j * tn, (j + 1) * tn)
        x0_ref[:, cols] = conv(2 * w + j * tn).astype(x0_ref.dtype)
        t_ref[:, cols] = conv(3 * w + j * tn) * conv(4 * w + j * tn)
    for j in range(rest_ref.shape[1] // tn):
        cols = slice(j * tn, (j + 1) * tn)
        rest_ref[:, cols] = jnp.dot(xb, wr_ref[:, cols], preferred_element_type=F32).astype(rest_ref.dtype)
    gates_ref[...] = jnp.dot(xb, wg_ref[...], preferred_element_type=F32)


def even_proj(x, g, w_conv, w_rest, w_gate, conv_w, conv_b, seq_len, tm=512, tn=512):
    m, d = x.shape
    w, h = D_MODEL, F32_SUBLANES
    tm = min(tm, seq_len)
    r = tm // h
    last_blk = m // h - 1
    res = lambda a: pl.BlockSpec(a.shape, lambda i: (0, 0), pipeline_mode=pl.Buffered(1))
    row = lambda n: pl.BlockSpec((tm, n), lambda i: (i, 0))
    return pl.pallas_call(
        functools.partial(_even_proj_kernel, tiles_per_seq=seq_len // tm, tn=tn),
        grid=(m // tm,),
        in_specs=[row(d),
                  pl.BlockSpec((h, d), lambda i: (jnp.maximum(i * r - 1, 0), 0)),
                  pl.BlockSpec((h, d), lambda i: (jnp.minimum((i + 1) * r, last_blk), 0)),
                  pl.BlockSpec((1, d), lambda i: (0, 0)),
                  res(w_conv), res(w_rest), res(w_gate), res(conv_w), res(conv_b)],
        out_specs=[row(2 * w), row(w), row(w), row(4 * w), row(LANES)],
        out_shape=[jax.ShapeDtypeStruct((m, 2 * w), BF16), jax.ShapeDtypeStruct((m, w), BF16),
                   jax.ShapeDtypeStruct((m, w), F32), jax.ShapeDtypeStruct((m, 4 * w), BF16),
                   jax.ShapeDtypeStruct((m, LANES), F32)],
        compiler_params=_params("parallel"),
        name="even_proj",
    )(x, x, x, g.reshape(1, d).astype(F32), w_conv, w_rest, w_gate, conv_w, conv_b)


def _conv3(x, prev_row, next_row, w, b):
    t = x.shape[0]
    row = lax.broadcasted_iota(jnp.int32, x.shape, 0)
    xm = jnp.where(row == 0, prev_row, pltpu.roll(x, 1, axis=0))
    xp = jnp.where(row == t - 1, next_row, pltpu.roll(x, t - 1, axis=0))
    return xm * w[0:1, :] + x * w[1:2, :] + xp * w[2:3, :] + b


def _halo_rows(prev_ref, next_ref, i, n_i):
    prev_row = prev_ref[0, BF16_SUBLANES - 1:BF16_SUBLANES, :].astype(F32)
    next_row = next_ref[0, 0:1, :].astype(F32)
    prev_row = jnp.where(i == 0, 0.0, prev_row)
    next_row = jnp.where(i == n_i - 1, 0.0, next_row)
    return prev_row, next_row


def _halo_specs(tl, tc, l, col_block):
    r = tl // BF16_SUBLANES
    last = l // BF16_SUBLANES - 1
    main = pl.BlockSpec((1, tl, tc), lambda b, i, c: (b, i, col_block(c)))
    prev = pl.BlockSpec((1, BF16_SUBLANES, tc),
                        lambda b, i, c: (b, jnp.maximum(i * r - 1, 0), col_block(c)))
    nxt = pl.BlockSpec((1, BF16_SUBLANES, tc),
                       lambda b, i, c: (b, jnp.minimum((i + 1) * r, last), col_block(c)))
    return main, prev, nxt


def _qk_conv_kernel(x_ref, p_ref, n_ref, w_ref, b_ref, o_ref):
    i = pl.program_id(1)
    prev_row, next_row = _halo_rows(p_ref, n_ref, i, pl.num_programs(1))
    y = _conv3(x_ref[0].astype(F32), prev_row, next_row, w_ref[...], b_ref[...])
    o_ref[0] = _silu(y).astype(o_ref.dtype)


def qk_conv(proj, w, b, tl=512, tc=512):
    bsz, l, _ = proj.shape
    c = w.shape[1]
    tl = min(tl, l)
    main, prev, nxt = _halo_specs(tl, tc, l, lambda cb: cb)
    return pl.pallas_call(
        _qk_conv_kernel,
        grid=(bsz, l // tl, c // tc),
        in_specs=[main, prev, nxt,
                  pl.BlockSpec((3, tc), lambda b_, i, cb: (0, cb)),
                  pl.BlockSpec((1, tc), lambda b_, i, cb: (0, cb))],
        out_specs=pl.BlockSpec((1, tl, tc), lambda b_, i, cb: (b_, i, cb)),
        out_shape=jax.ShapeDtypeStruct((bsz, l, c), BF16),
        compiler_params=_params("parallel", "parallel", "parallel"),
        name="qk_conv",
    )(proj, proj, proj, w.astype(F32), b.reshape(1, c).astype(F32))


def _hyena_pre_kernel(x0_ref, x0p_ref, x0n_ref, x1_ref, x1p_ref, x1n_ref, v_ref, vp_ref, vn_ref,
                      w_ref, b_ref, x0_out, t_out):
    i = pl.program_id(1)
    n_i = pl.num_programs(1)
    outs = []
    for k, (m_ref, p_ref, n_ref) in enumerate(((x0_ref, x0p_ref, x0n_ref), (x1_ref, x1p_ref, x1n_ref),
                                               (v_ref, vp_ref, vn_ref))):
        prev_row, next_row = _halo_rows(p_ref, n_ref, i, n_i)
        outs.append(_conv3(m_ref[0].astype(F32), prev_row, next_row, w_ref[k], b_ref[k]))
    x0_out[0] = outs[0].astype(x0_out.dtype)
    t_out[0] = outs[1] * outs[2]


def hyena_pre(proj, col0, w, b, tl=512, tc=512):
    bsz, l, _ = proj.shape
    wd = HYENA_WIDTH
    tl = min(tl, l)
    specs = []
    for k in range(3):
        base = (col0 + k * wd) // tc
        specs.extend(_halo_specs(tl, tc, l, lambda cb, base=base: base + cb))
    w3 = w.astype(F32).reshape(3, 3, wd).transpose(1, 0, 2)
    b3 = b.astype(F32).reshape(3, 1, wd)
    out_spec = pl.BlockSpec((1, tl, tc), lambda b_, i, cb: (b_, i, cb))
    return pl.pallas_call(
        _hyena_pre_kernel,
        grid=(bsz, l // tl, wd // tc),
        in_specs=specs + [pl.BlockSpec((3, 3, tc), lambda b_, i, cb: (0, 0, cb)),
                          pl.BlockSpec((3, 1, tc), lambda b_, i, cb: (0, 0, cb))],
        out_specs=[out_spec, out_spec],
        out_shape=[jax.ShapeDtypeStruct((bsz, l, wd), BF16), jax.ShapeDtypeStruct((bsz, l, wd), F32)],
        compiler_params=_params("parallel", "parallel", "parallel"),
        name="hyena_pre",
    )(*([proj] * 9), w3, b3)


def _seg_scan(x, op, identity, reverse):
    t = x.shape[0]
    r = lax.broadcasted_iota(jnp.int32, x.shape, 0) % CHUNK
    k = 1
    while k < CHUNK:
        if reverse:
            shifted = pltpu.roll(x, t - k, axis=0)
            valid = r < CHUNK - k
        else:
            shifted = pltpu.roll(x, k, axis=0)
            valid = r >= k
        x = op(x, jnp.where(valid, shifted, identity))
        k *= 2
    return x


GATE_A_LANE = 16
GATE_AMAX_LANE = 32


def _gate_lane(d, h):
    return d * 2 * MLSTM_HEADS + h


def _gate_prep_kernel(g_ref, bias_ref, out_ref):
    nh = MLSTM_HEADS
    g = g_ref[0] + bias_ref[...]
    lane = lax.broadcasted_iota(jnp.int32, g.shape, 1)
    live = (lane < 4 * nh) & (lane % (2 * nh) < nh)
    fwd = lane < 2 * nh
    lf = pltpu.roll(_log_sigmoid(g), LANES - nh, axis=1)
    bcum = jnp.where(fwd, _seg_scan(lf, jnp.add, 0.0, False), _seg_scan(lf, jnp.add, 0.0, True))
    a = g - bcum
    amax = jnp.where(fwd, _seg_scan(a, jnp.maximum, -jnp.inf, False),
                     _seg_scan(a, jnp.maximum, -jnp.inf, True))
    keep = lambda v: jnp.where(live, v, 0.0)
    out_ref[0] = (keep(bcum) + pltpu.roll(keep(a), GATE_A_LANE, axis=1)
                  + pltpu.roll(keep(amax), GATE_AMAX_LANE, axis=1))


def gate_prep(gates, bias, tl=1024):
    bsz, l, _ = gates.shape
    tl = min(tl, l)
    spec = pl.BlockSpec((1, tl, LANES), lambda b, i: (b, i, 0))
    return pl.pallas_call(
        _gate_prep_kernel,
        grid=(bsz, l // tl),
        in_specs=[spec, pl.BlockSpec((1, LANES), lambda b, i: (0, 0))],
        out_specs=spec,
        out_shape=jax.ShapeDtypeStruct((bsz, l, LANES), F32),
        compiler_params=_params("parallel", "parallel"),
        name="gate_prep",
    )(gates, bias)


def _mlstm_kernel(qf_ref, kf_ref, vf_ref, cf_ref, rf_ref, qb_ref, kb_ref, vb_ref, cb_ref, rb_ref,
                  hf_ref, hb_ref, c_st, cb_st, m_st):
    @pl.when(pl.program_id(1) == 0)
    def _():
        c_st[...] = jnp.zeros_like(c_st)
        cb_st[...] = jnp.zeros_like(cb_st)
        m_st[...] = jnp.zeros_like(m_st)

    nh, hd = MLSTM_HEADS, MLSTM_HD
    t_idx = lax.broadcasted_iota(jnp.int32, (CHUNK, CHUNK), 0)
    s_idx = lax.broadcasted_iota(jnp.int32, (CHUNK, CHUNK), 1)
    scale = hd ** -0.5
    ones_blk = jnp.ones((CHUNK, LANES), BF16)
    rep = lambda x, n: jnp.concatenate([x] * n, axis=1)
    lane_dense = lambda col: jnp.broadcast_to(col, (CHUNK, LANES))
    dirs = ((qf_ref, kf_ref, vf_ref, cf_ref, rf_ref, hf_ref, False),
            (qb_ref, kb_ref, vb_ref, cb_ref, rb_ref, hb_ref, True))
    for d, (q_ref, k_ref, v_ref, col_ref, row_ref, o_ref, reverse) in enumerate(dirs):
        mask = (s_idx >= t_idx) if reverse else (s_idx <= t_idx)
        last = 0 if reverse else CHUNK - 1
        for h in range(nh):
            j = d * nh + h
            hs = slice(h * hd, (h + 1) * hd)
            q = q_ref[0, :, hs]
            k = k_ref[0, :, hs] * scale
            v_aug = jnp.concatenate([v_ref[0, :, hs], ones_blk], axis=1)
            gl = _gate_lane(d, h)
            bc = lane_dense(col_ref[0, :, gl:gl + 1])
            a_c = lane_dense(col_ref[0, :, GATE_A_LANE + gl:GATE_A_LANE + gl + 1])
            amax = lane_dense(col_ref[0, :, GATE_AMAX_LANE + gl:GATE_AMAX_LANE + gl + 1])
            a_r = row_ref[0, gl:gl + 1, :]
            dmat = jnp.exp(jnp.where(mask, a_r - rep(amax, CHUNK // LANES), -jnp.inf))
            s = lax.dot_general(q, k, (((1,), (1,)), ((), ())), preferred_element_type=F32) * dmat
            nd_l = jnp.dot(s.astype(BF16), v_aug, preferred_element_type=F32)
            a_last = amax[last:last + 1, :]
            btot = bc[last:last + 1, :]
            kw = rep(jnp.exp(a_c - a_last), hd // LANES) * k.astype(F32)
            upd = lax.dot_general(kw.astype(BF16), v_aug, (((0,), (0,)), ((), ())),
                                  preferred_element_type=F32)

            m_prev = m_st[j, 0:1, :]
            mt = jnp.maximum(amax, m_prev)
            f_l = jnp.exp(amax - mt)
            sc = jnp.exp(m_prev - mt)
            nd_c = jnp.dot(q, cb_st[j], preferred_element_type=F32)
            den = f_l * nd_l[:, hd:] + sc * nd_c[:, hd:]
            inv = 1.0 / jnp.maximum(jnp.abs(den), jnp.exp(-(bc + mt)))
            num = rep(f_l, hd // LANES) * nd_l[:, :hd] + rep(sc, hd // LANES) * nd_c[:, :hd]
            o_ref[0, :, hs] = (num * rep(inv, hd // LANES)).astype(o_ref.dtype)

            m_last = jnp.maximum(a_last, m_prev)
            dec = jnp.exp(m_prev - m_last)
            f_u = jnp.exp(a_last - m_last)
            wide = (hd + LANES) // LANES
            c_new = rep(dec, wide) * c_st[j] + rep(f_u, wide) * upd
            c_st[j] = c_new
            cb_st[j] = c_new.astype(BF16)
            m_st[j] = jnp.broadcast_to(btot + m_last, m_st.shape[1:])


def mlstm_scan(qk, proj, v_col, cols, rows, out_dtype):
    bsz, l, _ = qk.shape
    w = D_MODEL
    nc = l // CHUNK
    vb = v_col // w
    fwd = lambda cb: (lambda b, c: (b, c, cb))
    bwd = lambda cb: (lambda b, c: (b, nc - 1 - c, cb))
    blk = lambda im: pl.BlockSpec((1, CHUNK, w), im)
    ncol = cols.shape[-1]
    in_specs = []
    for mk in (fwd, bwd):
        in_specs += [blk(mk(0)), blk(mk(1)), blk(mk(vb)),
                     pl.BlockSpec((1, CHUNK, ncol), mk(0)),
                     pl.BlockSpec((1, rows.shape[1], CHUNK),
                                  (lambda b, c: (b, 0, c)) if mk is fwd else (lambda b, c: (b, 0, nc - 1 - c)))]
    nst = 2 * MLSTM_HEADS
    shp = jax.ShapeDtypeStruct((bsz, l, w), out_dtype)
    return pl.pallas_call(
        _mlstm_kernel,
        grid=(bsz, nc),
        in_specs=in_specs,
        out_specs=[blk(fwd(0)), blk(bwd(0))],
        out_shape=[shp, shp],
        scratch_shapes=[pltpu.VMEM((nst, MLSTM_HD, MLSTM_HD + LANES), F32),
                        pltpu.VMEM((nst, MLSTM_HD, MLSTM_HD + LANES), BF16),
                        pltpu.VMEM((nst, 8, LANES), F32)],
        compiler_params=_params("parallel", "arbitrary"),
        name="mlstm_scan",
    )(qk, qk, proj, cols, rows, qk, qk, proj, cols, rows)


def _rope_table_kernel(inv_ref, cos_ref, sin_ref):
    tl = cos_ref.shape[0]
    pos = (lax.broadcasted_iota(jnp.int32, cos_ref.shape, 0) + pl.program_id(0) * tl).astype(F32)
    ang = pos * inv_ref[...]
    cos_ref[...] = jnp.cos(ang)
    sin_ref[...] = jnp.sin(ang)


def rope_tables(l, tl=1024):
    half = RET_HDK // 2
    tl = min(tl, l)
    inv = (ROPE_BASE ** (-np.arange(0, RET_HDK, 2, dtype=np.float32) / RET_HDK)).astype(np.float32)
    spec = pl.BlockSpec((tl, half), lambda i: (i, 0))
    shp = jax.ShapeDtypeStruct((l, half), F32)
    return pl.pallas_call(
        _rope_table_kernel,
        grid=(l // tl,),
        in_specs=[pl.BlockSpec((1, half), lambda i: (0, 0))],
        out_specs=[spec, spec],
        out_shape=[shp, shp],
        compiler_params=_params("parallel"),
        name="rope_tables",
    )(jnp.asarray(inv).reshape(1, half))


def _rope(x, cos, sin):
    half = x.shape[-1] // 2
    x1, x2 = x[:, :half], x[:, half:]
    return jnp.concatenate([x1 * cos - x2 * sin, x1 * sin + x2 * cos], axis=-1)


def _retention_kernel(qf_ref, kf_ref, vf_ref, cosf_ref, sinf_ref, qb_ref, kb_ref, vb_ref, cosb_ref, sinb_ref,
                      logit_ref, of_ref, ob_ref, r_st, rb_st):
    @pl.when(pl.program_id(1) == 0)
    def _():
        r_st[...] = jnp.zeros_like(r_st)
        rb_st[...] = jnp.zeros_like(rb_st)

    nh, dk, dv = RET_HEADS, RET_HDK, RET_HDV
    t_idx = lax.broadcasted_iota(jnp.int32, (CHUNK, CHUNK), 0)
    s_idx = lax.broadcasted_iota(jnp.int32, (CHUNK, CHUNK), 1)
    pos = lax.broadcasted_iota(jnp.int32, (CHUNK, LANES), 0).astype(F32)
    rep = lambda x, n: jnp.concatenate([x] * n, axis=1)
    dirs = ((qf_ref, kf_ref, vf_ref, cosf_ref, sinf_ref, of_ref, False),
            (qb_ref, kb_ref, vb_ref, cosb_ref, sinb_ref, ob_ref, True))
    for d, (q_ref, k_ref, v_ref, cos_ref, sin_ref, o_ref, reverse) in enumerate(dirs):
        cos = cos_ref[...]
        sin = sin_ref[...]
        rel = ((s_idx - t_idx) if reverse else (t_idx - s_idx)).astype(F32)
        for h in range(nh):
            j = d * nh + h
            lg = _log_sigmoid(logit_ref[j])[0:1, :]
            dmask = jnp.where(rel >= 0, jnp.exp(rep(lg, CHUNK // LANES) * jnp.maximum(rel, 0.0)), 0.0)
            if reverse:
                q_dec = jnp.exp(lg * (CHUNK - pos))
                k_dec = jnp.exp(lg * pos)
            else:
                q_dec = jnp.exp(lg * (pos + 1.0))
                k_dec = jnp.exp(lg * (CHUNK - 1.0 - pos))
            c_dec = jnp.exp(lg * CHUNK)
            q = _rope(q_ref[0, :, h * dk:(h + 1) * dk].astype(F32), cos, sin) * dk ** -0.5
            k = _rope(k_ref[0, :, h * dk:(h + 1) * dk].astype(F32), cos, sin)
            v = v_ref[0, :, h * dv:(h + 1) * dv]
            s = lax.dot_general(q.astype(BF16), k.astype(BF16), (((1,), (1,)), ((), ())),
                                preferred_element_type=F32) * dmask
            out = jnp.dot(s.astype(BF16), v, preferred_element_type=F32)
            out = out + jnp.dot((q * rep(q_dec, dk // LANES)).astype(BF16), rb_st[j],
                                preferred_element_type=F32)
            o_ref[0, :, h * dv:(h + 1) * dv] = out.astype(o_ref.dtype)
            upd = lax.dot_general((k * rep(k_dec, dk // LANES)).astype(BF16), v, (((0,), (0,)), ((), ())),
                                  preferred_element_type=F32)
            r_new = rep(c_dec, dv // LANES) * r_st[j] + upd
            r_st[j] = r_new
            rb_st[j] = r_new.astype(BF16)


def retention_scan(proj, logit_b, cos, sin, out_dtype):
    bsz, l, _ = proj.shape
    nc = l // CHUNK
    qw, vw = RET_HEADS * RET_HDK, RET_HEADS * RET_HDV
    half = RET_HDK // 2
    in_specs = []
    for rev in (False, True):
        ci = (lambda c: nc - 1 - c) if rev else (lambda c: c)
        in_specs += [pl.BlockSpec((1, CHUNK, qw), lambda b, c, ci=ci: (b, ci(c), 0)),
                     pl.BlockSpec((1, CHUNK, qw), lambda b, c, ci=ci: (b, ci(c), 1)),
                     pl.BlockSpec((1, CHUNK, vw), lambda b, c, ci=ci: (b, ci(c), 1)),
                     pl.BlockSpec((CHUNK, half), lambda b, c, ci=ci: (ci(c), 0)),
                     pl.BlockSpec((CHUNK, half), lambda b, c, ci=ci: (ci(c), 0))]
    in_specs.append(pl.BlockSpec((2 * RET_HEADS, 8, LANES), lambda b, c: (0, 0, 0)))
    shp = jax.ShapeDtypeStruct((bsz, l, vw), out_dtype)
    return pl.pallas_call(
        _retention_kernel,
        grid=(bsz, nc),
        in_specs=in_specs,
        out_specs=[pl.BlockSpec((1, CHUNK, vw), lambda b, c: (b, c, 0)),
                   pl.BlockSpec((1, CHUNK, vw), lambda b, c: (b, nc - 1 - c, 0))],
        out_shape=[shp, shp],
        scratch_shapes=[pltpu.VMEM((2 * RET_HEADS, RET_HDK, RET_HDV), F32),
                        pltpu.VMEM((2 * RET_HEADS, RET_HDK, RET_HDV), BF16)],
        compiler_params=_params("parallel", "arbitrary"),
        name="retention_scan",
    )(proj, proj, proj, cos, sin, proj, proj, proj, cos, sin, logit_b)


def _hyena_filter_kernel(bands_ref, w1t_ref, w1c_ref, w1s_ref, b1_ref, f1_ref, w2_ref, b2_ref, f2_ref,
                         w3_ref, delta_ref, kern_ref, asum_ref, *, l):
    i = pl.program_id(0)
    tr = kern_ref.shape[0]
    n = lax.broadcasted_iota(jnp.int32, (tr, 1), 0) + i * tr
    p = jnp.where(n < l, n, 2 * l - n).astype(F32)
    tt = p / (l - 1.0)
    ang = (2.0 * math.pi / l) * bands_ref[...] * p
    pre = tt * w1t_ref[...]
    pre = pre + jnp.dot(jnp.cos(ang), w1c_ref[...], preferred_element_type=F32, precision=HI)
    pre = pre + jnp.dot(-jnp.sin(ang), w1s_ref[...], preferred_element_type=F32, precision=HI)
    z = jnp.sin(f1_ref[...] * (pre + b1_ref[...]))
    z = jnp.sin(f2_ref[...] * (jnp.dot(z, w2_ref[...], preferred_element_type=F32, precision=HI) + b2_ref[...]))
    hk = jnp.dot(z, w3_ref[...], preferred_element_type=F32, precision=HI)
    hk = hk * (jnp.exp(-tt * jnp.abs(delta_ref[...])) + HYENA_SHIFT)
    hk = jnp.where(n == l, 0.0, hk)
    kern_ref[...] = hk

    @pl.when(i == 0)
    def _():
        asum_ref[...] = jnp.zeros_like(asum_ref)

    asum_ref[...] += jnp.broadcast_to(jnp.sum(jnp.abs(hk), axis=0, keepdims=True), asum_ref.shape)


def hyena_filter(l, w1, b1, f1, w2, b2, f2, w3, delta, tr=512):
    wd = HYENA_WIDTH
    hid = w1.shape[1]
    tr = min(tr, l)

    def pad(a, rows, cols):
        a = a.astype(F32)
        return jnp.pad(a, ((0, rows - a.shape[0]), (0, cols - a.shape[1])))

    bands = np.zeros((1, LANES), np.float32)
    bands[0, :HYENA_BANDS] = np.linspace(1e-4, HYENA_BANDS - 1, HYENA_BANDS, dtype=np.float32)
    w1t = pad(w1[0:1], 1, LANES)
    w1c = pad(w1[1:1 + HYENA_BANDS], LANES, LANES)
    w1s = pad(w1[1 + HYENA_BANDS:], LANES, LANES)
    vec = lambda a: pad(a.reshape(1, hid), 1, LANES)
    w2p = pad(w2, LANES, LANES)
    w3p = pad(w3, LANES, 2 * wd)
    full = lambda shape: pl.BlockSpec(shape, lambda i: (0, 0))
    half_sel = lambda i: (0, (i * tr) // l)
    return pl.pallas_call(
        functools.partial(_hyena_filter_kernel, l=l),
        grid=(2 * l // tr,),
        in_specs=[full((1, LANES)), full((1, LANES)), full((LANES, LANES)), full((LANES, LANES)),
                  full((1, LANES)), full((1, LANES)), full((LANES, LANES)), full((1, LANES)), full((1, LANES)),
                  pl.BlockSpec((LANES, wd), half_sel), pl.BlockSpec((1, wd), half_sel)],
        out_specs=[pl.BlockSpec((tr, wd), lambda i: (i, 0)), pl.BlockSpec((8, wd), lambda i: (0, 0))],
        out_shape=[jax.ShapeDtypeStruct((2 * l, wd), F32), jax.ShapeDtypeStruct((8, wd), F32)],
        compiler_params=_params("arbitrary"),
        name="hyena_filter",
    )(jnp.asarray(bands), w1t, w1c, w1s, vec(b1), vec(f1), w2p, vec(b2), vec(f2), w3p,
      delta.reshape(1, 2 * wd).astype(F32))


def _dft_consts(n1, n2):
    n = n1 * n2
    k1 = np.arange(n1)
    f1 = np.exp(-2j * np.pi * np.outer(k1, k1) / n1)
    k2 = np.arange(n2)
    f2 = np.exp(-2j * np.pi * np.outer(k2, k2) / n2)
    tw = np.exp(-2j * np.pi * np.outer(k1, k2) / n)
    g = f2[None, :, :] * tw[:, None, :]
    ginv = np.conj(np.transpose(g, (0, 2, 1))) / n
    f1inv = np.conj(f1.T)
    return f1, g, ginv, f1inv


def _stack(c):
    return jnp.asarray(np.concatenate([c.real, c.imag], axis=-2).astype(np.float32)).astype(BF16)


def _cdot(fs, xr, xi, rows):
    p = jnp.dot(fs, xr, preferred_element_type=F32)
    if xi is None:
        return p[:rows], p[rows:]
    q = jnp.dot(fs, xi, preferred_element_type=F32)
    return p[:rows] - q[rows:], q[:rows] + p[rows:]


FFT_ROW_CHUNK = 512


def _kron_eye(f, s_blk):
    k = np.kron(f, np.eye(s_blk))
    to = lambda a: jnp.asarray(a.astype(np.float32)).astype(BF16)
    return to(k.real), to(k.imag)


def _strided_stage_kernel(*refs, has_imag_in, n_out):
    mr_ref, mi_ref, xr_ref = refs[:3]
    xi_ref = refs[3] if has_imag_in else None
    n_in = 4 if has_imag_in else 3
    if len(refs[n_in].shape) == 5:
        outs = [refs[n_in].at[part, 0] for part in range(n_out)]
    else:
        outs = [r.at[0] for r in refs[n_in:]]
    flat = lambda r: r.reshape(r.shape[0] * r.shape[1], r.shape[2])
    xr = flat(xr_ref[0]).astype(BF16)
    xi = flat(xi_ref[0]).astype(BF16) if has_imag_in else None
    rows = mr_ref.shape[0]
    rc = min(FFT_ROW_CHUNK, rows)
    s_blk = xr_ref.shape[2]
    for c0 in range(0, rows, rc):
        mr = mr_ref[c0:c0 + rc, :]
        mi = mi_ref[c0:c0 + rc, :]
        re = jnp.dot(mr, xr, preferred_element_type=F32)
        im = jnp.dot(mi, xr, preferred_element_type=F32) if n_out == 2 else None
        if has_imag_in:
            re = re - jnp.dot(mi, xi, preferred_element_type=F32)
            if n_out == 2:
                im = im + jnp.dot(mr, xi, preferred_element_type=F32)
        k0, k1 = c0 // s_blk, (c0 + rc) // s_blk
        for o_ref, val in zip(outs, (re, im)):
            o_ref[k0:k1] = val.reshape(rc // s_blk, s_blk, val.shape[1]).astype(o_ref.dtype)


def fft_stage1(x, n1, rows_in, pairs, complex_in, s_blk, out_dtype):
    c = x.shape[-1]
    n2 = FFT_N2
    xv = x.reshape(x.shape[0], rows_in, n2, c)
    mr, mi = _kron_eye(_dft_consts(n1, n2)[0][:, :rows_in], s_blk)
    mat = pl.BlockSpec(mr.shape, lambda p, j: (0, 0), pipeline_mode=pl.Buffered(1))
    in_specs = [mat, mat, pl.BlockSpec((1, rows_in, s_blk, c), lambda p, j: (p, 0, j, 0))]
    args = [mr, mi, xv]
    if complex_in:
        in_specs.append(pl.BlockSpec((1, rows_in, s_blk, c), lambda p, j: (p + pairs, 0, j, 0)))
        args.append(xv)
    out_spec = pl.BlockSpec((1, n1, s_blk, c), lambda p, j: (p, 0, j, 0))
    shp = jax.ShapeDtypeStruct((pairs, n1, n2, c), out_dtype)
    return pl.pallas_call(
        functools.partial(_strided_stage_kernel, has_imag_in=complex_in, n_out=2),
        grid=(pairs, n2 // s_blk),
        in_specs=in_specs,
        out_specs=[out_spec, out_spec],
        out_shape=[shp, shp],
        compiler_params=_params("parallel", "parallel"),
        name="fft_stage1",
    )(*args)


def _fft_mid_kernel(*refs, conv):
    n2 = FFT_N2
    if conv:
        gs_ref, gis_ref, ar_ref, ai_ref, kr_ref, ki_ref, br_ref, bi_ref = refs
    else:
        gs_ref, ar_ref, ai_ref, br_ref, bi_ref = refs
    xr, xi = _cdot(gs_ref[0], ar_ref[0, 0].astype(BF16), ai_ref[0, 0].astype(BF16), n2)
    if conv:
        kr, ki = kr_ref[0].astype(F32), ki_ref[0].astype(F32)
        yr = xr * kr - xi * ki
        yi = xr * ki + xi * kr
        xr, xi = _cdot(gis_ref[0], yr.astype(BF16), yi.astype(BF16), n2)
    br_ref[0, 0] = xr.astype(br_ref.dtype)
    bi_ref[0, 0] = xi.astype(bi_ref.dtype)


def fft_mid(ar, ai, n1, c, kf=None, ct=1024):
    n2 = FFT_N2
    pairs = ar.shape[0]
    _, g, ginv, _ = _dft_consts(n1, n2)
    a4 = lambda a: a
    mat = pl.BlockSpec((1, 2 * n2, n2), lambda k, cb, p: (k, 0, 0))
    dat = pl.BlockSpec((1, 1, n2, ct), lambda k, cb, p: (p, k, 0, cb))
    in_specs = [mat]
    args = [_stack(g)]
    if kf is not None:
        in_specs.append(mat)
        args.append(_stack(ginv))
    in_specs += [dat, dat]
    args += [a4(ar), a4(ai)]
    if kf is not None:
        fil = pl.BlockSpec((1, n2, ct), lambda k, cb, p: (k, 0, cb))
        in_specs += [fil, fil]
        args += [kf[0], kf[1]]
    shp = jax.ShapeDtypeStruct((pairs, n1, n2, c), BF16)
    br, bi = pl.pallas_call(
        functools.partial(_fft_mid_kernel, conv=kf is not None),
        grid=(n1, c // ct, pairs),
        in_specs=in_specs,
        out_specs=[dat, dat],
        out_shape=[shp, shp],
        compiler_params=_params("parallel", "parallel", "arbitrary"),
        name="fft_mid",
    )(*args)
    return br, bi


def fft_stage1_inv(br, bi, n1, rows_out, c, complex_out, s_blk):
    pairs = br.shape[0]
    n2 = FFT_N2
    mr, mi = _kron_eye(_dft_consts(n1, n2)[3][:rows_out, :], s_blk)
    parts = 2 if complex_out else 1
    mat = pl.BlockSpec(mr.shape, lambda p, j: (0, 0), pipeline_mode=pl.Buffered(1))
    dat = pl.BlockSpec((1, n1, s_blk, c), lambda p, j: (p, 0, j, 0))
    y = pl.pallas_call(
        functools.partial(_strided_stage_kernel, has_imag_in=True, n_out=parts),
        grid=(pairs, n2 // s_blk),
        in_specs=[mat, mat, dat, dat],
        out_specs=pl.BlockSpec((parts, 1, rows_out, s_blk, c), lambda p, j: (0, p, 0, j, 0)),
        out_shape=jax.ShapeDtypeStruct((parts, pairs, rows_out, n2, c), F32),
        compiler_params=_params("parallel", "parallel"),
        name="fft_stage1_inv",
    )(mr, mi, br, bi)
    return y.reshape(parts * pairs, rows_out * n2, c)


def hyena_long_conv(t, kern):
    bsz, l, c = t.shape
    n1 = 2 * l // FFT_N2
    kr, ki = fft_stage1(kern[None], n1, n1, 1, False, 8, F32)
    kr, ki = fft_mid(kr, ki, n1, c)
    kf = (kr.reshape(n1, FFT_N2, c), ki.reshape(n1, FFT_N2, c))
    complex_in = bsz % 2 == 0
    pairs = bsz // 2 if complex_in else bsz
    ar, ai = fft_stage1(t, n1, n1 // 2, pairs, complex_in, BF16_SUBLANES, BF16)
    br, bi = fft_mid(ar, ai, n1, c, kf)
    return fft_stage1_inv(br, bi, n1, n1 // 2, c, complex_in, BF16_SUBLANES)


def _head_ln(x, nheads, g):
    hd = x.shape[-1] // nheads
    outs = []
    for h in range(nheads):
        seg = x[:, h * hd:(h + 1) * hd]
        mu = jnp.mean(seg, axis=-1, keepdims=True)
        cen = seg - mu
        var = jnp.mean(cen * cen, axis=-1, keepdims=True)
        outs.append(cen * lax.rsqrt(var + GN_EPS))
    return jnp.concatenate(outs, axis=-1) * g


def _even_out_kernel(x_ref, hf_ref, hb_ref, o_ref, za_ref, x0_ref, y_ref, t_ref, zb_ref,
                     g_ref, skip_ref, asum_ref, w_ref, out_ref):
    wd = D_MODEL
    ln = _head_ln(hf_ref[...].astype(F32) + hb_ref[...].astype(F32), MLSTM_HEADS, g_ref[...])
    ya = ln * _sigmoid(o_ref[...].astype(F32)) * _silu(za_ref[...].astype(F32))
    t = t_ref[...]
    conv = y_ref[...] * (1.0 / asum_ref[0:1, :])
    yb = x0_ref[...].astype(F32) * (conv + skip_ref[...] * t) * _silu(zb_ref[...].astype(F32))
    mix = jnp.dot(ya.astype(BF16), w_ref[0:wd, :], preferred_element_type=F32)
    mix = mix + jnp.dot(yb.astype(BF16), w_ref[wd:2 * wd, :], preferred_element_type=F32)
    out_ref[...] = x_ref[...] + mix


def even_out(x, hf, hb, proj, x0, y, t, g, skip, asum, w_out, tm=512):
    m, d = x.shape
    tm = min(tm, m)
    row = lambda cb: pl.BlockSpec((tm, d), lambda i, cb=cb: (i, cb))
    vec = pl.BlockSpec((1, d), lambda i: (0, 0))
    return pl.pallas_call(
        _even_out_kernel,
        grid=(m // tm,),
        in_specs=[row(0), row(0), row(0), row(1), row(2), row(0), row(0), row(0), row(3),
                  vec, vec, pl.BlockSpec((8, d), lambda i: (0, 0)),
                  pl.BlockSpec((2 * d, d), lambda i: (0, 0), pipeline_mode=pl.Buffered(1))],
        out_specs=row(0),
        out_shape=jax.ShapeDtypeStruct((m, d), F32),
        compiler_params=_params("parallel"),
        name="even_out",
    )(x, hf, hb, proj, proj, x0, y, t, proj, g.reshape(1, d).astype(F32), skip.reshape(1, d).astype(F32),
      asum, w_out)


def _odd_out_kernel(x_ref, of_ref, ob_ref, gate_ref, g_ref, w_ref, out_ref):
    o = _head_ln(of_ref[...].astype(F32) + ob_ref[...].astype(F32), RET_HEADS, g_ref[...])
    y = _silu(gate_ref[...].astype(F32)) * o
    out_ref[...] = x_ref[...] + jnp.dot(y.astype(BF16), w_ref[...], preferred_element_type=F32)


def odd_out(x, of, ob, proj, g, w_out, tm=512):
    m, d = x.shape
    vw = RET_HEADS * RET_HDV
    wide = lambda cb: pl.BlockSpec((tm, vw), lambda i, cb=cb: (i, cb))
    return pl.pallas_call(
        _odd_out_kernel,
        grid=(m // tm,),
        in_specs=[pl.BlockSpec((tm, d), lambda i: (i, 0)), wide(0), wide(0), wide(2),
                  pl.BlockSpec((1, vw), lambda i: (0, 0)),
                  pl.BlockSpec((vw, d), lambda i: (0, 0), pipeline_mode=pl.Buffered(1))],
        out_specs=pl.BlockSpec((tm, d), lambda i: (i, 0)),
        out_shape=jax.ShapeDtypeStruct((m, d), F32),
        compiler_params=_params("parallel"),
        name="odd_out",
    )(x, of, ob, proj, g.reshape(1, vw).astype(F32), w_out)


def _cross_attn_kernel(x_ref, kv_ref, gq_ref, wq_ref, wo_ref, gf_ref, out_ref, *, final_norm):
    d = D_MODEL
    x = x_ref[0]
    xn = x * lax.rsqrt(jnp.mean(x * x, axis=-1, keepdims=True) + EPS) * gq_ref[...]
    q = jnp.dot(xn.astype(BF16), wq_ref[...], preferred_element_type=F32).astype(BF16)
    outs = []
    for h in range(CA_HEADS):
        hs = slice(h * CA_HD, (h + 1) * CA_HD)
        k = kv_ref[0, :, hs]
        v = kv_ref[0, :, d + h * CA_HD:d + (h + 1) * CA_HD]
        s = lax.dot_general(q[:, hs], k, (((1,), (1,)), ((), ())), preferred_element_type=F32) * CA_HD ** -0.5
        p = jnp.exp(s - jnp.max(s, axis=-1, keepdims=True))
        o = jnp.dot(p.astype(BF16), v, preferred_element_type=F32)
        outs.append(o * (1.0 / jnp.sum(p, axis=-1, keepdims=True)))
    o = jnp.concatenate(outs, axis=-1).astype(BF16)
    y = x + jnp.dot(o, wo_ref[...], preferred_element_type=F32)
    if final_norm:
        y = y * lax.rsqrt(jnp.mean(y * y, axis=-1, keepdims=True) + EPS) * gf_ref[...]
    out_ref[0] = y


def cross_attn(x, kv, gq, wq, wo, gf, final_norm, tm=512):
    bsz, l, d = x.shape
    nm = kv.shape[1]
    tm = min(tm, l)
    vec = pl.BlockSpec((1, d), lambda b, i: (0, 0))
    mat = pl.BlockSpec((d, d), lambda b, i: (0, 0), pipeline_mode=pl.Buffered(1))
    return pl.pallas_call(
        functools.partial(_cross_attn_kernel, final_norm=final_norm),
        grid=(bsz, l // tm),
        in_specs=[pl.BlockSpec((1, tm, d), lambda b, i: (b, i, 0)),
                  pl.BlockSpec((1, nm, 2 * d), lambda b, i: (b, 0, 0)),
                  vec, mat, mat, vec],
        out_specs=pl.BlockSpec((1, tm, d), lambda b, i: (b, i, 0)),
        out_shape=jax.ShapeDtypeStruct((bsz, l, d), F32),
        compiler_params=_params("parallel", "parallel"),
        name="cross_attn",
    )(x, kv, gq.reshape(1, d).astype(F32), wq, wo, gf.reshape(1, d).astype(F32))


def _even_mixer(x, p, i):
    bsz, l, d = x.shape
    w = D_MODEL
    nh = MLSTM_HEADS
    m = bsz * l
    g_mix = p['norm_mix_g_layer']
    w_in = p['even_w_in'][i]
    gate0 = 5 * w
    hy0 = gate0 + N_GATE_COLS
    w_conv = jnp.concatenate([w_in[:, :2 * w], w_in[:, hy0:hy0 + 3 * w]], axis=1).astype(BF16)
    w_rest = jnp.concatenate([w_in[:, 2 * w:gate0], w_in[:, hy0 + 3 * w:]], axis=1).astype(BF16)
    w_gate = jnp.pad(w_in[:, gate0:hy0], ((0, 0), (0, LANES - N_GATE_COLS))).astype(BF16)
    conv_w = jnp.concatenate([p['mlstm_conv_w'][i], p['hyena_conv_w'][i]], axis=1).astype(F32)
    conv_b = jnp.concatenate([p['mlstm_conv_b'][i], p['hyena_conv_b'][i]]).astype(F32).reshape(1, 5 * w)
    xf = x.reshape(m, d)
    qk, x0, t, rest, gates = even_proj(xf, g_mix, w_conv, w_rest, w_gate, conv_w, conv_b, l)
    rest3 = rest.reshape(bsz, l, 4 * w)

    bias = jnp.pad(p['mlstm_gate_bias'][i].astype(F32).reshape(1, N_GATE_COLS),
                   ((0, 0), (0, LANES - N_GATE_COLS)))
    cols = gate_prep(gates.reshape(bsz, l, LANES), bias)
    rows = jnp.swapaxes(cols[..., GATE_A_LANE:GATE_A_LANE + N_GATE_COLS], 1, 2)

    hf, hb = mlstm_scan(qk.reshape(bsz, l, 2 * w), rest3, 0, cols, rows, BF16)

    t = t.reshape(bsz, l, w)
    kern, asum = hyena_filter(l, p['hyena_w1'][i], p['hyena_b1'][i], p['hyena_freq1'][i], p['hyena_w2'][i],
                              p['hyena_b2'][i], p['hyena_freq2'][i], p['hyena_w3'][i], p['hyena_delta'][i])
    y = hyena_long_conv(t, kern)

    out = even_out(xf, hf.reshape(m, w), hb.reshape(m, w), rest, x0, y.reshape(m, w),
                   t.reshape(m, w), p['mlstm_norm_g'][i], p['hyena_skip'][i], asum,
                   p['even_w_out'][i].astype(BF16))
    return out.reshape(bsz, l, d)


def _odd_mixer(x, p, i, cos, sin):
    bsz, l, d = x.shape
    m = bsz * l
    xf = x.reshape(m, d)
    proj, = rms_matmul(xf, p['norm_mix_g_layer'], [p['odd_w_in'][i].astype(BF16)], [BF16], tm=512)
    logit = p['ret_decay_logit'][i].astype(F32).reshape(2 * RET_HEADS, 1, 1)
    logit_b = jnp.broadcast_to(logit, (2 * RET_HEADS, 8, LANES))
    of, ob = retention_scan(proj.reshape(bsz, l, -1), logit_b, cos, sin, BF16)
    vw = RET_HEADS * RET_HDV
    out = odd_out(xf, of.reshape(m, vw), ob.reshape(m, vw), proj, p['ret_norm_g'][i],
                  p['odd_w_out'][i].astype(BF16))
    return out.reshape(bsz, l, d)


def _trunk(x, mem, p, cos, sin):
    depth = p['norm_mix_g'].shape[0]
    bsz, nm, d = mem.shape
    for layer in range(depth):
        i = layer // 2
        p['norm_mix_g_layer'] = p['norm_mix_g'][layer]
        x = _even_mixer(x, p, i) if layer % 2 == 0 else _odd_mixer(x, p, i, cos, sin)
        kv, = rms_matmul(mem.reshape(bsz * nm, d), p['norm_mem_g'][layer], [p['ca_wkv'][layer].astype(BF16)],
                         [BF16], tm=nm)
        kv = kv.reshape(bsz, nm, 2 * d)
        x = cross_attn(x, kv, p['norm_ca_g'][layer], p['ca_wq'][layer].astype(BF16),
                       p['ca_wo'][layer].astype(BF16), p['norm_final_g'], layer == depth - 1)
    return x


def kernel(x_prompt, x_sample, mem_prompt, mem_sample, norm_mix_g, norm_ca_g, norm_mem_g, norm_final_g, even_w_in, mlstm_conv_w, mlstm_conv_b, mlstm_gate_bias, mlstm_norm_g, hyena_conv_w, hyena_conv_b, hyena_w1, hyena_b1, hyena_freq1, hyena_w2, hyena_b2, hyena_freq2, hyena_w3, hyena_delta, hyena_skip, even_w_out, odd_w_in, ret_decay_logit, ret_norm_g, odd_w_out, ca_wq, ca_wkv, ca_wo):
    p = {'norm_mix_g': norm_mix_g, 'norm_ca_g': norm_ca_g, 'norm_mem_g': norm_mem_g, 'norm_final_g': norm_final_g,
         'even_w_in': even_w_in, 'mlstm_conv_w': mlstm_conv_w, 'mlstm_conv_b': mlstm_conv_b,
         'mlstm_gate_bias': mlstm_gate_bias, 'mlstm_norm_g': mlstm_norm_g,
         'hyena_conv_w': hyena_conv_w, 'hyena_conv_b': hyena_conv_b, 'hyena_w1': hyena_w1, 'hyena_b1': hyena_b1,
         'hyena_freq1': hyena_freq1, 'hyena_w2': hyena_w2, 'hyena_b2': hyena_b2, 'hyena_freq2': hyena_freq2,
         'hyena_w3': hyena_w3, 'hyena_delta': hyena_delta, 'hyena_skip': hyena_skip, 'even_w_out': even_w_out,
         'odd_w_in': odd_w_in, 'ret_decay_logit': ret_decay_logit, 'ret_norm_g': ret_norm_g, 'odd_w_out': odd_w_out,
         'ca_wq': ca_wq, 'ca_wkv': ca_wkv, 'ca_wo': ca_wo}
    l_max = max(x_prompt.shape[1], x_sample.shape[1])
    cos, sin = rope_tables(l_max)
    y_prompt = _trunk(x_prompt, mem_prompt, dict(p), cos, sin)
    y_sample = _trunk(x_sample, mem_sample, dict(p), cos, sin)
    return (y_prompt, y_sample)
```

```python
import functools
import math

import numpy as np
import jax
import jax.numpy as jnp
from jax import lax
from jax.experimental import pallas as pl
from jax.experimental.pallas import tpu as pltpu

F32 = jnp.float32
BF16 = jnp.bfloat16

D_MODEL = 1024
EPS = 1e-6
GN_EPS = 1e-5
CHUNK = 256

MLSTM_HEADS = 4
MLSTM_HD = D_MODEL // MLSTM_HEADS
N_GATE_COLS = 4 * MLSTM_HEADS

HYENA_WIDTH = D_MODEL
HYENA_EMB = 33
HYENA_BANDS = (HYENA_EMB - 1) // 2
HYENA_SHIFT = 0.05
FFT_N2 = 256

RET_HEADS = 4
RET_HDK = D_MODEL // RET_HEADS
RET_HDV = 2 * D_MODEL // RET_HEADS
ROPE_BASE = 10000.0

CA_HEADS = 4
CA_HD = D_MODEL // CA_HEADS

LANES = 128
BF16_SUBLANES = 16
VMEM_LIMIT = 56 * 1024 * 1024

HI = lax.Precision.HIGHEST


def _params(*sem):
    return pltpu.CompilerParams(dimension_semantics=sem, vmem_limit_bytes=VMEM_LIMIT)


def _silu(x):
    return x * (1.0 / (1.0 + jnp.exp(-x)))


def _sigmoid(x):
    return 1.0 / (1.0 + jnp.exp(-x))


def _log_sigmoid(x):
    return jnp.minimum(x, 0.0) - jnp.log(1.0 + jnp.exp(-jnp.abs(x)))


def _rms_matmul_kernel(x_ref, g_ref, *refs, tn):
    nw = len(refs) // 2
    x = x_ref[...]
    xn = (x * lax.rsqrt(jnp.mean(x * x, axis=-1, keepdims=True) + EPS) * g_ref[...]).astype(BF16)
    for w_ref, o_ref in zip(refs[:nw], refs[nw:]):
        n = o_ref.shape[1]
        step = min(tn, n)
        for j in range(n // step):
            cols = slice(j * step, (j + 1) * step)
            o_ref[:, cols] = jnp.dot(xn, w_ref[:, cols], preferred_element_type=F32).astype(o_ref.dtype)


def rms_matmul(x, g, ws, out_dtypes, tm, tn=1024):
    m, d = x.shape
    w_specs = [pl.BlockSpec(w.shape, lambda i: (0, 0), pipeline_mode=pl.Buffered(1)) for w in ws]
    return pl.pallas_call(
        functools.partial(_rms_matmul_kernel, tn=tn),
        grid=(m // tm,),
        in_specs=[pl.BlockSpec((tm, d), lambda i: (i, 0)), pl.BlockSpec((1, d), lambda i: (0, 0))] + w_specs,
        out_specs=[pl.BlockSpec((tm, w.shape[1]), lambda i: (i, 0)) for w in ws],
        out_shape=[jax.ShapeDtypeStruct((m, w.shape[1]), dt) for w, dt in zip(ws, out_dtypes)],
        compiler_params=_params("parallel"),
        name="rms_matmul",
    )(x, g.reshape(1, d).astype(F32), *ws)


F32_SUBLANES = 8


def _even_proj_kernel(x_ref, xp_ref, xn_ref, g_ref, wc_ref, wr_ref, wg_ref, cw_ref, cb_ref,
                      qk_ref, x0_ref, t_ref, rest_ref, gates_ref, *, tiles_per_seq, tn):
    i = pl.program_id(0)
    tm, d = x_ref.shape
    w, h = D_MODEL, F32_SUBLANES
    first = (i % tiles_per_seq) == 0
    last = (i % tiles_per_seq) == tiles_per_seq - 1
    g = g_ref[...]

    def norm(x):
        return x * lax.rsqrt(jnp.mean(x * x, axis=-1, keepdims=True) + EPS) * g

    xm = norm(x_ref[...])
    xp = jnp.where(first, 0.0, norm(xp_ref[...]))
    xn = jnp.where(last, 0.0, norm(xn_ref[...]))
    x_ext = jnp.concatenate([xp, xm, xn], axis=0).astype(BF16)
    xb = xm.astype(BF16)

    def conv(c0):
        cols = slice(c0, c0 + tn)
        r = jnp.dot(x_ext, wc_ref[:, cols], preferred_element_type=F32)
        t = r.shape[0]
        y = (pltpu.roll(r, 1, axis=0) * cw_ref[0:1, cols] + r * cw_ref[1:2, cols]
             + pltpu.roll(r, t - 1, axis=0) * cw_ref[2:3, cols] + cb_ref[:, cols])
        return y[h:h + tm]

    for j in range(2 * w // tn):
        qk_ref[:, j * tn:(j + 1) * tn] = _silu(conv(j * tn)).astype(qk_ref.dtype)
    for j in range(w // tn):
        cols = slice(j * tn, (j + 1) * tn)
        x0_ref[:, cols] = conv(2 * w + j * tn).astype(x0_ref.dtype)
        t_ref[:, cols] = conv(3 * w + j * tn) * conv(4 * w + j * tn)
    for j in range(rest_ref.shape[1] // tn):
        cols = slice(j * tn, (j + 1) * tn)
        rest_ref[:, cols] = jnp.dot(xb, wr_ref[:, cols], preferred_element_type=F32).astype(rest_ref.dtype)
    gates_ref[...] = jnp.dot(xb, wg_ref[...], preferred_element_type=F32)


def even_proj(x, g, w_conv, w_rest, w_gate, conv_w, conv_b, seq_len, tm=512, tn=512):
    m, d = x.shape
    w, h = D_MODEL, F32_SUBLANES
    tm = min(tm, seq_len)
    r = tm // h
    last_blk = m // h - 1
    res = lambda a: pl.BlockSpec(a.shape, lambda i: (0, 0), pipeline_mode=pl.Buffered(1))
    row = lambda n: pl.BlockSpec((tm, n), lambda i: (i, 0))
    return pl.pallas_call(
        functools.partial(_even_proj_kernel, tiles_per_seq=seq_len // tm, tn=tn),
        grid=(m // tm,),
        in_specs=[row(d),
                  pl.BlockSpec((h, d), lambda i: (jnp.maximum(i * r - 1, 0), 0)),
                  pl.BlockSpec((h, d), lambda i: (jnp.minimum((i + 1) * r, last_blk), 0)),
                  pl.BlockSpec((1, d), lambda i: (0, 0)),
                  res(w_conv), res(w_rest), res(w_gate), res(conv_w), res(conv_b)],
        out_specs=[row(2 * w), row(w), row(w), row(4 * w), row(LANES)],
        out_shape=[jax.ShapeDtypeStruct((m, 2 * w), BF16), jax.ShapeDtypeStruct((m, w), BF16),
                   jax.ShapeDtypeStruct((m, w), F32), jax.ShapeDtypeStruct((m, 4 * w), BF16),
                   jax.ShapeDtypeStruct((m, LANES), F32)],
        compiler_params=_params("parallel"),
        name="even_proj",
    )(x, x, x, g.reshape(1, d).astype(F32), w_conv, w_rest, w_gate, conv_w, conv_b)


def _seg_scan(x, op, identity, reverse):
    t = x.shape[0]
    r = lax.broadcasted_iota(jnp.int32, x.shape, 0) % CHUNK
    k = 1
    while k < CHUNK:
        if reverse:
            shifted = pltpu.roll(x, t - k, axis=0)
            valid = r < CHUNK - k
        else:
            shifted = pltpu.roll(x, k, axis=0)
            valid = r >= k
        x = op(x, jnp.where(valid, shifted, identity))
        k *= 2
    return x


GATE_A_LANE = 16
GATE_AMAX_LANE = 32


def _gate_lane(d, h):
    return d * 2 * MLSTM_HEADS + h


def _gate_prep_kernel(g_ref, bias_ref, out_ref):
    nh = MLSTM_HEADS
    g = g_ref[0] + bias_ref[...]
    lane = lax.broadcasted_iota(jnp.int32, g.shape, 1)
    live = (lane < 4 * nh) & (lane % (2 * nh) < nh)
    fwd = lane < 2 * nh
    lf = pltpu.roll(_log_sigmoid(g), LANES - nh, axis=1)
    bcum = jnp.where(fwd, _seg_scan(lf, jnp.add, 0.0, False), _seg_scan(lf, jnp.add, 0.0, True))
    a = g - bcum
    amax = jnp.where(fwd, _seg_scan(a, jnp.maximum, -jnp.inf, False),
                     _seg_scan(a, jnp.maximum, -jnp.inf, True))
    keep = lambda v: jnp.where(live, v, 0.0)
    out_ref[0] = (keep(bcum) + pltpu.roll(keep(a), GATE_A_LANE, axis=1)
                  + pltpu.roll(keep(amax), GATE_AMAX_LANE, axis=1))


def gate_prep(gates, bias, tl=1024):
    bsz, l, _ = gates.shape
    tl = min(tl, l)
    spec = pl.BlockSpec((1, tl, LANES), lambda b, i: (b, i, 0))
    return pl.pallas_call(
        _gate_prep_kernel,
        grid=(bsz, l // tl),
        in_specs=[spec, pl.BlockSpec((1, LANES), lambda b, i: (0, 0))],
        out_specs=spec,
        out_shape=jax.ShapeDtypeStruct((bsz, l, LANES), F32),
        compiler_params=_params("parallel", "parallel"),
        name="gate_prep",
    )(gates, bias)


def _mlstm_kernel(qf_ref, kf_ref, vf_ref, cf_ref, rf_ref, qb_ref, kb_ref, vb_ref, cb_ref, rb_ref,
                  hf_ref, hb_ref, c_st, cb_st, m_st):
    @pl.when(pl.program_id(1) == 0)
    def _():
        c_st[...] = jnp.zeros_like(c_st)
        cb_st[...] = jnp.zeros_like(cb_st)
        m_st[...] = jnp.zeros_like(m_st)

    nh, hd = MLSTM_HEADS, MLSTM_HD
    t_idx = lax.broadcasted_iota(jnp.int32, (CHUNK, CHUNK), 0)
    s_idx = lax.broadcasted_iota(jnp.int32, (CHUNK, CHUNK), 1)
    scale = hd ** -0.5
    ones_blk = jnp.ones((CHUNK, LANES), BF16)
    rep = lambda x, n: jnp.concatenate([x] * n, axis=1)
    lane_dense = lambda col: jnp.broadcast_to(col, (CHUNK, LANES))
    dirs = ((qf_ref, kf_ref, vf_ref, cf_ref, rf_ref, hf_ref, False),
            (qb_ref, kb_ref, vb_ref, cb_ref, rb_ref, hb_ref, True))
    for d, (q_ref, k_ref, v_ref, col_ref, row_ref, o_ref, reverse) in enumerate(dirs):
        mask = (s_idx >= t_idx) if reverse else (s_idx <= t_idx)
        last = 0 if reverse else CHUNK - 1
        for h in range(nh):
            j = d * nh + h
            hs = slice(h * hd, (h + 1) * hd)
            q = q_ref[0, :, hs]
            k = k_ref[0, :, hs] * scale
            v_aug = jnp.concatenate([v_ref[0, :, hs], ones_blk], axis=1)
            gl = _gate_lane(d, h)
            bc = lane_dense(col_ref[0, :, gl:gl + 1])
            a_c = lane_dense(col_ref[0, :, GATE_A_LANE + gl:GATE_A_LANE + gl + 1])
            amax = lane_dense(col_ref[0, :, GATE_AMAX_LANE + gl:GATE_AMAX_LANE + gl + 1])
            a_r = row_ref[0, gl:gl + 1, :]
            dmat = jnp.exp(jnp.where(mask, a_r - rep(amax, CHUNK // LANES), -jnp.inf))
            s = lax.dot_general(q, k, (((1,), (1,)), ((), ())), preferred_element_type=F32) * dmat
            nd_l = jnp.dot(s.astype(BF16), v_aug, preferred_element_type=F32)
            a_last = amax[last:last + 1, :]
            btot = bc[last:last + 1, :]
            kw = rep(jnp.exp(a_c - a_last), hd // LANES) * k.astype(F32)
            upd = lax.dot_general(kw.astype(BF16), v_aug, (((0,), (0,)), ((), ())),
                                  preferred_element_type=F32)

            m_prev = m_st[j, 0:1, :]
            mt = jnp.maximum(amax, m_prev)
            f_l = jnp.exp(amax - mt)
            sc = jnp.exp(m_prev - mt)
            nd_c = jnp.dot(q, cb_st[j], preferred_element_type=F32)
            den = f_l * nd_l[:, hd:] + sc * nd_c[:, hd:]
            inv = 1.0 / jnp.maximum(jnp.abs(den), jnp.exp(-(bc + mt)))
            num = rep(f_l, hd // LANES) * nd_l[:, :hd] + rep(sc, hd // LANES) * nd_c[:, :hd]
            o_ref[0, :, hs] = (num * rep(inv, hd // LANES)).astype(o_ref.dtype)

            m_last = jnp.maximum(a_last, m_prev)
            dec = jnp.exp(m_prev - m_last)
            f_u = jnp.exp(a_last - m_last)
            wide = (hd + LANES) // LANES
            c_new = rep(dec, wide) * c_st[j] + rep(f_u, wide) * upd
            c_st[j] = c_new
            cb_st[j] = c_new.astype(BF16)
            m_st[j] = jnp.broadcast_to(btot + m_last, m_st.shape[1:])


def mlstm_scan(qk, proj, v_col, cols, rows, out_dtype):
    bsz, l, _ = qk.shape
    w = D_MODEL
    nc = l // CHUNK
    vb = v_col // w
    fwd = lambda cb: (lambda b, c: (b, c, cb))
    bwd = lambda cb: (lambda b, c: (b, nc - 1 - c, cb))
    blk = lambda im: pl.BlockSpec((1, CHUNK, w), im)
    ncol = cols.shape[-1]
    in_specs = []
    for mk in (fwd, bwd):
        in_specs += [blk(mk(0)), blk(mk(1)), blk(mk(vb)),
                     pl.BlockSpec((1, CHUNK, ncol), mk(0)),
                     pl.BlockSpec((1, rows.shape[1], CHUNK),
                                  (lambda b, c: (b, 0, c)) if mk is fwd else (lambda b, c: (b, 0, nc - 1 - c)))]
    nst = 2 * MLSTM_HEADS
    shp = jax.ShapeDtypeStruct((bsz, l, w), out_dtype)
    return pl.pallas_call(
        _mlstm_kernel,
        grid=(bsz, nc),
        in_specs=in_specs,
        out_specs=[blk(fwd(0)), blk(bwd(0))],
        out_shape=[shp, shp],
        scratch_shapes=[pltpu.VMEM((nst, MLSTM_HD, MLSTM_HD + LANES), F32),
                        pltpu.VMEM((nst, MLSTM_HD, MLSTM_HD + LANES), BF16),
                        pltpu.VMEM((nst, 8, LANES), F32)],
        compiler_params=_params("parallel", "arbitrary"),
        name="mlstm_scan",
    )(qk, qk, proj, cols, rows, qk, qk, proj, cols, rows)


def _rope_table_kernel(inv_ref, cos_ref, sin_ref):
    tl = cos_ref.shape[0]
    pos = (lax.broadcasted_iota(jnp.int32, cos_ref.shape, 0) + pl.program_id(0) * tl).astype(F32)
    ang = pos * inv_ref[...]
    cos_ref[...] = jnp.cos(ang)
    sin_ref[...] = jnp.sin(ang)


def rope_tables(l, tl=1024):
    half = RET_HDK // 2
    tl = min(tl, l)
    inv = (ROPE_BASE ** (-np.arange(0, RET_HDK, 2, dtype=np.float32) / RET_HDK)).astype(np.float32)
    spec = pl.BlockSpec((tl, half), lambda i: (i, 0))
    shp = jax.ShapeDtypeStruct((l, half), F32)
    return pl.pallas_call(
        _rope_table_kernel,
        grid=(l // tl,),
        in_specs=[pl.BlockSpec((1, half), lambda i: (0, 0))],
        out_specs=[spec, spec],
        out_shape=[shp, shp],
        compiler_params=_params("parallel"),
        name="rope_tables",
    )(jnp.asarray(inv).reshape(1, half))


def _rope(x, cos, sin):
    half = x.shape[-1] // 2
    x1, x2 = x[:, :half], x[:, half:]
    return jnp.concatenate([x1 * cos - x2 * sin, x1 * sin + x2 * cos], axis=-1)


def _retention_kernel(qf_ref, kf_ref, vf_ref, cosf_ref, sinf_ref, qb_ref, kb_ref, vb_ref, cosb_ref, sinb_ref,
                      logit_ref, of_ref, ob_ref, r_st, rb_st):
    @pl.when(pl.program_id(1) == 0)
    def _():
        r_st[...] = jnp.zeros_like(r_st)
        rb_st[...] = jnp.zeros_like(rb_st)

    nh, dk, dv = RET_HEADS, RET_HDK, RET_HDV
    t_idx = lax.broadcasted_iota(jnp.int32, (CHUNK, CHUNK), 0)
    s_idx = lax.broadcasted_iota(jnp.int32, (CHUNK, CHUNK), 1)
    pos = lax.broadcasted_iota(jnp.int32, (CHUNK, LANES), 0).astype(F32)
    rep = lambda x, n: jnp.concatenate([x] * n, axis=1)
    dirs = ((qf_ref, kf_ref, vf_ref, cosf_ref, sinf_ref, of_ref, False),
            (qb_ref, kb_ref, vb_ref, cosb_ref, sinb_ref, ob_ref, True))
    for d, (q_ref, k_ref, v_ref, cos_ref, sin_ref, o_ref, reverse) in enumerate(dirs):
        cos = cos_ref[...]
        sin = sin_ref[...]
        rel = ((s_idx - t_idx) if reverse else (t_idx - s_idx)).astype(F32)
        for h in range(nh):
            j = d * nh + h
            lg = _log_sigmoid(logit_ref[j])[0:1, :]
            dmask = jnp.where(rel >= 0, jnp.exp(rep(lg, CHUNK // LANES) * jnp.maximum(rel, 0.0)), 0.0)
            if reverse:
                q_dec = jnp.exp(lg * (CHUNK - pos))
                k_dec = jnp.exp(lg * pos)
            else:
                q_dec = jnp.exp(lg * (pos + 1.0))
                k_dec = jnp.exp(lg * (CHUNK - 1.0 - pos))
            c_dec = jnp.exp(lg * CHUNK)
            q = _rope(q_ref[0, :, h * dk:(h + 1) * dk].astype(F32), cos, sin) * dk ** -0.5
            k = _rope(k_ref[0, :, h * dk:(h + 1) * dk].astype(F32), cos, sin)
            v = v_ref[0, :, h * dv:(h + 1) * dv]
            s = lax.dot_general(q.astype(BF16), k.astype(BF16), (((1,), (1,)), ((), ())),
                                preferred_element_type=F32) * dmask
            out = jnp.dot(s.astype(BF16), v, preferred_element_type=F32)
            out = out + jnp.dot((q * rep(q_dec, dk // LANES)).astype(BF16), rb_st[j],
                                preferred_element_type=F32)
            o_ref[0, :, h * dv:(h + 1) * dv] = out.astype(o_ref.dtype)
            upd = lax.dot_general((k * rep(k_dec, dk // LANES)).astype(BF16), v, (((0,), (0,)), ((), ())),
                                  preferred_element_type=F32)
            r_new = rep(c_dec, dv // LANES) * r_st[j] + upd
            r_st[j] = r_new
            rb_st[j] = r_new.astype(BF16)


def retention_scan(proj, logit_b, cos, sin, out_dtype):
    bsz, l, _ = proj.shape
    nc = l // CHUNK
    qw, vw = RET_HEADS * RET_HDK, RET_HEADS * RET_HDV
    half = RET_HDK // 2
    in_specs = []
    for rev in (False, True):
        ci = (lambda c: nc - 1 - c) if rev else (lambda c: c)
        in_specs += [pl.BlockSpec((1, CHUNK, qw), lambda b, c, ci=ci: (b, ci(c), 0)),
                     pl.BlockSpec((1, CHUNK, qw), lambda b, c, ci=ci: (b, ci(c), 1)),
                     pl.BlockSpec((1, CHUNK, vw), lambda b, c, ci=ci: (b, ci(c), 1)),
                     pl.BlockSpec((CHUNK, half), lambda b, c, ci=ci: (ci(c), 0)),
                     pl.BlockSpec((CHUNK, half), lambda b, c, ci=ci: (ci(c), 0))]
    in_specs.append(pl.BlockSpec((2 * RET_HEADS, 8, LANES), lambda b, c: (0, 0, 0)))
    shp = jax.ShapeDtypeStruct((bsz, l, vw), out_dtype)
    return pl.pallas_call(
        _retention_kernel,
        grid=(bsz, nc),
        in_specs=in_specs,
        out_specs=[pl.BlockSpec((1, CHUNK, vw), lambda b, c: (b, c, 0)),
                   pl.BlockSpec((1, CHUNK, vw), lambda b, c: (b, nc - 1 - c, 0))],
        out_shape=[shp, shp],
        scratch_shapes=[pltpu.VMEM((2 * RET_HEADS, RET_HDK, RET_HDV), F32),
                        pltpu.VMEM((2 * RET_HEADS, RET_HDK, RET_HDV), BF16)],
        compiler_params=_params("parallel", "arbitrary"),
        name="retention_scan",
    )(proj, proj, proj, cos, sin, proj, proj, proj, cos, sin, logit_b)


def _hyena_filter_kernel(cst_ref, w1_ref, w2_ref, w3a_ref, w3b_ref, delta_ref, kern_ref, asum_ref, *, l):
    i = pl.program_id(0)
    tr = kern_ref.shape[0]
    hr = tr // 2
    half = LANES // 2
    row = lax.broadcasted_iota(jnp.int32, (hr, LANES), 0) + i * tr
    n_a, n_b = row, row + hr
    lag = lambda n: jnp.where(n < l, n, 2 * l - n).astype(F32)
    tt_a, tt_b = lag(n_a) / (l - 1.0), lag(n_b) / (l - 1.0)
    lo = lax.broadcasted_iota(jnp.int32, (hr, LANES), 1) < half
    p2 = jnp.where(lo, lag(n_a), lag(n_b))
    tt2 = jnp.where(lo, tt_a, tt_b)
    bands, phase, w1t, b1, f1, b2, f2 = (cst_ref[k:k + 1, :] for k in range(7))
    ang = (2.0 * math.pi / l) * bands * p2 + phase
    pre = jnp.dot(jnp.cos(ang), w1_ref[...], preferred_element_type=F32, precision=HI) + tt2 * w1t
    z = jnp.sin(f1 * (pre + b1))
    z = jnp.sin(f2 * (jnp.dot(z, w2_ref[...], preferred_element_type=F32, precision=HI) + b2))
    rep = lambda x: jnp.concatenate([x] * (kern_ref.shape[1] // LANES), axis=1)
    dabs = jnp.abs(delta_ref[...])
    total = jnp.zeros((1, kern_ref.shape[1]), F32)
    z_hi = z.astype(BF16)
    z_lo = (z - z_hi.astype(F32)).astype(BF16)
    z3 = jnp.concatenate([z_hi, z_lo, z_hi], axis=1)
    for part, (w3_ref, n, tt) in enumerate(((w3a_ref, n_a, tt_a), (w3b_ref, n_b, tt_b))):
        hk = jnp.dot(z3, w3_ref[...], preferred_element_type=F32)
        hk = hk * (jnp.exp(-rep(tt) * dabs) + HYENA_SHIFT)
        hk = jnp.where(rep(n) == l, 0.0, hk)
        kern_ref[part * hr:(part + 1) * hr, :] = hk
        total = total + jnp.sum(jnp.abs(hk), axis=0, keepdims=True)

    @pl.when(i == 0)
    def _():
        asum_ref[...] = jnp.zeros_like(asum_ref)

    asum_ref[...] += jnp.broadcast_to(total, asum_ref.shape)


def hyena_filter(l, w1, b1, f1, w2, b2, f2, w3, delta, tr=512):
    wd = HYENA_WIDTH
    hid = w1.shape[1]
    half = LANES // 2
    nb = HYENA_BANDS
    tr = min(tr, l)
    w1, w2, w3 = w1.astype(F32), w2.astype(F32), w3.astype(F32)
    two = lambda v: jnp.tile(jnp.pad(v.astype(F32).reshape(1, -1), ((0, 0), (0, half - v.shape[-1]))), (1, 2))
    bands = np.zeros((1, half), np.float32)
    bands[0, :nb] = bands[0, nb:2 * nb] = np.linspace(1e-4, nb - 1, nb, dtype=np.float32)
    phase = np.zeros((1, half), np.float32)
    phase[0, nb:2 * nb] = 0.5 * np.pi
    cst = jnp.concatenate([two(jnp.asarray(bands)), two(jnp.asarray(phase)), two(w1[0]), two(b1), two(f1),
                           two(b2), two(f2), jnp.zeros((1, LANES), F32)], axis=0)
    blk = lambda a: jnp.pad(a, ((0, half - a.shape[0]), (0, half - a.shape[1])))
    diag2 = lambda a: jnp.concatenate([jnp.pad(blk(a), ((0, 0), (0, half))),
                                       jnp.pad(blk(a), ((0, 0), (half, 0)))], axis=0)
    w3h = jnp.pad(w3, ((0, half - hid), (0, 0)))

    def split3(a):
        hi = a.astype(BF16)
        lo = (a - hi.astype(F32)).astype(BF16)
        return jnp.concatenate([hi, hi, lo], axis=0)

    w3a = split3(jnp.pad(w3h, ((0, half), (0, 0))))
    w3b = split3(jnp.pad(w3h, ((half, 0), (0, 0))))
    full = lambda shape: pl.BlockSpec(shape, lambda i: (0, 0))
    half_sel = lambda i: (0, (i * tr) // l)
    return pl.pallas_call(
        functools.partial(_hyena_filter_kernel, l=l),
        grid=(2 * l // tr,),
        in_specs=[full((8, LANES)), full((LANES, LANES)), full((LANES, LANES)),
                  pl.BlockSpec((3 * LANES, wd), half_sel), pl.BlockSpec((3 * LANES, wd), half_sel),
                  pl.BlockSpec((1, wd), half_sel)],
        out_specs=[pl.BlockSpec((tr, wd), lambda i: (i, 0)), pl.BlockSpec((8, wd), lambda i: (0, 0))],
        out_shape=[jax.ShapeDtypeStruct((2 * l, wd), F32), jax.ShapeDtypeStruct((8, wd), F32)],
        compiler_params=_params("arbitrary"),
        name="hyena_filter",
    )(cst, diag2(w1[1:]), diag2(w2), w3a, w3b, delta.reshape(1, 2 * wd).astype(F32))


def _dft_consts(n1, n2):
    n = n1 * n2
    k1 = np.arange(n1)
    f1 = np.exp(-2j * np.pi * np.outer(k1, k1) / n1)
    k2 = np.arange(n2)
    f2 = np.exp(-2j * np.pi * np.outer(k2, k2) / n2)
    tw = np.exp(-2j * np.pi * np.outer(k1, k2) / n)
    g = f2[None, :, :] * tw[:, None, :]
    ginv = np.conj(np.transpose(g, (0, 2, 1))) / n
    f1inv = np.conj(f1.T)
    return f1, g, ginv, f1inv


def _stack(c):
    return jnp.asarray(np.concatenate([c.real, c.imag], axis=-2).astype(np.float32)).astype(BF16)


def _cdot(fs, xr, xi, rows):
    p = jnp.dot(fs, xr, preferred_element_type=F32)
    if xi is None:
        return p[:rows], p[rows:]
    q = jnp.dot(fs, xi, preferred_element_type=F32)
    return p[:rows] - q[rows:], q[:rows] + p[rows:]


FFT_ROW_CHUNK = 512


def _kron_eye(f, s_blk):
    k = np.kron(f, np.eye(s_blk))
    to = lambda a: jnp.asarray(a.astype(np.float32)).astype(BF16)
    return to(k.real), to(k.imag)


def _strided_stage_kernel(*refs, has_imag_in, n_out):
    mr_ref, mi_ref, xr_ref = refs[:3]
    xi_ref = refs[3] if has_imag_in else None
    n_in = 4 if has_imag_in else 3
    if len(refs[n_in].shape) == 5:
        outs = [refs[n_in].at[part, 0] for part in range(n_out)]
    else:
        outs = [r.at[0] for r in refs[n_in:]]
    flat = lambda r: r.reshape(r.shape[0] * r.shape[1], r.shape[2])
    xr = flat(xr_ref[0]).astype(BF16)
    xi = flat(xi_ref[0]).astype(BF16) if has_imag_in else None
    rows = mr_ref.shape[0]
    rc = min(FFT_ROW_CHUNK, rows)
    s_blk = xr_ref.shape[2]
    for c0 in range(0, rows, rc):
        mr = mr_ref[c0:c0 + rc, :]
        mi = mi_ref[c0:c0 + rc, :]
        re = jnp.dot(mr, xr, preferred_element_type=F32)
        im = jnp.dot(mi, xr, preferred_element_type=F32) if n_out == 2 else None
        if has_imag_in:
            re = re - jnp.dot(mi, xi, preferred_element_type=F32)
            if n_out == 2:
                im = im + jnp.dot(mr, xi, preferred_element_type=F32)
        k0, k1 = c0 // s_blk, (c0 + rc) // s_blk
        for o_ref, val in zip(outs, (re, im)):
            o_ref[k0:k1] = val.reshape(rc // s_blk, s_blk, val.shape[1]).astype(o_ref.dtype)


def fft_stage1(x, n1, rows_in, pairs, complex_in, s_blk, out_dtype):
    c = x.shape[-1]
    n2 = FFT_N2
    xv = x.reshape(x.shape[0], rows_in, n2, c)
    mr, mi = _kron_eye(_dft_consts(n1, n2)[0][:, :rows_in], s_blk)
    mat = pl.BlockSpec(mr.shape, lambda p, j: (0, 0), pipeline_mode=pl.Buffered(1))
    in_specs = [mat, mat, pl.BlockSpec((1, rows_in, s_blk, c), lambda p, j: (p, 0, j, 0))]
    args = [mr, mi, xv]
    if complex_in:
        in_specs.append(pl.BlockSpec((1, rows_in, s_blk, c), lambda p, j: (p + pairs, 0, j, 0)))
        args.append(xv)
    out_spec = pl.BlockSpec((1, n1, s_blk, c), lambda p, j: (p, 0, j, 0))
    shp = jax.ShapeDtypeStruct((pairs, n1, n2, c), out_dtype)
    return pl.pallas_call(
        functools.partial(_strided_stage_kernel, has_imag_in=complex_in, n_out=2),
        grid=(pairs, n2 // s_blk),
        in_specs=in_specs,
        out_specs=[out_spec, out_spec],
        out_shape=[shp, shp],
        compiler_params=_params("parallel", "parallel"),
        name="fft_stage1",
    )(*args)


def _fft_mid_kernel(*refs, conv):
    n2 = FFT_N2
    if conv:
        gs_ref, gis_ref, ar_ref, ai_ref, kr_ref, ki_ref, br_ref, bi_ref = refs
    else:
        gs_ref, ar_ref, ai_ref, br_ref, bi_ref = refs
    xr, xi = _cdot(gs_ref[0], ar_ref[0, 0].astype(BF16), ai_ref[0, 0].astype(BF16), n2)
    if conv:
        kr, ki = kr_ref[0].astype(F32), ki_ref[0].astype(F32)
        yr = xr * kr - xi * ki
        yi = xr * ki + xi * kr
        xr, xi = _cdot(gis_ref[0], yr.astype(BF16), yi.astype(BF16), n2)
    br_ref[0, 0] = xr.astype(br_ref.dtype)
    bi_ref[0, 0] = xi.astype(bi_ref.dtype)


def fft_mid(ar, ai, n1, c, kf=None, ct=1024):
    n2 = FFT_N2
    pairs = ar.shape[0]
    _, g, ginv, _ = _dft_consts(n1, n2)
    a4 = lambda a: a
    mat = pl.BlockSpec((1, 2 * n2, n2), lambda k, cb, p: (k, 0, 0))
    dat = pl.BlockSpec((1, 1, n2, ct), lambda k, cb, p: (p, k, 0, cb))
    in_specs = [mat]
    args = [_stack(g)]
    if kf is not None:
        in_specs.append(mat)
        args.append(_stack(ginv))
    in_specs += [dat, dat]
    args += [a4(ar), a4(ai)]
    if kf is not None:
        fil = pl.BlockSpec((1, n2, ct), lambda k, cb, p: (k, 0, cb))
        in_specs += [fil, fil]
        args += [kf[0], kf[1]]
    shp = jax.ShapeDtypeStruct((pairs, n1, n2, c), BF16)
    br, bi = pl.pallas_call(
        functools.partial(_fft_mid_kernel, conv=kf is not None),
        grid=(n1, c // ct, pairs),
        in_specs=in_specs,
        out_specs=[dat, dat],
        out_shape=[shp, shp],
        compiler_params=_params("parallel", "parallel", "arbitrary"),
        name="fft_mid",
    )(*args)
    return br, bi


def fft_stage1_inv(br, bi, n1, rows_out, c, complex_out, s_blk):
    pairs = br.shape[0]
    n2 = FFT_N2
    mr, mi = _kron_eye(_dft_consts(n1, n2)[3][:rows_out, :], s_blk)
    parts = 2 if complex_out else 1
    mat = pl.BlockSpec(mr.shape, lambda p, j: (0, 0), pipeline_mode=pl.Buffered(1))
    dat = pl.BlockSpec((1, n1, s_blk, c), lambda p, j: (p, 0, j, 0))
    y = pl.pallas_call(
        functools.partial(_strided_stage_kernel, has_imag_in=True, n_out=parts),
        grid=(pairs, n2 // s_blk),
        in_specs=[mat, mat, dat, dat],
        out_specs=pl.BlockSpec((parts, 1, rows_out, s_blk, c), lambda p, j: (0, p, 0, j, 0)),
        out_shape=jax.ShapeDtypeStruct((parts, pairs, rows_out, n2, c), F32),
        compiler_params=_params("parallel", "parallel"),
        name="fft_stage1_inv",
    )(mr, mi, br, bi)
    return y.reshape(parts * pairs, rows_out * n2, c)


def hyena_long_conv(t, kern):
    bsz, l, c = t.shape
    n1 = 2 * l // FFT_N2
    kr, ki = fft_stage1(kern[None], n1, n1, 1, False, 8, F32)
    kr, ki = fft_mid(kr, ki, n1, c)
    kf = (kr.reshape(n1, FFT_N2, c), ki.reshape(n1, FFT_N2, c))
    complex_in = bsz % 2 == 0
    pairs = bsz // 2 if complex_in else bsz
    ar, ai = fft_stage1(t, n1, n1 // 2, pairs, complex_in, BF16_SUBLANES, BF16)
    br, bi = fft_mid(ar, ai, n1, c, kf)
    return fft_stage1_inv(br, bi, n1, n1 // 2, c, complex_in, BF16_SUBLANES)


def _head_ln(x, nheads, g):
    hd = x.shape[-1] // nheads
    outs = []
    for h in range(nheads):
        seg = x[:, h * hd:(h + 1) * hd]
        mu = jnp.mean(seg, axis=-1, keepdims=True)
        cen = seg - mu
        var = jnp.mean(cen * cen, axis=-1, keepdims=True)
        outs.append(cen * lax.rsqrt(var + GN_EPS))
    return jnp.concatenate(outs, axis=-1) * g


def _even_out_kernel(x_ref, hf_ref, hb_ref, o_ref, za_ref, x0_ref, y_ref, t_ref, zb_ref,
                     g_ref, skip_ref, asum_ref, w_ref, out_ref):
    wd = D_MODEL
    ln = _head_ln(hf_ref[...].astype(F32) + hb_ref[...].astype(F32), MLSTM_HEADS, g_ref[...])
    ya = ln * _sigmoid(o_ref[...].astype(F32)) * _silu(za_ref[...].astype(F32))
    t = t_ref[...]
    conv = y_ref[...] * (1.0 / asum_ref[0:1, :])
    yb = x0_ref[...].astype(F32) * (conv + skip_ref[...] * t) * _silu(zb_ref[...].astype(F32))
    mix = jnp.dot(ya.astype(BF16), w_ref[0:wd, :], preferred_element_type=F32)
    mix = mix + jnp.dot(yb.astype(BF16), w_ref[wd:2 * wd, :], preferred_element_type=F32)
    out_ref[...] = x_ref[...] + mix


def even_out(x, hf, hb, proj, x0, y, t, g, skip, asum, w_out, tm=512):
    m, d = x.shape
    tm = min(tm, m)
    row = lambda cb: pl.BlockSpec((tm, d), lambda i, cb=cb: (i, cb))
    vec = pl.BlockSpec((1, d), lambda i: (0, 0))
    return pl.pallas_call(
        _even_out_kernel,
        grid=(m // tm,),
        in_specs=[row(0), row(0), row(0), row(1), row(2), row(0), row(0), row(0), row(3),
                  vec, vec, pl.BlockSpec((8, d), lambda i: (0, 0)),
                  pl.BlockSpec((2 * d, d), lambda i: (0, 0), pipeline_mode=pl.Buffered(1))],
        out_specs=row(0),
        out_shape=jax.ShapeDtypeStruct((m, d), F32),
        compiler_params=_params("parallel"),
        name="even_out",
    )(x, hf, hb, proj, proj, x0, y, t, proj, g.reshape(1, d).astype(F32), skip.reshape(1, d).astype(F32),
      asum, w_out)


def _odd_out_kernel(x_ref, of_ref, ob_ref, gate_ref, g_ref, w_ref, out_ref):
    o = _head_ln(of_ref[...].astype(F32) + ob_ref[...].astype(F32), RET_HEADS, g_ref[...])
    y = _silu(gate_ref[...].astype(F32)) * o
    out_ref[...] = x_ref[...] + jnp.dot(y.astype(BF16), w_ref[...], preferred_element_type=F32)


def odd_out(x, of, ob, proj, g, w_out, tm=512):
    m, d = x.shape
    vw = RET_HEADS * RET_HDV
    wide = lambda cb: pl.BlockSpec((tm, vw), lambda i, cb=cb: (i, cb))
    return pl.pallas_call(
        _odd_out_kernel,
        grid=(m // tm,),
        in_specs=[pl.BlockSpec((tm, d), lambda i: (i, 0)), wide(0), wide(0), wide(2),
                  pl.BlockSpec((1, vw), lambda i: (0, 0)),
                  pl.BlockSpec((vw, d), lambda i: (0, 0), pipeline_mode=pl.Buffered(1))],
        out_specs=pl.BlockSpec((tm, d), lambda i: (i, 0)),
        out_shape=jax.ShapeDtypeStruct((m, d), F32),
        compiler_params=_params("parallel"),
        name="odd_out",
    )(x, of, ob, proj, g.reshape(1, vw).astype(F32), w_out)


def _ca_fold_kernel(kv_ref, wq_ref, wo_ref, wqk_ref, vwo_ref):
    d = D_MODEL
    for h in range(CA_HEADS):
        hs = slice(h * CA_HD, (h + 1) * CA_HD)
        k = kv_ref[0, :, hs]
        v = kv_ref[0, :, d + h * CA_HD:d + (h + 1) * CA_HD]
        nm = k.shape[0]
        wqk = lax.dot_general(wq_ref[:, hs], k, (((1,), (1,)), ((), ())), preferred_element_type=F32)
        wqk_ref[0, :, h * nm:(h + 1) * nm] = (wqk * CA_HD ** -0.5).astype(wqk_ref.dtype)
        vwo_ref[0, h * nm:(h + 1) * nm, :] = jnp.dot(v, wo_ref[hs, :],
                                                     preferred_element_type=F32).astype(vwo_ref.dtype)


def ca_fold(kv, wq, wo):
    bsz, nm, d2 = kv.shape
    d = d2 // 2
    mat = pl.BlockSpec((d, d), lambda b: (0, 0))
    return pl.pallas_call(
        _ca_fold_kernel,
        grid=(bsz,),
        in_specs=[pl.BlockSpec((1, nm, d2), lambda b: (b, 0, 0)), mat, mat],
        out_specs=[pl.BlockSpec((1, d, CA_HEADS * nm), lambda b: (b, 0, 0)),
                   pl.BlockSpec((1, CA_HEADS * nm, d), lambda b: (b, 0, 0))],
        out_shape=[jax.ShapeDtypeStruct((bsz, d, CA_HEADS * nm), BF16),
                   jax.ShapeDtypeStruct((bsz, CA_HEADS * nm, d), BF16)],
        compiler_params=_params("parallel"),
        name="ca_fold",
    )(kv, wq, wo)


def _cross_attn_kernel(x_ref, wqk_ref, vwo_ref, gq_ref, gf_ref, out_ref, *, final_norm):
    x = x_ref[0]
    xn = x * lax.rsqrt(jnp.mean(x * x, axis=-1, keepdims=True) + EPS) * gq_ref[...]
    s_all = jnp.dot(xn.astype(BF16), wqk_ref[0], preferred_element_type=F32)
    nm = s_all.shape[1] // CA_HEADS
    probs = []
    for h in range(CA_HEADS):
        s = s_all[:, h * nm:(h + 1) * nm]
        p = jnp.exp(s - jnp.max(s, axis=-1, keepdims=True))
        probs.append((p * (1.0 / jnp.sum(p, axis=-1, keepdims=True))).astype(BF16))
    y = x + jnp.dot(jnp.concatenate(probs, axis=-1), vwo_ref[0], preferred_element_type=F32)
    if final_norm:
        y = y * lax.rsqrt(jnp.mean(y * y, axis=-1, keepdims=True) + EPS) * gf_ref[...]
    out_ref[0] = y


def cross_attn(x, wqk, vwo, gq, gf, final_norm, tm=512):
    bsz, l, d = x.shape
    tm = min(tm, l)
    vec = pl.BlockSpec((1, d), lambda b, i: (0, 0))
    return pl.pallas_call(
        functools.partial(_cross_attn_kernel, final_norm=final_norm),
        grid=(bsz, l // tm),
        in_specs=[pl.BlockSpec((1, tm, d), lambda b, i: (b, i, 0)),
                  pl.BlockSpec((1,) + wqk.shape[1:], lambda b, i: (b, 0, 0)),
                  pl.BlockSpec((1,) + vwo.shape[1:], lambda b, i: (b, 0, 0)),
                  vec, vec],
        out_specs=pl.BlockSpec((1, tm, d), lambda b, i: (b, i, 0)),
        out_shape=jax.ShapeDtypeStruct((bsz, l, d), F32),
        compiler_params=_params("parallel", "parallel"),
        name="cross_attn",
    )(x, wqk, vwo, gq.reshape(1, d).astype(F32), gf.reshape(1, d).astype(F32))


def _even_mixer(x, p, i):
    bsz, l, d = x.shape
    w = D_MODEL
    nh = MLSTM_HEADS
    m = bsz * l
    g_mix = p['norm_mix_g_layer']
    w_in = p['even_w_in'][i]
    gate0 = 5 * w
    hy0 = gate0 + N_GATE_COLS
    w_conv = jnp.concatenate([w_in[:, :2 * w], w_in[:, hy0:hy0 + 3 * w]], axis=1).astype(BF16)
    w_rest = jnp.concatenate([w_in[:, 2 * w:gate0], w_in[:, hy0 + 3 * w:]], axis=1).astype(BF16)
    w_gate = jnp.pad(w_in[:, gate0:hy0], ((0, 0), (0, LANES - N_GATE_COLS))).astype(BF16)
    conv_w = jnp.concatenate([p['mlstm_conv_w'][i], p['hyena_conv_w'][i]], axis=1).astype(F32)
    conv_b = jnp.concatenate([p['mlstm_conv_b'][i], p['hyena_conv_b'][i]]).astype(F32).reshape(1, 5 * w)
    xf = x.reshape(m, d)
    qk, x0, t, rest, gates = even_proj(xf, g_mix, w_conv, w_rest, w_gate, conv_w, conv_b, l)
    rest3 = rest.reshape(bsz, l, 4 * w)

    bias = jnp.pad(p['mlstm_gate_bias'][i].astype(F32).reshape(1, N_GATE_COLS),
                   ((0, 0), (0, LANES - N_GATE_COLS)))
    cols = gate_prep(gates.reshape(bsz, l, LANES), bias)
    rows = jnp.swapaxes(cols[..., GATE_A_LANE:GATE_A_LANE + N_GATE_COLS], 1, 2)

    hf, hb = mlstm_scan(qk.reshape(bsz, l, 2 * w), rest3, 0, cols, rows, BF16)

    t = t.reshape(bsz, l, w)
    kern, asum = hyena_filter(l, p['hyena_w1'][i], p['hyena_b1'][i], p['hyena_freq1'][i], p['hyena_w2'][i],
                              p['hyena_b2'][i], p['hyena_freq2'][i], p['hyena_w3'][i], p['hyena_delta'][i])
    y = hyena_long_conv(t, kern)

    out = even_out(xf, hf.reshape(m, w), hb.reshape(m, w), rest, x0, y.reshape(m, w),
                   t.reshape(m, w), p['mlstm_norm_g'][i], p['hyena_skip'][i], asum,
                   p['even_w_out'][i].astype(BF16))
    return out.reshape(bsz, l, d)


def _odd_mixer(x, p, i, cos, sin):
    bsz, l, d = x.shape
    m = bsz * l
    xf = x.reshape(m, d)
    proj, = rms_matmul(xf, p['norm_mix_g_layer'], [p['odd_w_in'][i].astype(BF16)], [BF16], tm=512)
    logit = p['ret_decay_logit'][i].astype(F32).reshape(2 * RET_HEADS, 1, 1)
    logit_b = jnp.broadcast_to(logit, (2 * RET_HEADS, 8, LANES))
    of, ob = retention_scan(proj.reshape(bsz, l, -1), logit_b, cos, sin, BF16)
    vw = RET_HEADS * RET_HDV
    out = odd_out(xf, of.reshape(m, vw), ob.reshape(m, vw), proj, p['ret_norm_g'][i],
                  p['odd_w_out'][i].astype(BF16))
    return out.reshape(bsz, l, d)


def _trunk(x, mem, p, cos, sin):
    depth = p['norm_mix_g'].shape[0]
    bsz, nm, d = mem.shape
    for layer in range(depth):
        i = layer // 2
        p['norm_mix_g_layer'] = p['norm_mix_g'][layer]
        x = _even_mixer(x, p, i) if layer % 2 == 0 else _odd_mixer(x, p, i, cos, sin)
        kv, = rms_matmul(mem.reshape(bsz * nm, d), p['norm_mem_g'][layer], [p['ca_wkv'][layer].astype(BF16)],
                         [BF16], tm=nm)
        wqk, vwo = ca_fold(kv.reshape(bsz, nm, 2 * d), p['ca_wq'][layer].astype(BF16),
                           p['ca_wo'][layer].astype(BF16))
        x = cross_attn(x, wqk, vwo, p['norm_ca_g'][layer], p['norm_final_g'], layer == depth - 1)
    return x


def kernel(x_prompt, x_sample, mem_prompt, mem_sample, norm_mix_g, norm_ca_g, norm_mem_g, norm_final_g, even_w_in, mlstm_conv_w, mlstm_conv_b, mlstm_gate_bias, mlstm_norm_g, hyena_conv_w, hyena_conv_b, hyena_w1, hyena_b1, hyena_freq1, hyena_w2, hyena_b2, hyena_freq2, hyena_w3, hyena_delta, hyena_skip, even_w_out, odd_w_in, ret_decay_logit, ret_norm_g, odd_w_out, ca_wq, ca_wkv, ca_wo):
    p = {'norm_mix_g': norm_mix_g, 'norm_ca_g': norm_ca_g, 'norm_mem_g': norm_mem_g, 'norm_final_g': norm_final_g,
         'even_w_in': even_w_in, 'mlstm_conv_w': mlstm_conv_w, 'mlstm_conv_b': mlstm_conv_b,
         'mlstm_gate_bias': mlstm_gate_bias, 'mlstm_norm_g': mlstm_norm_g,
         'hyena_conv_w': hyena_conv_w, 'hyena_conv_b': hyena_conv_b, 'hyena_w1': hyena_w1, 'hyena_b1': hyena_b1,
         'hyena_freq1': hyena_freq1, 'hyena_w2': hyena_w2, 'hyena_b2': hyena_b2, 'hyena_freq2': hyena_freq2,
         'hyena_w3': hyena_w3, 'hyena_delta': hyena_delta, 'hyena_skip': hyena_skip, 'even_w_out': even_w_out,
         'odd_w_in': odd_w_in, 'ret_decay_logit': ret_decay_logit, 'ret_norm_g': ret_norm_g, 'odd_w_out': odd_w_out,
         'ca_wq': ca_wq, 'ca_wkv': ca_wkv, 'ca_wo': ca_wo}
    l_max = max(x_prompt.shape[1], x_sample.shape[1])
    cos, sin = rope_tables(l_max)
    y_prompt = _trunk(x_prompt, mem_prompt, dict(p), cos, sin)
    y_sample = _trunk(x_sample, mem_sample, dict(p), cos, sin)
    return (y_prompt, y_sample)
```

```python
import functools
import math

import numpy as np
import jax
import jax.numpy as jnp
from jax import lax
from jax.experimental import pallas as pl
from jax.experimental.pallas import tpu as pltpu

F32 = jnp.float32
BF16 = jnp.bfloat16

D_MODEL = 1024
EPS = 1e-6
GN_EPS = 1e-5
CHUNK = 256

MLSTM_HEADS = 4
MLSTM_HD = D_MODEL // MLSTM_HEADS
N_GATE_COLS = 4 * MLSTM_HEADS

HYENA_WIDTH = D_MODEL
HYENA_EMB = 33
HYENA_BANDS = (HYENA_EMB - 1) // 2
HYENA_SHIFT = 0.05
FFT_N2 = 256

RET_HEADS = 4
RET_HDK = D_MODEL // RET_HEADS
RET_HDV = 2 * D_MODEL // RET_HEADS
ROPE_BASE = 10000.0

CA_HEADS = 4
CA_HD = D_MODEL // CA_HEADS

LANES = 128
BF16_SUBLANES = 16
VMEM_LIMIT = 56 * 1024 * 1024

HI = lax.Precision.HIGHEST


def _params(*sem):
    return pltpu.CompilerParams(dimension_semantics=sem, vmem_limit_bytes=VMEM_LIMIT)


def _silu(x):
    return x * (1.0 / (1.0 + jnp.exp(-x)))


def _sigmoid(x):
    return 1.0 / (1.0 + jnp.exp(-x))


def _log_sigmoid(x):
    return jnp.minimum(x, 0.0) - jnp.log(1.0 + jnp.exp(-jnp.abs(x)))


def _rms_matmul_kernel(x_ref, g_ref, *refs, tn):
    nw = len(refs) // 2
    x = x_ref[...]
    xn = (x * lax.rsqrt(jnp.mean(x * x, axis=-1, keepdims=True) + EPS) * g_ref[...]).astype(BF16)
    for w_ref, o_ref in zip(refs[:nw], refs[nw:]):
        n = o_ref.shape[1]
        step = min(tn, n)
        for j in range(n // step):
            cols = slice(j * step, (j + 1) * step)
            o_ref[:, cols] = jnp.dot(xn, w_ref[:, cols], preferred_element_type=F32).astype(o_ref.dtype)


def rms_matmul(x, g, ws, out_dtypes, tm, tn=1024):
    m, d = x.shape
    w_specs = [pl.BlockSpec(w.shape, lambda i: (0, 0), pipeline_mode=pl.Buffered(1)) for w in ws]
    return pl.pallas_call(
        functools.partial(_rms_matmul_kernel, tn=tn),
        grid=(m // tm,),
        in_specs=[pl.BlockSpec((tm, d), lambda i: (i, 0)), pl.BlockSpec((1, d), lambda i: (0, 0))] + w_specs,
        out_specs=[pl.BlockSpec((tm, w.shape[1]), lambda i: (i, 0)) for w in ws],
        out_shape=[jax.ShapeDtypeStruct((m, w.shape[1]), dt) for w, dt in zip(ws, out_dtypes)],
        compiler_params=_params("parallel"),
        name="rms_matmul",
    )(x, g.reshape(1, d).astype(F32), *ws)


F32_SUBLANES = 8


def _even_proj_kernel(x_ref, xp_ref, xn_ref, g_ref, wc_ref, wr_ref, wg_ref, cw_ref, cb_ref, gb_ref,
                      qk_ref, xg_ref, t_ref, v_ref, ga_ref, cols_ref, *, tiles_per_seq, tn):
    i = pl.program_id(0)
    tm, d = x_ref.shape
    w, h = D_MODEL, F32_SUBLANES
    first = (i % tiles_per_seq) == 0
    last = (i % tiles_per_seq) == tiles_per_seq - 1
    g = g_ref[...]

    def norm(x):
        return x * lax.rsqrt(jnp.mean(x * x, axis=-1, keepdims=True) + EPS) * g

    xm = norm(x_ref[...])
    xp = jnp.where(first, 0.0, norm(xp_ref[...]))
    xn = jnp.where(last, 0.0, norm(xn_ref[...]))
    x_ext = jnp.concatenate([xp, xm, xn], axis=0).astype(BF16)
    xb = xm.astype(BF16)

    def conv(c0):
        cols = slice(c0, c0 + tn)
        r = jnp.dot(x_ext, wc_ref[:, cols], preferred_element_type=F32)
        t = r.shape[0]
        y = (pltpu.roll(r, 1, axis=0) * cw_ref[0:1, cols] + r * cw_ref[1:2, cols]
             + pltpu.roll(r, t - 1, axis=0) * cw_ref[2:3, cols] + cb_ref[:, cols])
        return y[h:h + tm]

    for j in range(2 * w // tn):
        qk_ref[:, j * tn:(j + 1) * tn] = _silu(conv(j * tn)).astype(qk_ref.dtype)
    rest = lambda group, j: jnp.dot(xb, wr_ref[:, group * w + j * tn:group * w + (j + 1) * tn],
                                    preferred_element_type=F32)
    for j in range(w // tn):
        cols = slice(j * tn, (j + 1) * tn)
        xg_ref[:, cols] = (conv(2 * w + j * tn) * _silu(rest(3, j))).astype(xg_ref.dtype)
        t_ref[:, cols] = (conv(3 * w + j * tn) * conv(4 * w + j * tn)).astype(t_ref.dtype)
        v_ref[:, cols] = rest(0, j).astype(v_ref.dtype)
        ga_ref[:, cols] = (_sigmoid(rest(1, j)) * _silu(rest(2, j))).astype(ga_ref.dtype)
    cols_ref[...] = _gate_prep(jnp.dot(xb, wg_ref[...], preferred_element_type=F32) + gb_ref[...])


def even_proj(x, g, w_conv, w_rest, w_gate, conv_w, conv_b, gate_bias, seq_len, tm=512, tn=512):
    m, d = x.shape
    w, h = D_MODEL, F32_SUBLANES
    tm = min(tm, seq_len)
    r = tm // h
    last_blk = m // h - 1
    res = lambda a: pl.BlockSpec(a.shape, lambda i: (0, 0), pipeline_mode=pl.Buffered(1))
    row = lambda n: pl.BlockSpec((tm, n), lambda i: (i, 0))
    return pl.pallas_call(
        functools.partial(_even_proj_kernel, tiles_per_seq=seq_len // tm, tn=tn),
        grid=(m // tm,),
        in_specs=[row(d),
                  pl.BlockSpec((h, d), lambda i: (jnp.maximum(i * r - 1, 0), 0)),
                  pl.BlockSpec((h, d), lambda i: (jnp.minimum((i + 1) * r, last_blk), 0)),
                  pl.BlockSpec((1, d), lambda i: (0, 0)),
                  res(w_conv), res(w_rest), res(w_gate), res(conv_w), res(conv_b), res(gate_bias)],
        out_specs=[row(2 * w), row(w), row(w), row(w), row(w), row(LANES)],
        out_shape=[jax.ShapeDtypeStruct((m, 2 * w), BF16), jax.ShapeDtypeStruct((m, w), BF16),
                   jax.ShapeDtypeStruct((m, w), BF16), jax.ShapeDtypeStruct((m, w), BF16),
                   jax.ShapeDtypeStruct((m, w), BF16), jax.ShapeDtypeStruct((m, LANES), F32)],
        compiler_params=_params("parallel"),
        name="even_proj",
    )(x, x, x, g.reshape(1, d).astype(F32), w_conv, w_rest, w_gate, conv_w, conv_b, gate_bias)


def _seg_scan(x, op, identity, reverse):
    t = x.shape[0]
    r = lax.broadcasted_iota(jnp.int32, x.shape, 0) % CHUNK
    k = 1
    while k < CHUNK:
        if reverse:
            shifted = pltpu.roll(x, t - k, axis=0)
            valid = r < CHUNK - k
        else:
            shifted = pltpu.roll(x, k, axis=0)
            valid = r >= k
        x = op(x, jnp.where(valid, shifted, identity))
        k *= 2
    return x


GATE_A_LANE = 16
GATE_AMAX_LANE = 32


def _gate_lane(d, h):
    return d * 2 * MLSTM_HEADS + h


def _gate_prep(g):
    nh = MLSTM_HEADS
    lane = lax.broadcasted_iota(jnp.int32, g.shape, 1)
    live = (lane < 4 * nh) & (lane % (2 * nh) < nh)
    fwd = lane < 2 * nh
    lf = pltpu.roll(_log_sigmoid(g), LANES - nh, axis=1)
    bcum = jnp.where(fwd, _seg_scan(lf, jnp.add, 0.0, False), _seg_scan(lf, jnp.add, 0.0, True))
    a = g - bcum
    amax = jnp.where(fwd, _seg_scan(a, jnp.maximum, -jnp.inf, False),
                     _seg_scan(a, jnp.maximum, -jnp.inf, True))
    keep = lambda v: jnp.where(live, v, 0.0)
    return (keep(bcum) + pltpu.roll(keep(a), GATE_A_LANE, axis=1)
            + pltpu.roll(keep(amax), GATE_AMAX_LANE, axis=1))


def _mlstm_kernel(qf_ref, kf_ref, vf_ref, cf_ref, rf_ref, qb_ref, kb_ref, vb_ref, cb_ref, rb_ref,
                  hf_ref, hb_ref, c_st, cb_st, m_st):
    @pl.when(pl.program_id(1) == 0)
    def _():
        c_st[...] = jnp.zeros_like(c_st)
        cb_st[...] = jnp.zeros_like(cb_st)
        m_st[...] = jnp.zeros_like(m_st)

    nh, hd = MLSTM_HEADS, MLSTM_HD
    t_idx = lax.broadcasted_iota(jnp.int32, (CHUNK, CHUNK), 0)
    s_idx = lax.broadcasted_iota(jnp.int32, (CHUNK, CHUNK), 1)
    scale = hd ** -0.5
    ones_blk = jnp.ones((CHUNK, LANES), BF16)
    rep = lambda x, n: jnp.concatenate([x] * n, axis=1)
    lane_dense = lambda col: jnp.broadcast_to(col, (CHUNK, LANES))
    dirs = ((qf_ref, kf_ref, vf_ref, cf_ref, rf_ref, hf_ref, False),
            (qb_ref, kb_ref, vb_ref, cb_ref, rb_ref, hb_ref, True))
    for d, (q_ref, k_ref, v_ref, col_ref, row_ref, o_ref, reverse) in enumerate(dirs):
        mask = (s_idx >= t_idx) if reverse else (s_idx <= t_idx)
        last = 0 if reverse else CHUNK - 1
        for h in range(nh):
            j = d * nh + h
            hs = slice(h * hd, (h + 1) * hd)
            q = q_ref[0, :, hs]
            k = k_ref[0, :, hs] * scale
            v_aug = jnp.concatenate([v_ref[0, :, hs], ones_blk], axis=1)
            gl = _gate_lane(d, h)
            bc = lane_dense(col_ref[0, :, gl:gl + 1])
            a_c = lane_dense(col_ref[0, :, GATE_A_LANE + gl:GATE_A_LANE + gl + 1])
            amax = lane_dense(col_ref[0, :, GATE_AMAX_LANE + gl:GATE_AMAX_LANE + gl + 1])
            a_r = row_ref[0, gl:gl + 1, :]
            dmat = jnp.exp(jnp.where(mask, a_r - rep(amax, CHUNK // LANES), -jnp.inf))
            s = lax.dot_general(q, k, (((1,), (1,)), ((), ())), preferred_element_type=F32) * dmat
            nd_l = jnp.dot(s.astype(BF16), v_aug, preferred_element_type=F32)
            a_last = amax[last:last + 1, :]
            btot = bc[last:last + 1, :]
            kw = rep(jnp.exp(a_c - a_last), hd // LANES) * k.astype(F32)
            upd = lax.dot_general(kw.astype(BF16), v_aug, (((0,), (0,)), ((), ())),
                                  preferred_element_type=F32)

            m_prev = m_st[j, 0:1, :]
            mt = jnp.maximum(amax, m_prev)
            f_l = jnp.exp(amax - mt)
            sc = jnp.exp(m_prev - mt)
            nd_c = jnp.dot(q, cb_st[j], preferred_element_type=F32)
            den = f_l * nd_l[:, hd:] + sc * nd_c[:, hd:]
            inv = 1.0 / jnp.maximum(jnp.abs(den), jnp.exp(-(bc + mt)))
            num = rep(f_l, hd // LANES) * nd_l[:, :hd] + rep(sc, hd // LANES) * nd_c[:, :hd]
            o_ref[0, :, hs] = (num * rep(inv, hd // LANES)).astype(o_ref.dtype)

            m_last = jnp.maximum(a_last, m_prev)
            dec = jnp.exp(m_prev - m_last)
            f_u = jnp.exp(a_last - m_last)
            wide = (hd + LANES) // LANES
            c_new = rep(dec, wide) * c_st[j] + rep(f_u, wide) * upd
            c_st[j] = c_new
            cb_st[j] = c_new.astype(BF16)
            m_st[j] = jnp.broadcast_to(btot + m_last, m_st.shape[1:])


def mlstm_scan(qk, proj, v_col, cols, rows, out_dtype):
    bsz, l, _ = qk.shape
    w = D_MODEL
    nc = l // CHUNK
    vb = v_col // w
    fwd = lambda cb: (lambda b, c: (b, c, cb))
    bwd = lambda cb: (lambda b, c: (b, nc - 1 - c, cb))
    blk = lambda im: pl.BlockSpec((1, CHUNK, w), im)
    ncol = cols.shape[-1]
    in_specs = []
    for mk in (fwd, bwd):
        in_specs += [blk(mk(0)), blk(mk(1)), blk(mk(vb)),
                     pl.BlockSpec((1, CHUNK, ncol), mk(0)),
                     pl.BlockSpec((1, rows.shape[1], CHUNK),
                                  (lambda b, c: (b, 0, c)) if mk is fwd else (lambda b, c: (b, 0, nc - 1 - c)))]
    nst = 2 * MLSTM_HEADS
    shp = jax.ShapeDtypeStruct((bsz, l, w), out_dtype)
    return pl.pallas_call(
        _mlstm_kernel,
        grid=(bsz, nc),
        in_specs=in_specs,
        out_specs=[blk(fwd(0)), blk(bwd(0))],
        out_shape=[shp, shp],
        scratch_shapes=[pltpu.VMEM((nst, MLSTM_HD, MLSTM_HD + LANES), F32),
                        pltpu.VMEM((nst, MLSTM_HD, MLSTM_HD + LANES), BF16),
                        pltpu.VMEM((nst, 8, LANES), F32)],
        compiler_params=_params("parallel", "arbitrary"),
        name="mlstm_scan",
    )(qk, qk, proj, cols, rows, qk, qk, proj, cols, rows)


def _rope_table_kernel(inv_ref, cos_ref, sin_ref):
    tl = cos_ref.shape[0]
    pos = (lax.broadcasted_iota(jnp.int32, cos_ref.shape, 0) + pl.program_id(0) * tl).astype(F32)
    ang = pos * inv_ref[...]
    cos_ref[...] = jnp.cos(ang)
    sin_ref[...] = jnp.sin(ang)


def rope_tables(l, tl=1024):
    half = RET_HDK // 2
    tl = min(tl, l)
    inv = (ROPE_BASE ** (-np.arange(0, RET_HDK, 2, dtype=np.float32) / RET_HDK)).astype(np.float32)
    spec = pl.BlockSpec((tl, half), lambda i: (i, 0))
    shp = jax.ShapeDtypeStruct((l, half), F32)
    return pl.pallas_call(
        _rope_table_kernel,
        grid=(l // tl,),
        in_specs=[pl.BlockSpec((1, half), lambda i: (0, 0))],
        out_specs=[spec, spec],
        out_shape=[shp, shp],
        compiler_params=_params("parallel"),
        name="rope_tables",
    )(jnp.asarray(inv).reshape(1, half))


def _rope(x, cos, sin):
    half = x.shape[-1] // 2
    x1, x2 = x[:, :half], x[:, half:]
    return jnp.concatenate([x1 * cos - x2 * sin, x1 * sin + x2 * cos], axis=-1)


def _odd_proj_kernel(x_ref, g_ref, w_ref, cos_ref, sin_ref, qk_ref, v_ref, gate_ref, *, tn):
    x = x_ref[...]
    xb = (x * lax.rsqrt(jnp.mean(x * x, axis=-1, keepdims=True) + EPS) * g_ref[...]).astype(BF16)
    cos, sin = cos_ref[...], sin_ref[...]
    nh, dk = RET_HEADS, RET_HDK
    for j in range(2 * nh):
        r = jnp.dot(xb, w_ref[:, j * dk:(j + 1) * dk], preferred_element_type=F32)
        r = _rope(r, cos, sin)
        if j < nh:
            r = r * dk ** -0.5
        qk_ref[:, j * dk:(j + 1) * dk] = r.astype(qk_ref.dtype)
    c0 = 2 * nh * dk
    vw = v_ref.shape[1]
    for j in range(vw // tn):
        cols = slice(j * tn, (j + 1) * tn)
        v_ref[:, cols] = jnp.dot(xb, w_ref[:, c0 + j * tn:c0 + (j + 1) * tn],
                                 preferred_element_type=F32).astype(v_ref.dtype)
        gate = jnp.dot(xb, w_ref[:, c0 + vw + j * tn:c0 + vw + (j + 1) * tn], preferred_element_type=F32)
        gate_ref[:, cols] = _silu(gate).astype(gate_ref.dtype)


def odd_proj(x, g, w, cos, sin, seq_len, tm=512, tn=512):
    m, d = x.shape
    qw, vw = RET_HEADS * RET_HDK, RET_HEADS * RET_HDV
    half = RET_HDK // 2
    tm = min(tm, seq_len)
    tps = seq_len // tm
    row = lambda n: pl.BlockSpec((tm, n), lambda i: (i, 0))
    tab = pl.BlockSpec((tm, half), lambda i: (i % tps, 0))
    return pl.pallas_call(
        functools.partial(_odd_proj_kernel, tn=tn),
        grid=(m // tm,),
        in_specs=[row(d), pl.BlockSpec((1, d), lambda i: (0, 0)),
                  pl.BlockSpec(w.shape, lambda i: (0, 0), pipeline_mode=pl.Buffered(1)), tab, tab],
        out_specs=[row(2 * qw), row(vw), row(vw)],
        out_shape=[jax.ShapeDtypeStruct((m, 2 * qw), BF16), jax.ShapeDtypeStruct((m, vw), BF16),
                   jax.ShapeDtypeStruct((m, vw), BF16)],
        compiler_params=_params("parallel"),
        name="odd_proj",
    )(x, g.reshape(1, d).astype(F32), w, cos, sin)


def _retention_kernel(qf_ref, kf_ref, vf_ref, qb_ref, kb_ref, vb_ref,
                      logit_ref, of_ref, ob_ref, r_st, rb_st):
    @pl.when(pl.program_id(1) == 0)
    def _():
        r_st[...] = jnp.zeros_like(r_st)
        rb_st[...] = jnp.zeros_like(rb_st)

    nh, dk, dv = RET_HEADS, RET_HDK, RET_HDV
    t_idx = lax.broadcasted_iota(jnp.int32, (CHUNK, CHUNK), 0)
    s_idx = lax.broadcasted_iota(jnp.int32, (CHUNK, CHUNK), 1)
    pos = lax.broadcasted_iota(jnp.int32, (CHUNK, LANES), 0).astype(F32)
    rep = lambda x, n: jnp.concatenate([x] * n, axis=1)
    dirs = ((qf_ref, kf_ref, vf_ref, of_ref, False), (qb_ref, kb_ref, vb_ref, ob_ref, True))
    for d, (q_ref, k_ref, v_ref, o_ref, reverse) in enumerate(dirs):
        rel = ((s_idx - t_idx) if reverse else (t_idx - s_idx)).astype(F32)
        for h in range(nh):
            j = d * nh + h
            lg = _log_sigmoid(logit_ref[j])[0:1, :]
            dmask = jnp.where(rel >= 0, jnp.exp(rep(lg, CHUNK // LANES) * jnp.maximum(rel, 0.0)), 0.0)
            if reverse:
                q_dec = jnp.exp(lg * (CHUNK - pos))
                k_dec = jnp.exp(lg * pos)
            else:
                q_dec = jnp.exp(lg * (pos + 1.0))
                k_dec = jnp.exp(lg * (CHUNK - 1.0 - pos))
            c_dec = jnp.exp(lg * CHUNK)
            q = q_ref[0, :, h * dk:(h + 1) * dk]
            k = k_ref[0, :, h * dk:(h + 1) * dk]
            v = v_ref[0, :, h * dv:(h + 1) * dv]
            s = lax.dot_general(q, k, (((1,), (1,)), ((), ())), preferred_element_type=F32) * dmask
            out = jnp.dot(s.astype(BF16), v, preferred_element_type=F32)
            out = out + jnp.dot((q.astype(F32) * rep(q_dec, dk // LANES)).astype(BF16), rb_st[j],
                                preferred_element_type=F32)
            o_ref[0, :, h * dv:(h + 1) * dv] = out.astype(o_ref.dtype)
            upd = lax.dot_general((k.astype(F32) * rep(k_dec, dk // LANES)).astype(BF16), v,
                                  (((0,), (0,)), ((), ())), preferred_element_type=F32)
            r_new = rep(c_dec, dv // LANES) * r_st[j] + upd
            r_st[j] = r_new
            rb_st[j] = r_new.astype(BF16)


def retention_scan(qk, v, logit_b, out_dtype):
    bsz, l, _ = qk.shape
    nc = l // CHUNK
    qw, vw = RET_HEADS * RET_HDK, RET_HEADS * RET_HDV
    in_specs = []
    for rev in (False, True):
        ci = (lambda c: nc - 1 - c) if rev else (lambda c: c)
        in_specs += [pl.BlockSpec((1, CHUNK, qw), lambda b, c, ci=ci: (b, ci(c), 0)),
                     pl.BlockSpec((1, CHUNK, qw), lambda b, c, ci=ci: (b, ci(c), 1)),
                     pl.BlockSpec((1, CHUNK, vw), lambda b, c, ci=ci: (b, ci(c), 0))]
    in_specs.append(pl.BlockSpec((2 * RET_HEADS, 8, LANES), lambda b, c: (0, 0, 0)))
    shp = jax.ShapeDtypeStruct((bsz, l, vw), out_dtype)
    return pl.pallas_call(
        _retention_kernel,
        grid=(bsz, nc),
        in_specs=in_specs,
        out_specs=[pl.BlockSpec((1, CHUNK, vw), lambda b, c: (b, c, 0)),
                   pl.BlockSpec((1, CHUNK, vw), lambda b, c: (b, nc - 1 - c, 0))],
        out_shape=[shp, shp],
        scratch_shapes=[pltpu.VMEM((2 * RET_HEADS, RET_HDK, RET_HDV), F32),
                        pltpu.VMEM((2 * RET_HEADS, RET_HDK, RET_HDV), BF16)],
        compiler_params=_params("parallel", "arbitrary"),
        name="retention_scan",
    )(qk, qk, v, qk, qk, v, logit_b)


def _hyena_filter_kernel(cst_ref, w1_ref, w2_ref, w3a_ref, w3b_ref, delta_ref, kern_ref, asum_ref, *, l):
    i = pl.program_id(0)
    tr = kern_ref.shape[0]
    hr = tr // 2
    half = LANES // 2
    row = lax.broadcasted_iota(jnp.int32, (hr, LANES), 0) + i * tr
    n_a, n_b = row, row + hr
    lag = lambda n: jnp.where(n < l, n, 2 * l - n).astype(F32)
    tt_a, tt_b = lag(n_a) / (l - 1.0), lag(n_b) / (l - 1.0)
    lo = lax.broadcasted_iota(jnp.int32, (hr, LANES), 1) < half
    p2 = jnp.where(lo, lag(n_a), lag(n_b))
    tt2 = jnp.where(lo, tt_a, tt_b)
    bands, phase, w1t, b1, f1, b2, f2 = (cst_ref[k:k + 1, :] for k in range(7))
    ang = (2.0 * math.pi / l) * bands * p2 + phase
    pre = jnp.dot(jnp.cos(ang), w1_ref[...], preferred_element_type=F32, precision=HI) + tt2 * w1t
    z = jnp.sin(f1 * (pre + b1))
    z = jnp.sin(f2 * (jnp.dot(z, w2_ref[...], preferred_element_type=F32, precision=HI) + b2))
    rep = lambda x: jnp.concatenate([x] * (kern_ref.shape[1] // LANES), axis=1)
    dabs = jnp.abs(delta_ref[...])
    total = jnp.zeros((1, kern_ref.shape[1]), F32)
    z_hi = z.astype(BF16)
    z_lo = (z - z_hi.astype(F32)).astype(BF16)
    z3 = jnp.concatenate([z_hi, z_lo, z_hi], axis=1)
    for part, (w3_ref, n, tt) in enumerate(((w3a_ref, n_a, tt_a), (w3b_ref, n_b, tt_b))):
        hk = jnp.dot(z3, w3_ref[...], preferred_element_type=F32)
        hk = hk * (jnp.exp(-rep(tt) * dabs) + HYENA_SHIFT)
        hk = jnp.where(rep(n) == l, 0.0, hk)
        kern_ref[part * hr:(part + 1) * hr, :] = hk
        total = total + jnp.sum(jnp.abs(hk), axis=0, keepdims=True)

    @pl.when(i == 0)
    def _():
        asum_ref[...] = jnp.zeros_like(asum_ref)

    asum_ref[...] += jnp.broadcast_to(total, asum_ref.shape)


def hyena_filter(l, w1, b1, f1, w2, b2, f2, w3, delta, tr=512):
    wd = HYENA_WIDTH
    hid = w1.shape[1]
    half = LANES // 2
    nb = HYENA_BANDS
    tr = min(tr, l)
    w1, w2, w3 = w1.astype(F32), w2.astype(F32), w3.astype(F32)
    two = lambda v: jnp.tile(jnp.pad(v.astype(F32).reshape(1, -1), ((0, 0), (0, half - v.shape[-1]))), (1, 2))
    bands = np.zeros((1, half), np.float32)
    bands[0, :nb] = bands[0, nb:2 * nb] = np.linspace(1e-4, nb - 1, nb, dtype=np.float32)
    phase = np.zeros((1, half), np.float32)
    phase[0, nb:2 * nb] = 0.5 * np.pi
    cst = jnp.concatenate([two(jnp.asarray(bands)), two(jnp.asarray(phase)), two(w1[0]), two(b1), two(f1),
                           two(b2), two(f2), jnp.zeros((1, LANES), F32)], axis=0)
    blk = lambda a: jnp.pad(a, ((0, half - a.shape[0]), (0, half - a.shape[1])))
    diag2 = lambda a: jnp.concatenate([jnp.pad(blk(a), ((0, 0), (0, half))),
                                       jnp.pad(blk(a), ((0, 0), (half, 0)))], axis=0)
    w3h = jnp.pad(w3, ((0, half - hid), (0, 0)))

    def split3(a):
        hi = a.astype(BF16)
        lo = (a - hi.astype(F32)).astype(BF16)
        return jnp.concatenate([hi, hi, lo], axis=0)

    w3a = split3(jnp.pad(w3h, ((0, half), (0, 0))))
    w3b = split3(jnp.pad(w3h, ((half, 0), (0, 0))))
    full = lambda shape: pl.BlockSpec(shape, lambda i: (0, 0))
    half_sel = lambda i: (0, (i * tr) // l)
    return pl.pallas_call(
        functools.partial(_hyena_filter_kernel, l=l),
        grid=(2 * l // tr,),
        in_specs=[full((8, LANES)), full((LANES, LANES)), full((LANES, LANES)),
                  pl.BlockSpec((3 * LANES, wd), half_sel), pl.BlockSpec((3 * LANES, wd), half_sel),
                  pl.BlockSpec((1, wd), half_sel)],
        out_specs=[pl.BlockSpec((tr, wd), lambda i: (i, 0)), pl.BlockSpec((8, wd), lambda i: (0, 0))],
        out_shape=[jax.ShapeDtypeStruct((2 * l, wd), F32), jax.ShapeDtypeStruct((8, wd), F32)],
        compiler_params=_params("arbitrary"),
        name="hyena_filter",
    )(cst, diag2(w1[1:]), diag2(w2), w3a, w3b, delta.reshape(1, 2 * wd).astype(F32))


def _dft_consts(n1, n2):
    n = n1 * n2
    k1 = np.arange(n1)
    f1 = np.exp(-2j * np.pi * np.outer(k1, k1) / n1)
    k2 = np.arange(n2)
    f2 = np.exp(-2j * np.pi * np.outer(k2, k2) / n2)
    tw = np.exp(-2j * np.pi * np.outer(k1, k2) / n)
    g = f2[None, :, :] * tw[:, None, :]
    ginv = np.conj(np.transpose(g, (0, 2, 1))) / n
    f1inv = np.conj(f1.T)
    return f1, g, ginv, f1inv


def _stack(c):
    return jnp.asarray(np.concatenate([c.real, c.imag], axis=-2).astype(np.float32)).astype(BF16)


def _cdot(fs, xr, xi, rows):
    p = jnp.dot(fs, xr, preferred_element_type=F32)
    if xi is None:
        return p[:rows], p[rows:]
    q = jnp.dot(fs, xi, preferred_element_type=F32)
    return p[:rows] - q[rows:], q[:rows] + p[rows:]


FFT_ROW_CHUNK = 512


def _kron_eye(f, s_blk):
    k = np.kron(f, np.eye(s_blk))
    to = lambda a: jnp.asarray(a.astype(np.float32)).astype(BF16)
    return to(k.real), to(k.imag)


def _strided_stage_kernel(*refs, has_imag_in, n_out):
    mr_ref, mi_ref, xr_ref = refs[:3]
    xi_ref = refs[3] if has_imag_in else None
    n_in = 4 if has_imag_in else 3
    if len(refs[n_in].shape) == 5:
        outs = [refs[n_in].at[part, 0] for part in range(n_out)]
    else:
        outs = [r.at[0] for r in refs[n_in:]]
    flat = lambda r: r.reshape(r.shape[0] * r.shape[1], r.shape[2])
    xr = flat(xr_ref[0]).astype(BF16)
    xi = flat(xi_ref[0]).astype(BF16) if has_imag_in else None
    rows = mr_ref.shape[0]
    rc = min(FFT_ROW_CHUNK, rows)
    s_blk = xr_ref.shape[2]
    for c0 in range(0, rows, rc):
        mr = mr_ref[c0:c0 + rc, :]
        mi = mi_ref[c0:c0 + rc, :]
        re = jnp.dot(mr, xr, preferred_element_type=F32)
        im = jnp.dot(mi, xr, preferred_element_type=F32) if n_out == 2 else None
        if has_imag_in:
            re = re - jnp.dot(mi, xi, preferred_element_type=F32)
            if n_out == 2:
                im = im + jnp.dot(mr, xi, preferred_element_type=F32)
        k0, k1 = c0 // s_blk, (c0 + rc) // s_blk
        for o_ref, val in zip(outs, (re, im)):
            o_ref[k0:k1] = val.reshape(rc // s_blk, s_blk, val.shape[1]).astype(o_ref.dtype)


def fft_stage1(x, n1, rows_in, pairs, complex_in, s_blk, out_dtype):
    c = x.shape[-1]
    n2 = FFT_N2
    xv = x.reshape(x.shape[0], rows_in, n2, c)
    mr, mi = _kron_eye(_dft_consts(n1, n2)[0][:, :rows_in], s_blk)
    mat = pl.BlockSpec(mr.shape, lambda p, j: (0, 0), pipeline_mode=pl.Buffered(1))
    in_specs = [mat, mat, pl.BlockSpec((1, rows_in, s_blk, c), lambda p, j: (p, 0, j, 0))]
    args = [mr, mi, xv]
    if complex_in:
        in_specs.append(pl.BlockSpec((1, rows_in, s_blk, c), lambda p, j: (p + pairs, 0, j, 0)))
        args.append(xv)
    out_spec = pl.BlockSpec((1, n1, s_blk, c), lambda p, j: (p, 0, j, 0))
    shp = jax.ShapeDtypeStruct((pairs, n1, n2, c), out_dtype)
    return pl.pallas_call(
        functools.partial(_strided_stage_kernel, has_imag_in=complex_in, n_out=2),
        grid=(pairs, n2 // s_blk),
        in_specs=in_specs,
        out_specs=[out_spec, out_spec],
        out_shape=[shp, shp],
        compiler_params=_params("parallel", "parallel"),
        name="fft_stage1",
    )(*args)


def _fft_mid_kernel(*refs, conv):
    n2 = FFT_N2
    if conv:
        gs_ref, gis_ref, ar_ref, ai_ref, kr_ref, ki_ref, br_ref, bi_ref = refs
    else:
        gs_ref, ar_ref, ai_ref, br_ref, bi_ref = refs
    xr, xi = _cdot(gs_ref[0], ar_ref[0, 0].astype(BF16), ai_ref[0, 0].astype(BF16), n2)
    if conv:
        kr, ki = kr_ref[0].astype(F32), ki_ref[0].astype(F32)
        yr = xr * kr - xi * ki
        yi = xr * ki + xi * kr
        xr, xi = _cdot(gis_ref[0], yr.astype(BF16), yi.astype(BF16), n2)
    br_ref[0, 0] = xr.astype(br_ref.dtype)
    bi_ref[0, 0] = xi.astype(bi_ref.dtype)


def fft_mid(ar, ai, n1, c, kf=None, ct=1024):
    n2 = FFT_N2
    pairs = ar.shape[0]
    _, g, ginv, _ = _dft_consts(n1, n2)
    a4 = lambda a: a
    mat = pl.BlockSpec((1, 2 * n2, n2), lambda k, cb, p: (k, 0, 0))
    dat = pl.BlockSpec((1, 1, n2, ct), lambda k, cb, p: (p, k, 0, cb))
    in_specs = [mat]
    args = [_stack(g)]
    if kf is not None:
        in_specs.append(mat)
        args.append(_stack(ginv))
    in_specs += [dat, dat]
    args += [a4(ar), a4(ai)]
    if kf is not None:
        fil = pl.BlockSpec((1, n2, ct), lambda k, cb, p: (k, 0, cb))
        in_specs += [fil, fil]
        args += [kf[0], kf[1]]
    shp = jax.ShapeDtypeStruct((pairs, n1, n2, c), BF16)
    br, bi = pl.pallas_call(
        functools.partial(_fft_mid_kernel, conv=kf is not None),
        grid=(n1, c // ct, pairs),
        in_specs=in_specs,
        out_specs=[dat, dat],
        out_shape=[shp, shp],
        compiler_params=_params("parallel", "parallel", "arbitrary"),
        name="fft_mid",
    )(*args)
    return br, bi


def fft_stage1_inv(br, bi, n1, rows_out, c, complex_out, s_blk):
    pairs = br.shape[0]
    n2 = FFT_N2
    mr, mi = _kron_eye(_dft_consts(n1, n2)[3][:rows_out, :], s_blk)
    parts = 2 if complex_out else 1
    mat = pl.BlockSpec(mr.shape, lambda p, j: (0, 0), pipeline_mode=pl.Buffered(1))
    dat = pl.BlockSpec((1, n1, s_blk, c), lambda p, j: (p, 0, j, 0))
    y = pl.pallas_call(
        functools.partial(_strided_stage_kernel, has_imag_in=True, n_out=parts),
        grid=(pairs, n2 // s_blk),
        in_specs=[mat, mat, dat, dat],
        out_specs=pl.BlockSpec((parts, 1, rows_out, s_blk, c), lambda p, j: (0, p, 0, j, 0)),
        out_shape=jax.ShapeDtypeStruct((parts, pairs, rows_out, n2, c), BF16),
        compiler_params=_params("parallel", "parallel"),
        name="fft_stage1_inv",
    )(mr, mi, br, bi)
    return y.reshape(parts * pairs, rows_out * n2, c)


def hyena_long_conv(t, kern):
    bsz, l, c = t.shape
    n1 = 2 * l // FFT_N2
    kr, ki = fft_stage1(kern[None], n1, n1, 1, False, 8, F32)
    kr, ki = fft_mid(kr, ki, n1, c)
    kf = (kr.reshape(n1, FFT_N2, c), ki.reshape(n1, FFT_N2, c))
    complex_in = bsz % 2 == 0
    pairs = bsz // 2 if complex_in else bsz
    ar, ai = fft_stage1(t, n1, n1 // 2, pairs, complex_in, BF16_SUBLANES, BF16)
    br, bi = fft_mid(ar, ai, n1, c, kf)
    return fft_stage1_inv(br, bi, n1, n1 // 2, c, complex_in, BF16_SUBLANES)


def _head_ln(x, nheads, g):
    hd = x.shape[-1] // nheads
    outs = []
    for h in range(nheads):
        seg = x[:, h * hd:(h + 1) * hd]
        mu = jnp.mean(seg, axis=-1, keepdims=True)
        cen = seg - mu
        var = jnp.mean(cen * cen, axis=-1, keepdims=True)
        outs.append(cen * lax.rsqrt(var + GN_EPS))
    return jnp.concatenate(outs, axis=-1) * g


def _even_out_kernel(x_ref, hf_ref, hb_ref, ga_ref, xg_ref, y_ref, t_ref,
                     g_ref, skip_ref, asum_ref, w_ref, out_ref):
    wd = D_MODEL
    ln = _head_ln(hf_ref[...].astype(F32) + hb_ref[...].astype(F32), MLSTM_HEADS, g_ref[...])
    ya = ln * ga_ref[...].astype(F32)
    conv = y_ref[...].astype(F32) * (1.0 / asum_ref[0:1, :])
    yb = xg_ref[...].astype(F32) * (conv + skip_ref[...] * t_ref[...].astype(F32))
    mix = jnp.dot(ya.astype(BF16), w_ref[0:wd, :], preferred_element_type=F32)
    mix = mix + jnp.dot(yb.astype(BF16), w_ref[wd:2 * wd, :], preferred_element_type=F32)
    out_ref[...] = x_ref[...] + mix


def even_out(x, hf, hb, ga, xg, y, t, g, skip, asum, w_out, tm=512):
    m, d = x.shape
    tm = min(tm, m)
    row = pl.BlockSpec((tm, d), lambda i: (i, 0))
    vec = pl.BlockSpec((1, d), lambda i: (0, 0))
    return pl.pallas_call(
        _even_out_kernel,
        grid=(m // tm,),
        in_specs=[row] * 7 + [vec, vec, pl.BlockSpec((8, d), lambda i: (0, 0)),
                              pl.BlockSpec((2 * d, d), lambda i: (0, 0), pipeline_mode=pl.Buffered(1))],
        out_specs=row,
        out_shape=jax.ShapeDtypeStruct((m, d), F32),
        compiler_params=_params("parallel"),
        name="even_out",
    )(x, hf, hb, ga, xg, y, t, g.reshape(1, d).astype(F32), skip.reshape(1, d).astype(F32), asum, w_out)


def _odd_out_kernel(x_ref, of_ref, ob_ref, gate_ref, g_ref, w_ref, out_ref):
    o = _head_ln(of_ref[...].astype(F32) + ob_ref[...].astype(F32), RET_HEADS, g_ref[...])
    y = gate_ref[...].astype(F32) * o
    out_ref[...] = x_ref[...] + jnp.dot(y.astype(BF16), w_ref[...], preferred_element_type=F32)


def odd_out(x, of, ob, gate, g, w_out, tm=512):
    m, d = x.shape
    vw = RET_HEADS * RET_HDV
    tm = min(tm, m)
    wide = lambda cb: pl.BlockSpec((tm, vw), lambda i, cb=cb: (i, cb))
    return pl.pallas_call(
        _odd_out_kernel,
        grid=(m // tm,),
        in_specs=[pl.BlockSpec((tm, d), lambda i: (i, 0)), wide(0), wide(0), wide(0),
                  pl.BlockSpec((1, vw), lambda i: (0, 0)),
                  pl.BlockSpec((vw, d), lambda i: (0, 0), pipeline_mode=pl.Buffered(1))],
        out_specs=pl.BlockSpec((tm, d), lambda i: (i, 0)),
        out_shape=jax.ShapeDtypeStruct((m, d), F32),
        compiler_params=_params("parallel"),
        name="odd_out",
    )(x, of, ob, gate, g.reshape(1, vw).astype(F32), w_out)


def _ca_fold_kernel(kv_ref, wq_ref, wo_ref, wqk_ref, vwo_ref):
    d = D_MODEL
    for h in range(CA_HEADS):
        hs = slice(h * CA_HD, (h + 1) * CA_HD)
        k = kv_ref[0, :, hs]
        v = kv_ref[0, :, d + h * CA_HD:d + (h + 1) * CA_HD]
        nm = k.shape[0]
        wqk = lax.dot_general(wq_ref[:, hs], k, (((1,), (1,)), ((), ())), preferred_element_type=F32)
        wqk_ref[0, :, h * nm:(h + 1) * nm] = (wqk * CA_HD ** -0.5).astype(wqk_ref.dtype)
        vwo_ref[0, h * nm:(h + 1) * nm, :] = jnp.dot(v, wo_ref[hs, :],
                                                     preferred_element_type=F32).astype(vwo_ref.dtype)


def ca_fold(kv, wq, wo):
    bsz, nm, d2 = kv.shape
    d = d2 // 2
    mat = pl.BlockSpec((d, d), lambda b: (0, 0))
    return pl.pallas_call(
        _ca_fold_kernel,
        grid=(bsz,),
        in_specs=[pl.BlockSpec((1, nm, d2), lambda b: (b, 0, 0)), mat, mat],
        out_specs=[pl.BlockSpec((1, d, CA_HEADS * nm), lambda b: (b, 0, 0)),
                   pl.BlockSpec((1, CA_HEADS * nm, d), lambda b: (b, 0, 0))],
        out_shape=[jax.ShapeDtypeStruct((bsz, d, CA_HEADS * nm), BF16),
                   jax.ShapeDtypeStruct((bsz, CA_HEADS * nm, d), BF16)],
        compiler_params=_params("parallel"),
        name="ca_fold",
    )(kv, wq, wo)


def _cross_attn_kernel(x_ref, wqk_ref, vwo_ref, gq_ref, gf_ref, out_ref, *, final_norm):
    x = x_ref[0]
    xn = x * lax.rsqrt(jnp.mean(x * x, axis=-1, keepdims=True) + EPS) * gq_ref[...]
    s_all = jnp.dot(xn.astype(BF16), wqk_ref[0], preferred_element_type=F32)
    nm = s_all.shape[1] // CA_HEADS
    probs = []
    for h in range(CA_HEADS):
        s = s_all[:, h * nm:(h + 1) * nm]
        p = jnp.exp(s - jnp.max(s, axis=-1, keepdims=True))
        probs.append((p * (1.0 / jnp.sum(p, axis=-1, keepdims=True))).astype(BF16))
    y = x + jnp.dot(jnp.concatenate(probs, axis=-1), vwo_ref[0], preferred_element_type=F32)
    if final_norm:
        y = y * lax.rsqrt(jnp.mean(y * y, axis=-1, keepdims=True) + EPS) * gf_ref[...]
    out_ref[0] = y


def cross_attn(x, wqk, vwo, gq, gf, final_norm, tm=512):
    bsz, l, d = x.shape
    tm = min(tm, l)
    vec = pl.BlockSpec((1, d), lambda b, i: (0, 0))
    return pl.pallas_call(
        functools.partial(_cross_attn_kernel, final_norm=final_norm),
        grid=(bsz, l // tm),
        in_specs=[pl.BlockSpec((1, tm, d), lambda b, i: (b, i, 0)),
                  pl.BlockSpec((1,) + wqk.shape[1:], lambda b, i: (b, 0, 0)),
                  pl.BlockSpec((1,) + vwo.shape[1:], lambda b, i: (b, 0, 0)),
                  vec, vec],
        out_specs=pl.BlockSpec((1, tm, d), lambda b, i: (b, i, 0)),
        out_shape=jax.ShapeDtypeStruct((bsz, l, d), F32),
        compiler_params=_params("parallel", "parallel"),
        name="cross_attn",
    )(x, wqk, vwo, gq.reshape(1, d).astype(F32), gf.reshape(1, d).astype(F32))


def _even_mixer(x, p, i):
    bsz, l, d = x.shape
    w = D_MODEL
    nh = MLSTM_HEADS
    m = bsz * l
    g_mix = p['norm_mix_g_layer']
    w_in = p['even_w_in'][i]
    gate0 = 5 * w
    hy0 = gate0 + N_GATE_COLS
    w_conv = jnp.concatenate([w_in[:, :2 * w], w_in[:, hy0:hy0 + 3 * w]], axis=1).astype(BF16)
    w_rest = jnp.concatenate([w_in[:, 2 * w:gate0], w_in[:, hy0 + 3 * w:]], axis=1).astype(BF16)
    w_gate = jnp.pad(w_in[:, gate0:hy0], ((0, 0), (0, LANES - N_GATE_COLS))).astype(BF16)
    conv_w = jnp.concatenate([p['mlstm_conv_w'][i], p['hyena_conv_w'][i]], axis=1).astype(F32)
    conv_b = jnp.concatenate([p['mlstm_conv_b'][i], p['hyena_conv_b'][i]]).astype(F32).reshape(1, 5 * w)
    bias = jnp.pad(p['mlstm_gate_bias'][i].astype(F32).reshape(1, N_GATE_COLS),
                   ((0, 0), (0, LANES - N_GATE_COLS)))
    xf = x.reshape(m, d)
    qk, xg, t, v, ga, cols = even_proj(xf, g_mix, w_conv, w_rest, w_gate, conv_w, conv_b, bias, l)
    cols = cols.reshape(bsz, l, LANES)
    rows = jnp.swapaxes(cols[..., GATE_A_LANE:GATE_A_LANE + N_GATE_COLS], 1, 2)

    hf, hb = mlstm_scan(qk.reshape(bsz, l, 2 * w), v.reshape(bsz, l, w), 0, cols, rows, BF16)

    t = t.reshape(bsz, l, w)
    kern, asum = hyena_filter(l, p['hyena_w1'][i], p['hyena_b1'][i], p['hyena_freq1'][i], p['hyena_w2'][i],
                              p['hyena_b2'][i], p['hyena_freq2'][i], p['hyena_w3'][i], p['hyena_delta'][i])
    y = hyena_long_conv(t, kern)

    out = even_out(xf, hf.reshape(m, w), hb.reshape(m, w), ga, xg, y.reshape(m, w),
                   t.reshape(m, w), p['mlstm_norm_g'][i], p['hyena_skip'][i], asum,
                   p['even_w_out'][i].astype(BF16))
    return out.reshape(bsz, l, d)


def _odd_mixer(x, p, i, cos, sin):
    bsz, l, d = x.shape
    m = bsz * l
    xf = x.reshape(m, d)
    qk, v, gate = odd_proj(xf, p['norm_mix_g_layer'], p['odd_w_in'][i].astype(BF16), cos, sin, l)
    logit = p['ret_decay_logit'][i].astype(F32).reshape(2 * RET_HEADS, 1, 1)
    logit_b = jnp.broadcast_to(logit, (2 * RET_HEADS, 8, LANES))
    vw = RET_HEADS * RET_HDV
    of, ob = retention_scan(qk.reshape(bsz, l, -1), v.reshape(bsz, l, vw), logit_b, BF16)
    out = odd_out(xf, of.reshape(m, vw), ob.reshape(m, vw), gate, p['ret_norm_g'][i],
                  p['odd_w_out'][i].astype(BF16))
    return out.reshape(bsz, l, d)


def _trunk(x, mem, p, cos, sin):
    depth = p['norm_mix_g'].shape[0]
    bsz, nm, d = mem.shape
    for layer in range(depth):
        i = layer // 2
        p['norm_mix_g_layer'] = p['norm_mix_g'][layer]
        x = _even_mixer(x, p, i) if layer % 2 == 0 else _odd_mixer(x, p, i, cos, sin)
        kv, = rms_matmul(mem.reshape(bsz * nm, d), p['norm_mem_g'][layer], [p['ca_wkv'][layer].astype(BF16)],
                         [BF16], tm=nm)
        wqk, vwo = ca_fold(kv.reshape(bsz, nm, 2 * d), p['ca_wq'][layer].astype(BF16),
                           p['ca_wo'][layer].astype(BF16))
        x = cross_attn(x, wqk, vwo, p['norm_ca_g'][layer], p['norm_final_g'], layer == depth - 1)
    return x


def kernel(x_prompt, x_sample, mem_prompt, mem_sample, norm_mix_g, norm_ca_g, norm_mem_g, norm_final_g, even_w_in, mlstm_conv_w, mlstm_conv_b, mlstm_gate_bias, mlstm_norm_g, hyena_conv_w, hyena_conv_b, hyena_w1, hyena_b1, hyena_freq1, hyena_w2, hyena_b2, hyena_freq2, hyena_w3, hyena_delta, hyena_skip, even_w_out, odd_w_in, ret_decay_logit, ret_norm_g, odd_w_out, ca_wq, ca_wkv, ca_wo):
    p = {'norm_mix_g': norm_mix_g, 'norm_ca_g': norm_ca_g, 'norm_mem_g': norm_mem_g, 'norm_final_g': norm_final_g,
         'even_w_in': even_w_in, 'mlstm_conv_w': mlstm_conv_w, 'mlstm_conv_b': mlstm_conv_b,
         'mlstm_gate_bias': mlstm_gate_bias, 'mlstm_norm_g': mlstm_norm_g,
         'hyena_conv_w': hyena_conv_w, 'hyena_conv_b': hyena_conv_b, 'hyena_w1': hyena_w1, 'hyena_b1': hyena_b1,
         'hyena_freq1': hyena_freq1, 'hyena_w2': hyena_w2, 'hyena_b2': hyena_b2, 'hyena_freq2': hyena_freq2,
         'hyena_w3': hyena_w3, 'hyena_delta': hyena_delta, 'hyena_skip': hyena_skip, 'even_w_out': even_w_out,
         'odd_w_in': odd_w_in, 'ret_decay_logit': ret_decay_logit, 'ret_norm_g': ret_norm_g, 'odd_w_out': odd_w_out,
         'ca_wq': ca_wq, 'ca_wkv': ca_wkv, 'ca_wo': ca_wo}
    l_max = max(x_prompt.shape[1], x_sample.shape[1])
    cos, sin = rope_tables(l_max)
    y_prompt = _trunk(x_prompt, mem_prompt, dict(p), cos, sin)
    y_sample = _trunk(x_sample, mem_sample, dict(p), cos, sin)
    return (y_prompt, y_sample)
```

```python
import functools
import math

import numpy as np
import jax
import jax.numpy as jnp
from jax import lax
from jax.experimental import pallas as pl
from jax.experimental.pallas import tpu as pltpu

F32 = jnp.float32
BF16 = jnp.bfloat16

D_MODEL = 1024
EPS = 1e-6
GN_EPS = 1e-5
CHUNK = 256

MLSTM_HEADS = 4
MLSTM_HD = D_MODEL // MLSTM_HEADS
N_GATE_COLS = 4 * MLSTM_HEADS

HYENA_WIDTH = D_MODEL
HYENA_EMB = 33
HYENA_BANDS = (HYENA_EMB - 1) // 2
HYENA_SHIFT = 0.05
FFT_N2 = 256

RET_HEADS = 4
RET_HDK = D_MODEL // RET_HEADS
RET_HDV = 2 * D_MODEL // RET_HEADS
ROPE_BASE = 10000.0

CA_HEADS = 4
CA_HD = D_MODEL // CA_HEADS

LANES = 128
BF16_SUBLANES = 16
VMEM_LIMIT = 56 * 1024 * 1024

HI = lax.Precision.HIGHEST


def _params(*sem):
    return pltpu.CompilerParams(dimension_semantics=sem, vmem_limit_bytes=VMEM_LIMIT)


def _silu(x):
    return x * (1.0 / (1.0 + jnp.exp(-x)))


def _sigmoid(x):
    return 1.0 / (1.0 + jnp.exp(-x))


def _log_sigmoid(x):
    return jnp.minimum(x, 0.0) - jnp.log(1.0 + jnp.exp(-jnp.abs(x)))


def _rms_matmul_kernel(x_ref, g_ref, *refs, tn):
    nw = len(refs) // 2
    x = x_ref[...]
    xn = (x * lax.rsqrt(jnp.mean(x * x, axis=-1, keepdims=True) + EPS) * g_ref[...]).astype(BF16)
    for w_ref, o_ref in zip(refs[:nw], refs[nw:]):
        n = o_ref.shape[1]
        step = min(tn, n)
        for j in range(n // step):
            cols = slice(j * step, (j + 1) * step)
            o_ref[:, cols] = jnp.dot(xn, w_ref[:, cols], preferred_element_type=F32).astype(o_ref.dtype)


def rms_matmul(x, g, ws, out_dtypes, tm, tn=1024):
    m, d = x.shape
    w_specs = [pl.BlockSpec(w.shape, lambda i: (0, 0), pipeline_mode=pl.Buffered(1)) for w in ws]
    return pl.pallas_call(
        functools.partial(_rms_matmul_kernel, tn=tn),
        grid=(m // tm,),
        in_specs=[pl.BlockSpec((tm, d), lambda i: (i, 0)), pl.BlockSpec((1, d), lambda i: (0, 0))] + w_specs,
        out_specs=[pl.BlockSpec((tm, w.shape[1]), lambda i: (i, 0)) for w in ws],
        out_shape=[jax.ShapeDtypeStruct((m, w.shape[1]), dt) for w, dt in zip(ws, out_dtypes)],
        compiler_params=_params("parallel"),
        name="rms_matmul",
    )(x, g.reshape(1, d).astype(F32), *ws)


F32_SUBLANES = 8


def _even_proj_kernel(x_ref, xp_ref, xn_ref, g_ref, wc_ref, wr_ref, wg_ref, cw_ref, cb_ref, gb_ref,
                      qk_ref, xg_ref, t_ref, v_ref, ga_ref, cols_ref, *, tiles_per_seq, tn):
    i = pl.program_id(0)
    tm, d = x_ref.shape
    w, h = D_MODEL, F32_SUBLANES
    first = (i % tiles_per_seq) == 0
    last = (i % tiles_per_seq) == tiles_per_seq - 1
    g = g_ref[...]

    def norm(x):
        return x * lax.rsqrt(jnp.mean(x * x, axis=-1, keepdims=True) + EPS) * g

    xm = norm(x_ref[...])
    xp = jnp.where(first, 0.0, norm(xp_ref[...]))
    xn = jnp.where(last, 0.0, norm(xn_ref[...]))
    x_ext = jnp.concatenate([xp, xm, xn], axis=0).astype(BF16)
    xb = xm.astype(BF16)

    def conv(c0):
        cols = slice(c0, c0 + tn)
        r = jnp.dot(x_ext, wc_ref[:, cols], preferred_element_type=F32)
        t = r.shape[0]
        y = (pltpu.roll(r, 1, axis=0) * cw_ref[0:1, cols] + r * cw_ref[1:2, cols]
             + pltpu.roll(r, t - 1, axis=0) * cw_ref[2:3, cols] + cb_ref[:, cols])
        return y[h:h + tm]

    for j in range(2 * w // tn):
        qk_ref[:, j * tn:(j + 1) * tn] = _silu(conv(j * tn)).astype(qk_ref.dtype)
    rest = lambda group, j: jnp.dot(xb, wr_ref[:, group * w + j * tn:group * w + (j + 1) * tn],
                                    preferred_element_type=F32)
    for j in range(w // tn):
        cols = slice(j * tn, (j + 1) * tn)
        xg_ref[:, cols] = (conv(2 * w + j * tn) * _silu(rest(3, j))).astype(xg_ref.dtype)
        t_ref[:, cols] = (conv(3 * w + j * tn) * conv(4 * w + j * tn)).astype(t_ref.dtype)
        v_ref[:, cols] = rest(0, j).astype(v_ref.dtype)
        ga_ref[:, cols] = (_sigmoid(rest(1, j)) * _silu(rest(2, j))).astype(ga_ref.dtype)
    cols_ref[...] = _gate_prep(jnp.dot(xb, wg_ref[...], preferred_element_type=F32) + gb_ref[...])


def even_proj(x, g, w_conv, w_rest, w_gate, conv_w, conv_b, gate_bias, seq_len, tm=512, tn=512):
    m, d = x.shape
    w, h = D_MODEL, F32_SUBLANES
    tm = min(tm, seq_len)
    r = tm // h
    last_blk = m // h - 1
    res = lambda a: pl.BlockSpec(a.shape, lambda i: (0, 0), pipeline_mode=pl.Buffered(1))
    row = lambda n: pl.BlockSpec((tm, n), lambda i: (i, 0))
    return pl.pallas_call(
        functools.partial(_even_proj_kernel, tiles_per_seq=seq_len // tm, tn=tn),
        grid=(m // tm,),
        in_specs=[row(d),
                  pl.BlockSpec((h, d), lambda i: (jnp.maximum(i * r - 1, 0), 0)),
                  pl.BlockSpec((h, d), lambda i: (jnp.minimum((i + 1) * r, last_blk), 0)),
                  pl.BlockSpec((1, d), lambda i: (0, 0)),
                  res(w_conv), res(w_rest), res(w_gate), res(conv_w), res(conv_b), res(gate_bias)],
        out_specs=[row(2 * w), row(w), row(w), row(w), row(w), row(LANES)],
        out_shape=[jax.ShapeDtypeStruct((m, 2 * w), BF16), jax.ShapeDtypeStruct((m, w), BF16),
                   jax.ShapeDtypeStruct((m, w), BF16), jax.ShapeDtypeStruct((m, w), BF16),
                   jax.ShapeDtypeStruct((m, w), BF16), jax.ShapeDtypeStruct((m, LANES), F32)],
        compiler_params=_params("parallel"),
        name="even_proj",
    )(x, x, x, g.reshape(1, d).astype(F32), w_conv, w_rest, w_gate, conv_w, conv_b, gate_bias)


def _seg_scan(x, op, identity, reverse):
    t = x.shape[0]
    r = lax.broadcasted_iota(jnp.int32, x.shape, 0) % CHUNK
    k = 1
    while k < CHUNK:
        if reverse:
            shifted = pltpu.roll(x, t - k, axis=0)
            valid = r < CHUNK - k
        else:
            shifted = pltpu.roll(x, k, axis=0)
            valid = r >= k
        x = op(x, jnp.where(valid, shifted, identity))
        k *= 2
    return x


GATE_A_LANE = 16
GATE_AMAX_LANE = 32


def _gate_lane(d, h):
    return d * 2 * MLSTM_HEADS + h


def _gate_prep(g):
    nh = MLSTM_HEADS
    lane = lax.broadcasted_iota(jnp.int32, g.shape, 1)
    live = (lane < 4 * nh) & (lane % (2 * nh) < nh)
    fwd = lane < 2 * nh
    lf = pltpu.roll(_log_sigmoid(g), LANES - nh, axis=1)
    bcum = jnp.where(fwd, _seg_scan(lf, jnp.add, 0.0, False), _seg_scan(lf, jnp.add, 0.0, True))
    a = g - bcum
    amax = jnp.where(fwd, _seg_scan(a, jnp.maximum, -jnp.inf, False),
                     _seg_scan(a, jnp.maximum, -jnp.inf, True))
    keep = lambda v: jnp.where(live, v, 0.0)
    return (keep(bcum) + pltpu.roll(keep(a), GATE_A_LANE, axis=1)
            + pltpu.roll(keep(amax), GATE_AMAX_LANE, axis=1))


def _mlstm_kernel(qf_ref, kf_ref, vf_ref, cf_ref, rf_ref, qb_ref, kb_ref, vb_ref, cb_ref, rb_ref,
                  hf_ref, hb_ref, c_st, cb_st, m_st):
    @pl.when(pl.program_id(1) == 0)
    def _():
        c_st[...] = jnp.zeros_like(c_st)
        cb_st[...] = jnp.zeros_like(cb_st)
        m_st[...] = jnp.zeros_like(m_st)

    nh, hd = MLSTM_HEADS, MLSTM_HD
    t_idx = lax.broadcasted_iota(jnp.int32, (CHUNK, CHUNK), 0)
    s_idx = lax.broadcasted_iota(jnp.int32, (CHUNK, CHUNK), 1)
    scale = hd ** -0.5
    ones_blk = jnp.ones((CHUNK, LANES), BF16)
    rep = lambda x, n: jnp.concatenate([x] * n, axis=1)
    lane_dense = lambda col: jnp.broadcast_to(col, (CHUNK, LANES))
    dirs = ((qf_ref, kf_ref, vf_ref, cf_ref, rf_ref, hf_ref, False),
            (qb_ref, kb_ref, vb_ref, cb_ref, rb_ref, hb_ref, True))
    for d, (q_ref, k_ref, v_ref, col_ref, row_ref, o_ref, reverse) in enumerate(dirs):
        mask = (s_idx >= t_idx) if reverse else (s_idx <= t_idx)
        last = 0 if reverse else CHUNK - 1
        for h in range(nh):
            j = d * nh + h
            hs = slice(h * hd, (h + 1) * hd)
            q = q_ref[0, :, hs]
            k = k_ref[0, :, hs] * scale
            v_aug = jnp.concatenate([v_ref[0, :, hs], ones_blk], axis=1)
            gl = _gate_lane(d, h)
            bc = lane_dense(col_ref[0, :, gl:gl + 1])
            a_c = lane_dense(col_ref[0, :, GATE_A_LANE + gl:GATE_A_LANE + gl + 1])
            amax = lane_dense(col_ref[0, :, GATE_AMAX_LANE + gl:GATE_AMAX_LANE + gl + 1])
            a_r = row_ref[0, gl:gl + 1, :]
            dmat = jnp.exp(jnp.where(mask, a_r - rep(amax, CHUNK // LANES), -jnp.inf))
            s = lax.dot_general(q, k, (((1,), (1,)), ((), ())), preferred_element_type=F32) * dmat
            nd_l = jnp.dot(s.astype(BF16), v_aug, preferred_element_type=F32)
            a_last = amax[last:last + 1, :]
            btot = bc[last:last + 1, :]
            kw = rep(jnp.exp(a_c - a_last), hd // LANES) * k.astype(F32)
            upd = lax.dot_general(kw.astype(BF16), v_aug, (((0,), (0,)), ((), ())),
                                  preferred_element_type=F32)

            m_prev = m_st[j, 0:1, :]
            mt = jnp.maximum(amax, m_prev)
            f_l = jnp.exp(amax - mt)
            sc = jnp.exp(m_prev - mt)
            nd_c = jnp.dot(q, cb_st[j], preferred_element_type=F32)
            den = f_l * nd_l[:, hd:] + sc * nd_c[:, hd:]
            inv = 1.0 / jnp.maximum(jnp.abs(den), jnp.exp(-(bc + mt)))
            num = rep(f_l, hd // LANES) * nd_l[:, :hd] + rep(sc, hd // LANES) * nd_c[:, :hd]
            o_ref[0, :, hs] = (num * rep(inv, hd // LANES)).astype(o_ref.dtype)

            m_last = jnp.maximum(a_last, m_prev)
            dec = jnp.exp(m_prev - m_last)
            f_u = jnp.exp(a_last - m_last)
            wide = (hd + LANES) // LANES
            c_new = rep(dec, wide) * c_st[j] + rep(f_u, wide) * upd
            c_st[j] = c_new
            cb_st[j] = c_new.astype(BF16)
            m_st[j] = jnp.broadcast_to(btot + m_last, m_st.shape[1:])


def mlstm_scan(qk, proj, v_col, cols, rows, out_dtype):
    bsz, l, _ = qk.shape
    w = D_MODEL
    nc = l // CHUNK
    vb = v_col // w
    fwd = lambda cb: (lambda b, c: (b, c, cb))
    bwd = lambda cb: (lambda b, c: (b, nc - 1 - c, cb))
    blk = lambda im: pl.BlockSpec((1, CHUNK, w), im)
    ncol = cols.shape[-1]
    in_specs = []
    for mk in (fwd, bwd):
        in_specs += [blk(mk(0)), blk(mk(1)), blk(mk(vb)),
                     pl.BlockSpec((1, CHUNK, ncol), mk(0)),
                     pl.BlockSpec((1, rows.shape[1], CHUNK),
                                  (lambda b, c: (b, 0, c)) if mk is fwd else (lambda b, c: (b, 0, nc - 1 - c)))]
    nst = 2 * MLSTM_HEADS
    shp = jax.ShapeDtypeStruct((bsz, l, w), out_dtype)
    return pl.pallas_call(
        _mlstm_kernel,
        grid=(bsz, nc),
        in_specs=in_specs,
        out_specs=[blk(fwd(0)), blk(bwd(0))],
        out_shape=[shp, shp],
        scratch_shapes=[pltpu.VMEM((nst, MLSTM_HD, MLSTM_HD + LANES), F32),
                        pltpu.VMEM((nst, MLSTM_HD, MLSTM_HD + LANES), BF16),
                        pltpu.VMEM((nst, 8, LANES), F32)],
        compiler_params=_params("parallel", "arbitrary"),
        name="mlstm_scan",
    )(qk, qk, proj, cols, rows, qk, qk, proj, cols, rows)


def _rope_table_kernel(inv_ref, cos_ref, sin_ref):
    tl = cos_ref.shape[0]
    pos = (lax.broadcasted_iota(jnp.int32, cos_ref.shape, 0) + pl.program_id(0) * tl).astype(F32)
    ang = pos * inv_ref[...]
    cos_ref[...] = jnp.cos(ang)
    sin_ref[...] = jnp.sin(ang)


def rope_tables(l, tl=1024):
    half = RET_HDK // 2
    tl = min(tl, l)
    inv = (ROPE_BASE ** (-np.arange(0, RET_HDK, 2, dtype=np.float32) / RET_HDK)).astype(np.float32)
    spec = pl.BlockSpec((tl, half), lambda i: (i, 0))
    shp = jax.ShapeDtypeStruct((l, half), F32)
    return pl.pallas_call(
        _rope_table_kernel,
        grid=(l // tl,),
        in_specs=[pl.BlockSpec((1, half), lambda i: (0, 0))],
        out_specs=[spec, spec],
        out_shape=[shp, shp],
        compiler_params=_params("parallel"),
        name="rope_tables",
    )(jnp.asarray(inv).reshape(1, half))


def _rope(x, cos, sin):
    half = x.shape[-1] // 2
    x1, x2 = x[:, :half], x[:, half:]
    return jnp.concatenate([x1 * cos - x2 * sin, x1 * sin + x2 * cos], axis=-1)


def _odd_proj_kernel(x_ref, g_ref, w_ref, cos_ref, sin_ref, qk_ref, v_ref, gate_ref, *, tn):
    x = x_ref[...]
    xb = (x * lax.rsqrt(jnp.mean(x * x, axis=-1, keepdims=True) + EPS) * g_ref[...]).astype(BF16)
    cos, sin = cos_ref[...], sin_ref[...]
    nh, dk = RET_HEADS, RET_HDK
    for j in range(2 * nh):
        r = jnp.dot(xb, w_ref[:, j * dk:(j + 1) * dk], preferred_element_type=F32)
        r = _rope(r, cos, sin)
        if j < nh:
            r = r * dk ** -0.5
        qk_ref[:, j * dk:(j + 1) * dk] = r.astype(qk_ref.dtype)
    c0 = 2 * nh * dk
    vw = v_ref.shape[1]
    for j in range(vw // tn):
        cols = slice(j * tn, (j + 1) * tn)
        v_ref[:, cols] = jnp.dot(xb, w_ref[:, c0 + j * tn:c0 + (j + 1) * tn],
                                 preferred_element_type=F32).astype(v_ref.dtype)
        gate = jnp.dot(xb, w_ref[:, c0 + vw + j * tn:c0 + vw + (j + 1) * tn], preferred_element_type=F32)
        gate_ref[:, cols] = _silu(gate).astype(gate_ref.dtype)


def odd_proj(x, g, w, cos, sin, seq_len, tm=512, tn=512):
    m, d = x.shape
    qw, vw = RET_HEADS * RET_HDK, RET_HEADS * RET_HDV
    half = RET_HDK // 2
    tm = min(tm, seq_len)
    tps = seq_len // tm
    row = lambda n: pl.BlockSpec((tm, n), lambda i: (i, 0))
    tab = pl.BlockSpec((tm, half), lambda i: (i % tps, 0))
    return pl.pallas_call(
        functools.partial(_odd_proj_kernel, tn=tn),
        grid=(m // tm,),
        in_specs=[row(d), pl.BlockSpec((1, d), lambda i: (0, 0)),
                  pl.BlockSpec(w.shape, lambda i: (0, 0), pipeline_mode=pl.Buffered(1)), tab, tab],
        out_specs=[row(2 * qw), row(vw), row(vw)],
        out_shape=[jax.ShapeDtypeStruct((m, 2 * qw), BF16), jax.ShapeDtypeStruct((m, vw), BF16),
                   jax.ShapeDtypeStruct((m, vw), BF16)],
        compiler_params=_params("parallel"),
        name="odd_proj",
    )(x, g.reshape(1, d).astype(F32), w, cos, sin)


def _retention_kernel(qf_ref, kf_ref, vf_ref, qb_ref, kb_ref, vb_ref,
                      logit_ref, of_ref, ob_ref, r_st, rb_st):
    @pl.when(pl.program_id(1) == 0)
    def _():
        r_st[...] = jnp.zeros_like(r_st)
        rb_st[...] = jnp.zeros_like(rb_st)

    nh, dk, dv = RET_HEADS, RET_HDK, RET_HDV
    t_idx = lax.broadcasted_iota(jnp.int32, (CHUNK, CHUNK), 0)
    s_idx = lax.broadcasted_iota(jnp.int32, (CHUNK, CHUNK), 1)
    pos = lax.broadcasted_iota(jnp.int32, (CHUNK, LANES), 0).astype(F32)
    rep = lambda x, n: jnp.concatenate([x] * n, axis=1)
    dirs = ((qf_ref, kf_ref, vf_ref, of_ref, False), (qb_ref, kb_ref, vb_ref, ob_ref, True))
    for d, (q_ref, k_ref, v_ref, o_ref, reverse) in enumerate(dirs):
        rel = ((s_idx - t_idx) if reverse else (t_idx - s_idx)).astype(F32)
        for h in range(nh):
            j = d * nh + h
            lg = _log_sigmoid(logit_ref[j])[0:1, :]
            dmask = jnp.where(rel >= 0, jnp.exp(rep(lg, CHUNK // LANES) * jnp.maximum(rel, 0.0)), 0.0)
            if reverse:
                q_dec = jnp.exp(lg * (CHUNK - pos))
                k_dec = jnp.exp(lg * pos)
            else:
                q_dec = jnp.exp(lg * (pos + 1.0))
                k_dec = jnp.exp(lg * (CHUNK - 1.0 - pos))
            c_dec = jnp.exp(lg * CHUNK)
            q = q_ref[0, :, h * dk:(h + 1) * dk]
            k = k_ref[0, :, h * dk:(h + 1) * dk]
            v = v_ref[0, :, h * dv:(h + 1) * dv]
            s = lax.dot_general(q, k, (((1,), (1,)), ((), ())), preferred_element_type=F32) * dmask
            out = jnp.dot(s.astype(BF16), v, preferred_element_type=F32)
            out = out + jnp.dot((q.astype(F32) * rep(q_dec, dk // LANES)).astype(BF16), rb_st[j],
                                preferred_element_type=F32)
            o_ref[0, :, h * dv:(h + 1) * dv] = out.astype(o_ref.dtype)
            upd = lax.dot_general((k.astype(F32) * rep(k_dec, dk // LANES)).astype(BF16), v,
                                  (((0,), (0,)), ((), ())), preferred_element_type=F32)
            r_new = rep(c_dec, dv // LANES) * r_st[j] + upd
            r_st[j] = r_new
            rb_st[j] = r_new.astype(BF16)


def retention_scan(qk, v, logit_b, out_dtype):
    bsz, l, _ = qk.shape
    nc = l // CHUNK
    qw, vw = RET_HEADS * RET_HDK, RET_HEADS * RET_HDV
    in_specs = []
    for rev in (False, True):
        ci = (lambda c: nc - 1 - c) if rev else (lambda c: c)
        in_specs += [pl.BlockSpec((1, CHUNK, qw), lambda b, c, ci=ci: (b, ci(c), 0)),
                     pl.BlockSpec((1, CHUNK, qw), lambda b, c, ci=ci: (b, ci(c), 1)),
                     pl.BlockSpec((1, CHUNK, vw), lambda b, c, ci=ci: (b, ci(c), 0))]
    in_specs.append(pl.BlockSpec((2 * RET_HEADS, 8, LANES), lambda b, c: (0, 0, 0)))
    shp = jax.ShapeDtypeStruct((bsz, l, vw), out_dtype)
    return pl.pallas_call(
        _retention_kernel,
        grid=(bsz, nc),
        in_specs=in_specs,
        out_specs=[pl.BlockSpec((1, CHUNK, vw), lambda b, c: (b, c, 0)),
                   pl.BlockSpec((1, CHUNK, vw), lambda b, c: (b, nc - 1 - c, 0))],
        out_shape=[shp, shp],
        scratch_shapes=[pltpu.VMEM((2 * RET_HEADS, RET_HDK, RET_HDV), F32),
                        pltpu.VMEM((2 * RET_HEADS, RET_HDK, RET_HDV), BF16)],
        compiler_params=_params("parallel", "arbitrary"),
        name="retention_scan",
    )(qk, qk, v, qk, qk, v, logit_b)


def _hyena_filter_kernel(cst_ref, w1_ref, w2_ref, w3a_ref, w3b_ref, delta_ref, kern_ref, asum_ref, *, l):
    i = pl.program_id(0)
    tr = kern_ref.shape[0]
    hr = tr // 2
    half = LANES // 2
    row = lax.broadcasted_iota(jnp.int32, (hr, LANES), 0) + i * tr
    n_a, n_b = row, row + hr
    lag = lambda n: jnp.where(n < l, n, 2 * l - n).astype(F32)
    tt_a, tt_b = lag(n_a) / (l - 1.0), lag(n_b) / (l - 1.0)
    lo = lax.broadcasted_iota(jnp.int32, (hr, LANES), 1) < half
    p2 = jnp.where(lo, lag(n_a), lag(n_b))
    tt2 = jnp.where(lo, tt_a, tt_b)
    bands, phase, w1t, b1, f1, b2, f2 = (cst_ref[k:k + 1, :] for k in range(7))
    ang = (2.0 * math.pi / l) * bands * p2 + phase
    pre = jnp.dot(jnp.cos(ang), w1_ref[...], preferred_element_type=F32, precision=HI) + tt2 * w1t
    z = jnp.sin(f1 * (pre + b1))
    z = jnp.sin(f2 * (jnp.dot(z, w2_ref[...], preferred_element_type=F32, precision=HI) + b2))
    rep = lambda x: jnp.concatenate([x] * (kern_ref.shape[1] // LANES), axis=1)
    dabs = jnp.abs(delta_ref[...])
    total = jnp.zeros((1, kern_ref.shape[1]), F32)
    z_hi = z.astype(BF16)
    z_lo = (z - z_hi.astype(F32)).astype(BF16)
    z3 = jnp.concatenate([z_hi, z_lo, z_hi], axis=1)
    for part, (w3_ref, n, tt) in enumerate(((w3a_ref, n_a, tt_a), (w3b_ref, n_b, tt_b))):
        hk = jnp.dot(z3, w3_ref[...], preferred_element_type=F32)
        hk = hk * (jnp.exp(-rep(tt) * dabs) + HYENA_SHIFT)
        hk = jnp.where(rep(n) == l, 0.0, hk)
        kern_ref[part * hr:(part + 1) * hr, :] = hk
        total = total + jnp.sum(jnp.abs(hk), axis=0, keepdims=True)

    @pl.when(i == 0)
    def _():
        asum_ref[...] = jnp.zeros_like(asum_ref)

    asum_ref[...] += jnp.broadcast_to(total, asum_ref.shape)


def hyena_filter(l, w1, b1, f1, w2, b2, f2, w3, delta, tr=512):
    wd = HYENA_WIDTH
    hid = w1.shape[1]
    half = LANES // 2
    nb = HYENA_BANDS
    tr = min(tr, l)
    w1, w2, w3 = w1.astype(F32), w2.astype(F32), w3.astype(F32)
    two = lambda v: jnp.tile(jnp.pad(v.astype(F32).reshape(1, -1), ((0, 0), (0, half - v.shape[-1]))), (1, 2))
    bands = np.zeros((1, half), np.float32)
    bands[0, :nb] = bands[0, nb:2 * nb] = np.linspace(1e-4, nb - 1, nb, dtype=np.float32)
    phase = np.zeros((1, half), np.float32)
    phase[0, nb:2 * nb] = 0.5 * np.pi
    cst = jnp.concatenate([two(jnp.asarray(bands)), two(jnp.asarray(phase)), two(w1[0]), two(b1), two(f1),
                           two(b2), two(f2), jnp.zeros((1, LANES), F32)], axis=0)
    blk = lambda a: jnp.pad(a, ((0, half - a.shape[0]), (0, half - a.shape[1])))
    diag2 = lambda a: jnp.concatenate([jnp.pad(blk(a), ((0, 0), (0, half))),
                                       jnp.pad(blk(a), ((0, 0), (half, 0)))], axis=0)
    w3h = jnp.pad(w3, ((0, half - hid), (0, 0)))

    def split3(a):
        hi = a.astype(BF16)
        lo = (a - hi.astype(F32)).astype(BF16)
        return jnp.concatenate([hi, hi, lo], axis=0)

    w3a = split3(jnp.pad(w3h, ((0, half), (0, 0))))
    w3b = split3(jnp.pad(w3h, ((half, 0), (0, 0))))
    full = lambda shape: pl.BlockSpec(shape, lambda i: (0, 0))
    half_sel = lambda i: (0, (i * tr) // l)
    return pl.pallas_call(
        functools.partial(_hyena_filter_kernel, l=l),
        grid=(2 * l // tr,),
        in_specs=[full((8, LANES)), full((LANES, LANES)), full((LANES, LANES)),
                  pl.BlockSpec((3 * LANES, wd), half_sel), pl.BlockSpec((3 * LANES, wd), half_sel),
                  pl.BlockSpec((1, wd), half_sel)],
        out_specs=[pl.BlockSpec((tr, wd), lambda i: (i, 0)), pl.BlockSpec((8, wd), lambda i: (0, 0))],
        out_shape=[jax.ShapeDtypeStruct((2 * l, wd), F32), jax.ShapeDtypeStruct((8, wd), F32)],
        compiler_params=_params("arbitrary"),
        name="hyena_filter",
    )(cst, diag2(w1[1:]), diag2(w2), w3a, w3b, delta.reshape(1, 2 * wd).astype(F32))


def _dft_consts(n1, n2):
    n = n1 * n2
    k1 = np.arange(n1)
    f1 = np.exp(-2j * np.pi * np.outer(k1, k1) / n1)
    k2 = np.arange(n2)
    f2 = np.exp(-2j * np.pi * np.outer(k2, k2) / n2)
    tw = np.exp(-2j * np.pi * np.outer(k1, k2) / n)
    g = f2[None, :, :] * tw[:, None, :]
    ginv = np.conj(np.transpose(g, (0, 2, 1))) / n
    f1inv = np.conj(f1.T)
    return f1, g, ginv, f1inv


def _stack(c):
    return jnp.asarray(np.concatenate([c.real, c.imag], axis=-2).astype(np.float32)).astype(BF16)


def _cdot(fs, xr, xi, rows):
    p = jnp.dot(fs, xr, preferred_element_type=F32)
    if xi is None:
        return p[:rows], p[rows:]
    q = jnp.dot(fs, xi, preferred_element_type=F32)
    return p[:rows] - q[rows:], q[:rows] + p[rows:]


FFT_ROW_CHUNK = 512


def _kron_eye(f, s_blk):
    k = np.kron(f, np.eye(s_blk))
    to = lambda a: jnp.asarray(a.astype(np.float32)).astype(BF16)
    return to(k.real), to(k.imag)


def _strided_stage_kernel(*refs, has_imag_in, n_out):
    mr_ref, mi_ref, xr_ref = refs[:3]
    xi_ref = refs[3] if has_imag_in else None
    n_in = 4 if has_imag_in else 3
    if len(refs[n_in].shape) == 5:
        outs = [refs[n_in].at[part, 0] for part in range(n_out)]
    else:
        outs = [r.at[0] for r in refs[n_in:]]
    flat = lambda r: r.reshape(r.shape[0] * r.shape[1], r.shape[2])
    xr = flat(xr_ref[0]).astype(BF16)
    xi = flat(xi_ref[0]).astype(BF16) if has_imag_in else None
    rows = mr_ref.shape[0]
    rc = min(FFT_ROW_CHUNK, rows)
    s_blk = xr_ref.shape[2]
    for c0 in range(0, rows, rc):
        mr = mr_ref[c0:c0 + rc, :]
        mi = mi_ref[c0:c0 + rc, :]
        re = jnp.dot(mr, xr, preferred_element_type=F32)
        im = jnp.dot(mi, xr, preferred_element_type=F32) if n_out == 2 else None
        if has_imag_in:
            re = re - jnp.dot(mi, xi, preferred_element_type=F32)
            if n_out == 2:
                im = im + jnp.dot(mr, xi, preferred_element_type=F32)
        k0, k1 = c0 // s_blk, (c0 + rc) // s_blk
        for o_ref, val in zip(outs, (re, im)):
            o_ref[k0:k1] = val.reshape(rc // s_blk, s_blk, val.shape[1]).astype(o_ref.dtype)


def fft_stage1(x, n1, rows_in, pairs, complex_in, s_blk, out_dtype):
    c = x.shape[-1]
    n2 = FFT_N2
    xv = x.reshape(x.shape[0], rows_in, n2, c)
    mr, mi = _kron_eye(_dft_consts(n1, n2)[0][:, :rows_in], s_blk)
    mat = pl.BlockSpec(mr.shape, lambda p, j: (0, 0), pipeline_mode=pl.Buffered(1))
    in_specs = [mat, mat, pl.BlockSpec((1, rows_in, s_blk, c), lambda p, j: (p, 0, j, 0))]
    args = [mr, mi, xv]
    if complex_in:
        in_specs.append(pl.BlockSpec((1, rows_in, s_blk, c), lambda p, j: (p + pairs, 0, j, 0)))
        args.append(xv)
    out_spec = pl.BlockSpec((1, n1, s_blk, c), lambda p, j: (p, 0, j, 0))
    shp = jax.ShapeDtypeStruct((pairs, n1, n2, c), out_dtype)
    return pl.pallas_call(
        functools.partial(_strided_stage_kernel, has_imag_in=complex_in, n_out=2),
        grid=(pairs, n2 // s_blk),
        in_specs=in_specs,
        out_specs=[out_spec, out_spec],
        out_shape=[shp, shp],
        compiler_params=_params("parallel", "parallel"),
        name="fft_stage1",
    )(*args)


def _fft_mid_kernel(*refs, conv):
    n2 = FFT_N2
    if conv:
        gs_ref, gis_ref, ar_ref, ai_ref, kr_ref, ki_ref, br_ref, bi_ref = refs
    else:
        gs_ref, ar_ref, ai_ref, br_ref, bi_ref = refs
    xr, xi = _cdot(gs_ref[0], ar_ref[0, 0].astype(BF16), ai_ref[0, 0].astype(BF16), n2)
    if conv:
        kr, ki = kr_ref[0].astype(F32), ki_ref[0].astype(F32)
        yr = xr * kr - xi * ki
        yi = xr * ki + xi * kr
        xr, xi = _cdot(gis_ref[0], yr.astype(BF16), yi.astype(BF16), n2)
    br_ref[0, 0] = xr.astype(br_ref.dtype)
    bi_ref[0, 0] = xi.astype(bi_ref.dtype)


def fft_mid(ar, ai, n1, c, kf=None, ct=1024):
    n2 = FFT_N2
    pairs = ar.shape[0]
    _, g, ginv, _ = _dft_consts(n1, n2)
    a4 = lambda a: a
    mat = pl.BlockSpec((1, 2 * n2, n2), lambda k, cb, p: (k, 0, 0))
    dat = pl.BlockSpec((1, 1, n2, ct), lambda k, cb, p: (p, k, 0, cb))
    in_specs = [mat]
    args = [_stack(g)]
    if kf is not None:
        in_specs.append(mat)
        args.append(_stack(ginv))
    in_specs += [dat, dat]
    args += [a4(ar), a4(ai)]
    if kf is not None:
        fil = pl.BlockSpec((1, n2, ct), lambda k, cb, p: (k, 0, cb))
        in_specs += [fil, fil]
        args += [kf[0], kf[1]]
    shp = jax.ShapeDtypeStruct((pairs, n1, n2, c), BF16)
    br, bi = pl.pallas_call(
        functools.partial(_fft_mid_kernel, conv=kf is not None),
        grid=(n1, c // ct, pairs),
        in_specs=in_specs,
        out_specs=[dat, dat],
        out_shape=[shp, shp],
        compiler_params=_params("parallel", "parallel", "arbitrary"),
        name="fft_mid",
    )(*args)
    return br, bi


def fft_stage1_inv(br, bi, n1, rows_out, c, complex_out, s_blk):
    pairs = br.shape[0]
    n2 = FFT_N2
    mr, mi = _kron_eye(_dft_consts(n1, n2)[3][:rows_out, :], s_blk)
    parts = 2 if complex_out else 1
    mat = pl.BlockSpec(mr.shape, lambda p, j: (0, 0), pipeline_mode=pl.Buffered(1))
    dat = pl.BlockSpec((1, n1, s_blk, c), lambda p, j: (p, 0, j, 0))
    y = pl.pallas_call(
        functools.partial(_strided_stage_kernel, has_imag_in=True, n_out=parts),
        grid=(pairs, n2 // s_blk),
        in_specs=[mat, mat, dat, dat],
        out_specs=pl.BlockSpec((parts, 1, rows_out, s_blk, c), lambda p, j: (0, p, 0, j, 0)),
        out_shape=jax.ShapeDtypeStruct((parts, pairs, rows_out, n2, c), BF16),
        compiler_params=_params("parallel", "parallel"),
        name="fft_stage1_inv",
    )(mr, mi, br, bi)
    return y.reshape(parts * pairs, rows_out * n2, c)


def hyena_long_conv(t, kern):
    bsz, l, c = t.shape
    n1 = 2 * l // FFT_N2
    kr, ki = fft_stage1(kern[None], n1, n1, 1, False, 8, F32)
    kr, ki = fft_mid(kr, ki, n1, c)
    kf = (kr.reshape(n1, FFT_N2, c), ki.reshape(n1, FFT_N2, c))
    complex_in = bsz % 2 == 0
    pairs = bsz // 2 if complex_in else bsz
    ar, ai = fft_stage1(t, n1, n1 // 2, pairs, complex_in, BF16_SUBLANES, BF16)
    br, bi = fft_mid(ar, ai, n1, c, kf)
    return fft_stage1_inv(br, bi, n1, n1 // 2, c, complex_in, BF16_SUBLANES)


def _head_ln(x, nheads, g):
    hd = x.shape[-1] // nheads
    outs = []
    for h in range(nheads):
        seg = x[:, h * hd:(h + 1) * hd]
        mu = jnp.mean(seg, axis=-1, keepdims=True)
        cen = seg - mu
        var = jnp.mean(cen * cen, axis=-1, keepdims=True)
        outs.append(cen * lax.rsqrt(var + GN_EPS))
    return jnp.concatenate(outs, axis=-1) * g


def _cross_attn_tail(x, wqk_ref, vwo_ref, gq_ref, gf_ref, final_norm):
    xn = x * lax.rsqrt(jnp.mean(x * x, axis=-1, keepdims=True) + EPS) * gq_ref[...]
    s_all = jnp.dot(xn.astype(BF16), wqk_ref[0], preferred_element_type=F32)
    nm = s_all.shape[1] // CA_HEADS
    probs = []
    for h in range(CA_HEADS):
        s = s_all[:, h * nm:(h + 1) * nm]
        p = jnp.exp(s - jnp.max(s, axis=-1, keepdims=True))
        probs.append((p * (1.0 / jnp.sum(p, axis=-1, keepdims=True))).astype(BF16))
    y = x + jnp.dot(jnp.concatenate(probs, axis=-1), vwo_ref[0], preferred_element_type=F32)
    if final_norm:
        y = y * lax.rsqrt(jnp.mean(y * y, axis=-1, keepdims=True) + EPS) * gf_ref[...]
    return y


def _even_out_kernel(x_ref, hf_ref, hb_ref, ga_ref, xg_ref, y_ref, t_ref, g_ref, skip_ref, asum_ref, w_ref,
                     wqk_ref, vwo_ref, gq_ref, gf_ref, out_ref, *, final_norm):
    wd = D_MODEL
    ln = _head_ln(hf_ref[0].astype(F32) + hb_ref[0].astype(F32), MLSTM_HEADS, g_ref[...])
    ya = ln * ga_ref[0].astype(F32)
    conv = y_ref[0].astype(F32) * (1.0 / asum_ref[0:1, :])
    yb = xg_ref[0].astype(F32) * (conv + skip_ref[...] * t_ref[0].astype(F32))
    mix = jnp.dot(ya.astype(BF16), w_ref[0:wd, :], preferred_element_type=F32)
    mix = mix + jnp.dot(yb.astype(BF16), w_ref[wd:2 * wd, :], preferred_element_type=F32)
    out_ref[0] = _cross_attn_tail(x_ref[0] + mix, wqk_ref, vwo_ref, gq_ref, gf_ref, final_norm)


def _odd_out_kernel(x_ref, of_ref, ob_ref, gate_ref, g_ref, w_ref,
                    wqk_ref, vwo_ref, gq_ref, gf_ref, out_ref, *, final_norm):
    o = _head_ln(of_ref[0].astype(F32) + ob_ref[0].astype(F32), RET_HEADS, g_ref[...])
    y = gate_ref[0].astype(F32) * o
    mix = jnp.dot(y.astype(BF16), w_ref[...], preferred_element_type=F32)
    out_ref[0] = _cross_attn_tail(x_ref[0] + mix, wqk_ref, vwo_ref, gq_ref, gf_ref, final_norm)


def mixer_out(kernel, x, acts, vecs, w_out, ca, tm=512):
    bsz, l, d = x.shape
    tm = min(tm, l)
    wqk, vwo, gq, gf, final_norm = ca
    row = lambda a: pl.BlockSpec((1, tm, a.shape[-1]), lambda b, i: (b, i, 0))
    res = lambda a: pl.BlockSpec(a.shape, lambda b, i: (0, 0), pipeline_mode=pl.Buffered(1))
    per_b = lambda a: pl.BlockSpec((1,) + a.shape[1:], lambda b, i: (b, 0, 0))
    gq, gf = gq.reshape(1, d).astype(F32), gf.reshape(1, d).astype(F32)
    return pl.pallas_call(
        functools.partial(kernel, final_norm=final_norm),
        grid=(bsz, l // tm),
        in_specs=[row(x)] + [row(a) for a in acts] + [res(v) for v in vecs] + [res(w_out)]
                 + [per_b(wqk), per_b(vwo), res(gq), res(gf)],
        out_specs=row(x),
        out_shape=jax.ShapeDtypeStruct((bsz, l, d), F32),
        compiler_params=_params("parallel", "parallel"),
        name=kernel.__name__.strip("_").replace("_kernel", "_ca"),
    )(x, *acts, *vecs, w_out, wqk, vwo, gq, gf)


def _ca_fold_kernel(kv_ref, wq_ref, wo_ref, wqk_ref, vwo_ref):
    d = D_MODEL
    for h in range(CA_HEADS):
        hs = slice(h * CA_HD, (h + 1) * CA_HD)
        k = kv_ref[0, :, hs]
        v = kv_ref[0, :, d + h * CA_HD:d + (h + 1) * CA_HD]
        nm = k.shape[0]
        wqk = lax.dot_general(wq_ref[:, hs], k, (((1,), (1,)), ((), ())), preferred_element_type=F32)
        wqk_ref[0, :, h * nm:(h + 1) * nm] = (wqk * CA_HD ** -0.5).astype(wqk_ref.dtype)
        vwo_ref[0, h * nm:(h + 1) * nm, :] = jnp.dot(v, wo_ref[hs, :],
                                                     preferred_element_type=F32).astype(vwo_ref.dtype)


def ca_fold(kv, wq, wo):
    bsz, nm, d2 = kv.shape
    d = d2 // 2
    mat = pl.BlockSpec((d, d), lambda b: (0, 0))
    return pl.pallas_call(
        _ca_fold_kernel,
        grid=(bsz,),
        in_specs=[pl.BlockSpec((1, nm, d2), lambda b: (b, 0, 0)), mat, mat],
        out_specs=[pl.BlockSpec((1, d, CA_HEADS * nm), lambda b: (b, 0, 0)),
                   pl.BlockSpec((1, CA_HEADS * nm, d), lambda b: (b, 0, 0))],
        out_shape=[jax.ShapeDtypeStruct((bsz, d, CA_HEADS * nm), BF16),
                   jax.ShapeDtypeStruct((bsz, CA_HEADS * nm, d), BF16)],
        compiler_params=_params("parallel"),
        name="ca_fold",
    )(kv, wq, wo)


def _even_mixer(x, p, i, ca):
    bsz, l, d = x.shape
    w = D_MODEL
    nh = MLSTM_HEADS
    m = bsz * l
    g_mix = p['norm_mix_g_layer']
    w_in = p['even_w_in'][i]
    gate0 = 5 * w
    hy0 = gate0 + N_GATE_COLS
    w_conv = jnp.concatenate([w_in[:, :2 * w], w_in[:, hy0:hy0 + 3 * w]], axis=1).astype(BF16)
    w_rest = jnp.concatenate([w_in[:, 2 * w:gate0], w_in[:, hy0 + 3 * w:]], axis=1).astype(BF16)
    w_gate = jnp.pad(w_in[:, gate0:hy0], ((0, 0), (0, LANES - N_GATE_COLS))).astype(BF16)
    conv_w = jnp.concatenate([p['mlstm_conv_w'][i], p['hyena_conv_w'][i]], axis=1).astype(F32)
    conv_b = jnp.concatenate([p['mlstm_conv_b'][i], p['hyena_conv_b'][i]]).astype(F32).reshape(1, 5 * w)
    bias = jnp.pad(p['mlstm_gate_bias'][i].astype(F32).reshape(1, N_GATE_COLS),
                   ((0, 0), (0, LANES - N_GATE_COLS)))
    xf = x.reshape(m, d)
    qk, xg, t, v, ga, cols = even_proj(xf, g_mix, w_conv, w_rest, w_gate, conv_w, conv_b, bias, l)
    cols = cols.reshape(bsz, l, LANES)
    rows = jnp.swapaxes(cols[..., GATE_A_LANE:GATE_A_LANE + N_GATE_COLS], 1, 2)

    hf, hb = mlstm_scan(qk.reshape(bsz, l, 2 * w), v.reshape(bsz, l, w), 0, cols, rows, BF16)

    t = t.reshape(bsz, l, w)
    kern, asum = hyena_filter(l, p['hyena_w1'][i], p['hyena_b1'][i], p['hyena_freq1'][i], p['hyena_w2'][i],
                              p['hyena_b2'][i], p['hyena_freq2'][i], p['hyena_w3'][i], p['hyena_delta'][i])
    y = hyena_long_conv(t, kern)

    b3 = lambda a: a.reshape(bsz, l, w)
    vec = lambda a: a.reshape(1, w).astype(F32)
    return mixer_out(_even_out_kernel, x, [hf, hb, b3(ga), b3(xg), y, t],
                     [vec(p['mlstm_norm_g'][i]), vec(p['hyena_skip'][i]), asum],
                     p['even_w_out'][i].astype(BF16), ca)


def _odd_mixer(x, p, i, cos, sin, ca):
    bsz, l, d = x.shape
    m = bsz * l
    xf = x.reshape(m, d)
    qk, v, gate = odd_proj(xf, p['norm_mix_g_layer'], p['odd_w_in'][i].astype(BF16), cos, sin, l)
    logit = p['ret_decay_logit'][i].astype(F32).reshape(2 * RET_HEADS, 1, 1)
    logit_b = jnp.broadcast_to(logit, (2 * RET_HEADS, 8, LANES))
    vw = RET_HEADS * RET_HDV
    of, ob = retention_scan(qk.reshape(bsz, l, -1), v.reshape(bsz, l, vw), logit_b, BF16)
    return mixer_out(_odd_out_kernel, x, [of, ob, gate.reshape(bsz, l, vw)],
                     [p['ret_norm_g'][i].reshape(1, vw).astype(F32)], p['odd_w_out'][i].astype(BF16), ca)


def _trunk(x, mem, p, cos, sin):
    depth = p['norm_mix_g'].shape[0]
    bsz, nm, d = mem.shape
    for layer in range(depth):
        i = layer // 2
        p['norm_mix_g_layer'] = p['norm_mix_g'][layer]
        kv, = rms_matmul(mem.reshape(bsz * nm, d), p['norm_mem_g'][layer], [p['ca_wkv'][layer].astype(BF16)],
                         [BF16], tm=nm)
        wqk, vwo = ca_fold(kv.reshape(bsz, nm, 2 * d), p['ca_wq'][layer].astype(BF16),
                           p['ca_wo'][layer].astype(BF16))
        ca = (wqk, vwo, p['norm_ca_g'][layer], p['norm_final_g'], layer == depth - 1)
        x = _even_mixer(x, p, i, ca) if layer % 2 == 0 else _odd_mixer(x, p, i, cos, sin, ca)
    return x


def kernel(x_prompt, x_sample, mem_prompt, mem_sample, norm_mix_g, norm_ca_g, norm_mem_g, norm_final_g, even_w_in, mlstm_conv_w, mlstm_conv_b, mlstm_gate_bias, mlstm_norm_g, hyena_conv_w, hyena_conv_b, hyena_w1, hyena_b1, hyena_freq1, hyena_w2, hyena_b2, hyena_freq2, hyena_w3, hyena_delta, hyena_skip, even_w_out, odd_w_in, ret_decay_logit, ret_norm_g, odd_w_out, ca_wq, ca_wkv, ca_wo):
    p = {'norm_mix_g': norm_mix_g, 'norm_ca_g': norm_ca_g, 'norm_mem_g': norm_mem_g, 'norm_final_g': norm_final_g,
         'even_w_in': even_w_in, 'mlstm_conv_w': mlstm_conv_w, 'mlstm_conv_b': mlstm_conv_b,
         'mlstm_gate_bias': mlstm_gate_bias, 'mlstm_norm_g': mlstm_norm_g,
         'hyena_conv_w': hyena_conv_w, 'hyena_conv_b': hyena_conv_b, 'hyena_w1': hyena_w1, 'hyena_b1': hyena_b1,
         'hyena_freq1': hyena_freq1, 'hyena_w2': hyena_w2, 'hyena_b2': hyena_b2, 'hyena_freq2': hyena_freq2,
         'hyena_w3': hyena_w3, 'hyena_delta': hyena_delta, 'hyena_skip': hyena_skip, 'even_w_out': even_w_out,
         'odd_w_in': odd_w_in, 'ret_decay_logit': ret_decay_logit, 'ret_norm_g': ret_norm_g, 'odd_w_out': odd_w_out,
         'ca_wq': ca_wq, 'ca_wkv': ca_wkv, 'ca_wo': ca_wo}
    l_max = max(x_prompt.shape[1], x_sample.shape[1])
    cos, sin = rope_tables(l_max)
    y_prompt = _trunk(x_prompt, mem_prompt, dict(p), cos, sin)
    y_sample = _trunk(x_sample, mem_sample, dict(p), cos, sin)
    return (y_prompt, y_sample)
```

```python
import functools
import math

import numpy as np
import jax
import jax.numpy as jnp
from jax import lax
from jax.experimental import pallas as pl
from jax.experimental.pallas import tpu as pltpu

F32 = jnp.float32
BF16 = jnp.bfloat16

D_MODEL = 1024
EPS = 1e-6
GN_EPS = 1e-5
CHUNK = 256

MLSTM_HEADS = 4
MLSTM_HD = D_MODEL // MLSTM_HEADS
N_GATE_COLS = 4 * MLSTM_HEADS

HYENA_WIDTH = D_MODEL
HYENA_EMB = 33
HYENA_BANDS = (HYENA_EMB - 1) // 2
HYENA_SHIFT = 0.05
FFT_N2 = 256

RET_HEADS = 4
RET_HDK = D_MODEL // RET_HEADS
RET_HDV = 2 * D_MODEL // RET_HEADS
ROPE_BASE = 10000.0

CA_HEADS = 4
CA_HD = D_MODEL // CA_HEADS

LANES = 128
BF16_SUBLANES = 16
VMEM_LIMIT = 56 * 1024 * 1024

HI = lax.Precision.HIGHEST


def _params(*sem):
    return pltpu.CompilerParams(dimension_semantics=sem, vmem_limit_bytes=VMEM_LIMIT)


def _silu(x):
    return x * (1.0 / (1.0 + jnp.exp(-x)))


def _sigmoid(x):
    return 1.0 / (1.0 + jnp.exp(-x))


def _log_sigmoid(x):
    return jnp.minimum(x, 0.0) - jnp.log(1.0 + jnp.exp(-jnp.abs(x)))


def _rms_matmul_kernel(x_ref, g_ref, *refs, tn):
    nw = len(refs) // 2
    x = x_ref[...]
    xn = (x * lax.rsqrt(jnp.mean(x * x, axis=-1, keepdims=True) + EPS) * g_ref[...]).astype(BF16)
    for w_ref, o_ref in zip(refs[:nw], refs[nw:]):
        n = o_ref.shape[1]
        step = min(tn, n)
        for j in range(n // step):
            cols = slice(j * step, (j + 1) * step)
            o_ref[:, cols] = jnp.dot(xn, w_ref[:, cols], preferred_element_type=F32).astype(o_ref.dtype)


def rms_matmul(x, g, ws, out_dtypes, tm, tn=1024):
    m, d = x.shape
    w_specs = [pl.BlockSpec(w.shape, lambda i: (0, 0), pipeline_mode=pl.Buffered(1)) for w in ws]
    return pl.pallas_call(
        functools.partial(_rms_matmul_kernel, tn=tn),
        grid=(m // tm,),
        in_specs=[pl.BlockSpec((tm, d), lambda i: (i, 0)), pl.BlockSpec((1, d), lambda i: (0, 0))] + w_specs,
        out_specs=[pl.BlockSpec((tm, w.shape[1]), lambda i: (i, 0)) for w in ws],
        out_shape=[jax.ShapeDtypeStruct((m, w.shape[1]), dt) for w, dt in zip(ws, out_dtypes)],
        compiler_params=_params("parallel"),
        name="rms_matmul",
    )(x, g.reshape(1, d).astype(F32), *ws)


F32_SUBLANES = 8


def _even_proj_kernel(x_ref, xp_ref, xn_ref, g_ref, wc_ref, wr_ref, wg_ref, cw_ref, cb_ref, gb_ref,
                      qk_ref, xg_ref, t_ref, v_ref, ga_ref, cols_ref, *, tiles_per_seq, tn):
    i = pl.program_id(0)
    tm, d = x_ref.shape
    w, h = D_MODEL, F32_SUBLANES
    first = (i % tiles_per_seq) == 0
    last = (i % tiles_per_seq) == tiles_per_seq - 1
    g = g_ref[...]

    def norm(x):
        return x * lax.rsqrt(jnp.mean(x * x, axis=-1, keepdims=True) + EPS) * g

    xm = norm(x_ref[...])
    xp = jnp.where(first, 0.0, norm(xp_ref[...]))
    xn = jnp.where(last, 0.0, norm(xn_ref[...]))
    x_ext = jnp.concatenate([xp, xm, xn], axis=0).astype(BF16)
    xb = xm.astype(BF16)

    def conv(c0):
        cols = slice(c0, c0 + tn)
        r = jnp.dot(x_ext, wc_ref[:, cols], preferred_element_type=F32)
        t = r.shape[0]
        y = (pltpu.roll(r, 1, axis=0) * cw_ref[0:1, cols] + r * cw_ref[1:2, cols]
             + pltpu.roll(r, t - 1, axis=0) * cw_ref[2:3, cols] + cb_ref[:, cols])
        return y[h:h + tm]

    for j in range(2 * w // tn):
        qk_ref[:, j * tn:(j + 1) * tn] = _silu(conv(j * tn)).astype(qk_ref.dtype)
    rest = lambda group, j: jnp.dot(xb, wr_ref[:, group * w + j * tn:group * w + (j + 1) * tn],
                                    preferred_element_type=F32)
    for j in range(w // tn):
        cols = slice(j * tn, (j + 1) * tn)
        xg_ref[:, cols] = (conv(2 * w + j * tn) * _silu(rest(3, j))).astype(xg_ref.dtype)
        t_ref[:, cols] = (conv(3 * w + j * tn) * conv(4 * w + j * tn)).astype(t_ref.dtype)
        v_ref[:, cols] = rest(0, j).astype(v_ref.dtype)
        ga_ref[:, cols] = (_sigmoid(rest(1, j)) * _silu(rest(2, j))).astype(ga_ref.dtype)
    cols_ref[...] = _gate_prep(jnp.dot(xb, wg_ref[...], preferred_element_type=F32) + gb_ref[...])


def even_proj(x, g, w_conv, w_rest, w_gate, conv_w, conv_b, gate_bias, seq_len, tm=512, tn=512):
    m, d = x.shape
    w, h = D_MODEL, F32_SUBLANES
    tm = min(tm, seq_len)
    assert m % seq_len == 0 and seq_len % tm == 0 and tm % CHUNK == 0
    r = tm // h
    last_blk = m // h - 1
    res = lambda a: pl.BlockSpec(a.shape, lambda i: (0, 0), pipeline_mode=pl.Buffered(1))
    row = lambda n: pl.BlockSpec((tm, n), lambda i: (i, 0))
    return pl.pallas_call(
        functools.partial(_even_proj_kernel, tiles_per_seq=seq_len // tm, tn=tn),
        grid=(m // tm,),
        in_specs=[row(d),
                  pl.BlockSpec((h, d), lambda i: (jnp.maximum(i * r - 1, 0), 0)),
                  pl.BlockSpec((h, d), lambda i: (jnp.minimum((i + 1) * r, last_blk), 0)),
                  pl.BlockSpec((1, d), lambda i: (0, 0)),
                  res(w_conv), res(w_rest), res(w_gate), res(conv_w), res(conv_b), res(gate_bias)],
        out_specs=[row(2 * w), row(w), row(w), row(w), row(w), row(LANES)],
        out_shape=[jax.ShapeDtypeStruct((m, 2 * w), BF16), jax.ShapeDtypeStruct((m, w), BF16),
                   jax.ShapeDtypeStruct((m, w), BF16), jax.ShapeDtypeStruct((m, w), BF16),
                   jax.ShapeDtypeStruct((m, w), BF16), jax.ShapeDtypeStruct((m, LANES), F32)],
        compiler_params=_params("parallel"),
        name="even_proj",
    )(x, x, x, g.reshape(1, d).astype(F32), w_conv, w_rest, w_gate, conv_w, conv_b, gate_bias)


def _seg_scan(x, op, identity, reverse):
    t = x.shape[0]
    r = lax.broadcasted_iota(jnp.int32, x.shape, 0) % CHUNK
    k = 1
    while k < CHUNK:
        if reverse:
            shifted = pltpu.roll(x, t - k, axis=0)
            valid = r < CHUNK - k
        else:
            shifted = pltpu.roll(x, k, axis=0)
            valid = r >= k
        x = op(x, jnp.where(valid, shifted, identity))
        k *= 2
    return x


GATE_A_LANE = 16
GATE_AMAX_LANE = 32


def _gate_lane(d, h):
    return d * 2 * MLSTM_HEADS + h


def _gate_prep(g):
    nh = MLSTM_HEADS
    lane = lax.broadcasted_iota(jnp.int32, g.shape, 1)
    live = (lane < 4 * nh) & (lane % (2 * nh) < nh)
    fwd = lane < 2 * nh
    lf = pltpu.roll(_log_sigmoid(g), LANES - nh, axis=1)
    bcum = jnp.where(fwd, _seg_scan(lf, jnp.add, 0.0, False), _seg_scan(lf, jnp.add, 0.0, True))
    a = g - bcum
    amax = jnp.where(fwd, _seg_scan(a, jnp.maximum, -jnp.inf, False),
                     _seg_scan(a, jnp.maximum, -jnp.inf, True))
    keep = lambda v: jnp.where(live, v, 0.0)
    return (keep(bcum) + pltpu.roll(keep(a), GATE_A_LANE, axis=1)
            + pltpu.roll(keep(amax), GATE_AMAX_LANE, axis=1))


def _mlstm_kernel(qkf_ref, vf_ref, cf_ref, rf_ref, qkb_ref, vb_ref, cb_ref, rb_ref,
                  hf_ref, hb_ref, c_st, cb_st, m_st):
    @pl.when(pl.program_id(1) == 0)
    def _():
        c_st[...] = jnp.zeros_like(c_st)
        cb_st[...] = jnp.zeros_like(cb_st)
        m_st[...] = jnp.zeros_like(m_st)

    nh, hd = MLSTM_HEADS, MLSTM_HD
    t_idx = lax.broadcasted_iota(jnp.int32, (CHUNK, CHUNK), 0)
    s_idx = lax.broadcasted_iota(jnp.int32, (CHUNK, CHUNK), 1)
    scale = hd ** -0.5
    ones_blk = jnp.ones((CHUNK, LANES), BF16)
    rep = lambda x, n: jnp.concatenate([x] * n, axis=1)
    lane_dense = lambda col: jnp.broadcast_to(col, (CHUNK, LANES))
    dirs = ((qkf_ref, vf_ref, cf_ref, rf_ref, hf_ref, False), (qkb_ref, vb_ref, cb_ref, rb_ref, hb_ref, True))
    for d, (qk_ref, v_ref, col_ref, row_ref, o_ref, reverse) in enumerate(dirs):
        mask = (s_idx >= t_idx) if reverse else (s_idx <= t_idx)
        last = 0 if reverse else CHUNK - 1
        for h in range(nh):
            j = d * nh + h
            hs = slice(h * hd, (h + 1) * hd)
            q = qk_ref[0, :, hs]
            k = qk_ref[0, :, (nh + h) * hd:(nh + h + 1) * hd] * scale
            v_aug = jnp.concatenate([v_ref[0, :, hs], ones_blk], axis=1)
            gl = _gate_lane(d, h)
            bc = lane_dense(col_ref[0, :, gl:gl + 1])
            a_c = lane_dense(col_ref[0, :, GATE_A_LANE + gl:GATE_A_LANE + gl + 1])
            amax = lane_dense(col_ref[0, :, GATE_AMAX_LANE + gl:GATE_AMAX_LANE + gl + 1])
            a_r = row_ref[0, gl:gl + 1, :]
            dmat = jnp.exp(jnp.where(mask, a_r - rep(amax, CHUNK // LANES), -jnp.inf))
            s = lax.dot_general(q, k, (((1,), (1,)), ((), ())), preferred_element_type=F32) * dmat
            nd_l = jnp.dot(s.astype(BF16), v_aug, preferred_element_type=F32)
            a_last = amax[last:last + 1, :]
            btot = bc[last:last + 1, :]
            kw = rep(jnp.exp(a_c - a_last), hd // LANES) * k.astype(F32)
            upd = lax.dot_general(kw.astype(BF16), v_aug, (((0,), (0,)), ((), ())),
                                  preferred_element_type=F32)

            m_prev = m_st[j, 0:1, :]
            mt = jnp.maximum(amax, m_prev)
            f_l = jnp.exp(amax - mt)
            sc = jnp.exp(m_prev - mt)
            nd_c = jnp.dot(q, cb_st[j], preferred_element_type=F32)
            den = f_l * nd_l[:, hd:] + sc * nd_c[:, hd:]
            inv = 1.0 / jnp.maximum(jnp.abs(den), jnp.exp(-(bc + mt)))
            num = rep(f_l, hd // LANES) * nd_l[:, :hd] + rep(sc, hd // LANES) * nd_c[:, :hd]
            o_ref[0, :, hs] = (num * rep(inv, hd // LANES)).astype(o_ref.dtype)

            m_last = jnp.maximum(a_last, m_prev)
            dec = jnp.exp(m_prev - m_last)
            f_u = jnp.exp(a_last - m_last)
            wide = (hd + LANES) // LANES
            c_new = rep(dec, wide) * c_st[j] + rep(f_u, wide) * upd
            c_st[j] = c_new
            cb_st[j] = c_new.astype(BF16)
            m_st[j] = jnp.broadcast_to(btot + m_last, m_st.shape[1:])


def mlstm_scan(qk, v, cols, rows, out_dtype):
    bsz, l, _ = qk.shape
    w = D_MODEL
    assert l % CHUNK == 0
    nc = l // CHUNK
    fwd = lambda cb: (lambda b, c: (b, c, cb))
    bwd = lambda cb: (lambda b, c: (b, nc - 1 - c, cb))
    blk = lambda im: pl.BlockSpec((1, CHUNK, w), im)
    ncol = cols.shape[-1]
    in_specs = []
    for mk in (fwd, bwd):
        in_specs += [pl.BlockSpec((1, CHUNK, 2 * w), mk(0)), blk(mk(0)),
                     pl.BlockSpec((1, CHUNK, ncol), mk(0)),
                     pl.BlockSpec((1, rows.shape[1], CHUNK),
                                  (lambda b, c: (b, 0, c)) if mk is fwd else (lambda b, c: (b, 0, nc - 1 - c)))]
    nst = 2 * MLSTM_HEADS
    shp = jax.ShapeDtypeStruct((bsz, l, w), out_dtype)
    return pl.pallas_call(
        _mlstm_kernel,
        grid=(bsz, nc),
        in_specs=in_specs,
        out_specs=[blk(fwd(0)), blk(bwd(0))],
        out_shape=[shp, shp],
        scratch_shapes=[pltpu.VMEM((nst, MLSTM_HD, MLSTM_HD + LANES), F32),
                        pltpu.VMEM((nst, MLSTM_HD, MLSTM_HD + LANES), BF16),
                        pltpu.VMEM((nst, 8, LANES), F32)],
        compiler_params=_params("parallel", "arbitrary"),
        name="mlstm_scan",
    )(qk, v, cols, rows, qk, v, cols, rows)


def _rope_table_kernel(inv_ref, cos_ref, sin_ref):
    tl = cos_ref.shape[0]
    pos = (lax.broadcasted_iota(jnp.int32, cos_ref.shape, 0) + pl.program_id(0) * tl).astype(F32)
    ang = pos * inv_ref[...]
    cos_ref[...] = jnp.cos(ang)
    sin_ref[...] = jnp.sin(ang)


def rope_tables(l, tl=1024):
    half = RET_HDK // 2
    tl = min(tl, l)
    inv = (ROPE_BASE ** (-np.arange(0, RET_HDK, 2, dtype=np.float32) / RET_HDK)).astype(np.float32)
    spec = pl.BlockSpec((tl, half), lambda i: (i, 0))
    shp = jax.ShapeDtypeStruct((l, half), F32)
    return pl.pallas_call(
        _rope_table_kernel,
        grid=(l // tl,),
        in_specs=[pl.BlockSpec((1, half), lambda i: (0, 0))],
        out_specs=[spec, spec],
        out_shape=[shp, shp],
        compiler_params=_params("parallel"),
        name="rope_tables",
    )(jnp.asarray(inv).reshape(1, half))


def _rope(x, cos, sin):
    half = x.shape[-1] // 2
    x1, x2 = x[:, :half], x[:, half:]
    return jnp.concatenate([x1 * cos - x2 * sin, x1 * sin + x2 * cos], axis=-1)


def _odd_proj_kernel(x_ref, g_ref, w_ref, cos_ref, sin_ref, qk_ref, v_ref, gate_ref, *, tn):
    x = x_ref[...]
    xb = (x * lax.rsqrt(jnp.mean(x * x, axis=-1, keepdims=True) + EPS) * g_ref[...]).astype(BF16)
    cos, sin = cos_ref[...], sin_ref[...]
    nh, dk = RET_HEADS, RET_HDK
    for j in range(2 * nh):
        r = jnp.dot(xb, w_ref[:, j * dk:(j + 1) * dk], preferred_element_type=F32)
        r = _rope(r, cos, sin)
        if j < nh:
            r = r * dk ** -0.5
        qk_ref[:, j * dk:(j + 1) * dk] = r.astype(qk_ref.dtype)
    c0 = 2 * nh * dk
    vw = v_ref.shape[1]
    for j in range(vw // tn):
        cols = slice(j * tn, (j + 1) * tn)
        v_ref[:, cols] = jnp.dot(xb, w_ref[:, c0 + j * tn:c0 + (j + 1) * tn],
                                 preferred_element_type=F32).astype(v_ref.dtype)
        gate = jnp.dot(xb, w_ref[:, c0 + vw + j * tn:c0 + vw + (j + 1) * tn], preferred_element_type=F32)
        gate_ref[:, cols] = _silu(gate).astype(gate_ref.dtype)


def odd_proj(x, g, w, cos, sin, seq_len, tm=512, tn=512):
    m, d = x.shape
    qw, vw = RET_HEADS * RET_HDK, RET_HEADS * RET_HDV
    half = RET_HDK // 2
    tm = min(tm, seq_len)
    assert m % seq_len == 0 and seq_len % tm == 0
    tps = seq_len // tm
    row = lambda n: pl.BlockSpec((tm, n), lambda i: (i, 0))
    tab = pl.BlockSpec((tm, half), lambda i: (i % tps, 0))
    return pl.pallas_call(
        functools.partial(_odd_proj_kernel, tn=tn),
        grid=(m // tm,),
        in_specs=[row(d), pl.BlockSpec((1, d), lambda i: (0, 0)),
                  pl.BlockSpec(w.shape, lambda i: (0, 0), pipeline_mode=pl.Buffered(1)), tab, tab],
        out_specs=[row(2 * qw), row(vw), row(vw)],
        out_shape=[jax.ShapeDtypeStruct((m, 2 * qw), BF16), jax.ShapeDtypeStruct((m, vw), BF16),
                   jax.ShapeDtypeStruct((m, vw), BF16)],
        compiler_params=_params("parallel"),
        name="odd_proj",
    )(x, g.reshape(1, d).astype(F32), w, cos, sin)


def _retention_kernel(qkf_ref, vf_ref, qkb_ref, vb_ref, logit_ref, of_ref, ob_ref, r_st, rb_st):
    @pl.when(pl.program_id(1) == 0)
    def _():
        r_st[...] = jnp.zeros_like(r_st)
        rb_st[...] = jnp.zeros_like(rb_st)

    nh, dk, dv = RET_HEADS, RET_HDK, RET_HDV
    t_idx = lax.broadcasted_iota(jnp.int32, (CHUNK, CHUNK), 0)
    s_idx = lax.broadcasted_iota(jnp.int32, (CHUNK, CHUNK), 1)
    pos = lax.broadcasted_iota(jnp.int32, (CHUNK, LANES), 0).astype(F32)
    rep = lambda x, n: jnp.concatenate([x] * n, axis=1)
    dirs = ((qkf_ref, vf_ref, of_ref, False), (qkb_ref, vb_ref, ob_ref, True))
    for d, (qk_ref, v_ref, o_ref, reverse) in enumerate(dirs):
        rel = ((s_idx - t_idx) if reverse else (t_idx - s_idx)).astype(F32)
        for h in range(nh):
            j = d * nh + h
            lg = _log_sigmoid(logit_ref[j])[0:1, :]
            dmask = jnp.where(rel >= 0, jnp.exp(rep(lg, CHUNK // LANES) * jnp.maximum(rel, 0.0)), 0.0)
            if reverse:
                q_dec = jnp.exp(lg * (CHUNK - pos))
                k_dec = jnp.exp(lg * pos)
            else:
                q_dec = jnp.exp(lg * (pos + 1.0))
                k_dec = jnp.exp(lg * (CHUNK - 1.0 - pos))
            c_dec = jnp.exp(lg * CHUNK)
            q = qk_ref[0, :, h * dk:(h + 1) * dk]
            k = qk_ref[0, :, (nh + h) * dk:(nh + h + 1) * dk]
            v = v_ref[0, :, h * dv:(h + 1) * dv]
            s = lax.dot_general(q, k, (((1,), (1,)), ((), ())), preferred_element_type=F32) * dmask
            out = jnp.dot(s.astype(BF16), v, preferred_element_type=F32)
            out = out + jnp.dot((q.astype(F32) * rep(q_dec, dk // LANES)).astype(BF16), rb_st[j],
                                preferred_element_type=F32)
            o_ref[0, :, h * dv:(h + 1) * dv] = out.astype(o_ref.dtype)
            upd = lax.dot_general((k.astype(F32) * rep(k_dec, dk // LANES)).astype(BF16), v,
                                  (((0,), (0,)), ((), ())), preferred_element_type=F32)
            r_new = rep(c_dec, dv // LANES) * r_st[j] + upd
            r_st[j] = r_new
            rb_st[j] = r_new.astype(BF16)


def retention_scan(qk, v, logit_b, out_dtype):
    bsz, l, _ = qk.shape
    nc = l // CHUNK
    qw, vw = RET_HEADS * RET_HDK, RET_HEADS * RET_HDV
    in_specs = []
    for rev in (False, True):
        ci = (lambda c: nc - 1 - c) if rev else (lambda c: c)
        in_specs += [pl.BlockSpec((1, CHUNK, 2 * qw), lambda b, c, ci=ci: (b, ci(c), 0)),
                     pl.BlockSpec((1, CHUNK, vw), lambda b, c, ci=ci: (b, ci(c), 0))]
    in_specs.append(pl.BlockSpec((2 * RET_HEADS, 8, LANES), lambda b, c: (0, 0, 0)))
    shp = jax.ShapeDtypeStruct((bsz, l, vw), out_dtype)
    return pl.pallas_call(
        _retention_kernel,
        grid=(bsz, nc),
        in_specs=in_specs,
        out_specs=[pl.BlockSpec((1, CHUNK, vw), lambda b, c: (b, c, 0)),
                   pl.BlockSpec((1, CHUNK, vw), lambda b, c: (b, nc - 1 - c, 0))],
        out_shape=[shp, shp],
        scratch_shapes=[pltpu.VMEM((2 * RET_HEADS, RET_HDK, RET_HDV), F32),
                        pltpu.VMEM((2 * RET_HEADS, RET_HDK, RET_HDV), BF16)],
        compiler_params=_params("parallel", "arbitrary"),
        name="retention_scan",
    )(qk, v, qk, v, logit_b)


def _hyena_filter_kernel(cst_ref, w1_ref, w2_ref, w3a_ref, w3b_ref, delta_ref, kern_ref, asum_ref, *, l):
    i = pl.program_id(0)
    tr = kern_ref.shape[0]
    hr = tr // 2
    half = LANES // 2
    row = lax.broadcasted_iota(jnp.int32, (hr, LANES), 0) + i * tr
    n_a, n_b = row, row + hr
    lag = lambda n: jnp.where(n < l, n, 2 * l - n).astype(F32)
    tt_a, tt_b = lag(n_a) / (l - 1.0), lag(n_b) / (l - 1.0)
    lo = lax.broadcasted_iota(jnp.int32, (hr, LANES), 1) < half
    p2 = jnp.where(lo, lag(n_a), lag(n_b))
    tt2 = jnp.where(lo, tt_a, tt_b)
    bands, phase, w1t, b1, f1, b2, f2 = (cst_ref[k:k + 1, :] for k in range(7))
    ang = (2.0 * math.pi / l) * bands * p2 + phase
    pre = jnp.dot(jnp.cos(ang), w1_ref[...], preferred_element_type=F32, precision=HI) + tt2 * w1t
    z = jnp.sin(f1 * (pre + b1))
    z = jnp.sin(f2 * (jnp.dot(z, w2_ref[...], preferred_element_type=F32, precision=HI) + b2))
    rep = lambda x: jnp.concatenate([x] * (kern_ref.shape[1] // LANES), axis=1)
    dabs = jnp.abs(delta_ref[...])
    total = jnp.zeros((1, kern_ref.shape[1]), F32)
    z_hi = z.astype(BF16)
    z_lo = (z - z_hi.astype(F32)).astype(BF16)
    z3 = jnp.concatenate([z_hi, z_lo, z_hi], axis=1)
    for part, (w3_ref, n, tt) in enumerate(((w3a_ref, n_a, tt_a), (w3b_ref, n_b, tt_b))):
        hk = jnp.dot(z3, w3_ref[...], preferred_element_type=F32)
        hk = hk * (jnp.exp(-rep(tt) * dabs) + HYENA_SHIFT)
        hk = jnp.where(rep(n) == l, 0.0, hk)
        kern_ref[part * hr:(part + 1) * hr, :] = hk
        total = total + jnp.sum(jnp.abs(hk), axis=0, keepdims=True)

    @pl.when(i == 0)
    def _():
        asum_ref[...] = jnp.zeros_like(asum_ref)

    asum_ref[...] += jnp.broadcast_to(total, asum_ref.shape)


def hyena_filter(l, w1, b1, f1, w2, b2, f2, w3, delta, tr=1024):
    wd = HYENA_WIDTH
    hid = w1.shape[1]
    half = LANES // 2
    nb = HYENA_BANDS
    tr = min(tr, l)
    w1, w2, w3 = w1.astype(F32), w2.astype(F32), w3.astype(F32)
    two = lambda v: jnp.tile(jnp.pad(v.astype(F32).reshape(1, -1), ((0, 0), (0, half - v.shape[-1]))), (1, 2))
    bands = np.zeros((1, half), np.float32)
    bands[0, :nb] = bands[0, nb:2 * nb] = np.linspace(1e-4, nb - 1, nb, dtype=np.float32)
    phase = np.zeros((1, half), np.float32)
    phase[0, nb:2 * nb] = 0.5 * np.pi
    cst = jnp.concatenate([two(jnp.asarray(bands)), two(jnp.asarray(phase)), two(w1[0]), two(b1), two(f1),
                           two(b2), two(f2), jnp.zeros((1, LANES), F32)], axis=0)
    blk = lambda a: jnp.pad(a, ((0, half - a.shape[0]), (0, half - a.shape[1])))
    diag2 = lambda a: jnp.concatenate([jnp.pad(blk(a), ((0, 0), (0, half))),
                                       jnp.pad(blk(a), ((0, 0), (half, 0)))], axis=0)
    w3h = jnp.pad(w3, ((0, half - hid), (0, 0)))

    def split3(a):
        hi = a.astype(BF16)
        lo = (a - hi.astype(F32)).astype(BF16)
        return jnp.concatenate([hi, hi, lo], axis=0)

    w3a = split3(jnp.pad(w3h, ((0, half), (0, 0))))
    w3b = split3(jnp.pad(w3h, ((half, 0), (0, 0))))
    full = lambda shape: pl.BlockSpec(shape, lambda i: (0, 0))
    half_sel = lambda i: (0, (i * tr) // l)
    return pl.pallas_call(
        functools.partial(_hyena_filter_kernel, l=l),
        grid=(2 * l // tr,),
        in_specs=[full((8, LANES)), full((LANES, LANES)), full((LANES, LANES)),
                  pl.BlockSpec((3 * LANES, wd), half_sel), pl.BlockSpec((3 * LANES, wd), half_sel),
                  pl.BlockSpec((1, wd), half_sel)],
        out_specs=[pl.BlockSpec((tr, wd), lambda i: (i, 0)), pl.BlockSpec((8, wd), lambda i: (0, 0))],
        out_shape=[jax.ShapeDtypeStruct((2 * l, wd), F32), jax.ShapeDtypeStruct((8, wd), F32)],
        compiler_params=_params("arbitrary"),
        name="hyena_filter",
    )(cst, diag2(w1[1:]), diag2(w2), w3a, w3b, delta.reshape(1, 2 * wd).astype(F32))


def _dft_consts(n1, n2):
    n = n1 * n2
    k1 = np.arange(n1)
    f1 = np.exp(-2j * np.pi * np.outer(k1, k1) / n1)
    k2 = np.arange(n2)
    f2 = np.exp(-2j * np.pi * np.outer(k2, k2) / n2)
    tw = np.exp(-2j * np.pi * np.outer(k1, k2) / n)
    g = f2[None, :, :] * tw[:, None, :]
    ginv = np.conj(np.transpose(g, (0, 2, 1))) / n
    f1inv = np.conj(f1.T)
    return f1, g, ginv, f1inv


def _stack(c):
    return jnp.asarray(np.concatenate([c.real, c.imag], axis=-2).astype(np.float32)).astype(BF16)


def _cdot(fs, xr, xi, rows):
    p = jnp.dot(fs, xr, preferred_element_type=F32)
    if xi is None:
        return p[:rows], p[rows:]
    q = jnp.dot(fs, xi, preferred_element_type=F32)
    return p[:rows] - q[rows:], q[:rows] + p[rows:]


FFT_ROW_CHUNK = 512


def _kron_eye(f, s_blk):
    k = np.kron(f, np.eye(s_blk))
    to = lambda a: jnp.asarray(a.astype(np.float32)).astype(BF16)
    return to(k.real), to(k.imag)


def _strided_stage_kernel(*refs, has_imag_in, n_out):
    mr_ref, mi_ref, xr_ref = refs[:3]
    xi_ref = refs[3] if has_imag_in else None
    n_in = 4 if has_imag_in else 3
    if len(refs[n_in].shape) == 5:
        outs = [refs[n_in].at[part, 0] for part in range(n_out)]
    else:
        outs = [r.at[0] for r in refs[n_in:]]
    flat = lambda r: r.reshape(r.shape[0] * r.shape[1], r.shape[2])
    xr = flat(xr_ref[0]).astype(BF16)
    xi = flat(xi_ref[0]).astype(BF16) if has_imag_in else None
    rows = mr_ref.shape[0]
    rc = min(FFT_ROW_CHUNK, rows)
    s_blk = xr_ref.shape[2]
    for c0 in range(0, rows, rc):
        mr = mr_ref[c0:c0 + rc, :]
        mi = mi_ref[c0:c0 + rc, :]
        re = jnp.dot(mr, xr, preferred_element_type=F32)
        im = jnp.dot(mi, xr, preferred_element_type=F32) if n_out == 2 else None
        if has_imag_in:
            re = re - jnp.dot(mi, xi, preferred_element_type=F32)
            if n_out == 2:
                im = im + jnp.dot(mr, xi, preferred_element_type=F32)
        k0, k1 = c0 // s_blk, (c0 + rc) // s_blk
        for o_ref, val in zip(outs, (re, im)):
            o_ref[k0:k1] = val.reshape(rc // s_blk, s_blk, val.shape[1]).astype(o_ref.dtype)


def fft_stage1(x, n1, rows_in, pairs, complex_in, s_blk, out_dtype):
    c = x.shape[-1]
    n2 = FFT_N2
    xv = x.reshape(x.shape[0], rows_in, n2, c)
    mr, mi = _kron_eye(_dft_consts(n1, n2)[0][:, :rows_in], s_blk)
    mat = pl.BlockSpec(mr.shape, lambda p, j: (0, 0), pipeline_mode=pl.Buffered(1))
    in_specs = [mat, mat, pl.BlockSpec((1, rows_in, s_blk, c), lambda p, j: (p, 0, j, 0))]
    args = [mr, mi, xv]
    if complex_in:
        in_specs.append(pl.BlockSpec((1, rows_in, s_blk, c), lambda p, j: (p + pairs, 0, j, 0)))
        args.append(xv)
    out_spec = pl.BlockSpec((1, n1, s_blk, c), lambda p, j: (p, 0, j, 0))
    shp = jax.ShapeDtypeStruct((pairs, n1, n2, c), out_dtype)
    return pl.pallas_call(
        functools.partial(_strided_stage_kernel, has_imag_in=complex_in, n_out=2),
        grid=(pairs, n2 // s_blk),
        in_specs=in_specs,
        out_specs=[out_spec, out_spec],
        out_shape=[shp, shp],
        compiler_params=_params("parallel", "parallel"),
        name="fft_stage1",
    )(*args)


def _fft_mid_kernel(*refs, conv):
    n2 = FFT_N2
    if conv:
        gs_ref, gis_ref, ar_ref, ai_ref, kr_ref, ki_ref, br_ref, bi_ref = refs
    else:
        gs_ref, ar_ref, ai_ref, br_ref, bi_ref = refs
    xr, xi = _cdot(gs_ref[0], ar_ref[0, 0].astype(BF16), ai_ref[0, 0].astype(BF16), n2)
    if conv:
        kr, ki = kr_ref[0].astype(F32), ki_ref[0].astype(F32)
        yr = xr * kr - xi * ki
        yi = xr * ki + xi * kr
        xr, xi = _cdot(gis_ref[0], yr.astype(BF16), yi.astype(BF16), n2)
    br_ref[0, 0] = xr.astype(br_ref.dtype)
    bi_ref[0, 0] = xi.astype(bi_ref.dtype)


def fft_mid(ar, ai, n1, c, kf=None, ct=1024):
    n2 = FFT_N2
    pairs = ar.shape[0]
    _, g, ginv, _ = _dft_consts(n1, n2)
    a4 = lambda a: a
    mat = pl.BlockSpec((1, 2 * n2, n2), lambda k, cb, p: (k, 0, 0))
    dat = pl.BlockSpec((1, 1, n2, ct), lambda k, cb, p: (p, k, 0, cb))
    in_specs = [mat]
    args = [_stack(g)]
    if kf is not None:
        in_specs.append(mat)
        args.append(_stack(ginv))
    in_specs += [dat, dat]
    args += [a4(ar), a4(ai)]
    if kf is not None:
        fil = pl.BlockSpec((1, n2, ct), lambda k, cb, p: (k, 0, cb))
        in_specs += [fil, fil]
        args += [kf[0], kf[1]]
    shp = jax.ShapeDtypeStruct((pairs, n1, n2, c), BF16)
    br, bi = pl.pallas_call(
        functools.partial(_fft_mid_kernel, conv=kf is not None),
        grid=(n1, c // ct, pairs),
        in_specs=in_specs,
        out_specs=[dat, dat],
        out_shape=[shp, shp],
        compiler_params=_params("parallel", "parallel", "arbitrary"),
        name="fft_mid",
    )(*args)
    return br, bi


def fft_stage1_inv(br, bi, n1, rows_out, c, complex_out, s_blk):
    pairs = br.shape[0]
    n2 = FFT_N2
    mr, mi = _kron_eye(_dft_consts(n1, n2)[3][:rows_out, :], s_blk)
    parts = 2 if complex_out else 1
    mat = pl.BlockSpec(mr.shape, lambda p, j: (0, 0), pipeline_mode=pl.Buffered(1))
    dat = pl.BlockSpec((1, n1, s_blk, c), lambda p, j: (p, 0, j, 0))
    y = pl.pallas_call(
        functools.partial(_strided_stage_kernel, has_imag_in=True, n_out=parts),
        grid=(pairs, n2 // s_blk),
        in_specs=[mat, mat, dat, dat],
        out_specs=pl.BlockSpec((parts, 1, rows_out, s_blk, c), lambda p, j: (0, p, 0, j, 0)),
        out_shape=jax.ShapeDtypeStruct((parts, pairs, rows_out, n2, c), BF16),
        compiler_params=_params("parallel", "parallel"),
        name="fft_stage1_inv",
    )(mr, mi, br, bi)
    return y.reshape(parts * pairs, rows_out * n2, c)


def hyena_long_conv(t, kern):
    bsz, l, c = t.shape
    assert l % FFT_N2 == 0 and kern.shape == (2 * l, c)
    n1 = 2 * l // FFT_N2
    kr, ki = fft_stage1(kern[None], n1, n1, 1, False, 8, F32)
    kr, ki = fft_mid(kr, ki, n1, c)
    kf = (kr.reshape(n1, FFT_N2, c), ki.reshape(n1, FFT_N2, c))
    complex_in = bsz % 2 == 0
    pairs = bsz // 2 if complex_in else bsz
    ar, ai = fft_stage1(t, n1, n1 // 2, pairs, complex_in, BF16_SUBLANES, BF16)
    br, bi = fft_mid(ar, ai, n1, c, kf)
    return fft_stage1_inv(br, bi, n1, n1 // 2, c, complex_in, BF16_SUBLANES)


def _head_ln(x, nheads, g):
    hd = x.shape[-1] // nheads
    outs = []
    for h in range(nheads):
        seg = x[:, h * hd:(h + 1) * hd]
        mu = jnp.mean(seg, axis=-1, keepdims=True)
        cen = seg - mu
        var = jnp.mean(cen * cen, axis=-1, keepdims=True)
        outs.append(cen * lax.rsqrt(var + GN_EPS))
    return jnp.concatenate(outs, axis=-1) * g


def _cross_attn_tail(x, wqk_ref, vwo_ref, gq_ref, gf_ref, final_norm):
    xn = x * lax.rsqrt(jnp.mean(x * x, axis=-1, keepdims=True) + EPS) * gq_ref[...]
    s_all = jnp.dot(xn.astype(BF16), wqk_ref[0], preferred_element_type=F32)
    nm = s_all.shape[1] // CA_HEADS
    probs = []
    for h in range(CA_HEADS):
        s = s_all[:, h * nm:(h + 1) * nm]
        p = jnp.exp(s - jnp.max(s, axis=-1, keepdims=True))
        probs.append((p * (1.0 / jnp.sum(p, axis=-1, keepdims=True))).astype(BF16))
    y = x + jnp.dot(jnp.concatenate(probs, axis=-1), vwo_ref[0], preferred_element_type=F32)
    if final_norm:
        y = y * lax.rsqrt(jnp.mean(y * y, axis=-1, keepdims=True) + EPS) * gf_ref[...]
    return y


def _even_out_kernel(x_ref, hf_ref, hb_ref, ga_ref, xg_ref, y_ref, t_ref, g_ref, skip_ref, asum_ref, w_ref,
                     wqk_ref, vwo_ref, gq_ref, gf_ref, out_ref, *, final_norm):
    wd = D_MODEL
    ln = _head_ln(hf_ref[0].astype(F32) + hb_ref[0].astype(F32), MLSTM_HEADS, g_ref[...])
    ya = ln * ga_ref[0].astype(F32)
    conv = y_ref[0].astype(F32) * (1.0 / asum_ref[0:1, :])
    yb = xg_ref[0].astype(F32) * (conv + skip_ref[...] * t_ref[0].astype(F32))
    mix = jnp.dot(ya.astype(BF16), w_ref[0:wd, :], preferred_element_type=F32)
    mix = mix + jnp.dot(yb.astype(BF16), w_ref[wd:2 * wd, :], preferred_element_type=F32)
    out_ref[0] = _cross_attn_tail(x_ref[0] + mix, wqk_ref, vwo_ref, gq_ref, gf_ref, final_norm)


def _odd_out_kernel(x_ref, of_ref, ob_ref, gate_ref, g_ref, w_ref,
                    wqk_ref, vwo_ref, gq_ref, gf_ref, out_ref, *, final_norm):
    o = _head_ln(of_ref[0].astype(F32) + ob_ref[0].astype(F32), RET_HEADS, g_ref[...])
    y = gate_ref[0].astype(F32) * o
    mix = jnp.dot(y.astype(BF16), w_ref[...], preferred_element_type=F32)
    out_ref[0] = _cross_attn_tail(x_ref[0] + mix, wqk_ref, vwo_ref, gq_ref, gf_ref, final_norm)


def mixer_out(kernel, x, acts, vecs, w_out, ca, tm=512):
    bsz, l, d = x.shape
    tm = min(tm, l)
    wqk, vwo, gq, gf, final_norm = ca
    row = lambda a: pl.BlockSpec((1, tm, a.shape[-1]), lambda b, i: (b, i, 0))
    res = lambda a: pl.BlockSpec(a.shape, lambda b, i: (0, 0), pipeline_mode=pl.Buffered(1))
    per_b = lambda a: pl.BlockSpec((1,) + a.shape[1:], lambda b, i: (b, 0, 0))
    gq, gf = gq.reshape(1, d).astype(F32), gf.reshape(1, d).astype(F32)
    return pl.pallas_call(
        functools.partial(kernel, final_norm=final_norm),
        grid=(bsz, l // tm),
        in_specs=[row(x)] + [row(a) for a in acts] + [res(v) for v in vecs] + [res(w_out)]
                 + [per_b(wqk), per_b(vwo), res(gq), res(gf)],
        out_specs=row(x),
        out_shape=jax.ShapeDtypeStruct((bsz, l, d), F32),
        compiler_params=_params("parallel", "parallel"),
        name=kernel.__name__.strip("_").replace("_kernel", "_ca"),
    )(x, *acts, *vecs, w_out, wqk, vwo, gq, gf)


def _ca_fold_kernel(kv_ref, wq_ref, wo_ref, wqk_ref, vwo_ref):
    d = D_MODEL
    for h in range(CA_HEADS):
        hs = slice(h * CA_HD, (h + 1) * CA_HD)
        k = kv_ref[0, :, hs]
        v = kv_ref[0, :, d + h * CA_HD:d + (h + 1) * CA_HD]
        nm = k.shape[0]
        wqk = lax.dot_general(wq_ref[:, hs], k, (((1,), (1,)), ((), ())), preferred_element_type=F32)
        wqk_ref[0, :, h * nm:(h + 1) * nm] = (wqk * CA_HD ** -0.5).astype(wqk_ref.dtype)
        vwo_ref[0, h * nm:(h + 1) * nm, :] = jnp.dot(v, wo_ref[hs, :],
                                                     preferred_element_type=F32).astype(vwo_ref.dtype)


def ca_fold(kv, wq, wo):
    bsz, nm, d2 = kv.shape
    d = d2 // 2
    mat = pl.BlockSpec((d, d), lambda b: (0, 0))
    return pl.pallas_call(
        _ca_fold_kernel,
        grid=(bsz,),
        in_specs=[pl.BlockSpec((1, nm, d2), lambda b: (b, 0, 0)), mat, mat],
        out_specs=[pl.BlockSpec((1, d, CA_HEADS * nm), lambda b: (b, 0, 0)),
                   pl.BlockSpec((1, CA_HEADS * nm, d), lambda b: (b, 0, 0))],
        out_shape=[jax.ShapeDtypeStruct((bsz, d, CA_HEADS * nm), BF16),
                   jax.ShapeDtypeStruct((bsz, CA_HEADS * nm, d), BF16)],
        compiler_params=_params("parallel"),
        name="ca_fold",
    )(kv, wq, wo)


def _even_mixer(x, p, i, ca):
    bsz, l, d = x.shape
    w = D_MODEL
    nh = MLSTM_HEADS
    m = bsz * l
    g_mix = p['norm_mix_g_layer']
    w_in = p['even_w_in'][i]
    gate0 = 5 * w
    hy0 = gate0 + N_GATE_COLS
    w_conv = jnp.concatenate([w_in[:, :2 * w], w_in[:, hy0:hy0 + 3 * w]], axis=1).astype(BF16)
    w_rest = jnp.concatenate([w_in[:, 2 * w:gate0], w_in[:, hy0 + 3 * w:]], axis=1).astype(BF16)
    w_gate = jnp.pad(w_in[:, gate0:hy0], ((0, 0), (0, LANES - N_GATE_COLS))).astype(BF16)
    conv_w = jnp.concatenate([p['mlstm_conv_w'][i], p['hyena_conv_w'][i]], axis=1).astype(F32)
    conv_b = jnp.concatenate([p['mlstm_conv_b'][i], p['hyena_conv_b'][i]]).astype(F32).reshape(1, 5 * w)
    bias = jnp.pad(p['mlstm_gate_bias'][i].astype(F32).reshape(1, N_GATE_COLS),
                   ((0, 0), (0, LANES - N_GATE_COLS)))
    xf = x.reshape(m, d)
    qk, xg, t, v, ga, cols = even_proj(xf, g_mix, w_conv, w_rest, w_gate, conv_w, conv_b, bias, l)
    cols = cols.reshape(bsz, l, LANES)
    rows = jnp.swapaxes(cols[..., GATE_A_LANE:GATE_A_LANE + N_GATE_COLS], 1, 2)

    hf, hb = mlstm_scan(qk.reshape(bsz, l, 2 * w), v.reshape(bsz, l, w), cols, rows, BF16)

    t = t.reshape(bsz, l, w)
    kern, asum = hyena_filter(l, p['hyena_w1'][i], p['hyena_b1'][i], p['hyena_freq1'][i], p['hyena_w2'][i],
                              p['hyena_b2'][i], p['hyena_freq2'][i], p['hyena_w3'][i], p['hyena_delta'][i])
    y = hyena_long_conv(t, kern)

    b3 = lambda a: a.reshape(bsz, l, w)
    vec = lambda a: a.reshape(1, w).astype(F32)
    return mixer_out(_even_out_kernel, x, [hf, hb, b3(ga), b3(xg), y, t],
                     [vec(p['mlstm_norm_g'][i]), vec(p['hyena_skip'][i]), asum],
                     p['even_w_out'][i].astype(BF16), ca)


def _odd_mixer(x, p, i, cos, sin, ca):
    bsz, l, d = x.shape
    m = bsz * l
    xf = x.reshape(m, d)
    qk, v, gate = odd_proj(xf, p['norm_mix_g_layer'], p['odd_w_in'][i].astype(BF16), cos, sin, l)
    logit = p['ret_decay_logit'][i].astype(F32).reshape(2 * RET_HEADS, 1, 1)
    logit_b = jnp.broadcast_to(logit, (2 * RET_HEADS, 8, LANES))
    vw = RET_HEADS * RET_HDV
    of, ob = retention_scan(qk.reshape(bsz, l, -1), v.reshape(bsz, l, vw), logit_b, BF16)
    return mixer_out(_odd_out_kernel, x, [of, ob, gate.reshape(bsz, l, vw)],
                     [p['ret_norm_g'][i].reshape(1, vw).astype(F32)], p['odd_w_out'][i].astype(BF16), ca)


def _trunk(x, mem, p, cos, sin):
    depth = p['norm_mix_g'].shape[0]
    bsz, nm, d = mem.shape
    for layer in range(depth):
        i = layer // 2
        p['norm_mix_g_layer'] = p['norm_mix_g'][layer]
        kv, = rms_matmul(mem.reshape(bsz * nm, d), p['norm_mem_g'][layer], [p['ca_wkv'][layer].astype(BF16)],
                         [BF16], tm=nm)
        wqk, vwo = ca_fold(kv.reshape(bsz, nm, 2 * d), p['ca_wq'][layer].astype(BF16),
                           p['ca_wo'][layer].astype(BF16))
        ca = (wqk, vwo, p['norm_ca_g'][layer], p['norm_final_g'], layer == depth - 1)
        x = _even_mixer(x, p, i, ca) if layer % 2 == 0 else _odd_mixer(x, p, i, cos, sin, ca)
    return x


def kernel(x_prompt, x_sample, mem_prompt, mem_sample, norm_mix_g, norm_ca_g, norm_mem_g, norm_final_g, even_w_in, mlstm_conv_w, mlstm_conv_b, mlstm_gate_bias, mlstm_norm_g, hyena_conv_w, hyena_conv_b, hyena_w1, hyena_b1, hyena_freq1, hyena_w2, hyena_b2, hyena_freq2, hyena_w3, hyena_delta, hyena_skip, even_w_out, odd_w_in, ret_decay_logit, ret_norm_g, odd_w_out, ca_wq, ca_wkv, ca_wo):
    p = {'norm_mix_g': norm_mix_g, 'norm_ca_g': norm_ca_g, 'norm_mem_g': norm_mem_g, 'norm_final_g': norm_final_g,
         'even_w_in': even_w_in, 'mlstm_conv_w': mlstm_conv_w, 'mlstm_conv_b': mlstm_conv_b,
         'mlstm_gate_bias': mlstm_gate_bias, 'mlstm_norm_g': mlstm_norm_g,
         'hyena_conv_w': hyena_conv_w, 'hyena_conv_b': hyena_conv_b, 'hyena_w1': hyena_w1, 'hyena_b1': hyena_b1,
         'hyena_freq1': hyena_freq1, 'hyena_w2': hyena_w2, 'hyena_b2': hyena_b2, 'hyena_freq2': hyena_freq2,
         'hyena_w3': hyena_w3, 'hyena_delta': hyena_delta, 'hyena_skip': hyena_skip, 'even_w_out': even_w_out,
         'odd_w_in': odd_w_in, 'ret_decay_logit': ret_decay_logit, 'ret_norm_g': ret_norm_g, 'odd_w_out': odd_w_out,
         'ca_wq': ca_wq, 'ca_wkv': ca_wkv, 'ca_wo': ca_wo}
    l_max = max(x_prompt.shape[1], x_sample.shape[1])
    cos, sin = rope_tables(l_max)
    y_prompt = _trunk(x_prompt, mem_prompt, dict(p), cos, sin)
    y_sample = _trunk(x_sample, mem_sample, dict(p), cos, sin)
    return (y_prompt, y_sample)
```

```python
import functools
import math

import numpy as np
import jax
import jax.numpy as jnp
from jax import lax
from jax.experimental import pallas as pl
from jax.experimental.pallas import tpu as pltpu

F32 = jnp.float32
BF16 = jnp.bfloat16

D_MODEL = 1024
EPS = 1e-6
GN_EPS = 1e-5
CHUNK = 256

MLSTM_HEADS = 4
MLSTM_HD = D_MODEL // MLSTM_HEADS
N_GATE_COLS = 4 * MLSTM_HEADS

HYENA_WIDTH = D_MODEL
HYENA_EMB = 33
HYENA_BANDS = (HYENA_EMB - 1) // 2
HYENA_SHIFT = 0.05
FFT_N2 = 256

RET_HEADS = 4
RET_HDK = D_MODEL // RET_HEADS
RET_HDV = 2 * D_MODEL // RET_HEADS
ROPE_BASE = 10000.0

CA_HEADS = 4
CA_HD = D_MODEL // CA_HEADS

LANES = 128
BF16_SUBLANES = 16
VMEM_LIMIT = 56 * 1024 * 1024

HI = lax.Precision.HIGHEST


def _params(*sem):
    return pltpu.CompilerParams(dimension_semantics=sem, vmem_limit_bytes=VMEM_LIMIT)


def _silu(x):
    return x * (1.0 / (1.0 + jnp.exp(-x)))


def _sigmoid(x):
    return 1.0 / (1.0 + jnp.exp(-x))


def _log_sigmoid(x):
    return jnp.minimum(x, 0.0) - jnp.log(1.0 + jnp.exp(-jnp.abs(x)))


def _rms_matmul_kernel(x_ref, g_ref, *refs, tn):
    nw = len(refs) // 2
    x = x_ref[...]
    xn = (x * lax.rsqrt(jnp.mean(x * x, axis=-1, keepdims=True) + EPS) * g_ref[...]).astype(BF16)
    for w_ref, o_ref in zip(refs[:nw], refs[nw:]):
        n = o_ref.shape[1]
        step = min(tn, n)
        for j in range(n // step):
            cols = slice(j * step, (j + 1) * step)
            o_ref[:, cols] = jnp.dot(xn, w_ref[:, cols], preferred_element_type=F32).astype(o_ref.dtype)


def rms_matmul(x, g, ws, out_dtypes, tm, tn=1024):
    m, d = x.shape
    w_specs = [pl.BlockSpec(w.shape, lambda i: (0, 0), pipeline_mode=pl.Buffered(1)) for w in ws]
    return pl.pallas_call(
        functools.partial(_rms_matmul_kernel, tn=tn),
        grid=(m // tm,),
        in_specs=[pl.BlockSpec((tm, d), lambda i: (i, 0)), pl.BlockSpec((1, d), lambda i: (0, 0))] + w_specs,
        out_specs=[pl.BlockSpec((tm, w.shape[1]), lambda i: (i, 0)) for w in ws],
        out_shape=[jax.ShapeDtypeStruct((m, w.shape[1]), dt) for w, dt in zip(ws, out_dtypes)],
        compiler_params=_params("parallel"),
        name="rms_matmul",
    )(x, g.reshape(1, d).astype(F32), *ws)


F32_SUBLANES = 8


def _even_proj_kernel(x_ref, xp_ref, xn_ref, g_ref, wc_ref, wr_ref, wg_ref, cw_ref, cb_ref, gb_ref,
                      qk_ref, xg_ref, t_ref, v_ref, ga_ref, cols_ref, *, tiles_per_seq, tn):
    i = pl.program_id(0)
    tm, d = x_ref.shape
    w, h = D_MODEL, F32_SUBLANES
    first = (i % tiles_per_seq) == 0
    last = (i % tiles_per_seq) == tiles_per_seq - 1
    g = g_ref[...]

    def norm(x):
        return x * lax.rsqrt(jnp.mean(x * x, axis=-1, keepdims=True) + EPS) * g

    xm = norm(x_ref[...])
    xp = jnp.where(first, 0.0, norm(xp_ref[...]))
    xn = jnp.where(last, 0.0, norm(xn_ref[...]))
    x_ext = jnp.concatenate([xp, xm, xn], axis=0).astype(BF16)
    xb = xm.astype(BF16)

    def conv(c0):
        cols = slice(c0, c0 + tn)
        r = jnp.dot(x_ext, wc_ref[:, cols], preferred_element_type=F32)
        t = r.shape[0]
        y = (pltpu.roll(r, 1, axis=0) * cw_ref[0:1, cols] + r * cw_ref[1:2, cols]
             + pltpu.roll(r, t - 1, axis=0) * cw_ref[2:3, cols] + cb_ref[:, cols])
        return y[h:h + tm]

    for j in range(2 * w // tn):
        qk_ref[:, j * tn:(j + 1) * tn] = _silu(conv(j * tn)).astype(qk_ref.dtype)
    rest = lambda group, j: jnp.dot(xb, wr_ref[:, group * w + j * tn:group * w + (j + 1) * tn],
                                    preferred_element_type=F32)
    for j in range(w // tn):
        cols = slice(j * tn, (j + 1) * tn)
        xg_ref[:, cols] = (conv(2 * w + j * tn) * _silu(rest(3, j))).astype(xg_ref.dtype)
        t_ref[:, cols] = (conv(3 * w + j * tn) * conv(4 * w + j * tn)).astype(t_ref.dtype)
        v_ref[:, cols] = rest(0, j).astype(v_ref.dtype)
        ga_ref[:, cols] = (_sigmoid(rest(1, j)) * _silu(rest(2, j))).astype(ga_ref.dtype)
    cols_ref[...] = _gate_prep(jnp.dot(xb, wg_ref[...], preferred_element_type=F32) + gb_ref[...])


def even_proj(x, g, w_conv, w_rest, w_gate, conv_w, conv_b, gate_bias, seq_len, tm=512, tn=512):
    m, d = x.shape
    w, h = D_MODEL, F32_SUBLANES
    tm = min(tm, seq_len)
    assert m % seq_len == 0 and seq_len % tm == 0 and tm % CHUNK == 0
    r = tm // h
    last_blk = m // h - 1
    res = lambda a: pl.BlockSpec(a.shape, lambda i: (0, 0), pipeline_mode=pl.Buffered(1))
    row = lambda n: pl.BlockSpec((tm, n), lambda i: (i, 0))
    return pl.pallas_call(
        functools.partial(_even_proj_kernel, tiles_per_seq=seq_len // tm, tn=tn),
        grid=(m // tm,),
        in_specs=[row(d),
                  pl.BlockSpec((h, d), lambda i: (jnp.maximum(i * r - 1, 0), 0)),
                  pl.BlockSpec((h, d), lambda i: (jnp.minimum((i + 1) * r, last_blk), 0)),
                  pl.BlockSpec((1, d), lambda i: (0, 0)),
                  res(w_conv), res(w_rest), res(w_gate), res(conv_w), res(conv_b), res(gate_bias)],
        out_specs=[row(2 * w), row(w), row(w), row(w), row(w), row(LANES)],
        out_shape=[jax.ShapeDtypeStruct((m, 2 * w), BF16), jax.ShapeDtypeStruct((m, w), BF16),
                   jax.ShapeDtypeStruct((m, w), BF16), jax.ShapeDtypeStruct((m, w), BF16),
                   jax.ShapeDtypeStruct((m, w), BF16), jax.ShapeDtypeStruct((m, LANES), F32)],
        compiler_params=_params("parallel"),
        name="even_proj",
    )(x, x, x, g.reshape(1, d).astype(F32), w_conv, w_rest, w_gate, conv_w, conv_b, gate_bias)


def _seg_scan(x, op, identity, reverse):
    t = x.shape[0]
    r = lax.broadcasted_iota(jnp.int32, x.shape, 0) % CHUNK
    k = 1
    while k < CHUNK:
        if reverse:
            shifted = pltpu.roll(x, t - k, axis=0)
            valid = r < CHUNK - k
        else:
            shifted = pltpu.roll(x, k, axis=0)
            valid = r >= k
        x = op(x, jnp.where(valid, shifted, identity))
        k *= 2
    return x


GATE_A_LANE = 16
GATE_AMAX_LANE = 32


def _gate_lane(d, h):
    return d * 2 * MLSTM_HEADS + h


def _gate_prep(g):
    nh = MLSTM_HEADS
    lane = lax.broadcasted_iota(jnp.int32, g.shape, 1)
    live = (lane < 4 * nh) & (lane % (2 * nh) < nh)
    fwd = lane < 2 * nh
    lf = pltpu.roll(_log_sigmoid(g), LANES - nh, axis=1)
    bcum = jnp.where(fwd, _seg_scan(lf, jnp.add, 0.0, False), _seg_scan(lf, jnp.add, 0.0, True))
    a = g - bcum
    amax = jnp.where(fwd, _seg_scan(a, jnp.maximum, -jnp.inf, False),
                     _seg_scan(a, jnp.maximum, -jnp.inf, True))
    keep = lambda v: jnp.where(live, v, 0.0)
    return (keep(bcum) + pltpu.roll(keep(a), GATE_A_LANE, axis=1)
            + pltpu.roll(keep(amax), GATE_AMAX_LANE, axis=1))


def _mlstm_kernel(qkf_ref, vf_ref, cf_ref, rf_ref, qkb_ref, vb_ref, cb_ref, rb_ref,
                  hf_ref, hb_ref, c_st, cb_st, m_st):
    @pl.when(pl.program_id(1) == 0)
    def _():
        c_st[...] = jnp.zeros_like(c_st)
        cb_st[...] = jnp.zeros_like(cb_st)
        m_st[...] = jnp.zeros_like(m_st)

    nh, hd = MLSTM_HEADS, MLSTM_HD
    t_idx = lax.broadcasted_iota(jnp.int32, (CHUNK, CHUNK), 0)
    s_idx = lax.broadcasted_iota(jnp.int32, (CHUNK, CHUNK), 1)
    scale = hd ** -0.5
    ones_blk = jnp.ones((CHUNK, LANES), BF16)
    rep = lambda x, n: jnp.concatenate([x] * n, axis=1)
    lane_dense = lambda col: jnp.broadcast_to(col, (CHUNK, LANES))
    dirs = ((qkf_ref, vf_ref, cf_ref, rf_ref, hf_ref, False), (qkb_ref, vb_ref, cb_ref, rb_ref, hb_ref, True))
    for d, (qk_ref, v_ref, col_ref, row_ref, o_ref, reverse) in enumerate(dirs):
        mask = (s_idx >= t_idx) if reverse else (s_idx <= t_idx)
        last = 0 if reverse else CHUNK - 1
        for h in range(nh):
            j = d * nh + h
            hs = slice(h * hd, (h + 1) * hd)
            q = qk_ref[0, :, hs]
            k = qk_ref[0, :, (nh + h) * hd:(nh + h + 1) * hd] * scale
            v_aug = jnp.concatenate([v_ref[0, :, hs], ones_blk], axis=1)
            gl = _gate_lane(d, h)
            bc = lane_dense(col_ref[0, :, gl:gl + 1])
            a_c = lane_dense(col_ref[0, :, GATE_A_LANE + gl:GATE_A_LANE + gl + 1])
            amax = lane_dense(col_ref[0, :, GATE_AMAX_LANE + gl:GATE_AMAX_LANE + gl + 1])
            a_r = row_ref[0, gl:gl + 1, :]
            dmat = jnp.exp(jnp.where(mask, a_r - rep(amax, CHUNK // LANES), -jnp.inf))
            s = lax.dot_general(q, k, (((1,), (1,)), ((), ())), preferred_element_type=F32) * dmat
            nd_l = jnp.dot(s.astype(BF16), v_aug, preferred_element_type=F32)
            a_last = amax[last:last + 1, :]
            btot = bc[last:last + 1, :]
            kw = rep(jnp.exp(a_c - a_last), hd // LANES) * k.astype(F32)
            upd = lax.dot_general(kw.astype(BF16), v_aug, (((0,), (0,)), ((), ())),
                                  preferred_element_type=F32)

            m_prev = m_st[j, 0:1, :]
            mt = jnp.maximum(amax, m_prev)
            f_l = jnp.exp(amax - mt)
            sc = jnp.exp(m_prev - mt)
            nd_c = jnp.dot(q, cb_st[j], preferred_element_type=F32)
            den = f_l * nd_l[:, hd:] + sc * nd_c[:, hd:]
            inv = 1.0 / jnp.maximum(jnp.abs(den), jnp.exp(-(bc + mt)))
            num = rep(f_l, hd // LANES) * nd_l[:, :hd] + rep(sc, hd // LANES) * nd_c[:, :hd]
            o_ref[0, :, hs] = (num * rep(inv, hd // LANES)).astype(o_ref.dtype)

            m_last = jnp.maximum(a_last, m_prev)
            dec = jnp.exp(m_prev - m_last)
            f_u = jnp.exp(a_last - m_last)
            wide = (hd + LANES) // LANES
            c_new = rep(dec, wide) * c_st[j] + rep(f_u, wide) * upd
            c_st[j] = c_new
            cb_st[j] = c_new.astype(BF16)
            m_st[j] = jnp.broadcast_to(btot + m_last, m_st.shape[1:])


def mlstm_scan(qk, v, cols, rows, out_dtype):
    bsz, l, _ = qk.shape
    w = D_MODEL
    assert l % CHUNK == 0
    nc = l // CHUNK
    fwd = lambda cb: (lambda b, c: (b, c, cb))
    bwd = lambda cb: (lambda b, c: (b, nc - 1 - c, cb))
    blk = lambda im: pl.BlockSpec((1, CHUNK, w), im)
    ncol = cols.shape[-1]
    in_specs = []
    for mk in (fwd, bwd):
        in_specs += [pl.BlockSpec((1, CHUNK, 2 * w), mk(0)), blk(mk(0)),
                     pl.BlockSpec((1, CHUNK, ncol), mk(0)),
                     pl.BlockSpec((1, rows.shape[1], CHUNK),
                                  (lambda b, c: (b, 0, c)) if mk is fwd else (lambda b, c: (b, 0, nc - 1 - c)))]
    nst = 2 * MLSTM_HEADS
    shp = jax.ShapeDtypeStruct((bsz, l, w), out_dtype)
    return pl.pallas_call(
        _mlstm_kernel,
        grid=(bsz, nc),
        in_specs=in_specs,
        out_specs=[blk(fwd(0)), blk(bwd(0))],
        out_shape=[shp, shp],
        scratch_shapes=[pltpu.VMEM((nst, MLSTM_HD, MLSTM_HD + LANES), F32),
                        pltpu.VMEM((nst, MLSTM_HD, MLSTM_HD + LANES), BF16),
                        pltpu.VMEM((nst, 8, LANES), F32)],
        compiler_params=_params("parallel", "arbitrary"),
        name="mlstm_scan",
    )(qk, v, cols, rows, qk, v, cols, rows)


def _rope_table_kernel(inv_ref, cos_ref, sin_ref):
    tl = cos_ref.shape[0]
    pos = (lax.broadcasted_iota(jnp.int32, cos_ref.shape, 0) + pl.program_id(0) * tl).astype(F32)
    ang = pos * inv_ref[...]
    cos_ref[...] = jnp.cos(ang)
    sin_ref[...] = jnp.sin(ang)


def rope_tables(l, tl=1024):
    half = RET_HDK // 2
    tl = min(tl, l)
    inv = (ROPE_BASE ** (-np.arange(0, RET_HDK, 2, dtype=np.float32) / RET_HDK)).astype(np.float32)
    spec = pl.BlockSpec((tl, half), lambda i: (i, 0))
    shp = jax.ShapeDtypeStruct((l, half), F32)
    return pl.pallas_call(
        _rope_table_kernel,
        grid=(l // tl,),
        in_specs=[pl.BlockSpec((1, half), lambda i: (0, 0))],
        out_specs=[spec, spec],
        out_shape=[shp, shp],
        compiler_params=_params("parallel"),
        name="rope_tables",
    )(jnp.asarray(inv).reshape(1, half))


def _rope(x, cos, sin):
    half = x.shape[-1] // 2
    x1, x2 = x[:, :half], x[:, half:]
    return jnp.concatenate([x1 * cos - x2 * sin, x1 * sin + x2 * cos], axis=-1)


def _odd_proj_kernel(x_ref, g_ref, w_ref, cos_ref, sin_ref, qk_ref, v_ref, gate_ref, *, tn):
    x = x_ref[...]
    xb = (x * lax.rsqrt(jnp.mean(x * x, axis=-1, keepdims=True) + EPS) * g_ref[...]).astype(BF16)
    cos, sin = cos_ref[...], sin_ref[...]
    nh, dk = RET_HEADS, RET_HDK
    for j in range(2 * nh):
        r = jnp.dot(xb, w_ref[:, j * dk:(j + 1) * dk], preferred_element_type=F32)
        r = _rope(r, cos, sin)
        if j < nh:
            r = r * dk ** -0.5
        qk_ref[:, j * dk:(j + 1) * dk] = r.astype(qk_ref.dtype)
    c0 = 2 * nh * dk
    vw = v_ref.shape[1]
    for j in range(vw // tn):
        cols = slice(j * tn, (j + 1) * tn)
        v_ref[:, cols] = jnp.dot(xb, w_ref[:, c0 + j * tn:c0 + (j + 1) * tn],
                                 preferred_element_type=F32).astype(v_ref.dtype)
        gate = jnp.dot(xb, w_ref[:, c0 + vw + j * tn:c0 + vw + (j + 1) * tn], preferred_element_type=F32)
        gate_ref[:, cols] = _silu(gate).astype(gate_ref.dtype)


def odd_proj(x, g, w, cos, sin, seq_len, tm=512, tn=512):
    m, d = x.shape
    qw, vw = RET_HEADS * RET_HDK, RET_HEADS * RET_HDV
    half = RET_HDK // 2
    tm = min(tm, seq_len)
    assert m % seq_len == 0 and seq_len % tm == 0
    tps = seq_len // tm
    row = lambda n: pl.BlockSpec((tm, n), lambda i: (i, 0))
    tab = pl.BlockSpec((tm, half), lambda i: (i % tps, 0))
    return pl.pallas_call(
        functools.partial(_odd_proj_kernel, tn=tn),
        grid=(m // tm,),
        in_specs=[row(d), pl.BlockSpec((1, d), lambda i: (0, 0)),
                  pl.BlockSpec(w.shape, lambda i: (0, 0), pipeline_mode=pl.Buffered(1)), tab, tab],
        out_specs=[row(2 * qw), row(vw), row(vw)],
        out_shape=[jax.ShapeDtypeStruct((m, 2 * qw), BF16), jax.ShapeDtypeStruct((m, vw), BF16),
                   jax.ShapeDtypeStruct((m, vw), BF16)],
        compiler_params=_params("parallel"),
        name="odd_proj",
    )(x, g.reshape(1, d).astype(F32), w, cos, sin)


def _retention_kernel(qkf_ref, vf_ref, qkb_ref, vb_ref, logit_ref, of_ref, ob_ref, r_st, rb_st):
    @pl.when(pl.program_id(1) == 0)
    def _():
        r_st[...] = jnp.zeros_like(r_st)
        rb_st[...] = jnp.zeros_like(rb_st)

    nh, dk, dv = RET_HEADS, RET_HDK, RET_HDV
    t_idx = lax.broadcasted_iota(jnp.int32, (CHUNK, CHUNK), 0)
    s_idx = lax.broadcasted_iota(jnp.int32, (CHUNK, CHUNK), 1)
    pos = lax.broadcasted_iota(jnp.int32, (CHUNK, LANES), 0).astype(F32)
    rep = lambda x, n: jnp.concatenate([x] * n, axis=1)
    dirs = ((qkf_ref, vf_ref, of_ref, False), (qkb_ref, vb_ref, ob_ref, True))
    for d, (qk_ref, v_ref, o_ref, reverse) in enumerate(dirs):
        rel = ((s_idx - t_idx) if reverse else (t_idx - s_idx)).astype(F32)
        for h in range(nh):
            j = d * nh + h
            lg = _log_sigmoid(logit_ref[j])[0:1, :]
            dmask = jnp.where(rel >= 0, jnp.exp(rep(lg, CHUNK // LANES) * jnp.maximum(rel, 0.0)), 0.0)
            if reverse:
                q_dec = jnp.exp(lg * (CHUNK - pos))
                k_dec = jnp.exp(lg * pos)
            else:
                q_dec = jnp.exp(lg * (pos + 1.0))
                k_dec = jnp.exp(lg * (CHUNK - 1.0 - pos))
            c_dec = jnp.exp(lg * CHUNK)
            q = qk_ref[0, :, h * dk:(h + 1) * dk]
            k = qk_ref[0, :, (nh + h) * dk:(nh + h + 1) * dk]
            v = v_ref[0, :, h * dv:(h + 1) * dv]
            s = lax.dot_general(q, k, (((1,), (1,)), ((), ())), preferred_element_type=F32) * dmask
            out = jnp.dot(s.astype(BF16), v, preferred_element_type=F32)
            out = out + jnp.dot((q.astype(F32) * rep(q_dec, dk // LANES)).astype(BF16), rb_st[j],
                                preferred_element_type=F32)
            o_ref[0, :, h * dv:(h + 1) * dv] = out.astype(o_ref.dtype)
            upd = lax.dot_general((k.astype(F32) * rep(k_dec, dk // LANES)).astype(BF16), v,
                                  (((0,), (0,)), ((), ())), preferred_element_type=F32)
            r_new = rep(c_dec, dv // LANES) * r_st[j] + upd
            r_st[j] = r_new
            rb_st[j] = r_new.astype(BF16)


def retention_scan(qk, v, logit_b, out_dtype):
    bsz, l, _ = qk.shape
    nc = l // CHUNK
    qw, vw = RET_HEADS * RET_HDK, RET_HEADS * RET_HDV
    in_specs = []
    for rev in (False, True):
        ci = (lambda c: nc - 1 - c) if rev else (lambda c: c)
        in_specs += [pl.BlockSpec((1, CHUNK, 2 * qw), lambda b, c, ci=ci: (b, ci(c), 0)),
                     pl.BlockSpec((1, CHUNK, vw), lambda b, c, ci=ci: (b, ci(c), 0))]
    in_specs.append(pl.BlockSpec((2 * RET_HEADS, 8, LANES), lambda b, c: (0, 0, 0)))
    shp = jax.ShapeDtypeStruct((bsz, l, vw), out_dtype)
    return pl.pallas_call(
        _retention_kernel,
        grid=(bsz, nc),
        in_specs=in_specs,
        out_specs=[pl.BlockSpec((1, CHUNK, vw), lambda b, c: (b, c, 0)),
                   pl.BlockSpec((1, CHUNK, vw), lambda b, c: (b, nc - 1 - c, 0))],
        out_shape=[shp, shp],
        scratch_shapes=[pltpu.VMEM((2 * RET_HEADS, RET_HDK, RET_HDV), F32),
                        pltpu.VMEM((2 * RET_HEADS, RET_HDK, RET_HDV), BF16)],
        compiler_params=_params("parallel", "arbitrary"),
        name="retention_scan",
    )(qk, v, qk, v, logit_b)


def _hyena_filter_kernel(cst_ref, w1_ref, w2_ref, w3a_ref, w3b_ref, delta_ref, kern_ref, asum_ref, *, l):
    i = pl.program_id(0)
    tr = kern_ref.shape[0]
    hr = tr // 2
    half = LANES // 2
    row = lax.broadcasted_iota(jnp.int32, (hr, LANES), 0) + i * tr
    n_a, n_b = row, row + hr
    lag = lambda n: jnp.where(n < l, n, 2 * l - n).astype(F32)
    tt_a, tt_b = lag(n_a) / (l - 1.0), lag(n_b) / (l - 1.0)
    lo = lax.broadcasted_iota(jnp.int32, (hr, LANES), 1) < half
    p2 = jnp.where(lo, lag(n_a), lag(n_b))
    tt2 = jnp.where(lo, tt_a, tt_b)
    bands, phase, w1t, b1, f1, b2, f2 = (cst_ref[k:k + 1, :] for k in range(7))
    ang = (2.0 * math.pi / l) * bands * p2 + phase
    pre = jnp.dot(jnp.cos(ang), w1_ref[...], preferred_element_type=F32, precision=HI) + tt2 * w1t
    z = jnp.sin(f1 * (pre + b1))
    z = jnp.sin(f2 * (jnp.dot(z, w2_ref[...], preferred_element_type=F32, precision=HI) + b2))
    rep = lambda x: jnp.concatenate([x] * (kern_ref.shape[1] // LANES), axis=1)
    dabs = jnp.abs(delta_ref[...])
    total = jnp.zeros((1, kern_ref.shape[1]), F32)
    z_hi = z.astype(BF16)
    z_lo = (z - z_hi.astype(F32)).astype(BF16)
    z3 = jnp.concatenate([z_hi, z_lo, z_hi], axis=1)
    for part, (w3_ref, n, tt) in enumerate(((w3a_ref, n_a, tt_a), (w3b_ref, n_b, tt_b))):
        hk = jnp.dot(z3, w3_ref[...], preferred_element_type=F32)
        hk = hk * (jnp.exp(-rep(tt) * dabs) + HYENA_SHIFT)
        hk = jnp.where(rep(n) == l, 0.0, hk)
        kern_ref[part * hr:(part + 1) * hr, :] = hk
        total = total + jnp.sum(jnp.abs(hk), axis=0, keepdims=True)

    @pl.when(i == 0)
    def _():
        asum_ref[...] = jnp.zeros_like(asum_ref)

    asum_ref[...] += jnp.broadcast_to(total, asum_ref.shape)


def hyena_filter(l, w1, b1, f1, w2, b2, f2, w3, delta, tr=1024):
    wd = HYENA_WIDTH
    hid = w1.shape[1]
    half = LANES // 2
    nb = HYENA_BANDS
    tr = min(tr, l)
    w1, w2, w3 = w1.astype(F32), w2.astype(F32), w3.astype(F32)
    two = lambda v: jnp.tile(jnp.pad(v.astype(F32).reshape(1, -1), ((0, 0), (0, half - v.shape[-1]))), (1, 2))
    bands = np.zeros((1, half), np.float32)
    bands[0, :nb] = bands[0, nb:2 * nb] = np.linspace(1e-4, nb - 1, nb, dtype=np.float32)
    phase = np.zeros((1, half), np.float32)
    phase[0, nb:2 * nb] = 0.5 * np.pi
    cst = jnp.concatenate([two(jnp.asarray(bands)), two(jnp.asarray(phase)), two(w1[0]), two(b1), two(f1),
                           two(b2), two(f2), jnp.zeros((1, LANES), F32)], axis=0)
    blk = lambda a: jnp.pad(a, ((0, half - a.shape[0]), (0, half - a.shape[1])))
    diag2 = lambda a: jnp.concatenate([jnp.pad(blk(a), ((0, 0), (0, half))),
                                       jnp.pad(blk(a), ((0, 0), (half, 0)))], axis=0)
    w3h = jnp.pad(w3, ((0, half - hid), (0, 0)))

    def split3(a):
        hi = a.astype(BF16)
        lo = (a - hi.astype(F32)).astype(BF16)
        return jnp.concatenate([hi, hi, lo], axis=0)

    w3a = split3(jnp.pad(w3h, ((0, half), (0, 0))))
    w3b = split3(jnp.pad(w3h, ((half, 0), (0, 0))))
    full = lambda shape: pl.BlockSpec(shape, lambda i: (0, 0))
    half_sel = lambda i: (0, (i * tr) // l)
    return pl.pallas_call(
        functools.partial(_hyena_filter_kernel, l=l),
        grid=(2 * l // tr,),
        in_specs=[full((8, LANES)), full((LANES, LANES)), full((LANES, LANES)),
                  pl.BlockSpec((3 * LANES, wd), half_sel), pl.BlockSpec((3 * LANES, wd), half_sel),
                  pl.BlockSpec((1, wd), half_sel)],
        out_specs=[pl.BlockSpec((tr, wd), lambda i: (i, 0)), pl.BlockSpec((8, wd), lambda i: (0, 0))],
        out_shape=[jax.ShapeDtypeStruct((2 * l, wd), F32), jax.ShapeDtypeStruct((8, wd), F32)],
        compiler_params=_params("arbitrary"),
        name="hyena_filter",
    )(cst, diag2(w1[1:]), diag2(w2), w3a, w3b, delta.reshape(1, 2 * wd).astype(F32))


def _dft_consts(n1, n2):
    n = n1 * n2
    k1 = np.arange(n1)
    f1 = np.exp(-2j * np.pi * np.outer(k1, k1) / n1)
    k2 = np.arange(n2)
    f2 = np.exp(-2j * np.pi * np.outer(k2, k2) / n2)
    tw = np.exp(-2j * np.pi * np.outer(k1, k2) / n)
    g = f2[None, :, :] * tw[:, None, :]
    ginv = np.conj(np.transpose(g, (0, 2, 1))) / n
    f1inv = np.conj(f1.T)
    return f1, g, ginv, f1inv


def _stack(c):
    return jnp.asarray(np.concatenate([c.real, c.imag], axis=-2).astype(np.float32)).astype(BF16)


def _cdot(fs, xr, xi, rows):
    p = jnp.dot(fs, xr, preferred_element_type=F32)
    if xi is None:
        return p[:rows], p[rows:]
    q = jnp.dot(fs, xi, preferred_element_type=F32)
    return p[:rows] - q[rows:], q[:rows] + p[rows:]


FFT_ROW_CHUNK = 512


def _kron_eye(f, s_blk):
    k = np.kron(f, np.eye(s_blk))
    to = lambda a: jnp.asarray(a.astype(np.float32)).astype(BF16)
    return to(k.real), to(k.imag)


def _strided_stage_kernel(*refs, has_imag_in, n_out):
    mr_ref, mi_ref, xr_ref = refs[:3]
    xi_ref = refs[3] if has_imag_in else None
    n_in = 4 if has_imag_in else 3
    if len(refs[n_in].shape) == 5:
        outs = [refs[n_in].at[part, 0] for part in range(n_out)]
    else:
        outs = [r.at[0] for r in refs[n_in:]]
    flat = lambda r: r.reshape(r.shape[0] * r.shape[1], r.shape[2])
    xr = flat(xr_ref[0]).astype(BF16)
    xi = flat(xi_ref[0]).astype(BF16) if has_imag_in else None
    rows = mr_ref.shape[0]
    rc = min(FFT_ROW_CHUNK, rows)
    s_blk = xr_ref.shape[2]
    for c0 in range(0, rows, rc):
        mr = mr_ref[c0:c0 + rc, :]
        mi = mi_ref[c0:c0 + rc, :]
        re = jnp.dot(mr, xr, preferred_element_type=F32)
        im = jnp.dot(mi, xr, preferred_element_type=F32) if n_out == 2 else None
        if has_imag_in:
            re = re - jnp.dot(mi, xi, preferred_element_type=F32)
            if n_out == 2:
                im = im + jnp.dot(mr, xi, preferred_element_type=F32)
        k0, k1 = c0 // s_blk, (c0 + rc) // s_blk
        for o_ref, val in zip(outs, (re, im)):
            o_ref[k0:k1] = val.reshape(rc // s_blk, s_blk, val.shape[1]).astype(o_ref.dtype)


def fft_stage1(x, n1, rows_in, pairs, complex_in, s_blk, out_dtype):
    c = x.shape[-1]
    n2 = FFT_N2
    xv = x.reshape(x.shape[0], rows_in, n2, c)
    mr, mi = _kron_eye(_dft_consts(n1, n2)[0][:, :rows_in], s_blk)
    mat = pl.BlockSpec(mr.shape, lambda p, j: (0, 0), pipeline_mode=pl.Buffered(1))
    in_specs = [mat, mat, pl.BlockSpec((1, rows_in, s_blk, c), lambda p, j: (p, 0, j, 0))]
    args = [mr, mi, xv]
    if complex_in:
        in_specs.append(pl.BlockSpec((1, rows_in, s_blk, c), lambda p, j: (p + pairs, 0, j, 0)))
        args.append(xv)
    out_spec = pl.BlockSpec((1, n1, s_blk, c), lambda p, j: (p, 0, j, 0))
    shp = jax.ShapeDtypeStruct((pairs, n1, n2, c), out_dtype)
    return pl.pallas_call(
        functools.partial(_strided_stage_kernel, has_imag_in=complex_in, n_out=2),
        grid=(pairs, n2 // s_blk),
        in_specs=in_specs,
        out_specs=[out_spec, out_spec],
        out_shape=[shp, shp],
        compiler_params=_params("parallel", "parallel"),
        name="fft_stage1",
    )(*args)


def _fft_mid_kernel(*refs, conv):
    n2 = FFT_N2
    if conv:
        gs_ref, gis_ref, ar_ref, ai_ref, kr_ref, ki_ref, br_ref, bi_ref = refs
    else:
        gs_ref, ar_ref, ai_ref, br_ref, bi_ref = refs
    for kk in range(ar_ref.shape[1]):
        xr, xi = _cdot(gs_ref[kk], ar_ref[0, kk].astype(BF16), ai_ref[0, kk].astype(BF16), n2)
        if conv:
            kr, ki = kr_ref[kk].astype(F32), ki_ref[kk].astype(F32)
            yr = xr * kr - xi * ki
            yi = xr * ki + xi * kr
            xr, xi = _cdot(gis_ref[kk], yr.astype(BF16), yi.astype(BF16), n2)
        br_ref[0, kk] = xr.astype(br_ref.dtype)
        bi_ref[0, kk] = xi.astype(bi_ref.dtype)


def fft_mid(ar, ai, n1, c, kf=None, ct=1024, kb=4):
    n2 = FFT_N2
    pairs = ar.shape[0]
    kb = min(kb, n1)
    _, g, ginv, _ = _dft_consts(n1, n2)
    a4 = lambda a: a
    mat = pl.BlockSpec((kb, 2 * n2, n2), lambda k, cb, p: (k, 0, 0))
    dat = pl.BlockSpec((1, kb, n2, ct), lambda k, cb, p: (p, k, 0, cb))
    in_specs = [mat]
    args = [_stack(g)]
    if kf is not None:
        in_specs.append(mat)
        args.append(_stack(ginv))
    in_specs += [dat, dat]
    args += [a4(ar), a4(ai)]
    if kf is not None:
        fil = pl.BlockSpec((kb, n2, ct), lambda k, cb, p: (k, 0, cb))
        in_specs += [fil, fil]
        args += [kf[0], kf[1]]
    shp = jax.ShapeDtypeStruct((pairs, n1, n2, c), BF16)
    br, bi = pl.pallas_call(
        functools.partial(_fft_mid_kernel, conv=kf is not None),
        grid=(n1 // kb, c // ct, pairs),
        in_specs=in_specs,
        out_specs=[dat, dat],
        out_shape=[shp, shp],
        compiler_params=_params("parallel", "parallel", "arbitrary"),
        name="fft_mid",
    )(*args)
    return br, bi


def fft_stage1_inv(br, bi, n1, rows_out, c, complex_out, s_blk):
    pairs = br.shape[0]
    n2 = FFT_N2
    mr, mi = _kron_eye(_dft_consts(n1, n2)[3][:rows_out, :], s_blk)
    parts = 2 if complex_out else 1
    mat = pl.BlockSpec(mr.shape, lambda p, j: (0, 0), pipeline_mode=pl.Buffered(1))
    dat = pl.BlockSpec((1, n1, s_blk, c), lambda p, j: (p, 0, j, 0))
    y = pl.pallas_call(
        functools.partial(_strided_stage_kernel, has_imag_in=True, n_out=parts),
        grid=(pairs, n2 // s_blk),
        in_specs=[mat, mat, dat, dat],
        out_specs=pl.BlockSpec((parts, 1, rows_out, s_blk, c), lambda p, j: (0, p, 0, j, 0)),
        out_shape=jax.ShapeDtypeStruct((parts, pairs, rows_out, n2, c), BF16),
        compiler_params=_params("parallel", "parallel"),
        name="fft_stage1_inv",
    )(mr, mi, br, bi)
    return y.reshape(parts * pairs, rows_out * n2, c)


def hyena_long_conv(t, kern):
    bsz, l, c = t.shape
    assert l % FFT_N2 == 0 and kern.shape == (2 * l, c)
    n1 = 2 * l // FFT_N2
    kr, ki = fft_stage1(kern[None], n1, n1, 1, False, 8, F32)
    kr, ki = fft_mid(kr, ki, n1, c)
    kf = (kr.reshape(n1, FFT_N2, c), ki.reshape(n1, FFT_N2, c))
    complex_in = bsz % 2 == 0
    pairs = bsz // 2 if complex_in else bsz
    ar, ai = fft_stage1(t, n1, n1 // 2, pairs, complex_in, BF16_SUBLANES, BF16)
    br, bi = fft_mid(ar, ai, n1, c, kf)
    return fft_stage1_inv(br, bi, n1, n1 // 2, c, complex_in, BF16_SUBLANES)


def _head_ln(x, nheads, g):
    hd = x.shape[-1] // nheads
    outs = []
    for h in range(nheads):
        seg = x[:, h * hd:(h + 1) * hd]
        mu = jnp.mean(seg, axis=-1, keepdims=True)
        cen = seg - mu
        var = jnp.mean(cen * cen, axis=-1, keepdims=True)
        outs.append(cen * lax.rsqrt(var + GN_EPS))
    return jnp.concatenate(outs, axis=-1) * g


def _cross_attn_tail(x, wqk_ref, vwo_ref, gq_ref, gf_ref, final_norm):
    xn = x * lax.rsqrt(jnp.mean(x * x, axis=-1, keepdims=True) + EPS) * gq_ref[...]
    s_all = jnp.dot(xn.astype(BF16), wqk_ref[0], preferred_element_type=F32)
    nm = s_all.shape[1] // CA_HEADS
    probs = []
    for h in range(CA_HEADS):
        s = s_all[:, h * nm:(h + 1) * nm]
        p = jnp.exp(s - jnp.max(s, axis=-1, keepdims=True))
        probs.append((p * (1.0 / jnp.sum(p, axis=-1, keepdims=True))).astype(BF16))
    y = x + jnp.dot(jnp.concatenate(probs, axis=-1), vwo_ref[0], preferred_element_type=F32)
    if final_norm:
        y = y * lax.rsqrt(jnp.mean(y * y, axis=-1, keepdims=True) + EPS) * gf_ref[...]
    return y


def _even_out_kernel(x_ref, hf_ref, hb_ref, ga_ref, xg_ref, y_ref, t_ref, g_ref, skip_ref, asum_ref, w_ref,
                     wqk_ref, vwo_ref, gq_ref, gf_ref, out_ref, *, final_norm):
    wd = D_MODEL
    ln = _head_ln(hf_ref[0].astype(F32) + hb_ref[0].astype(F32), MLSTM_HEADS, g_ref[...])
    ya = ln * ga_ref[0].astype(F32)
    conv = y_ref[0].astype(F32) * (1.0 / asum_ref[0:1, :])
    yb = xg_ref[0].astype(F32) * (conv + skip_ref[...] * t_ref[0].astype(F32))
    mix = jnp.dot(ya.astype(BF16), w_ref[0:wd, :], preferred_element_type=F32)
    mix = mix + jnp.dot(yb.astype(BF16), w_ref[wd:2 * wd, :], preferred_element_type=F32)
    out_ref[0] = _cross_attn_tail(x_ref[0] + mix, wqk_ref, vwo_ref, gq_ref, gf_ref, final_norm)


def _odd_out_kernel(x_ref, of_ref, ob_ref, gate_ref, g_ref, w_ref,
                    wqk_ref, vwo_ref, gq_ref, gf_ref, out_ref, *, final_norm):
    o = _head_ln(of_ref[0].astype(F32) + ob_ref[0].astype(F32), RET_HEADS, g_ref[...])
    y = gate_ref[0].astype(F32) * o
    mix = jnp.dot(y.astype(BF16), w_ref[...], preferred_element_type=F32)
    out_ref[0] = _cross_attn_tail(x_ref[0] + mix, wqk_ref, vwo_ref, gq_ref, gf_ref, final_norm)


def mixer_out(kernel, x, acts, vecs, w_out, ca, tm=512):
    bsz, l, d = x.shape
    tm = min(tm, l)
    wqk, vwo, gq, gf, final_norm = ca
    row = lambda a: pl.BlockSpec((1, tm, a.shape[-1]), lambda b, i: (b, i, 0))
    res = lambda a: pl.BlockSpec(a.shape, lambda b, i: (0, 0), pipeline_mode=pl.Buffered(1))
    per_b = lambda a: pl.BlockSpec((1,) + a.shape[1:], lambda b, i: (b, 0, 0))
    gq, gf = gq.reshape(1, d).astype(F32), gf.reshape(1, d).astype(F32)
    return pl.pallas_call(
        functools.partial(kernel, final_norm=final_norm),
        grid=(bsz, l // tm),
        in_specs=[row(x)] + [row(a) for a in acts] + [res(v) for v in vecs] + [res(w_out)]
                 + [per_b(wqk), per_b(vwo), res(gq), res(gf)],
        out_specs=row(x),
        out_shape=jax.ShapeDtypeStruct((bsz, l, d), F32),
        compiler_params=_params("parallel", "parallel"),
        name=kernel.__name__.strip("_").replace("_kernel", "_ca"),
    )(x, *acts, *vecs, w_out, wqk, vwo, gq, gf)


def _ca_fold_kernel(kv_ref, wq_ref, wo_ref, wqk_ref, vwo_ref):
    d = D_MODEL
    for h in range(CA_HEADS):
        hs = slice(h * CA_HD, (h + 1) * CA_HD)
        k = kv_ref[0, :, hs]
        v = kv_ref[0, :, d + h * CA_HD:d + (h + 1) * CA_HD]
        nm = k.shape[0]
        wqk = lax.dot_general(wq_ref[:, hs], k, (((1,), (1,)), ((), ())), preferred_element_type=F32)
        wqk_ref[0, :, h * nm:(h + 1) * nm] = (wqk * CA_HD ** -0.5).astype(wqk_ref.dtype)
        vwo_ref[0, h * nm:(h + 1) * nm, :] = jnp.dot(v, wo_ref[hs, :],
                                                     preferred_element_type=F32).astype(vwo_ref.dtype)


def ca_fold(kv, wq, wo):
    bsz, nm, d2 = kv.shape
    d = d2 // 2
    mat = pl.BlockSpec((d, d), lambda b: (0, 0))
    return pl.pallas_call(
        _ca_fold_kernel,
        grid=(bsz,),
        in_specs=[pl.BlockSpec((1, nm, d2), lambda b: (b, 0, 0)), mat, mat],
        out_specs=[pl.BlockSpec((1, d, CA_HEADS * nm), lambda b: (b, 0, 0)),
                   pl.BlockSpec((1, CA_HEADS * nm, d), lambda b: (b, 0, 0))],
        out_shape=[jax.ShapeDtypeStruct((bsz, d, CA_HEADS * nm), BF16),
                   jax.ShapeDtypeStruct((bsz, CA_HEADS * nm, d), BF16)],
        compiler_params=_params("parallel"),
        name="ca_fold",
    )(kv, wq, wo)


def _even_mixer(x, p, i, ca):
    bsz, l, d = x.shape
    w = D_MODEL
    nh = MLSTM_HEADS
    m = bsz * l
    g_mix = p['norm_mix_g_layer']
    w_in = p['even_w_in'][i]
    gate0 = 5 * w
    hy0 = gate0 + N_GATE_COLS
    w_conv = jnp.concatenate([w_in[:, :2 * w], w_in[:, hy0:hy0 + 3 * w]], axis=1).astype(BF16)
    w_rest = jnp.concatenate([w_in[:, 2 * w:gate0], w_in[:, hy0 + 3 * w:]], axis=1).astype(BF16)
    w_gate = jnp.pad(w_in[:, gate0:hy0], ((0, 0), (0, LANES - N_GATE_COLS))).astype(BF16)
    conv_w = jnp.concatenate([p['mlstm_conv_w'][i], p['hyena_conv_w'][i]], axis=1).astype(F32)
    conv_b = jnp.concatenate([p['mlstm_conv_b'][i], p['hyena_conv_b'][i]]).astype(F32).reshape(1, 5 * w)
    bias = jnp.pad(p['mlstm_gate_bias'][i].astype(F32).reshape(1, N_GATE_COLS),
                   ((0, 0), (0, LANES - N_GATE_COLS)))
    xf = x.reshape(m, d)
    qk, xg, t, v, ga, cols = even_proj(xf, g_mix, w_conv, w_rest, w_gate, conv_w, conv_b, bias, l)
    cols = cols.reshape(bsz, l, LANES)
    rows = jnp.swapaxes(cols[..., GATE_A_LANE:GATE_A_LANE + N_GATE_COLS], 1, 2)

    hf, hb = mlstm_scan(qk.reshape(bsz, l, 2 * w), v.reshape(bsz, l, w), cols, rows, BF16)

    t = t.reshape(bsz, l, w)
    kern, asum = hyena_filter(l, p['hyena_w1'][i], p['hyena_b1'][i], p['hyena_freq1'][i], p['hyena_w2'][i],
                              p['hyena_b2'][i], p['hyena_freq2'][i], p['hyena_w3'][i], p['hyena_delta'][i])
    y = hyena_long_conv(t, kern)

    b3 = lambda a: a.reshape(bsz, l, w)
    vec = lambda a: a.reshape(1, w).astype(F32)
    return mixer_out(_even_out_kernel, x, [hf, hb, b3(ga), b3(xg), y, t],
                     [vec(p['mlstm_norm_g'][i]), vec(p['hyena_skip'][i]), asum],
                     p['even_w_out'][i].astype(BF16), ca)


def _odd_mixer(x, p, i, cos, sin, ca):
    bsz, l, d = x.shape
    m = bsz * l
    xf = x.reshape(m, d)
    qk, v, gate = odd_proj(xf, p['norm_mix_g_layer'], p['odd_w_in'][i].astype(BF16), cos, sin, l)
    logit = p['ret_decay_logit'][i].astype(F32).reshape(2 * RET_HEADS, 1, 1)
    logit_b = jnp.broadcast_to(logit, (2 * RET_HEADS, 8, LANES))
    vw = RET_HEADS * RET_HDV
    of, ob = retention_scan(qk.reshape(bsz, l, -1), v.reshape(bsz, l, vw), logit_b, BF16)
    return mixer_out(_odd_out_kernel, x, [of, ob, gate.reshape(bsz, l, vw)],
                     [p['ret_norm_g'][i].reshape(1, vw).astype(F32)], p['odd_w_out'][i].astype(BF16), ca)


def _trunk(x, mem, p, cos, sin):
    depth = p['norm_mix_g'].shape[0]
    bsz, nm, d = mem.shape
    for layer in range(depth):
        i = layer // 2
        p['norm_mix_g_layer'] = p['norm_mix_g'][layer]
        kv, = rms_matmul(mem.reshape(bsz * nm, d), p['norm_mem_g'][layer], [p['ca_wkv'][layer].astype(BF16)],
                         [BF16], tm=nm)
        wqk, vwo = ca_fold(kv.reshape(bsz, nm, 2 * d), p['ca_wq'][layer].astype(BF16),
                           p['ca_wo'][layer].astype(BF16))
        ca = (wqk, vwo, p['norm_ca_g'][layer], p['norm_final_g'], layer == depth - 1)
        x = _even_mixer(x, p, i, ca) if layer % 2 == 0 else _odd_mixer(x, p, i, cos, sin, ca)
    return x


def kernel(x_prompt, x_sample, mem_prompt, mem_sample, norm_mix_g, norm_ca_g, norm_mem_g, norm_final_g, even_w_in, mlstm_conv_w, mlstm_conv_b, mlstm_gate_bias, mlstm_norm_g, hyena_conv_w, hyena_conv_b, hyena_w1, hyena_b1, hyena_freq1, hyena_w2, hyena_b2, hyena_freq2, hyena_w3, hyena_delta, hyena_skip, even_w_out, odd_w_in, ret_decay_logit, ret_norm_g, odd_w_out, ca_wq, ca_wkv, ca_wo):
    p = {'norm_mix_g': norm_mix_g, 'norm_ca_g': norm_ca_g, 'norm_mem_g': norm_mem_g, 'norm_final_g': norm_final_g,
         'even_w_in': even_w_in, 'mlstm_conv_w': mlstm_conv_w, 'mlstm_conv_b': mlstm_conv_b,
         'mlstm_gate_bias': mlstm_gate_bias, 'mlstm_norm_g': mlstm_norm_g,
         'hyena_conv_w': hyena_conv_w, 'hyena_conv_b': hyena_conv_b, 'hyena_w1': hyena_w1, 'hyena_b1': hyena_b1,
         'hyena_freq1': hyena_freq1, 'hyena_w2': hyena_w2, 'hyena_b2': hyena_b2, 'hyena_freq2': hyena_freq2,
         'hyena_w3': hyena_w3, 'hyena_delta': hyena_delta, 'hyena_skip': hyena_skip, 'even_w_out': even_w_out,
         'odd_w_in': odd_w_in, 'ret_decay_logit': ret_decay_logit, 'ret_norm_g': ret_norm_g, 'odd_w_out': odd_w_out,
         'ca_wq': ca_wq, 'ca_wkv': ca_wkv, 'ca_wo': ca_wo}
    l_max = max(x_prompt.shape[1], x_sample.shape[1])
    cos, sin = rope_tables(l_max)
    y_prompt = _trunk(x_prompt, mem_prompt, dict(p), cos, sin)
    y_sample = _trunk(x_sample, mem_sample, dict(p), cos, sin)
    return (y_prompt, y_sample)
```

```python
import functools
import math

import numpy as np
import jax
import jax.numpy as jnp
from jax import lax
from jax.experimental import pallas as pl
from jax.experimental.pallas import tpu as pltpu

F32 = jnp.float32
BF16 = jnp.bfloat16

D_MODEL = 1024
EPS = 1e-6
GN_EPS = 1e-5
CHUNK = 256
SCAN_SUB = 2

MLSTM_HEADS = 4
MLSTM_HD = D_MODEL // MLSTM_HEADS
N_GATE_COLS = 4 * MLSTM_HEADS

HYENA_WIDTH = D_MODEL
HYENA_EMB = 33
HYENA_BANDS = (HYENA_EMB - 1) // 2
HYENA_SHIFT = 0.05
FFT_N2 = 256

RET_HEADS = 4
RET_HDK = D_MODEL // RET_HEADS
RET_HDV = 2 * D_MODEL // RET_HEADS
ROPE_BASE = 10000.0

CA_HEADS = 4
CA_HD = D_MODEL // CA_HEADS

LANES = 128
BF16_SUBLANES = 16
VMEM_LIMIT = 56 * 1024 * 1024

HI = lax.Precision.HIGHEST


def _params(*sem):
    return pltpu.CompilerParams(dimension_semantics=sem, vmem_limit_bytes=VMEM_LIMIT)


def _silu(x):
    return x * (1.0 / (1.0 + jnp.exp(-x)))


def _sigmoid(x):
    return 1.0 / (1.0 + jnp.exp(-x))


def _log_sigmoid(x):
    return jnp.minimum(x, 0.0) - jnp.log(1.0 + jnp.exp(-jnp.abs(x)))


def _rms_matmul_kernel(x_ref, g_ref, *refs, tn):
    nw = len(refs) // 2
    x = x_ref[...]
    xn = (x * lax.rsqrt(jnp.mean(x * x, axis=-1, keepdims=True) + EPS) * g_ref[...]).astype(BF16)
    for w_ref, o_ref in zip(refs[:nw], refs[nw:]):
        n = o_ref.shape[1]
        step = min(tn, n)
        for j in range(n // step):
            cols = slice(j * step, (j + 1) * step)
            o_ref[:, cols] = jnp.dot(xn, w_ref[:, cols], preferred_element_type=F32).astype(o_ref.dtype)


def rms_matmul(x, g, ws, out_dtypes, tm, tn=1024):
    m, d = x.shape
    w_specs = [pl.BlockSpec(w.shape, lambda i: (0, 0), pipeline_mode=pl.Buffered(1)) for w in ws]
    return pl.pallas_call(
        functools.partial(_rms_matmul_kernel, tn=tn),
        grid=(m // tm,),
        in_specs=[pl.BlockSpec((tm, d), lambda i: (i, 0)), pl.BlockSpec((1, d), lambda i: (0, 0))] + w_specs,
        out_specs=[pl.BlockSpec((tm, w.shape[1]), lambda i: (i, 0)) for w in ws],
        out_shape=[jax.ShapeDtypeStruct((m, w.shape[1]), dt) for w, dt in zip(ws, out_dtypes)],
        compiler_params=_params("parallel"),
        name="rms_matmul",
    )(x, g.reshape(1, d).astype(F32), *ws)


F32_SUBLANES = 8


def _even_proj_kernel(x_ref, xp_ref, xn_ref, g_ref, wc_ref, wr_ref, wg_ref, cw_ref, cb_ref, gb_ref,
                      qk_ref, xg_ref, t_ref, v_ref, ga_ref, cols_ref, *, tiles_per_seq, tn):
    i = pl.program_id(0)
    tm, d = x_ref.shape
    w, h = D_MODEL, F32_SUBLANES
    first = (i % tiles_per_seq) == 0
    last = (i % tiles_per_seq) == tiles_per_seq - 1
    g = g_ref[...]

    def norm(x):
        return x * lax.rsqrt(jnp.mean(x * x, axis=-1, keepdims=True) + EPS) * g

    xm = norm(x_ref[...])
    xp = jnp.where(first, 0.0, norm(xp_ref[...]))
    xn = jnp.where(last, 0.0, norm(xn_ref[...]))
    x_ext = jnp.concatenate([xp, xm, xn], axis=0).astype(BF16)
    xb = xm.astype(BF16)

    def conv(c0):
        cols = slice(c0, c0 + tn)
        r = jnp.dot(x_ext, wc_ref[:, cols], preferred_element_type=F32)
        t = r.shape[0]
        y = (pltpu.roll(r, 1, axis=0) * cw_ref[0:1, cols] + r * cw_ref[1:2, cols]
             + pltpu.roll(r, t - 1, axis=0) * cw_ref[2:3, cols] + cb_ref[:, cols])
        return y[h:h + tm]

    for j in range(2 * w // tn):
        qk_ref[:, j * tn:(j + 1) * tn] = _silu(conv(j * tn)).astype(qk_ref.dtype)
    rest = lambda group, j: jnp.dot(xb, wr_ref[:, group * w + j * tn:group * w + (j + 1) * tn],
                                    preferred_element_type=F32)
    for j in range(w // tn):
        cols = slice(j * tn, (j + 1) * tn)
        xg_ref[:, cols] = (conv(2 * w + j * tn) * _silu(rest(3, j))).astype(xg_ref.dtype)
        t_ref[:, cols] = (conv(3 * w + j * tn) * conv(4 * w + j * tn)).astype(t_ref.dtype)
        v_ref[:, cols] = rest(0, j).astype(v_ref.dtype)
        ga_ref[:, cols] = (_sigmoid(rest(1, j)) * _silu(rest(2, j))).astype(ga_ref.dtype)
    cols_ref[...] = _gate_prep(jnp.dot(xb, wg_ref[...], preferred_element_type=F32) + gb_ref[...])


def even_proj(x, g, w_conv, w_rest, w_gate, conv_w, conv_b, gate_bias, seq_len, tm=512, tn=512):
    m, d = x.shape
    w, h = D_MODEL, F32_SUBLANES
    tm = min(tm, seq_len)
    assert m % seq_len == 0 and seq_len % tm == 0 and tm % CHUNK == 0
    r = tm // h
    last_blk = m // h - 1
    res = lambda a: pl.BlockSpec(a.shape, lambda i: (0, 0), pipeline_mode=pl.Buffered(1))
    row = lambda n: pl.BlockSpec((tm, n), lambda i: (i, 0))
    return pl.pallas_call(
        functools.partial(_even_proj_kernel, tiles_per_seq=seq_len // tm, tn=tn),
        grid=(m // tm,),
        in_specs=[row(d),
                  pl.BlockSpec((h, d), lambda i: (jnp.maximum(i * r - 1, 0), 0)),
                  pl.BlockSpec((h, d), lambda i: (jnp.minimum((i + 1) * r, last_blk), 0)),
                  pl.BlockSpec((1, d), lambda i: (0, 0)),
                  res(w_conv), res(w_rest), res(w_gate), res(conv_w), res(conv_b), res(gate_bias)],
        out_specs=[row(2 * w), row(w), row(w), row(w), row(w), row(LANES)],
        out_shape=[jax.ShapeDtypeStruct((m, 2 * w), BF16), jax.ShapeDtypeStruct((m, w), BF16),
                   jax.ShapeDtypeStruct((m, w), BF16), jax.ShapeDtypeStruct((m, w), BF16),
                   jax.ShapeDtypeStruct((m, w), BF16), jax.ShapeDtypeStruct((m, LANES), F32)],
        compiler_params=_params("parallel"),
        name="even_proj",
    )(x, x, x, g.reshape(1, d).astype(F32), w_conv, w_rest, w_gate, conv_w, conv_b, gate_bias)


def _seg_scan(x, op, identity, reverse):
    t = x.shape[0]
    r = lax.broadcasted_iota(jnp.int32, x.shape, 0) % CHUNK
    k = 1
    while k < CHUNK:
        if reverse:
            shifted = pltpu.roll(x, t - k, axis=0)
            valid = r < CHUNK - k
        else:
            shifted = pltpu.roll(x, k, axis=0)
            valid = r >= k
        x = op(x, jnp.where(valid, shifted, identity))
        k *= 2
    return x


GATE_A_LANE = 16
GATE_AMAX_LANE = 32


def _gate_lane(d, h):
    return d * 2 * MLSTM_HEADS + h


def _gate_prep(g):
    nh = MLSTM_HEADS
    lane = lax.broadcasted_iota(jnp.int32, g.shape, 1)
    live = (lane < 4 * nh) & (lane % (2 * nh) < nh)
    fwd = lane < 2 * nh
    lf = pltpu.roll(_log_sigmoid(g), LANES - nh, axis=1)
    bcum = jnp.where(fwd, _seg_scan(lf, jnp.add, 0.0, False), _seg_scan(lf, jnp.add, 0.0, True))
    a = g - bcum
    amax = jnp.where(fwd, _seg_scan(a, jnp.maximum, -jnp.inf, False),
                     _seg_scan(a, jnp.maximum, -jnp.inf, True))
    keep = lambda v: jnp.where(live, v, 0.0)
    return (keep(bcum) + pltpu.roll(keep(a), GATE_A_LANE, axis=1)
            + pltpu.roll(keep(amax), GATE_AMAX_LANE, axis=1))


def _mlstm_kernel(qkf_ref, vf_ref, cf_ref, rf_ref, qkb_ref, vb_ref, cb_ref, rb_ref,
                  hf_ref, hb_ref, c_st, cb_st, m_st):
    @pl.when(pl.program_id(1) == 0)
    def _():
        c_st[...] = jnp.zeros_like(c_st)
        cb_st[...] = jnp.zeros_like(cb_st)
        m_st[...] = jnp.zeros_like(m_st)

    nh, hd = MLSTM_HEADS, MLSTM_HD
    t_idx = lax.broadcasted_iota(jnp.int32, (CHUNK, CHUNK), 0)
    s_idx = lax.broadcasted_iota(jnp.int32, (CHUNK, CHUNK), 1)
    scale = hd ** -0.5
    ones_blk = jnp.ones((CHUNK, LANES), BF16)
    rep = lambda x, n: jnp.concatenate([x] * n, axis=1)
    lane_dense = lambda col: jnp.broadcast_to(col, (CHUNK, LANES))
    dirs = ((qkf_ref, vf_ref, cf_ref, rf_ref, hf_ref, False), (qkb_ref, vb_ref, cb_ref, rb_ref, hb_ref, True))
    units = [(d, cc, h) for d in range(2)
             for cc in (range(SCAN_SUB) if d == 0 else reversed(range(SCAN_SUB))) for h in range(nh)]
    for d, cc, h in units:
        qk_ref, v_ref, col_ref, row_ref, o_ref, reverse = dirs[d]
        rows = slice(cc * CHUNK, (cc + 1) * CHUNK)
        mask = (s_idx >= t_idx) if reverse else (s_idx <= t_idx)
        last = 0 if reverse else CHUNK - 1
        j = d * nh + h
        hs = slice(h * hd, (h + 1) * hd)
        q = qk_ref[0, rows, hs]
        k = qk_ref[0, rows, (nh + h) * hd:(nh + h + 1) * hd] * scale
        v_aug = jnp.concatenate([v_ref[0, rows, hs], ones_blk], axis=1)
        gl = _gate_lane(d, h)
        bc = lane_dense(col_ref[0, rows, gl:gl + 1])
        a_c = lane_dense(col_ref[0, rows, GATE_A_LANE + gl:GATE_A_LANE + gl + 1])
        amax = lane_dense(col_ref[0, rows, GATE_AMAX_LANE + gl:GATE_AMAX_LANE + gl + 1])
        a_r = row_ref[0, gl:gl + 1, rows]
        dmat = jnp.exp(jnp.where(mask, a_r - rep(amax, CHUNK // LANES), -jnp.inf))
        s = lax.dot_general(q, k, (((1,), (1,)), ((), ())), preferred_element_type=F32) * dmat
        nd_l = jnp.dot(s.astype(BF16), v_aug, preferred_element_type=F32)
        a_last = amax[last:last + 1, :]
        btot = bc[last:last + 1, :]
        kw = rep(jnp.exp(a_c - a_last), hd // LANES) * k.astype(F32)
        upd = lax.dot_general(kw.astype(BF16), v_aug, (((0,), (0,)), ((), ())),
                              preferred_element_type=F32)

        m_prev = m_st[j, 0:1, :]
        mt = jnp.maximum(amax, m_prev)
        f_l = jnp.exp(amax - mt)
        sc = jnp.exp(m_prev - mt)
        nd_c = jnp.dot(q, cb_st[j], preferred_element_type=F32)
        den = f_l * nd_l[:, hd:] + sc * nd_c[:, hd:]
        inv = 1.0 / jnp.maximum(jnp.abs(den), jnp.exp(-(bc + mt)))
        num = rep(f_l, hd // LANES) * nd_l[:, :hd] + rep(sc, hd // LANES) * nd_c[:, :hd]
        o_ref[0, rows, hs] = (num * rep(inv, hd // LANES)).astype(o_ref.dtype)

        m_last = jnp.maximum(a_last, m_prev)
        dec = jnp.exp(m_prev - m_last)
        f_u = jnp.exp(a_last - m_last)
        wide = (hd + LANES) // LANES
        c_new = rep(dec, wide) * c_st[j] + rep(f_u, wide) * upd
        c_st[j] = c_new
        cb_st[j] = c_new.astype(BF16)
        m_st[j] = jnp.broadcast_to(btot + m_last, m_st.shape[1:])


def mlstm_scan(qk, v, cols, rows, out_dtype):
    bsz, l, _ = qk.shape
    w = D_MODEL
    step = SCAN_SUB * CHUNK
    assert l % step == 0
    nc = l // step
    fwd = lambda cb: (lambda b, c: (b, c, cb))
    bwd = lambda cb: (lambda b, c: (b, nc - 1 - c, cb))
    blk = lambda im: pl.BlockSpec((1, step, w), im)
    ncol = cols.shape[-1]
    in_specs = []
    for mk in (fwd, bwd):
        in_specs += [pl.BlockSpec((1, step, 2 * w), mk(0)), blk(mk(0)),
                     pl.BlockSpec((1, step, ncol), mk(0)),
                     pl.BlockSpec((1, rows.shape[1], step),
                                  (lambda b, c: (b, 0, c)) if mk is fwd else (lambda b, c: (b, 0, nc - 1 - c)))]
    nst = 2 * MLSTM_HEADS
    shp = jax.ShapeDtypeStruct((bsz, l, w), out_dtype)
    return pl.pallas_call(
        _mlstm_kernel,
        grid=(bsz, nc),
        in_specs=in_specs,
        out_specs=[blk(fwd(0)), blk(bwd(0))],
        out_shape=[shp, shp],
        scratch_shapes=[pltpu.VMEM((nst, MLSTM_HD, MLSTM_HD + LANES), F32),
                        pltpu.VMEM((nst, MLSTM_HD, MLSTM_HD + LANES), BF16),
                        pltpu.VMEM((nst, 8, LANES), F32)],
        compiler_params=_params("parallel", "arbitrary"),
        name="mlstm_scan",
    )(qk, v, cols, rows, qk, v, cols, rows)


def _rope_table_kernel(inv_ref, cos_ref, sin_ref):
    tl = cos_ref.shape[0]
    pos = (lax.broadcasted_iota(jnp.int32, cos_ref.shape, 0) + pl.program_id(0) * tl).astype(F32)
    ang = pos * inv_ref[...]
    cos_ref[...] = jnp.cos(ang)
    sin_ref[...] = jnp.sin(ang)


def rope_tables(l, tl=1024):
    half = RET_HDK // 2
    tl = min(tl, l)
    inv = (ROPE_BASE ** (-np.arange(0, RET_HDK, 2, dtype=np.float32) / RET_HDK)).astype(np.float32)
    spec = pl.BlockSpec((tl, half), lambda i: (i, 0))
    shp = jax.ShapeDtypeStruct((l, half), F32)
    return pl.pallas_call(
        _rope_table_kernel,
        grid=(l // tl,),
        in_specs=[pl.BlockSpec((1, half), lambda i: (0, 0))],
        out_specs=[spec, spec],
        out_shape=[shp, shp],
        compiler_params=_params("parallel"),
        name="rope_tables",
    )(jnp.asarray(inv).reshape(1, half))


def _rope(x, cos, sin):
    half = x.shape[-1] // 2
    x1, x2 = x[:, :half], x[:, half:]
    return jnp.concatenate([x1 * cos - x2 * sin, x1 * sin + x2 * cos], axis=-1)


def _odd_proj_kernel(x_ref, g_ref, w_ref, cos_ref, sin_ref, qk_ref, v_ref, gate_ref, *, tn):
    x = x_ref[...]
    xb = (x * lax.rsqrt(jnp.mean(x * x, axis=-1, keepdims=True) + EPS) * g_ref[...]).astype(BF16)
    cos, sin = cos_ref[...], sin_ref[...]
    nh, dk = RET_HEADS, RET_HDK
    for j in range(2 * nh):
        r = jnp.dot(xb, w_ref[:, j * dk:(j + 1) * dk], preferred_element_type=F32)
        r = _rope(r, cos, sin)
        if j < nh:
            r = r * dk ** -0.5
        qk_ref[:, j * dk:(j + 1) * dk] = r.astype(qk_ref.dtype)
    c0 = 2 * nh * dk
    vw = v_ref.shape[1]
    for j in range(vw // tn):
        cols = slice(j * tn, (j + 1) * tn)
        v_ref[:, cols] = jnp.dot(xb, w_ref[:, c0 + j * tn:c0 + (j + 1) * tn],
                                 preferred_element_type=F32).astype(v_ref.dtype)
        gate = jnp.dot(xb, w_ref[:, c0 + vw + j * tn:c0 + vw + (j + 1) * tn], preferred_element_type=F32)
        gate_ref[:, cols] = _silu(gate).astype(gate_ref.dtype)


def odd_proj(x, g, w, cos, sin, seq_len, tm=512, tn=512):
    m, d = x.shape
    qw, vw = RET_HEADS * RET_HDK, RET_HEADS * RET_HDV
    half = RET_HDK // 2
    tm = min(tm, seq_len)
    assert m % seq_len == 0 and seq_len % tm == 0
    tps = seq_len // tm
    row = lambda n: pl.BlockSpec((tm, n), lambda i: (i, 0))
    tab = pl.BlockSpec((tm, half), lambda i: (i % tps, 0))
    return pl.pallas_call(
        functools.partial(_odd_proj_kernel, tn=tn),
        grid=(m // tm,),
        in_specs=[row(d), pl.BlockSpec((1, d), lambda i: (0, 0)),
                  pl.BlockSpec(w.shape, lambda i: (0, 0), pipeline_mode=pl.Buffered(1)), tab, tab],
        out_specs=[row(2 * qw), row(vw), row(vw)],
        out_shape=[jax.ShapeDtypeStruct((m, 2 * qw), BF16), jax.ShapeDtypeStruct((m, vw), BF16),
                   jax.ShapeDtypeStruct((m, vw), BF16)],
        compiler_params=_params("parallel"),
        name="odd_proj",
    )(x, g.reshape(1, d).astype(F32), w, cos, sin)


def _retention_kernel(qkf_ref, vf_ref, qkb_ref, vb_ref, logit_ref, of_ref, ob_ref, r_st, rb_st):
    @pl.when(pl.program_id(1) == 0)
    def _():
        r_st[...] = jnp.zeros_like(r_st)
        rb_st[...] = jnp.zeros_like(rb_st)

    nh, dk, dv = RET_HEADS, RET_HDK, RET_HDV
    t_idx = lax.broadcasted_iota(jnp.int32, (CHUNK, CHUNK), 0)
    s_idx = lax.broadcasted_iota(jnp.int32, (CHUNK, CHUNK), 1)
    pos = lax.broadcasted_iota(jnp.int32, (CHUNK, LANES), 0).astype(F32)
    rep = lambda x, n: jnp.concatenate([x] * n, axis=1)
    dirs = ((qkf_ref, vf_ref, of_ref, False), (qkb_ref, vb_ref, ob_ref, True))
    for d, (qk_ref, v_ref, o_ref, reverse) in enumerate(dirs):
        rel = ((s_idx - t_idx) if reverse else (t_idx - s_idx)).astype(F32)
        for h in range(nh):
            j = d * nh + h
            lg = _log_sigmoid(logit_ref[j])[0:1, :]
            dmask = jnp.where(rel >= 0, jnp.exp(rep(lg, CHUNK // LANES) * jnp.maximum(rel, 0.0)), 0.0)
            if reverse:
                q_dec = jnp.exp(lg * (CHUNK - pos))
                k_dec = jnp.exp(lg * pos)
            else:
                q_dec = jnp.exp(lg * (pos + 1.0))
                k_dec = jnp.exp(lg * (CHUNK - 1.0 - pos))
            c_dec = jnp.exp(lg * CHUNK)
            for cc in (reversed(range(SCAN_SUB)) if reverse else range(SCAN_SUB)):
                rows = slice(cc * CHUNK, (cc + 1) * CHUNK)
                q = qk_ref[0, rows, h * dk:(h + 1) * dk]
                k = qk_ref[0, rows, (nh + h) * dk:(nh + h + 1) * dk]
                v = v_ref[0, rows, h * dv:(h + 1) * dv]
                s = lax.dot_general(q, k, (((1,), (1,)), ((), ())), preferred_element_type=F32) * dmask
                out = jnp.dot(s.astype(BF16), v, preferred_element_type=F32)
                out = out + jnp.dot((q.astype(F32) * rep(q_dec, dk // LANES)).astype(BF16), rb_st[j],
                                    preferred_element_type=F32)
                o_ref[0, rows, h * dv:(h + 1) * dv] = out.astype(o_ref.dtype)
                upd = lax.dot_general((k.astype(F32) * rep(k_dec, dk // LANES)).astype(BF16), v,
                                      (((0,), (0,)), ((), ())), preferred_element_type=F32)
                r_new = rep(c_dec, dv // LANES) * r_st[j] + upd
                r_st[j] = r_new
                rb_st[j] = r_new.astype(BF16)


def retention_scan(qk, v, logit_b, out_dtype):
    bsz, l, _ = qk.shape
    step = SCAN_SUB * CHUNK
    assert l % step == 0
    nc = l // step
    qw, vw = RET_HEADS * RET_HDK, RET_HEADS * RET_HDV
    in_specs = []
    for rev in (False, True):
        ci = (lambda c: nc - 1 - c) if rev else (lambda c: c)
        in_specs += [pl.BlockSpec((1, step, 2 * qw), lambda b, c, ci=ci: (b, ci(c), 0)),
                     pl.BlockSpec((1, step, vw), lambda b, c, ci=ci: (b, ci(c), 0))]
    in_specs.append(pl.BlockSpec((2 * RET_HEADS, 8, LANES), lambda b, c: (0, 0, 0)))
    shp = jax.ShapeDtypeStruct((bsz, l, vw), out_dtype)
    return pl.pallas_call(
        _retention_kernel,
        grid=(bsz, nc),
        in_specs=in_specs,
        out_specs=[pl.BlockSpec((1, step, vw), lambda b, c: (b, c, 0)),
                   pl.BlockSpec((1, step, vw), lambda b, c: (b, nc - 1 - c, 0))],
        out_shape=[shp, shp],
        scratch_shapes=[pltpu.VMEM((2 * RET_HEADS, RET_HDK, RET_HDV), F32),
                        pltpu.VMEM((2 * RET_HEADS, RET_HDK, RET_HDV), BF16)],
        compiler_params=_params("parallel", "arbitrary"),
        name="retention_scan",
    )(qk, v, qk, v, logit_b)


def _hyena_filter_kernel(cst_ref, w1_ref, w2_ref, w3a_ref, w3b_ref, delta_ref, kern_ref, asum_ref, *, l):
    i = pl.program_id(0)
    tr = kern_ref.shape[0]
    hr = tr // 2
    half = LANES // 2
    row = lax.broadcasted_iota(jnp.int32, (hr, LANES), 0) + i * tr
    n_a, n_b = row, row + hr
    lag = lambda n: jnp.where(n < l, n, 2 * l - n).astype(F32)
    tt_a, tt_b = lag(n_a) / (l - 1.0), lag(n_b) / (l - 1.0)
    lo = lax.broadcasted_iota(jnp.int32, (hr, LANES), 1) < half
    p2 = jnp.where(lo, lag(n_a), lag(n_b))
    tt2 = jnp.where(lo, tt_a, tt_b)
    bands, phase, w1t, b1, f1, b2, f2 = (cst_ref[k:k + 1, :] for k in range(7))
    ang = (2.0 * math.pi / l) * bands * p2 + phase
    pre = jnp.dot(jnp.cos(ang), w1_ref[...], preferred_element_type=F32, precision=HI) + tt2 * w1t
    z = jnp.sin(f1 * (pre + b1))
    z = jnp.sin(f2 * (jnp.dot(z, w2_ref[...], preferred_element_type=F32, precision=HI) + b2))
    rep = lambda x: jnp.concatenate([x] * (kern_ref.shape[1] // LANES), axis=1)
    dabs = jnp.abs(delta_ref[...])
    total = jnp.zeros((1, kern_ref.shape[1]), F32)
    z_hi = z.astype(BF16)
    z_lo = (z - z_hi.astype(F32)).astype(BF16)
    z3 = jnp.concatenate([z_hi, z_lo, z_hi], axis=1)
    for part, (w3_ref, n, tt) in enumerate(((w3a_ref, n_a, tt_a), (w3b_ref, n_b, tt_b))):
        hk = jnp.dot(z3, w3_ref[...], preferred_element_type=F32)
        hk = hk * (jnp.exp(-rep(tt) * dabs) + HYENA_SHIFT)
        hk = jnp.where(rep(n) == l, 0.0, hk)
        kern_ref[part * hr:(part + 1) * hr, :] = hk
        total = total + jnp.sum(jnp.abs(hk), axis=0, keepdims=True)

    @pl.when(i == 0)
    def _():
        asum_ref[...] = jnp.zeros_like(asum_ref)

    asum_ref[...] += jnp.broadcast_to(total, asum_ref.shape)


def hyena_filter(l, w1, b1, f1, w2, b2, f2, w3, delta, tr=1024):
    wd = HYENA_WIDTH
    hid = w1.shape[1]
    half = LANES // 2
    nb = HYENA_BANDS
    tr = min(tr, l)
    w1, w2, w3 = w1.astype(F32), w2.astype(F32), w3.astype(F32)
    two = lambda v: jnp.tile(jnp.pad(v.astype(F32).reshape(1, -1), ((0, 0), (0, half - v.shape[-1]))), (1, 2))
    bands = np.zeros((1, half), np.float32)
    bands[0, :nb] = bands[0, nb:2 * nb] = np.linspace(1e-4, nb - 1, nb, dtype=np.float32)
    phase = np.zeros((1, half), np.float32)
    phase[0, nb:2 * nb] = 0.5 * np.pi
    cst = jnp.concatenate([two(jnp.asarray(bands)), two(jnp.asarray(phase)), two(w1[0]), two(b1), two(f1),
                           two(b2), two(f2), jnp.zeros((1, LANES), F32)], axis=0)
    blk = lambda a: jnp.pad(a, ((0, half - a.shape[0]), (0, half - a.shape[1])))
    diag2 = lambda a: jnp.concatenate([jnp.pad(blk(a), ((0, 0), (0, half))),
                                       jnp.pad(blk(a), ((0, 0), (half, 0)))], axis=0)
    w3h = jnp.pad(w3, ((0, half - hid), (0, 0)))

    def split3(a):
        hi = a.astype(BF16)
        lo = (a - hi.astype(F32)).astype(BF16)
        return jnp.concatenate([hi, hi, lo], axis=0)

    w3a = split3(jnp.pad(w3h, ((0, half), (0, 0))))
    w3b = split3(jnp.pad(w3h, ((half, 0), (0, 0))))
    full = lambda shape: pl.BlockSpec(shape, lambda i: (0, 0))
    half_sel = lambda i: (0, (i * tr) // l)
    return pl.pallas_call(
        functools.partial(_hyena_filter_kernel, l=l),
        grid=(2 * l // tr,),
        in_specs=[full((8, LANES)), full((LANES, LANES)), full((LANES, LANES)),
                  pl.BlockSpec((3 * LANES, wd), half_sel), pl.BlockSpec((3 * LANES, wd), half_sel),
                  pl.BlockSpec((1, wd), half_sel)],
        out_specs=[pl.BlockSpec((tr, wd), lambda i: (i, 0)), pl.BlockSpec((8, wd), lambda i: (0, 0))],
        out_shape=[jax.ShapeDtypeStruct((2 * l, wd), F32), jax.ShapeDtypeStruct((8, wd), F32)],
        compiler_params=_params("arbitrary"),
        name="hyena_filter",
    )(cst, diag2(w1[1:]), diag2(w2), w3a, w3b, delta.reshape(1, 2 * wd).astype(F32))


def _dft_consts(n1, n2):
    n = n1 * n2
    k1 = np.arange(n1)
    f1 = np.exp(-2j * np.pi * np.outer(k1, k1) / n1)
    k2 = np.arange(n2)
    f2 = np.exp(-2j * np.pi * np.outer(k2, k2) / n2)
    tw = np.exp(-2j * np.pi * np.outer(k1, k2) / n)
    g = f2[None, :, :] * tw[:, None, :]
    ginv = np.conj(np.transpose(g, (0, 2, 1))) / n
    f1inv = np.conj(f1.T)
    return f1, g, ginv, f1inv


def _stack(c):
    return jnp.asarray(np.concatenate([c.real, c.imag], axis=-2).astype(np.float32)).astype(BF16)


def _cdot(fs, xr, xi, rows):
    p = jnp.dot(fs, xr, preferred_element_type=F32)
    if xi is None:
        return p[:rows], p[rows:]
    q = jnp.dot(fs, xi, preferred_element_type=F32)
    return p[:rows] - q[rows:], q[:rows] + p[rows:]


FFT_ROW_CHUNK = 512


def _kron_eye(f, s_blk):
    k = np.kron(f, np.eye(s_blk))
    to = lambda a: jnp.asarray(a.astype(np.float32)).astype(BF16)
    return to(k.real), to(k.imag)


def _strided_stage_kernel(*refs, has_imag_in, n_out):
    mr_ref, mi_ref, xr_ref = refs[:3]
    xi_ref = refs[3] if has_imag_in else None
    n_in = 4 if has_imag_in else 3
    if len(refs[n_in].shape) == 5:
        outs = [refs[n_in].at[part, 0] for part in range(n_out)]
    else:
        outs = [r.at[0] for r in refs[n_in:]]
    flat = lambda r: r.reshape(r.shape[0] * r.shape[1], r.shape[2])
    xr = flat(xr_ref[0]).astype(BF16)
    xi = flat(xi_ref[0]).astype(BF16) if has_imag_in else None
    rows = mr_ref.shape[0]
    rc = min(FFT_ROW_CHUNK, rows)
    s_blk = xr_ref.shape[2]
    for c0 in range(0, rows, rc):
        mr = mr_ref[c0:c0 + rc, :]
        mi = mi_ref[c0:c0 + rc, :]
        re = jnp.dot(mr, xr, preferred_element_type=F32)
        im = jnp.dot(mi, xr, preferred_element_type=F32) if n_out == 2 else None
        if has_imag_in:
            re = re - jnp.dot(mi, xi, preferred_element_type=F32)
            if n_out == 2:
                im = im + jnp.dot(mr, xi, preferred_element_type=F32)
        k0, k1 = c0 // s_blk, (c0 + rc) // s_blk
        for o_ref, val in zip(outs, (re, im)):
            o_ref[k0:k1] = val.reshape(rc // s_blk, s_blk, val.shape[1]).astype(o_ref.dtype)


def fft_stage1(x, n1, rows_in, pairs, complex_in, s_blk, out_dtype):
    c = x.shape[-1]
    n2 = FFT_N2
    xv = x.reshape(x.shape[0], rows_in, n2, c)
    mr, mi = _kron_eye(_dft_consts(n1, n2)[0][:, :rows_in], s_blk)
    mat = pl.BlockSpec(mr.shape, lambda p, j: (0, 0), pipeline_mode=pl.Buffered(1))
    in_specs = [mat, mat, pl.BlockSpec((1, rows_in, s_blk, c), lambda p, j: (p, 0, j, 0))]
    args = [mr, mi, xv]
    if complex_in:
        in_specs.append(pl.BlockSpec((1, rows_in, s_blk, c), lambda p, j: (p + pairs, 0, j, 0)))
        args.append(xv)
    out_spec = pl.BlockSpec((1, n1, s_blk, c), lambda p, j: (p, 0, j, 0))
    shp = jax.ShapeDtypeStruct((pairs, n1, n2, c), out_dtype)
    return pl.pallas_call(
        functools.partial(_strided_stage_kernel, has_imag_in=complex_in, n_out=2),
        grid=(pairs, n2 // s_blk),
        in_specs=in_specs,
        out_specs=[out_spec, out_spec],
        out_shape=[shp, shp],
        compiler_params=_params("parallel", "parallel"),
        name="fft_stage1",
    )(*args)


def _fft_mid_kernel(*refs, conv):
    n2 = FFT_N2
    if conv:
        gs_ref, gis_ref, ar_ref, ai_ref, kr_ref, ki_ref, br_ref, bi_ref = refs
    else:
        gs_ref, ar_ref, ai_ref, br_ref, bi_ref = refs
    for kk in range(ar_ref.shape[1]):
        xr, xi = _cdot(gs_ref[kk], ar_ref[0, kk].astype(BF16), ai_ref[0, kk].astype(BF16), n2)
        if conv:
            kr, ki = kr_ref[kk].astype(F32), ki_ref[kk].astype(F32)
            yr = xr * kr - xi * ki
            yi = xr * ki + xi * kr
            xr, xi = _cdot(gis_ref[kk], yr.astype(BF16), yi.astype(BF16), n2)
        br_ref[0, kk] = xr.astype(br_ref.dtype)
        bi_ref[0, kk] = xi.astype(bi_ref.dtype)


def fft_mid(ar, ai, n1, c, kf=None, ct=1024, kb=4):
    n2 = FFT_N2
    pairs = ar.shape[0]
    kb = min(kb, n1)
    _, g, ginv, _ = _dft_consts(n1, n2)
    a4 = lambda a: a
    mat = pl.BlockSpec((kb, 2 * n2, n2), lambda k, cb, p: (k, 0, 0))
    dat = pl.BlockSpec((1, kb, n2, ct), lambda k, cb, p: (p, k, 0, cb))
    in_specs = [mat]
    args = [_stack(g)]
    if kf is not None:
        in_specs.append(mat)
        args.append(_stack(ginv))
    in_specs += [dat, dat]
    args += [a4(ar), a4(ai)]
    if kf is not None:
        fil = pl.BlockSpec((kb, n2, ct), lambda k, cb, p: (k, 0, cb))
        in_specs += [fil, fil]
        args += [kf[0], kf[1]]
    shp = jax.ShapeDtypeStruct((pairs, n1, n2, c), BF16)
    br, bi = pl.pallas_call(
        functools.partial(_fft_mid_kernel, conv=kf is not None),
        grid=(n1 // kb, c // ct, pairs),
        in_specs=in_specs,
        out_specs=[dat, dat],
        out_shape=[shp, shp],
        compiler_params=_params("parallel", "parallel", "arbitrary"),
        name="fft_mid",
    )(*args)
    return br, bi


def fft_stage1_inv(br, bi, n1, rows_out, c, complex_out, s_blk):
    pairs = br.shape[0]
    n2 = FFT_N2
    mr, mi = _kron_eye(_dft_consts(n1, n2)[3][:rows_out, :], s_blk)
    parts = 2 if complex_out else 1
    mat = pl.BlockSpec(mr.shape, lambda p, j: (0, 0), pipeline_mode=pl.Buffered(1))
    dat = pl.BlockSpec((1, n1, s_blk, c), lambda p, j: (p, 0, j, 0))
    y = pl.pallas_call(
        functools.partial(_strided_stage_kernel, has_imag_in=True, n_out=parts),
        grid=(pairs, n2 // s_blk),
        in_specs=[mat, mat, dat, dat],
        out_specs=pl.BlockSpec((parts, 1, rows_out, s_blk, c), lambda p, j: (0, p, 0, j, 0)),
        out_shape=jax.ShapeDtypeStruct((parts, pairs, rows_out, n2, c), BF16),
        compiler_params=_params("parallel", "parallel"),
        name="fft_stage1_inv",
    )(mr, mi, br, bi)
    return y.reshape(parts * pairs, rows_out * n2, c)


def hyena_long_conv(t, kern):
    bsz, l, c = t.shape
    assert l % FFT_N2 == 0 and kern.shape == (2 * l, c)
    n1 = 2 * l // FFT_N2
    kr, ki = fft_stage1(kern[None], n1, n1, 1, False, 8, F32)
    kr, ki = fft_mid(kr, ki, n1, c)
    kf = (kr.reshape(n1, FFT_N2, c), ki.reshape(n1, FFT_N2, c))
    complex_in = bsz % 2 == 0
    pairs = bsz // 2 if complex_in else bsz
    ar, ai = fft_stage1(t, n1, n1 // 2, pairs, complex_in, BF16_SUBLANES, BF16)
    br, bi = fft_mid(ar, ai, n1, c, kf)
    return fft_stage1_inv(br, bi, n1, n1 // 2, c, complex_in, BF16_SUBLANES)


def _head_ln(x, nheads, g):
    hd = x.shape[-1] // nheads
    outs = []
    for h in range(nheads):
        seg = x[:, h * hd:(h + 1) * hd]
        mu = jnp.mean(seg, axis=-1, keepdims=True)
        cen = seg - mu
        var = jnp.mean(cen * cen, axis=-1, keepdims=True)
        outs.append(cen * lax.rsqrt(var + GN_EPS))
    return jnp.concatenate(outs, axis=-1) * g


def _cross_attn_tail(x, wqk_ref, vwo_ref, gq_ref, gf_ref, final_norm):
    xn = x * lax.rsqrt(jnp.mean(x * x, axis=-1, keepdims=True) + EPS) * gq_ref[...]
    s_all = jnp.dot(xn.astype(BF16), wqk_ref[0], preferred_element_type=F32)
    nm = s_all.shape[1] // CA_HEADS
    probs = []
    for h in range(CA_HEADS):
        s = s_all[:, h * nm:(h + 1) * nm]
        p = jnp.exp(s - jnp.max(s, axis=-1, keepdims=True))
        probs.append((p * (1.0 / jnp.sum(p, axis=-1, keepdims=True))).astype(BF16))
    y = x + jnp.dot(jnp.concatenate(probs, axis=-1), vwo_ref[0], preferred_element_type=F32)
    if final_norm:
        y = y * lax.rsqrt(jnp.mean(y * y, axis=-1, keepdims=True) + EPS) * gf_ref[...]
    return y


def _even_out_kernel(x_ref, hf_ref, hb_ref, ga_ref, xg_ref, y_ref, t_ref, g_ref, skip_ref, asum_ref, w_ref,
                     wqk_ref, vwo_ref, gq_ref, gf_ref, out_ref, *, final_norm):
    wd = D_MODEL
    ln = _head_ln(hf_ref[0].astype(F32) + hb_ref[0].astype(F32), MLSTM_HEADS, g_ref[...])
    ya = ln * ga_ref[0].astype(F32)
    conv = y_ref[0].astype(F32) * (1.0 / asum_ref[0:1, :])
    yb = xg_ref[0].astype(F32) * (conv + skip_ref[...] * t_ref[0].astype(F32))
    mix = jnp.dot(ya.astype(BF16), w_ref[0:wd, :], preferred_element_type=F32)
    mix = mix + jnp.dot(yb.astype(BF16), w_ref[wd:2 * wd, :], preferred_element_type=F32)
    out_ref[0] = _cross_attn_tail(x_ref[0] + mix, wqk_ref, vwo_ref, gq_ref, gf_ref, final_norm)


def _odd_out_kernel(x_ref, of_ref, ob_ref, gate_ref, g_ref, w_ref,
                    wqk_ref, vwo_ref, gq_ref, gf_ref, out_ref, *, final_norm):
    o = _head_ln(of_ref[0].astype(F32) + ob_ref[0].astype(F32), RET_HEADS, g_ref[...])
    y = gate_ref[0].astype(F32) * o
    mix = jnp.dot(y.astype(BF16), w_ref[...], preferred_element_type=F32)
    out_ref[0] = _cross_attn_tail(x_ref[0] + mix, wqk_ref, vwo_ref, gq_ref, gf_ref, final_norm)


def mixer_out(kernel, x, acts, vecs, w_out, ca, tm=512):
    bsz, l, d = x.shape
    tm = min(tm, l)
    wqk, vwo, gq, gf, final_norm = ca
    row = lambda a: pl.BlockSpec((1, tm, a.shape[-1]), lambda b, i: (b, i, 0))
    res = lambda a: pl.BlockSpec(a.shape, lambda b, i: (0, 0), pipeline_mode=pl.Buffered(1))
    per_b = lambda a: pl.BlockSpec((1,) + a.shape[1:], lambda b, i: (b, 0, 0))
    gq, gf = gq.reshape(1, d).astype(F32), gf.reshape(1, d).astype(F32)
    return pl.pallas_call(
        functools.partial(kernel, final_norm=final_norm),
        grid=(bsz, l // tm),
        in_specs=[row(x)] + [row(a) for a in acts] + [res(v) for v in vecs] + [res(w_out)]
                 + [per_b(wqk), per_b(vwo), res(gq), res(gf)],
        out_specs=row(x),
        out_shape=jax.ShapeDtypeStruct((bsz, l, d), F32),
        compiler_params=_params("parallel", "parallel"),
        name=kernel.__name__.strip("_").replace("_kernel", "_ca"),
    )(x, *acts, *vecs, w_out, wqk, vwo, gq, gf)


def _ca_fold_kernel(kv_ref, wq_ref, wo_ref, wqk_ref, vwo_ref):
    d = D_MODEL
    for h in range(CA_HEADS):
        hs = slice(h * CA_HD, (h + 1) * CA_HD)
        k = kv_ref[0, :, hs]
        v = kv_ref[0, :, d + h * CA_HD:d + (h + 1) * CA_HD]
        nm = k.shape[0]
        wqk = lax.dot_general(wq_ref[:, hs], k, (((1,), (1,)), ((), ())), preferred_element_type=F32)
        wqk_ref[0, :, h * nm:(h + 1) * nm] = (wqk * CA_HD ** -0.5).astype(wqk_ref.dtype)
        vwo_ref[0, h * nm:(h + 1) * nm, :] = jnp.dot(v, wo_ref[hs, :],
                                                     preferred_element_type=F32).astype(vwo_ref.dtype)


def ca_fold(kv, wq, wo):
    bsz, nm, d2 = kv.shape
    d = d2 // 2
    mat = pl.BlockSpec((d, d), lambda b: (0, 0))
    return pl.pallas_call(
        _ca_fold_kernel,
        grid=(bsz,),
        in_specs=[pl.BlockSpec((1, nm, d2), lambda b: (b, 0, 0)), mat, mat],
        out_specs=[pl.BlockSpec((1, d, CA_HEADS * nm), lambda b: (b, 0, 0)),
                   pl.BlockSpec((1, CA_HEADS * nm, d), lambda b: (b, 0, 0))],
        out_shape=[jax.ShapeDtypeStruct((bsz, d, CA_HEADS * nm), BF16),
                   jax.ShapeDtypeStruct((bsz, CA_HEADS * nm, d), BF16)],
        compiler_params=_params("parallel"),
        name="ca_fold",
    )(kv, wq, wo)


def _even_mixer(x, p, i, ca):
    bsz, l, d = x.shape
    w = D_MODEL
    nh = MLSTM_HEADS
    m = bsz * l
    g_mix = p['norm_mix_g_layer']
    w_in = p['even_w_in'][i]
    gate0 = 5 * w
    hy0 = gate0 + N_GATE_COLS
    w_conv = jnp.concatenate([w_in[:, :2 * w], w_in[:, hy0:hy0 + 3 * w]], axis=1).astype(BF16)
    w_rest = jnp.concatenate([w_in[:, 2 * w:gate0], w_in[:, hy0 + 3 * w:]], axis=1).astype(BF16)
    w_gate = jnp.pad(w_in[:, gate0:hy0], ((0, 0), (0, LANES - N_GATE_COLS))).astype(BF16)
    conv_w = jnp.concatenate([p['mlstm_conv_w'][i], p['hyena_conv_w'][i]], axis=1).astype(F32)
    conv_b = jnp.concatenate([p['mlstm_conv_b'][i], p['hyena_conv_b'][i]]).astype(F32).reshape(1, 5 * w)
    bias = jnp.pad(p['mlstm_gate_bias'][i].astype(F32).reshape(1, N_GATE_COLS),
                   ((0, 0), (0, LANES - N_GATE_COLS)))
    xf = x.reshape(m, d)
    qk, xg, t, v, ga, cols = even_proj(xf, g_mix, w_conv, w_rest, w_gate, conv_w, conv_b, bias, l)
    cols = cols.reshape(bsz, l, LANES)
    rows = jnp.swapaxes(cols[..., GATE_A_LANE:GATE_A_LANE + N_GATE_COLS], 1, 2)

    hf, hb = mlstm_scan(qk.reshape(bsz, l, 2 * w), v.reshape(bsz, l, w), cols, rows, BF16)

    t = t.reshape(bsz, l, w)
    kern, asum = hyena_filter(l, p['hyena_w1'][i], p['hyena_b1'][i], p['hyena_freq1'][i], p['hyena_w2'][i],
                              p['hyena_b2'][i], p['hyena_freq2'][i], p['hyena_w3'][i], p['hyena_delta'][i])
    y = hyena_long_conv(t, kern)

    b3 = lambda a: a.reshape(bsz, l, w)
    vec = lambda a: a.reshape(1, w).astype(F32)
    return mixer_out(_even_out_kernel, x, [hf, hb, b3(ga), b3(xg), y, t],
                     [vec(p['mlstm_norm_g'][i]), vec(p['hyena_skip'][i]), asum],
                     p['even_w_out'][i].astype(BF16), ca)


def _odd_mixer(x, p, i, cos, sin, ca):
    bsz, l, d = x.shape
    m = bsz * l
    xf = x.reshape(m, d)
    qk, v, gate = odd_proj(xf, p['norm_mix_g_layer'], p['odd_w_in'][i].astype(BF16), cos, sin, l)
    logit = p['ret_decay_logit'][i].astype(F32).reshape(2 * RET_HEADS, 1, 1)
    logit_b = jnp.broadcast_to(logit, (2 * RET_HEADS, 8, LANES))
    vw = RET_HEADS * RET_HDV
    of, ob = retention_scan(qk.reshape(bsz, l, -1), v.reshape(bsz, l, vw), logit_b, BF16)
    return mixer_out(_odd_out_kernel, x, [of, ob, gate.reshape(bsz, l, vw)],
                     [p['ret_norm_g'][i].reshape(1, vw).astype(F32)], p['odd_w_out'][i].astype(BF16), ca)


def _trunk(x, mem, p, cos, sin):
    depth = p['norm_mix_g'].shape[0]
    bsz, nm, d = mem.shape
    for layer in range(depth):
        i = layer // 2
        p['norm_mix_g_layer'] = p['norm_mix_g'][layer]
        kv, = rms_matmul(mem.reshape(bsz * nm, d), p['norm_mem_g'][layer], [p['ca_wkv'][layer].astype(BF16)],
                         [BF16], tm=nm)
        wqk, vwo = ca_fold(kv.reshape(bsz, nm, 2 * d), p['ca_wq'][layer].astype(BF16),
                           p['ca_wo'][layer].astype(BF16))
        ca = (wqk, vwo, p['norm_ca_g'][layer], p['norm_final_g'], layer == depth - 1)
        x = _even_mixer(x, p, i, ca) if layer % 2 == 0 else _odd_mixer(x, p, i, cos, sin, ca)
    return x


def kernel(x_prompt, x_sample, mem_prompt, mem_sample, norm_mix_g, norm_ca_g, norm_mem_g, norm_final_g, even_w_in, mlstm_conv_w, mlstm_conv_b, mlstm_gate_bias, mlstm_norm_g, hyena_conv_w, hyena_conv_b, hyena_w1, hyena_b1, hyena_freq1, hyena_w2, hyena_b2, hyena_freq2, hyena_w3, hyena_delta, hyena_skip, even_w_out, odd_w_in, ret_decay_logit, ret_norm_g, odd_w_out, ca_wq, ca_wkv, ca_wo):
    p = {'norm_mix_g': norm_mix_g, 'norm_ca_g': norm_ca_g, 'norm_mem_g': norm_mem_g, 'norm_final_g': norm_final_g,
         'even_w_in': even_w_in, 'mlstm_conv_w': mlstm_conv_w, 'mlstm_conv_b': mlstm_conv_b,
         'mlstm_gate_bias': mlstm_gate_bias, 'mlstm_norm_g': mlstm_norm_g,
         'hyena_conv_w': hyena_conv_w, 'hyena_conv_b': hyena_conv_b, 'hyena_w1': hyena_w1, 'hyena_b1': hyena_b1,
         'hyena_freq1': hyena_freq1, 'hyena_w2': hyena_w2, 'hyena_b2': hyena_b2, 'hyena_freq2': hyena_freq2,
         'hyena_w3': hyena_w3, 'hyena_delta': hyena_delta, 'hyena_skip': hyena_skip, 'even_w_out': even_w_out,
         'odd_w_in': odd_w_in, 'ret_decay_logit': ret_decay_logit, 'ret_norm_g': ret_norm_g, 'odd_w_out': odd_w_out,
         'ca_wq': ca_wq, 'ca_wkv': ca_wkv, 'ca_wo': ca_wo}
    l_max = max(x_prompt.shape[1], x_sample.shape[1])
    cos, sin = rope_tables(l_max)
    y_prompt = _trunk(x_prompt, mem_prompt, dict(p), cos, sin)
    y_sample = _trunk(x_sample, mem_sample, dict(p), cos, sin)
    return (y_prompt, y_sample)
```

```python
import functools
import math

import numpy as np
import jax
import jax.numpy as jnp
from jax import lax
from jax.experimental import pallas as pl
from jax.experimental.pallas import tpu as pltpu

F32 = jnp.float32
BF16 = jnp.bfloat16

D_MODEL = 1024
EPS = 1e-6
GN_EPS = 1e-5
CHUNK = 256

MLSTM_HEADS = 4
MLSTM_HD = D_MODEL // MLSTM_HEADS
N_GATE_COLS = 4 * MLSTM_HEADS

HYENA_WIDTH = D_MODEL
HYENA_EMB = 33
HYENA_BANDS = (HYENA_EMB - 1) // 2
HYENA_SHIFT = 0.05
FFT_N2 = 256

RET_HEADS = 4
RET_HDK = D_MODEL // RET_HEADS
RET_HDV = 2 * D_MODEL // RET_HEADS
ROPE_BASE = 10000.0

CA_HEADS = 4
CA_HD = D_MODEL // CA_HEADS

LANES = 128
BF16_SUBLANES = 16
VMEM_LIMIT = 56 * 1024 * 1024

HI = lax.Precision.HIGHEST


def _params(*sem):
    return pltpu.CompilerParams(dimension_semantics=sem, vmem_limit_bytes=VMEM_LIMIT)


def _silu(x):
    return x * (1.0 / (1.0 + jnp.exp(-x)))


def _sigmoid(x):
    return 1.0 / (1.0 + jnp.exp(-x))


def _log_sigmoid(x):
    return jnp.minimum(x, 0.0) - jnp.log(1.0 + jnp.exp(-jnp.abs(x)))


def _rms_matmul_kernel(x_ref, g_ref, *refs, tn):
    nw = len(refs) // 2
    x = x_ref[...]
    xn = (x * lax.rsqrt(jnp.mean(x * x, axis=-1, keepdims=True) + EPS) * g_ref[...]).astype(BF16)
    for w_ref, o_ref in zip(refs[:nw], refs[nw:]):
        n = o_ref.shape[1]
        step = min(tn, n)
        for j in range(n // step):
            cols = slice(j * step, (j + 1) * step)
            o_ref[:, cols] = jnp.dot(xn, w_ref[:, cols], preferred_element_type=F32).astype(o_ref.dtype)


def rms_matmul(x, g, ws, out_dtypes, tm, tn=1024):
    m, d = x.shape
    w_specs = [pl.BlockSpec(w.shape, lambda i: (0, 0), pipeline_mode=pl.Buffered(1)) for w in ws]
    return pl.pallas_call(
        functools.partial(_rms_matmul_kernel, tn=tn),
        grid=(m // tm,),
        in_specs=[pl.BlockSpec((tm, d), lambda i: (i, 0)), pl.BlockSpec((1, d), lambda i: (0, 0))] + w_specs,
        out_specs=[pl.BlockSpec((tm, w.shape[1]), lambda i: (i, 0)) for w in ws],
        out_shape=[jax.ShapeDtypeStruct((m, w.shape[1]), dt) for w, dt in zip(ws, out_dtypes)],
        compiler_params=_params("parallel"),
        name="rms_matmul",
    )(x, g.reshape(1, d).astype(F32), *ws)


F32_SUBLANES = 8


def _even_proj_kernel(x_ref, xp_ref, xn_ref, g_ref, wc_ref, wr_ref, wg_ref, cw_ref, cb_ref, gb_ref,
                      qk_ref, xg_ref, t_ref, v_ref, ga_ref, cols_ref, *, tiles_per_seq, tn):
    i = pl.program_id(0)
    tm, d = x_ref.shape
    w, h = D_MODEL, F32_SUBLANES
    first = (i % tiles_per_seq) == 0
    last = (i % tiles_per_seq) == tiles_per_seq - 1
    g = g_ref[...]

    def norm(x):
        return x * lax.rsqrt(jnp.mean(x * x, axis=-1, keepdims=True) + EPS) * g

    xm = norm(x_ref[...])
    xp = jnp.where(first, 0.0, norm(xp_ref[...]))
    xn = jnp.where(last, 0.0, norm(xn_ref[...]))
    x_ext = jnp.concatenate([xp, xm, xn], axis=0).astype(BF16)
    xb = xm.astype(BF16)

    def conv(c0):
        cols = slice(c0, c0 + tn)
        r = jnp.dot(x_ext, wc_ref[:, cols], preferred_element_type=F32)
        t = r.shape[0]
        y = (pltpu.roll(r, 1, axis=0) * cw_ref[0:1, cols] + r * cw_ref[1:2, cols]
             + pltpu.roll(r, t - 1, axis=0) * cw_ref[2:3, cols] + cb_ref[:, cols])
        return y[h:h + tm]

    for j in range(2 * w // tn):
        qk_ref[:, j * tn:(j + 1) * tn] = _silu(conv(j * tn)).astype(qk_ref.dtype)
    rest = lambda group, j: jnp.dot(xb, wr_ref[:, group * w + j * tn:group * w + (j + 1) * tn],
                                    preferred_element_type=F32)
    for j in range(w // tn):
        cols = slice(j * tn, (j + 1) * tn)
        xg_ref[:, cols] = (conv(2 * w + j * tn) * _silu(rest(3, j))).astype(xg_ref.dtype)
        t_ref[:, cols] = (conv(3 * w + j * tn) * conv(4 * w + j * tn)).astype(t_ref.dtype)
        v_ref[:, cols] = rest(0, j).astype(v_ref.dtype)
        ga_ref[:, cols] = (_sigmoid(rest(1, j)) * _silu(rest(2, j))).astype(ga_ref.dtype)
    cols_ref[...] = _gate_prep(jnp.dot(xb, wg_ref[...], preferred_element_type=F32) + gb_ref[...])


def even_proj(x, g, w_conv, w_rest, w_gate, conv_w, conv_b, gate_bias, seq_len, tm=512, tn=512):
    m, d = x.shape
    w, h = D_MODEL, F32_SUBLANES
    tm = min(tm, seq_len)
    assert m % seq_len == 0 and seq_len % tm == 0 and tm % CHUNK == 0
    r = tm // h
    last_blk = m // h - 1
    res = lambda a: pl.BlockSpec(a.shape, lambda i: (0, 0), pipeline_mode=pl.Buffered(1))
    row = lambda n: pl.BlockSpec((tm, n), lambda i: (i, 0))
    return pl.pallas_call(
        functools.partial(_even_proj_kernel, tiles_per_seq=seq_len // tm, tn=tn),
        grid=(m // tm,),
        in_specs=[row(d),
                  pl.BlockSpec((h, d), lambda i: (jnp.maximum(i * r - 1, 0), 0)),
                  pl.BlockSpec((h, d), lambda i: (jnp.minimum((i + 1) * r, last_blk), 0)),
                  pl.BlockSpec((1, d), lambda i: (0, 0)),
                  res(w_conv), res(w_rest), res(w_gate), res(conv_w), res(conv_b), res(gate_bias)],
        out_specs=[row(2 * w), row(w), row(w), row(w), row(w), row(LANES)],
        out_shape=[jax.ShapeDtypeStruct((m, 2 * w), BF16), jax.ShapeDtypeStruct((m, w), BF16),
                   jax.ShapeDtypeStruct((m, w), BF16), jax.ShapeDtypeStruct((m, w), BF16),
                   jax.ShapeDtypeStruct((m, w), BF16), jax.ShapeDtypeStruct((m, LANES), F32)],
        compiler_params=_params("parallel"),
        name="even_proj",
    )(x, x, x, g.reshape(1, d).astype(F32), w_conv, w_rest, w_gate, conv_w, conv_b, gate_bias)


def _seg_scan(x, op, identity, reverse):
    t = x.shape[0]
    r = lax.broadcasted_iota(jnp.int32, x.shape, 0) % CHUNK
    k = 1
    while k < CHUNK:
        if reverse:
            shifted = pltpu.roll(x, t - k, axis=0)
            valid = r < CHUNK - k
        else:
            shifted = pltpu.roll(x, k, axis=0)
            valid = r >= k
        x = op(x, jnp.where(valid, shifted, identity))
        k *= 2
    return x


GATE_A_LANE = 16
GATE_AMAX_LANE = 32


def _gate_lane(d, h):
    return d * 2 * MLSTM_HEADS + h


def _gate_prep(g):
    nh = MLSTM_HEADS
    lane = lax.broadcasted_iota(jnp.int32, g.shape, 1)
    live = (lane < 4 * nh) & (lane % (2 * nh) < nh)
    fwd = lane < 2 * nh
    lf = pltpu.roll(_log_sigmoid(g), LANES - nh, axis=1)
    bcum = jnp.where(fwd, _seg_scan(lf, jnp.add, 0.0, False), _seg_scan(lf, jnp.add, 0.0, True))
    a = g - bcum
    amax = jnp.where(fwd, _seg_scan(a, jnp.maximum, -jnp.inf, False),
                     _seg_scan(a, jnp.maximum, -jnp.inf, True))
    keep = lambda v: jnp.where(live, v, 0.0)
    return (keep(bcum) + pltpu.roll(keep(a), GATE_A_LANE, axis=1)
            + pltpu.roll(keep(amax), GATE_AMAX_LANE, axis=1))


def _mlstm_kernel(qkf_ref, vf_ref, cf_ref, rf_ref, qkb_ref, vb_ref, cb_ref, rb_ref,
                  hf_ref, hb_ref, c_st, m_st):
    @pl.when(pl.program_id(1) == 0)
    def _():
        c_st[...] = jnp.zeros_like(c_st)
        m_st[...] = jnp.zeros_like(m_st)

    nh, hd = MLSTM_HEADS, MLSTM_HD
    t_idx = lax.broadcasted_iota(jnp.int32, (CHUNK, CHUNK), 0)
    s_idx = lax.broadcasted_iota(jnp.int32, (CHUNK, CHUNK), 1)
    scale = hd ** -0.5
    ones_blk = jnp.ones((CHUNK, LANES), BF16)
    rep = lambda x, n: jnp.concatenate([x] * n, axis=1)
    lane_dense = lambda col: jnp.broadcast_to(col, (CHUNK, LANES))
    dirs = ((qkf_ref, vf_ref, cf_ref, rf_ref, hf_ref, False), (qkb_ref, vb_ref, cb_ref, rb_ref, hb_ref, True))
    for d, (qk_ref, v_ref, col_ref, row_ref, o_ref, reverse) in enumerate(dirs):
        mask = (s_idx >= t_idx) if reverse else (s_idx <= t_idx)
        last = 0 if reverse else CHUNK - 1
        for h in range(nh):
            j = d * nh + h
            hs = slice(h * hd, (h + 1) * hd)
            q = qk_ref[0, :, hs]
            k = qk_ref[0, :, (nh + h) * hd:(nh + h + 1) * hd] * scale
            v_aug = jnp.concatenate([v_ref[0, :, hs], ones_blk], axis=1)
            gl = _gate_lane(d, h)
            bc = lane_dense(col_ref[0, :, gl:gl + 1])
            a_c = lane_dense(col_ref[0, :, GATE_A_LANE + gl:GATE_A_LANE + gl + 1])
            amax = lane_dense(col_ref[0, :, GATE_AMAX_LANE + gl:GATE_AMAX_LANE + gl + 1])
            a_r = row_ref[0, gl:gl + 1, :]
            dmat = jnp.exp(jnp.where(mask, a_r - rep(amax, CHUNK // LANES), -jnp.inf))
            s = lax.dot_general(q, k, (((1,), (1,)), ((), ())), preferred_element_type=F32) * dmat
            nd_l = jnp.dot(s.astype(BF16), v_aug, preferred_element_type=F32)
            a_last = amax[last:last + 1, :]
            btot = bc[last:last + 1, :]
            kw = rep(jnp.exp(a_c - a_last), hd // LANES) * k.astype(F32)
            upd = lax.dot_general(kw.astype(BF16), v_aug, (((0,), (0,)), ((), ())),
                                  preferred_element_type=F32)

            m_prev = m_st[j, 0:1, :]
            mt = jnp.maximum(amax, m_prev)
            f_l = jnp.exp(amax - mt)
            sc = jnp.exp(m_prev - mt)
            nd_c = jnp.dot(q, c_st[j].astype(BF16), preferred_element_type=F32)
            den = f_l * nd_l[:, hd:] + sc * nd_c[:, hd:]
            inv = 1.0 / jnp.maximum(jnp.abs(den), jnp.exp(-(bc + mt)))
            num = rep(f_l, hd // LANES) * nd_l[:, :hd] + rep(sc, hd // LANES) * nd_c[:, :hd]
            o_ref[0, :, hs] = (num * rep(inv, hd // LANES)).astype(o_ref.dtype)

            m_last = jnp.maximum(a_last, m_prev)
            dec = jnp.exp(m_prev - m_last)
            f_u = jnp.exp(a_last - m_last)
            wide = (hd + LANES) // LANES
            c_new = rep(dec, wide) * c_st[j] + rep(f_u, wide) * upd
            c_st[j] = c_new
            m_st[j] = jnp.broadcast_to(btot + m_last, m_st.shape[1:])


def mlstm_scan(qk, v, cols, rows, out_dtype):
    bsz, l, _ = qk.shape
    w = D_MODEL
    assert l % CHUNK == 0
    nc = l // CHUNK
    fwd = lambda cb: (lambda b, c: (b, c, cb))
    bwd = lambda cb: (lambda b, c: (b, nc - 1 - c, cb))
    blk = lambda im: pl.BlockSpec((1, CHUNK, w), im)
    ncol = cols.shape[-1]
    in_specs = []
    for mk in (fwd, bwd):
        in_specs += [pl.BlockSpec((1, CHUNK, 2 * w), mk(0)), blk(mk(0)),
                     pl.BlockSpec((1, CHUNK, ncol), mk(0)),
                     pl.BlockSpec((1, rows.shape[1], CHUNK),
                                  (lambda b, c: (b, 0, c)) if mk is fwd else (lambda b, c: (b, 0, nc - 1 - c)))]
    nst = 2 * MLSTM_HEADS
    shp = jax.ShapeDtypeStruct((bsz, l, w), out_dtype)
    return pl.pallas_call(
        _mlstm_kernel,
        grid=(bsz, nc),
        in_specs=in_specs,
        out_specs=[blk(fwd(0)), blk(bwd(0))],
        out_shape=[shp, shp],
        scratch_shapes=[pltpu.VMEM((nst, MLSTM_HD, MLSTM_HD + LANES), F32),
                        pltpu.VMEM((nst, 8, LANES), F32)],
        compiler_params=_params("parallel", "arbitrary"),
        name="mlstm_scan",
    )(qk, v, cols, rows, qk, v, cols, rows)


def _rope_table_kernel(inv_ref, cos_ref, sin_ref):
    tl = cos_ref.shape[0]
    pos = (lax.broadcasted_iota(jnp.int32, cos_ref.shape, 0) + pl.program_id(0) * tl).astype(F32)
    ang = pos * inv_ref[...]
    cos_ref[...] = jnp.cos(ang)
    sin_ref[...] = jnp.sin(ang)


def rope_tables(l, tl=1024):
    half = RET_HDK // 2
    tl = min(tl, l)
    inv = (ROPE_BASE ** (-np.arange(0, RET_HDK, 2, dtype=np.float32) / RET_HDK)).astype(np.float32)
    spec = pl.BlockSpec((tl, half), lambda i: (i, 0))
    shp = jax.ShapeDtypeStruct((l, half), F32)
    return pl.pallas_call(
        _rope_table_kernel,
        grid=(l // tl,),
        in_specs=[pl.BlockSpec((1, half), lambda i: (0, 0))],
        out_specs=[spec, spec],
        out_shape=[shp, shp],
        compiler_params=_params("parallel"),
        name="rope_tables",
    )(jnp.asarray(inv).reshape(1, half))


def _rope(x, cos, sin):
    half = x.shape[-1] // 2
    x1, x2 = x[:, :half], x[:, half:]
    return jnp.concatenate([x1 * cos - x2 * sin, x1 * sin + x2 * cos], axis=-1)


def _odd_proj_kernel(x_ref, g_ref, w_ref, cos_ref, sin_ref, qk_ref, v_ref, gate_ref, *, tn):
    x = x_ref[...]
    xb = (x * lax.rsqrt(jnp.mean(x * x, axis=-1, keepdims=True) + EPS) * g_ref[...]).astype(BF16)
    cos, sin = cos_ref[...], sin_ref[...]
    nh, dk = RET_HEADS, RET_HDK
    for j in range(2 * nh):
        r = jnp.dot(xb, w_ref[:, j * dk:(j + 1) * dk], preferred_element_type=F32)
        r = _rope(r, cos, sin)
        if j < nh:
            r = r * dk ** -0.5
        qk_ref[:, j * dk:(j + 1) * dk] = r.astype(qk_ref.dtype)
    c0 = 2 * nh * dk
    vw = v_ref.shape[1]
    for j in range(vw // tn):
        cols = slice(j * tn, (j + 1) * tn)
        v_ref[:, cols] = jnp.dot(xb, w_ref[:, c0 + j * tn:c0 + (j + 1) * tn],
                                 preferred_element_type=F32).astype(v_ref.dtype)
        gate = jnp.dot(xb, w_ref[:, c0 + vw + j * tn:c0 + vw + (j + 1) * tn], preferred_element_type=F32)
        gate_ref[:, cols] = _silu(gate).astype(gate_ref.dtype)


def odd_proj(x, g, w, cos, sin, seq_len, tm=512, tn=512):
    m, d = x.shape
    qw, vw = RET_HEADS * RET_HDK, RET_HEADS * RET_HDV
    half = RET_HDK // 2
    tm = min(tm, seq_len)
    assert m % seq_len == 0 and seq_len % tm == 0
    tps = seq_len // tm
    row = lambda n: pl.BlockSpec((tm, n), lambda i: (i, 0))
    tab = pl.BlockSpec((tm, half), lambda i: (i % tps, 0))
    return pl.pallas_call(
        functools.partial(_odd_proj_kernel, tn=tn),
        grid=(m // tm,),
        in_specs=[row(d), pl.BlockSpec((1, d), lambda i: (0, 0)),
                  pl.BlockSpec(w.shape, lambda i: (0, 0), pipeline_mode=pl.Buffered(1)), tab, tab],
        out_specs=[row(2 * qw), row(vw), row(vw)],
        out_shape=[jax.ShapeDtypeStruct((m, 2 * qw), BF16), jax.ShapeDtypeStruct((m, vw), BF16),
                   jax.ShapeDtypeStruct((m, vw), BF16)],
        compiler_params=_params("parallel"),
        name="odd_proj",
    )(x, g.reshape(1, d).astype(F32), w, cos, sin)


def _retention_kernel(qkf_ref, vf_ref, qkb_ref, vb_ref, logit_ref, of_ref, ob_ref, r_st):
    @pl.when(pl.program_id(1) == 0)
    def _():
        r_st[...] = jnp.zeros_like(r_st)

    nh, dk, dv = RET_HEADS, RET_HDK, RET_HDV
    t_idx = lax.broadcasted_iota(jnp.int32, (CHUNK, CHUNK), 0)
    s_idx = lax.broadcasted_iota(jnp.int32, (CHUNK, CHUNK), 1)
    pos = lax.broadcasted_iota(jnp.int32, (CHUNK, LANES), 0).astype(F32)
    rep = lambda x, n: jnp.concatenate([x] * n, axis=1)
    dirs = ((qkf_ref, vf_ref, of_ref, False), (qkb_ref, vb_ref, ob_ref, True))
    for d, (qk_ref, v_ref, o_ref, reverse) in enumerate(dirs):
        rel = ((s_idx - t_idx) if reverse else (t_idx - s_idx)).astype(F32)
        for h in range(nh):
            j = d * nh + h
            lg = _log_sigmoid(logit_ref[j])[0:1, :]
            dmask = jnp.where(rel >= 0, jnp.exp(rep(lg, CHUNK // LANES) * jnp.maximum(rel, 0.0)), 0.0)
            if reverse:
                q_dec = jnp.exp(lg * (CHUNK - pos))
                k_dec = jnp.exp(lg * pos)
            else:
                q_dec = jnp.exp(lg * (pos + 1.0))
                k_dec = jnp.exp(lg * (CHUNK - 1.0 - pos))
            c_dec = jnp.exp(lg * CHUNK)
            q = qk_ref[0, :, h * dk:(h + 1) * dk]
            k = qk_ref[0, :, (nh + h) * dk:(nh + h + 1) * dk]
            v = v_ref[0, :, h * dv:(h + 1) * dv]
            s = lax.dot_general(q, k, (((1,), (1,)), ((), ())), preferred_element_type=F32) * dmask
            out = jnp.dot(s.astype(BF16), v, preferred_element_type=F32)
            out = out + jnp.dot((q.astype(F32) * rep(q_dec, dk // LANES)).astype(BF16), r_st[j].astype(BF16),
                                preferred_element_type=F32)
            o_ref[0, :, h * dv:(h + 1) * dv] = out.astype(o_ref.dtype)
            upd = lax.dot_general((k.astype(F32) * rep(k_dec, dk // LANES)).astype(BF16), v,
                                  (((0,), (0,)), ((), ())), preferred_element_type=F32)
            r_st[j] = rep(c_dec, dv // LANES) * r_st[j] + upd


def retention_scan(qk, v, logit_b, out_dtype):
    bsz, l, _ = qk.shape
    nc = l // CHUNK
    qw, vw = RET_HEADS * RET_HDK, RET_HEADS * RET_HDV
    in_specs = []
    for rev in (False, True):
        ci = (lambda c: nc - 1 - c) if rev else (lambda c: c)
        in_specs += [pl.BlockSpec((1, CHUNK, 2 * qw), lambda b, c, ci=ci: (b, ci(c), 0)),
                     pl.BlockSpec((1, CHUNK, vw), lambda b, c, ci=ci: (b, ci(c), 0))]
    in_specs.append(pl.BlockSpec((2 * RET_HEADS, 8, LANES), lambda b, c: (0, 0, 0)))
    shp = jax.ShapeDtypeStruct((bsz, l, vw), out_dtype)
    return pl.pallas_call(
        _retention_kernel,
        grid=(bsz, nc),
        in_specs=in_specs,
        out_specs=[pl.BlockSpec((1, CHUNK, vw), lambda b, c: (b, c, 0)),
                   pl.BlockSpec((1, CHUNK, vw), lambda b, c: (b, nc - 1 - c, 0))],
        out_shape=[shp, shp],
        scratch_shapes=[pltpu.VMEM((2 * RET_HEADS, RET_HDK, RET_HDV), F32)],
        compiler_params=_params("parallel", "arbitrary"),
        name="retention_scan",
    )(qk, v, qk, v, logit_b)


def _hyena_filter_kernel(cst_ref, w1_ref, w2_ref, w3a_ref, w3b_ref, delta_ref, kern_ref, asum_ref, *, l):
    i = pl.program_id(0)
    tr = kern_ref.shape[0]
    hr = tr // 2
    half = LANES // 2
    row = lax.broadcasted_iota(jnp.int32, (hr, LANES), 0) + i * tr
    n_a, n_b = row, row + hr
    lag = lambda n: jnp.where(n < l, n, 2 * l - n).astype(F32)
    tt_a, tt_b = lag(n_a) / (l - 1.0), lag(n_b) / (l - 1.0)
    lo = lax.broadcasted_iota(jnp.int32, (hr, LANES), 1) < half
    p2 = jnp.where(lo, lag(n_a), lag(n_b))
    tt2 = jnp.where(lo, tt_a, tt_b)
    bands, phase, w1t, b1, f1, b2, f2 = (cst_ref[k:k + 1, :] for k in range(7))
    ang = (2.0 * math.pi / l) * bands * p2 + phase
    pre = jnp.dot(jnp.cos(ang), w1_ref[...], preferred_element_type=F32, precision=HI) + tt2 * w1t
    z = jnp.sin(f1 * (pre + b1))
    z = jnp.sin(f2 * (jnp.dot(z, w2_ref[...], preferred_element_type=F32, precision=HI) + b2))
    rep = lambda x: jnp.concatenate([x] * (kern_ref.shape[1] // LANES), axis=1)
    dabs = jnp.abs(delta_ref[...])
    total = jnp.zeros((1, kern_ref.shape[1]), F32)
    z_hi = z.astype(BF16)
    z_lo = (z - z_hi.astype(F32)).astype(BF16)
    z3 = jnp.concatenate([z_hi, z_lo, z_hi], axis=1)
    for part, (w3_ref, n, tt) in enumerate(((w3a_ref, n_a, tt_a), (w3b_ref, n_b, tt_b))):
        hk = jnp.dot(z3, w3_ref[...], preferred_element_type=F32)
        hk = hk * (jnp.exp(-rep(tt) * dabs) + HYENA_SHIFT)
        hk = jnp.where(rep(n) == l, 0.0, hk)
        kern_ref[part * hr:(part + 1) * hr, :] = hk
        total = total + jnp.sum(jnp.abs(hk), axis=0, keepdims=True)

    @pl.when(i == 0)
    def _():
        asum_ref[...] = jnp.zeros_like(asum_ref)

    asum_ref[...] += jnp.broadcast_to(total, asum_ref.shape)


def hyena_filter(l, w1, b1, f1, w2, b2, f2, w3, delta, tr=1024):
    wd = HYENA_WIDTH
    hid = w1.shape[1]
    half = LANES // 2
    nb = HYENA_BANDS
    tr = min(tr, l)
    w1, w2, w3 = w1.astype(F32), w2.astype(F32), w3.astype(F32)
    two = lambda v: jnp.tile(jnp.pad(v.astype(F32).reshape(1, -1), ((0, 0), (0, half - v.shape[-1]))), (1, 2))
    bands = np.zeros((1, half), np.float32)
    bands[0, :nb] = bands[0, nb:2 * nb] = np.linspace(1e-4, nb - 1, nb, dtype=np.float32)
    phase = np.zeros((1, half), np.float32)
    phase[0, nb:2 * nb] = 0.5 * np.pi
    cst = jnp.concatenate([two(jnp.asarray(bands)), two(jnp.asarray(phase)), two(w1[0]), two(b1), two(f1),
                           two(b2), two(f2), jnp.zeros((1, LANES), F32)], axis=0)
    blk = lambda a: jnp.pad(a, ((0, half - a.shape[0]), (0, half - a.shape[1])))
    diag2 = lambda a: jnp.concatenate([jnp.pad(blk(a), ((0, 0), (0, half))),
                                       jnp.pad(blk(a), ((0, 0), (half, 0)))], axis=0)
    w3h = jnp.pad(w3, ((0, half - hid), (0, 0)))

    def split3(a):
        hi = a.astype(BF16)
        lo = (a - hi.astype(F32)).astype(BF16)
        return jnp.concatenate([hi, hi, lo], axis=0)

    w3a = split3(jnp.pad(w3h, ((0, half), (0, 0))))
    w3b = split3(jnp.pad(w3h, ((half, 0), (0, 0))))
    full = lambda shape: pl.BlockSpec(shape, lambda i: (0, 0))
    half_sel = lambda i: (0, (i * tr) // l)
    return pl.pallas_call(
        functools.partial(_hyena_filter_kernel, l=l),
        grid=(2 * l // tr,),
        in_specs=[full((8, LANES)), full((LANES, LANES)), full((LANES, LANES)),
                  pl.BlockSpec((3 * LANES, wd), half_sel), pl.BlockSpec((3 * LANES, wd), half_sel),
                  pl.BlockSpec((1, wd), half_sel)],
        out_specs=[pl.BlockSpec((tr, wd), lambda i: (i, 0)), pl.BlockSpec((8, wd), lambda i: (0, 0))],
        out_shape=[jax.ShapeDtypeStruct((2 * l, wd), F32), jax.ShapeDtypeStruct((8, wd), F32)],
        compiler_params=_params("arbitrary"),
        name="hyena_filter",
    )(cst, diag2(w1[1:]), diag2(w2), w3a, w3b, delta.reshape(1, 2 * wd).astype(F32))


def _dft_consts(n1, n2):
    n = n1 * n2
    k1 = np.arange(n1)
    f1 = np.exp(-2j * np.pi * np.outer(k1, k1) / n1)
    k2 = np.arange(n2)
    f2 = np.exp(-2j * np.pi * np.outer(k2, k2) / n2)
    tw = np.exp(-2j * np.pi * np.outer(k1, k2) / n)
    g = f2[None, :, :] * tw[:, None, :]
    ginv = np.conj(np.transpose(g, (0, 2, 1))) / n
    f1inv = np.conj(f1.T)
    return f1, g, ginv, f1inv


def _stack(c):
    return jnp.asarray(np.concatenate([c.real, c.imag], axis=-2).astype(np.float32)).astype(BF16)


def _cdot(fs, xr, xi, rows):
    p = jnp.dot(fs, xr, preferred_element_type=F32)
    if xi is None:
        return p[:rows], p[rows:]
    q = jnp.dot(fs, xi, preferred_element_type=F32)
    return p[:rows] - q[rows:], q[:rows] + p[rows:]


FFT_ROW_CHUNK = 512


def _kron_eye(f, s_blk):
    k = np.kron(f, np.eye(s_blk))
    to = lambda a: jnp.asarray(a.astype(np.float32)).astype(BF16)
    return to(k.real), to(k.imag)


def _strided_stage_kernel(*refs, has_imag_in, n_out):
    mr_ref, mi_ref, xr_ref = refs[:3]
    xi_ref = refs[3] if has_imag_in else None
    n_in = 4 if has_imag_in else 3
    if len(refs[n_in].shape) == 5:
        outs = [refs[n_in].at[part, 0] for part in range(n_out)]
    else:
        outs = [r.at[0] for r in refs[n_in:]]
    flat = lambda r: r.reshape(r.shape[0] * r.shape[1], r.shape[2])
    xr = flat(xr_ref[0]).astype(BF16)
    xi = flat(xi_ref[0]).astype(BF16) if has_imag_in else None
    rows = mr_ref.shape[0]
    rc = min(FFT_ROW_CHUNK, rows)
    s_blk = xr_ref.shape[2]
    for c0 in range(0, rows, rc):
        mr = mr_ref[c0:c0 + rc, :]
        mi = mi_ref[c0:c0 + rc, :]
        re = jnp.dot(mr, xr, preferred_element_type=F32)
        im = jnp.dot(mi, xr, preferred_element_type=F32) if n_out == 2 else None
        if has_imag_in:
            re = re - jnp.dot(mi, xi, preferred_element_type=F32)
            if n_out == 2:
                im = im + jnp.dot(mr, xi, preferred_element_type=F32)
        k0, k1 = c0 // s_blk, (c0 + rc) // s_blk
        for o_ref, val in zip(outs, (re, im)):
            o_ref[k0:k1] = val.reshape(rc // s_blk, s_blk, val.shape[1]).astype(o_ref.dtype)


def fft_stage1(x, n1, rows_in, pairs, complex_in, s_blk, out_dtype):
    c = x.shape[-1]
    n2 = FFT_N2
    xv = x.reshape(x.shape[0], rows_in, n2, c)
    mr, mi = _kron_eye(_dft_consts(n1, n2)[0][:, :rows_in], s_blk)
    mat = pl.BlockSpec(mr.shape, lambda p, j: (0, 0), pipeline_mode=pl.Buffered(1))
    in_specs = [mat, mat, pl.BlockSpec((1, rows_in, s_blk, c), lambda p, j: (p, 0, j, 0))]
    args = [mr, mi, xv]
    if complex_in:
        in_specs.append(pl.BlockSpec((1, rows_in, s_blk, c), lambda p, j: (p + pairs, 0, j, 0)))
        args.append(xv)
    out_spec = pl.BlockSpec((1, n1, s_blk, c), lambda p, j: (p, 0, j, 0))
    shp = jax.ShapeDtypeStruct((pairs, n1, n2, c), out_dtype)
    return pl.pallas_call(
        functools.partial(_strided_stage_kernel, has_imag_in=complex_in, n_out=2),
        grid=(pairs, n2 // s_blk),
        in_specs=in_specs,
        out_specs=[out_spec, out_spec],
        out_shape=[shp, shp],
        compiler_params=_params("parallel", "parallel"),
        name="fft_stage1",
    )(*args)


def _fft_mid_kernel(*refs, conv):
    n2 = FFT_N2
    if conv:
        gs_ref, gis_ref, ar_ref, ai_ref, kr_ref, ki_ref, br_ref, bi_ref = refs
    else:
        gs_ref, ar_ref, ai_ref, br_ref, bi_ref = refs
    for kk in range(ar_ref.shape[1]):
        xr, xi = _cdot(gs_ref[kk], ar_ref[0, kk].astype(BF16), ai_ref[0, kk].astype(BF16), n2)
        if conv:
            kr, ki = kr_ref[kk].astype(F32), ki_ref[kk].astype(F32)
            yr = xr * kr - xi * ki
            yi = xr * ki + xi * kr
            xr, xi = _cdot(gis_ref[kk], yr.astype(BF16), yi.astype(BF16), n2)
        br_ref[0, kk] = xr.astype(br_ref.dtype)
        bi_ref[0, kk] = xi.astype(bi_ref.dtype)


def fft_mid(ar, ai, n1, c, kf=None, ct=1024, kb=4):
    n2 = FFT_N2
    pairs = ar.shape[0]
    kb = min(kb, n1)
    _, g, ginv, _ = _dft_consts(n1, n2)
    a4 = lambda a: a
    mat = pl.BlockSpec((kb, 2 * n2, n2), lambda k, cb, p: (k, 0, 0))
    dat = pl.BlockSpec((1, kb, n2, ct), lambda k, cb, p: (p, k, 0, cb))
    in_specs = [mat]
    args = [_stack(g)]
    if kf is not None:
        in_specs.append(mat)
        args.append(_stack(ginv))
    in_specs += [dat, dat]
    args += [a4(ar), a4(ai)]
    if kf is not None:
        fil = pl.BlockSpec((kb, n2, ct), lambda k, cb, p: (k, 0, cb))
        in_specs += [fil, fil]
        args += [kf[0], kf[1]]
    shp = jax.ShapeDtypeStruct((pairs, n1, n2, c), BF16)
    br, bi = pl.pallas_call(
        functools.partial(_fft_mid_kernel, conv=kf is not None),
        grid=(n1 // kb, c // ct, pairs),
        in_specs=in_specs,
        out_specs=[dat, dat],
        out_shape=[shp, shp],
        compiler_params=_params("parallel", "parallel", "arbitrary"),
        name="fft_mid",
    )(*args)
    return br, bi


def fft_stage1_inv(br, bi, n1, rows_out, c, complex_out, s_blk):
    pairs = br.shape[0]
    n2 = FFT_N2
    mr, mi = _kron_eye(_dft_consts(n1, n2)[3][:rows_out, :], s_blk)
    parts = 2 if complex_out else 1
    mat = pl.BlockSpec(mr.shape, lambda p, j: (0, 0), pipeline_mode=pl.Buffered(1))
    dat = pl.BlockSpec((1, n1, s_blk, c), lambda p, j: (p, 0, j, 0))
    y = pl.pallas_call(
        functools.partial(_strided_stage_kernel, has_imag_in=True, n_out=parts),
        grid=(pairs, n2 // s_blk),
        in_specs=[mat, mat, dat, dat],
        out_specs=pl.BlockSpec((parts, 1, rows_out, s_blk, c), lambda p, j: (0, p, 0, j, 0)),
        out_shape=jax.ShapeDtypeStruct((parts, pairs, rows_out, n2, c), BF16),
        compiler_params=_params("parallel", "parallel"),
        name="fft_stage1_inv",
    )(mr, mi, br, bi)
    return y.reshape(parts * pairs, rows_out * n2, c)


def hyena_long_conv(t, kern):
    bsz, l, c = t.shape
    assert l % FFT_N2 == 0 and kern.shape == (2 * l, c)
    n1 = 2 * l // FFT_N2
    kr, ki = fft_stage1(kern[None], n1, n1, 1, False, 8, F32)
    kr, ki = fft_mid(kr, ki, n1, c)
    kf = (kr.reshape(n1, FFT_N2, c), ki.reshape(n1, FFT_N2, c))
    complex_in = bsz % 2 == 0
    pairs = bsz // 2 if complex_in else bsz
    ar, ai = fft_stage1(t, n1, n1 // 2, pairs, complex_in, BF16_SUBLANES, BF16)
    br, bi = fft_mid(ar, ai, n1, c, kf)
    return fft_stage1_inv(br, bi, n1, n1 // 2, c, complex_in, BF16_SUBLANES)


def _head_ln(x, nheads, g):
    hd = x.shape[-1] // nheads
    outs = []
    for h in range(nheads):
        seg = x[:, h * hd:(h + 1) * hd]
        mu = jnp.mean(seg, axis=-1, keepdims=True)
        cen = seg - mu
        var = jnp.mean(cen * cen, axis=-1, keepdims=True)
        outs.append(cen * lax.rsqrt(var + GN_EPS))
    return jnp.concatenate(outs, axis=-1) * g


def _cross_attn_tail(x, wqk_ref, vwo_ref, gq_ref, gf_ref, final_norm):
    xn = x * lax.rsqrt(jnp.mean(x * x, axis=-1, keepdims=True) + EPS) * gq_ref[...]
    s_all = jnp.dot(xn.astype(BF16), wqk_ref[0], preferred_element_type=F32)
    nm = s_all.shape[1] // CA_HEADS
    probs = []
    for h in range(CA_HEADS):
        s = s_all[:, h * nm:(h + 1) * nm]
        p = jnp.exp(s - jnp.max(s, axis=-1, keepdims=True))
        probs.append((p * (1.0 / jnp.sum(p, axis=-1, keepdims=True))).astype(BF16))
    y = x + jnp.dot(jnp.concatenate(probs, axis=-1), vwo_ref[0], preferred_element_type=F32)
    if final_norm:
        y = y * lax.rsqrt(jnp.mean(y * y, axis=-1, keepdims=True) + EPS) * gf_ref[...]
    return y


def _even_out_kernel(x_ref, hf_ref, hb_ref, ga_ref, xg_ref, y_ref, t_ref, g_ref, skip_ref, asum_ref, w_ref,
                     wqk_ref, vwo_ref, gq_ref, gf_ref, out_ref, *, final_norm):
    wd = D_MODEL
    ln = _head_ln(hf_ref[0].astype(F32) + hb_ref[0].astype(F32), MLSTM_HEADS, g_ref[...])
    ya = ln * ga_ref[0].astype(F32)
    conv = y_ref[0].astype(F32) * (1.0 / asum_ref[0:1, :])
    yb = xg_ref[0].astype(F32) * (conv + skip_ref[...] * t_ref[0].astype(F32))
    mix = jnp.dot(ya.astype(BF16), w_ref[0:wd, :], preferred_element_type=F32)
    mix = mix + jnp.dot(yb.astype(BF16), w_ref[wd:2 * wd, :], preferred_element_type=F32)
    out_ref[0] = _cross_attn_tail(x_ref[0] + mix, wqk_ref, vwo_ref, gq_ref, gf_ref, final_norm)


def _odd_out_kernel(x_ref, of_ref, ob_ref, gate_ref, g_ref, w_ref,
                    wqk_ref, vwo_ref, gq_ref, gf_ref, out_ref, *, final_norm):
    o = _head_ln(of_ref[0].astype(F32) + ob_ref[0].astype(F32), RET_HEADS, g_ref[...])
    y = gate_ref[0].astype(F32) * o
    mix = jnp.dot(y.astype(BF16), w_ref[...], preferred_element_type=F32)
    out_ref[0] = _cross_attn_tail(x_ref[0] + mix, wqk_ref, vwo_ref, gq_ref, gf_ref, final_norm)


def mixer_out(kernel, x, acts, vecs, w_out, ca, tm=512):
    bsz, l, d = x.shape
    tm = min(tm, l)
    wqk, vwo, gq, gf, final_norm = ca
    row = lambda a: pl.BlockSpec((1, tm, a.shape[-1]), lambda b, i: (b, i, 0))
    res = lambda a: pl.BlockSpec(a.shape, lambda b, i: (0, 0), pipeline_mode=pl.Buffered(1))
    per_b = lambda a: pl.BlockSpec((1,) + a.shape[1:], lambda b, i: (b, 0, 0))
    gq, gf = gq.reshape(1, d).astype(F32), gf.reshape(1, d).astype(F32)
    return pl.pallas_call(
        functools.partial(kernel, final_norm=final_norm),
        grid=(bsz, l // tm),
        in_specs=[row(x)] + [row(a) for a in acts] + [res(v) for v in vecs] + [res(w_out)]
                 + [per_b(wqk), per_b(vwo), res(gq), res(gf)],
        out_specs=row(x),
        out_shape=jax.ShapeDtypeStruct((bsz, l, d), F32),
        compiler_params=_params("parallel", "parallel"),
        name=kernel.__name__.strip("_").replace("_kernel", "_ca"),
    )(x, *acts, *vecs, w_out, wqk, vwo, gq, gf)


def _ca_fold_kernel(kv_ref, wq_ref, wo_ref, wqk_ref, vwo_ref):
    d = D_MODEL
    for h in range(CA_HEADS):
        hs = slice(h * CA_HD, (h + 1) * CA_HD)
        k = kv_ref[0, :, hs]
        v = kv_ref[0, :, d + h * CA_HD:d + (h + 1) * CA_HD]
        nm = k.shape[0]
        wqk = lax.dot_general(wq_ref[:, hs], k, (((1,), (1,)), ((), ())), preferred_element_type=F32)
        wqk_ref[0, :, h * nm:(h + 1) * nm] = (wqk * CA_HD ** -0.5).astype(wqk_ref.dtype)
        vwo_ref[0, h * nm:(h + 1) * nm, :] = jnp.dot(v, wo_ref[hs, :],
                                                     preferred_element_type=F32).astype(vwo_ref.dtype)


def ca_fold(kv, wq, wo):
    bsz, nm, d2 = kv.shape
    d = d2 // 2
    mat = pl.BlockSpec((d, d), lambda b: (0, 0))
    return pl.pallas_call(
        _ca_fold_kernel,
        grid=(bsz,),
        in_specs=[pl.BlockSpec((1, nm, d2), lambda b: (b, 0, 0)), mat, mat],
        out_specs=[pl.BlockSpec((1, d, CA_HEADS * nm), lambda b: (b, 0, 0)),
                   pl.BlockSpec((1, CA_HEADS * nm, d), lambda b: (b, 0, 0))],
        out_shape=[jax.ShapeDtypeStruct((bsz, d, CA_HEADS * nm), BF16),
                   jax.ShapeDtypeStruct((bsz, CA_HEADS * nm, d), BF16)],
        compiler_params=_params("parallel"),
        name="ca_fold",
    )(kv, wq, wo)


def _even_mixer(x, p, i, ca):
    bsz, l, d = x.shape
    w = D_MODEL
    nh = MLSTM_HEADS
    m = bsz * l
    g_mix = p['norm_mix_g_layer']
    w_in = p['even_w_in'][i]
    gate0 = 5 * w
    hy0 = gate0 + N_GATE_COLS
    w_conv = jnp.concatenate([w_in[:, :2 * w], w_in[:, hy0:hy0 + 3 * w]], axis=1).astype(BF16)
    w_rest = jnp.concatenate([w_in[:, 2 * w:gate0], w_in[:, hy0 + 3 * w:]], axis=1).astype(BF16)
    w_gate = jnp.pad(w_in[:, gate0:hy0], ((0, 0), (0, LANES - N_GATE_COLS))).astype(BF16)
    conv_w = jnp.concatenate([p['mlstm_conv_w'][i], p['hyena_conv_w'][i]], axis=1).astype(F32)
    conv_b = jnp.concatenate([p['mlstm_conv_b'][i], p['hyena_conv_b'][i]]).astype(F32).reshape(1, 5 * w)
    bias = jnp.pad(p['mlstm_gate_bias'][i].astype(F32).reshape(1, N_GATE_COLS),
                   ((0, 0), (0, LANES - N_GATE_COLS)))
    xf = x.reshape(m, d)
    qk, xg, t, v, ga, cols = even_proj(xf, g_mix, w_conv, w_rest, w_gate, conv_w, conv_b, bias, l)
    cols = cols.reshape(bsz, l, LANES)
    rows = jnp.swapaxes(cols[..., GATE_A_LANE:GATE_A_LANE + N_GATE_COLS], 1, 2)

    hf, hb = mlstm_scan(qk.reshape(bsz, l, 2 * w), v.reshape(bsz, l, w), cols, rows, BF16)

    t = t.reshape(bsz, l, w)
    kern, asum = hyena_filter(l, p['hyena_w1'][i], p['hyena_b1'][i], p['hyena_freq1'][i], p['hyena_w2'][i],
                              p['hyena_b2'][i], p['hyena_freq2'][i], p['hyena_w3'][i], p['hyena_delta'][i])
    y = hyena_long_conv(t, kern)

    b3 = lambda a: a.reshape(bsz, l, w)
    vec = lambda a: a.reshape(1, w).astype(F32)
    return mixer_out(_even_out_kernel, x, [hf, hb, b3(ga), b3(xg), y, t],
                     [vec(p['mlstm_norm_g'][i]), vec(p['hyena_skip'][i]), asum],
                     p['even_w_out'][i].astype(BF16), ca)


def _odd_mixer(x, p, i, cos, sin, ca):
    bsz, l, d = x.shape
    m = bsz * l
    xf = x.reshape(m, d)
    qk, v, gate = odd_proj(xf, p['norm_mix_g_layer'], p['odd_w_in'][i].astype(BF16), cos, sin, l)
    logit = p['ret_decay_logit'][i].astype(F32).reshape(2 * RET_HEADS, 1, 1)
    logit_b = jnp.broadcast_to(logit, (2 * RET_HEADS, 8, LANES))
    vw = RET_HEADS * RET_HDV
    of, ob = retention_scan(qk.reshape(bsz, l, -1), v.reshape(bsz, l, vw), logit_b, BF16)
    return mixer_out(_odd_out_kernel, x, [of, ob, gate.reshape(bsz, l, vw)],
                     [p['ret_norm_g'][i].reshape(1, vw).astype(F32)], p['odd_w_out'][i].astype(BF16), ca)


def _trunk(x, mem, p, cos, sin):
    depth = p['norm_mix_g'].shape[0]
    bsz, nm, d = mem.shape
    for layer in range(depth):
        i = layer // 2
        p['norm_mix_g_layer'] = p['norm_mix_g'][layer]
        kv, = rms_matmul(mem.reshape(bsz * nm, d), p['norm_mem_g'][layer], [p['ca_wkv'][layer].astype(BF16)],
                         [BF16], tm=nm)
        wqk, vwo = ca_fold(kv.reshape(bsz, nm, 2 * d), p['ca_wq'][layer].astype(BF16),
                           p['ca_wo'][layer].astype(BF16))
        ca = (wqk, vwo, p['norm_ca_g'][layer], p['norm_final_g'], layer == depth - 1)
        x = _even_mixer(x, p, i, ca) if layer % 2 == 0 else _odd_mixer(x, p, i, cos, sin, ca)
    return x


def kernel(x_prompt, x_sample, mem_prompt, mem_sample, norm_mix_g, norm_ca_g, norm_mem_g, norm_final_g, even_w_in, mlstm_conv_w, mlstm_conv_b, mlstm_gate_bias, mlstm_norm_g, hyena_conv_w, hyena_conv_b, hyena_w1, hyena_b1, hyena_freq1, hyena_w2, hyena_b2, hyena_freq2, hyena_w3, hyena_delta, hyena_skip, even_w_out, odd_w_in, ret_decay_logit, ret_norm_g, odd_w_out, ca_wq, ca_wkv, ca_wo):
    p = {'norm_mix_g': norm_mix_g, 'norm_ca_g': norm_ca_g, 'norm_mem_g': norm_mem_g, 'norm_final_g': norm_final_g,
         'even_w_in': even_w_in, 'mlstm_conv_w': mlstm_conv_w, 'mlstm_conv_b': mlstm_conv_b,
         'mlstm_gate_bias': mlstm_gate_bias, 'mlstm_norm_g': mlstm_norm_g,
         'hyena_conv_w': hyena_conv_w, 'hyena_conv_b': hyena_conv_b, 'hyena_w1': hyena_w1, 'hyena_b1': hyena_b1,
         'hyena_freq1': hyena_freq1, 'hyena_w2': hyena_w2, 'hyena_b2': hyena_b2, 'hyena_freq2': hyena_freq2,
         'hyena_w3': hyena_w3, 'hyena_delta': hyena_delta, 'hyena_skip': hyena_skip, 'even_w_out': even_w_out,
         'odd_w_in': odd_w_in, 'ret_decay_logit': ret_decay_logit, 'ret_norm_g': ret_norm_g, 'odd_w_out': odd_w_out,
         'ca_wq': ca_wq, 'ca_wkv': ca_wkv, 'ca_wo': ca_wo}
    l_max = max(x_prompt.shape[1], x_sample.shape[1])
    cos, sin = rope_tables(l_max)
    y_prompt = _trunk(x_prompt, mem_prompt, dict(p), cos, sin)
    y_sample = _trunk(x_sample, mem_sample, dict(p), cos, sin)
    return (y_prompt, y_sample)
```

```python
import functools
import math

import numpy as np
import jax
import jax.numpy as jnp
from jax import lax
from jax.experimental import pallas as pl
from jax.experimental.pallas import tpu as pltpu

F32 = jnp.float32
BF16 = jnp.bfloat16

D_MODEL = 1024
EPS = 1e-6
GN_EPS = 1e-5
CHUNK = 256

MLSTM_HEADS = 4
MLSTM_HD = D_MODEL // MLSTM_HEADS
N_GATE_COLS = 4 * MLSTM_HEADS

HYENA_WIDTH = D_MODEL
HYENA_EMB = 33
HYENA_BANDS = (HYENA_EMB - 1) // 2
HYENA_SHIFT = 0.05
FFT_N2 = 256

RET_HEADS = 4
RET_HDK = D_MODEL // RET_HEADS
RET_HDV = 2 * D_MODEL // RET_HEADS
ROPE_BASE = 10000.0

CA_HEADS = 4
CA_HD = D_MODEL // CA_HEADS

LANES = 128
BF16_SUBLANES = 16
VMEM_LIMIT = 56 * 1024 * 1024

HI = lax.Precision.HIGHEST


def _params(*sem):
    return pltpu.CompilerParams(dimension_semantics=sem, vmem_limit_bytes=VMEM_LIMIT)


def _silu(x):
    return x * (1.0 / (1.0 + jnp.exp(-x)))


def _sigmoid(x):
    return 1.0 / (1.0 + jnp.exp(-x))


def _log_sigmoid(x):
    return jnp.minimum(x, 0.0) - jnp.log(1.0 + jnp.exp(-jnp.abs(x)))


def _rms_matmul_kernel(x_ref, g_ref, *refs, tn):
    nw = len(refs) // 2
    x = x_ref[...]
    xn = (x * lax.rsqrt(jnp.mean(x * x, axis=-1, keepdims=True) + EPS) * g_ref[...]).astype(BF16)
    for w_ref, o_ref in zip(refs[:nw], refs[nw:]):
        n = o_ref.shape[1]
        step = min(tn, n)
        for j in range(n // step):
            cols = slice(j * step, (j + 1) * step)
            o_ref[:, cols] = jnp.dot(xn, w_ref[:, cols], preferred_element_type=F32).astype(o_ref.dtype)


def rms_matmul(x, g, ws, out_dtypes, tm, tn=1024):
    m, d = x.shape
    w_specs = [pl.BlockSpec(w.shape, lambda i: (0, 0), pipeline_mode=pl.Buffered(1)) for w in ws]
    return pl.pallas_call(
        functools.partial(_rms_matmul_kernel, tn=tn),
        grid=(m // tm,),
        in_specs=[pl.BlockSpec((tm, d), lambda i: (i, 0)), pl.BlockSpec((1, d), lambda i: (0, 0))] + w_specs,
        out_specs=[pl.BlockSpec((tm, w.shape[1]), lambda i: (i, 0)) for w in ws],
        out_shape=[jax.ShapeDtypeStruct((m, w.shape[1]), dt) for w, dt in zip(ws, out_dtypes)],
        compiler_params=_params("parallel"),
        name="rms_matmul",
    )(x, g.reshape(1, d).astype(F32), *ws)


F32_SUBLANES = 8


def _even_proj_kernel(x_ref, xp_ref, xn_ref, g_ref, wc_ref, wr_ref, wg_ref, cw_ref, cb_ref, gb_ref,
                      qk_ref, xg_ref, t_ref, v_ref, ga_ref, cols_ref, *, tiles_per_seq, tn):
    i = pl.program_id(0)
    tm, d = x_ref.shape
    w, h = D_MODEL, F32_SUBLANES
    first = (i % tiles_per_seq) == 0
    last = (i % tiles_per_seq) == tiles_per_seq - 1
    g = g_ref[...]

    def norm(x):
        return x * lax.rsqrt(jnp.mean(x * x, axis=-1, keepdims=True) + EPS) * g

    xm = norm(x_ref[...])
    xp = jnp.where(first, 0.0, norm(xp_ref[...]))
    xn = jnp.where(last, 0.0, norm(xn_ref[...]))
    x_ext = jnp.concatenate([xp, xm, xn], axis=0).astype(BF16)
    xb = xm.astype(BF16)

    def conv(c0):
        cols = slice(c0, c0 + tn)
        r = jnp.dot(x_ext, wc_ref[:, cols], preferred_element_type=F32)
        t = r.shape[0]
        y = (pltpu.roll(r, 1, axis=0) * cw_ref[0:1, cols] + r * cw_ref[1:2, cols]
             + pltpu.roll(r, t - 1, axis=0) * cw_ref[2:3, cols] + cb_ref[:, cols])
        return y[h:h + tm]

    for j in range(2 * w // tn):
        qk_ref[:, j * tn:(j + 1) * tn] = _silu(conv(j * tn)).astype(qk_ref.dtype)
    rest = lambda group, j: jnp.dot(xb, wr_ref[:, group * w + j * tn:group * w + (j + 1) * tn],
                                    preferred_element_type=F32)
    for j in range(w // tn):
        cols = slice(j * tn, (j + 1) * tn)
        xg_ref[:, cols] = (conv(2 * w + j * tn) * _silu(rest(3, j))).astype(xg_ref.dtype)
        t_ref[:, cols] = (conv(3 * w + j * tn) * conv(4 * w + j * tn)).astype(t_ref.dtype)
        v_ref[:, cols] = rest(0, j).astype(v_ref.dtype)
        ga_ref[:, cols] = (_sigmoid(rest(1, j)) * _silu(rest(2, j))).astype(ga_ref.dtype)
    cols_ref[...] = _gate_prep(jnp.dot(xb, wg_ref[...], preferred_element_type=F32) + gb_ref[...])


def even_proj(x, g, w_conv, w_rest, w_gate, conv_w, conv_b, gate_bias, seq_len, tm=512, tn=512):
    m, d = x.shape
    w, h = D_MODEL, F32_SUBLANES
    tm = min(tm, seq_len)
    assert m % seq_len == 0 and seq_len % tm == 0 and tm % CHUNK == 0
    r = tm // h
    last_blk = m // h - 1
    res = lambda a: pl.BlockSpec(a.shape, lambda i: (0, 0), pipeline_mode=pl.Buffered(1))
    row = lambda n: pl.BlockSpec((tm, n), lambda i: (i, 0))
    return pl.pallas_call(
        functools.partial(_even_proj_kernel, tiles_per_seq=seq_len // tm, tn=tn),
        grid=(m // tm,),
        in_specs=[row(d),
                  pl.BlockSpec((h, d), lambda i: (jnp.maximum(i * r - 1, 0), 0)),
                  pl.BlockSpec((h, d), lambda i: (jnp.minimum((i + 1) * r, last_blk), 0)),
                  pl.BlockSpec((1, d), lambda i: (0, 0)),
                  res(w_conv), res(w_rest), res(w_gate), res(conv_w), res(conv_b), res(gate_bias)],
        out_specs=[row(2 * w), row(w), row(w), row(w), row(w), row(LANES)],
        out_shape=[jax.ShapeDtypeStruct((m, 2 * w), BF16), jax.ShapeDtypeStruct((m, w), BF16),
                   jax.ShapeDtypeStruct((m, w), BF16), jax.ShapeDtypeStruct((m, w), BF16),
                   jax.ShapeDtypeStruct((m, w), BF16), jax.ShapeDtypeStruct((m, LANES), F32)],
        compiler_params=_params("parallel"),
        name="even_proj",
    )(x, x, x, g.reshape(1, d).astype(F32), w_conv, w_rest, w_gate, conv_w, conv_b, gate_bias)


def _seg_scan(x, op, identity, reverse):
    t = x.shape[0]
    r = lax.broadcasted_iota(jnp.int32, x.shape, 0) % CHUNK
    k = 1
    while k < CHUNK:
        if reverse:
            shifted = pltpu.roll(x, t - k, axis=0)
            valid = r < CHUNK - k
        else:
            shifted = pltpu.roll(x, k, axis=0)
            valid = r >= k
        x = op(x, jnp.where(valid, shifted, identity))
        k *= 2
    return x


GATE_A_LANE = 16
GATE_AMAX_LANE = 32


def _gate_lane(d, h):
    return d * 2 * MLSTM_HEADS + h


def _gate_prep(g):
    nh = MLSTM_HEADS
    lane = lax.broadcasted_iota(jnp.int32, g.shape, 1)
    live = (lane < 4 * nh) & (lane % (2 * nh) < nh)
    fwd = lane < 2 * nh
    lf = pltpu.roll(_log_sigmoid(g), LANES - nh, axis=1)
    bcum = jnp.where(fwd, _seg_scan(lf, jnp.add, 0.0, False), _seg_scan(lf, jnp.add, 0.0, True))
    a = g - bcum
    amax = jnp.where(fwd, _seg_scan(a, jnp.maximum, -jnp.inf, False),
                     _seg_scan(a, jnp.maximum, -jnp.inf, True))
    keep = lambda v: jnp.where(live, v, 0.0)
    return (keep(bcum) + pltpu.roll(keep(a), GATE_A_LANE, axis=1)
            + pltpu.roll(keep(amax), GATE_AMAX_LANE, axis=1))


def _mlstm_kernel(qkf_ref, vf_ref, cf_ref, rf_ref, qkb_ref, vb_ref, cb_ref, rb_ref,
                  hf_ref, hb_ref, c_st, m_st):
    @pl.when(pl.program_id(1) == 0)
    def _():
        c_st[...] = jnp.zeros_like(c_st)
        m_st[...] = jnp.zeros_like(m_st)

    nh, hd = MLSTM_HEADS, MLSTM_HD
    t_idx = lax.broadcasted_iota(jnp.int32, (CHUNK, CHUNK), 0)
    s_idx = lax.broadcasted_iota(jnp.int32, (CHUNK, CHUNK), 1)
    scale = hd ** -0.5
    ones_blk = jnp.ones((CHUNK, LANES), BF16)
    rep = lambda x, n: jnp.concatenate([x] * n, axis=1)
    lane_dense = lambda col: jnp.broadcast_to(col, (CHUNK, LANES))
    dirs = ((qkf_ref, vf_ref, cf_ref, rf_ref, hf_ref, False), (qkb_ref, vb_ref, cb_ref, rb_ref, hb_ref, True))
    for d, (qk_ref, v_ref, col_ref, row_ref, o_ref, reverse) in enumerate(dirs):
        mask = (s_idx >= t_idx) if reverse else (s_idx <= t_idx)
        last = 0 if reverse else CHUNK - 1
        for h in range(nh):
            j = d * nh + h
            hs = slice(h * hd, (h + 1) * hd)
            q = qk_ref[0, :, hs]
            k = qk_ref[0, :, (nh + h) * hd:(nh + h + 1) * hd] * scale
            v_aug = jnp.concatenate([v_ref[0, :, hs], ones_blk], axis=1)
            gl = _gate_lane(d, h)
            bc = lane_dense(col_ref[0, :, gl:gl + 1])
            a_c = lane_dense(col_ref[0, :, GATE_A_LANE + gl:GATE_A_LANE + gl + 1])
            amax = lane_dense(col_ref[0, :, GATE_AMAX_LANE + gl:GATE_AMAX_LANE + gl + 1])
            a_r = row_ref[0, gl:gl + 1, :]
            dmat = jnp.exp(jnp.where(mask, a_r - rep(amax, CHUNK // LANES), -jnp.inf))
            s = lax.dot_general(q, k, (((1,), (1,)), ((), ())), preferred_element_type=F32) * dmat
            nd_l = jnp.dot(s.astype(BF16), v_aug, preferred_element_type=F32)
            a_last = amax[last:last + 1, :]
            btot = bc[last:last + 1, :]
            m_prev = m_st[j, 0:1, :]
            m_last = jnp.maximum(a_last, m_prev)
            kw = rep(jnp.exp(a_c - m_last), hd // LANES) * k.astype(F32)
            upd = lax.dot_general(kw.astype(BF16), v_aug, (((0,), (0,)), ((), ())),
                                  preferred_element_type=F32)

            mt = jnp.maximum(amax, m_prev)
            f_l = jnp.exp(amax - mt)
            sc = jnp.exp(m_prev - mt)
            nd_c = jnp.dot(q, c_st[j].astype(BF16), preferred_element_type=F32)
            den = f_l * nd_l[:, hd:] + sc * nd_c[:, hd:]
            inv = 1.0 / jnp.maximum(jnp.abs(den), jnp.exp(-(bc + mt)))
            num = rep(f_l, hd // LANES) * nd_l[:, :hd] + rep(sc, hd // LANES) * nd_c[:, :hd]
            o_ref[0, :, hs] = (num * rep(inv, hd // LANES)).astype(o_ref.dtype)

            dec = jnp.exp(m_prev - m_last)
            c_st[j] = rep(dec, (hd + LANES) // LANES) * c_st[j] + upd
            m_st[j] = jnp.broadcast_to(btot + m_last, m_st.shape[1:])


def mlstm_scan(qk, v, cols, rows, out_dtype):
    bsz, l, _ = qk.shape
    w = D_MODEL
    assert l % CHUNK == 0
    nc = l // CHUNK
    fwd = lambda cb: (lambda b, c: (b, c, cb))
    bwd = lambda cb: (lambda b, c: (b, nc - 1 - c, cb))
    blk = lambda im: pl.BlockSpec((1, CHUNK, w), im)
    ncol = cols.shape[-1]
    in_specs = []
    for mk in (fwd, bwd):
        in_specs += [pl.BlockSpec((1, CHUNK, 2 * w), mk(0)), blk(mk(0)),
                     pl.BlockSpec((1, CHUNK, ncol), mk(0)),
                     pl.BlockSpec((1, rows.shape[1], CHUNK),
                                  (lambda b, c: (b, 0, c)) if mk is fwd else (lambda b, c: (b, 0, nc - 1 - c)))]
    nst = 2 * MLSTM_HEADS
    shp = jax.ShapeDtypeStruct((bsz, l, w), out_dtype)
    return pl.pallas_call(
        _mlstm_kernel,
        grid=(bsz, nc),
        in_specs=in_specs,
        out_specs=[blk(fwd(0)), blk(bwd(0))],
        out_shape=[shp, shp],
        scratch_shapes=[pltpu.VMEM((nst, MLSTM_HD, MLSTM_HD + LANES), F32),
                        pltpu.VMEM((nst, 8, LANES), F32)],
        compiler_params=_params("parallel", "arbitrary"),
        name="mlstm_scan",
    )(qk, v, cols, rows, qk, v, cols, rows)


def _rope_table_kernel(inv_ref, cos_ref, sin_ref):
    tl = cos_ref.shape[0]
    pos = (lax.broadcasted_iota(jnp.int32, cos_ref.shape, 0) + pl.program_id(0) * tl).astype(F32)
    ang = pos * inv_ref[...]
    cos_ref[...] = jnp.cos(ang)
    sin_ref[...] = jnp.sin(ang)


def rope_tables(l, tl=1024):
    half = RET_HDK // 2
    tl = min(tl, l)
    inv = (ROPE_BASE ** (-np.arange(0, RET_HDK, 2, dtype=np.float32) / RET_HDK)).astype(np.float32)
    spec = pl.BlockSpec((tl, half), lambda i: (i, 0))
    shp = jax.ShapeDtypeStruct((l, half), F32)
    return pl.pallas_call(
        _rope_table_kernel,
        grid=(l // tl,),
        in_specs=[pl.BlockSpec((1, half), lambda i: (0, 0))],
        out_specs=[spec, spec],
        out_shape=[shp, shp],
        compiler_params=_params("parallel"),
        name="rope_tables",
    )(jnp.asarray(inv).reshape(1, half))


def _rope(x, cos, sin):
    half = x.shape[-1] // 2
    x1, x2 = x[:, :half], x[:, half:]
    return jnp.concatenate([x1 * cos - x2 * sin, x1 * sin + x2 * cos], axis=-1)


def _odd_proj_kernel(x_ref, g_ref, w_ref, cos_ref, sin_ref, qk_ref, v_ref, gate_ref, *, tn):
    x = x_ref[...]
    xb = (x * lax.rsqrt(jnp.mean(x * x, axis=-1, keepdims=True) + EPS) * g_ref[...]).astype(BF16)
    cos, sin = cos_ref[...], sin_ref[...]
    nh, dk = RET_HEADS, RET_HDK
    for j in range(2 * nh):
        r = jnp.dot(xb, w_ref[:, j * dk:(j + 1) * dk], preferred_element_type=F32)
        r = _rope(r, cos, sin)
        if j < nh:
            r = r * dk ** -0.5
        qk_ref[:, j * dk:(j + 1) * dk] = r.astype(qk_ref.dtype)
    c0 = 2 * nh * dk
    vw = v_ref.shape[1]
    for j in range(vw // tn):
        cols = slice(j * tn, (j + 1) * tn)
        v_ref[:, cols] = jnp.dot(xb, w_ref[:, c0 + j * tn:c0 + (j + 1) * tn],
                                 preferred_element_type=F32).astype(v_ref.dtype)
        gate = jnp.dot(xb, w_ref[:, c0 + vw + j * tn:c0 + vw + (j + 1) * tn], preferred_element_type=F32)
        gate_ref[:, cols] = _silu(gate).astype(gate_ref.dtype)


def odd_proj(x, g, w, cos, sin, seq_len, tm=512, tn=512):
    m, d = x.shape
    qw, vw = RET_HEADS * RET_HDK, RET_HEADS * RET_HDV
    half = RET_HDK // 2
    tm = min(tm, seq_len)
    assert m % seq_len == 0 and seq_len % tm == 0
    tps = seq_len // tm
    row = lambda n: pl.BlockSpec((tm, n), lambda i: (i, 0))
    tab = pl.BlockSpec((tm, half), lambda i: (i % tps, 0))
    return pl.pallas_call(
        functools.partial(_odd_proj_kernel, tn=tn),
        grid=(m // tm,),
        in_specs=[row(d), pl.BlockSpec((1, d), lambda i: (0, 0)),
                  pl.BlockSpec(w.shape, lambda i: (0, 0), pipeline_mode=pl.Buffered(1)), tab, tab],
        out_specs=[row(2 * qw), row(vw), row(vw)],
        out_shape=[jax.ShapeDtypeStruct((m, 2 * qw), BF16), jax.ShapeDtypeStruct((m, vw), BF16),
                   jax.ShapeDtypeStruct((m, vw), BF16)],
        compiler_params=_params("parallel"),
        name="odd_proj",
    )(x, g.reshape(1, d).astype(F32), w, cos, sin)


def _retention_kernel(qkf_ref, vf_ref, qkb_ref, vb_ref, logit_ref, of_ref, ob_ref, r_st):
    @pl.when(pl.program_id(1) == 0)
    def _():
        r_st[...] = jnp.zeros_like(r_st)

    nh, dk, dv = RET_HEADS, RET_HDK, RET_HDV
    t_idx = lax.broadcasted_iota(jnp.int32, (CHUNK, CHUNK), 0)
    s_idx = lax.broadcasted_iota(jnp.int32, (CHUNK, CHUNK), 1)
    pos = lax.broadcasted_iota(jnp.int32, (CHUNK, LANES), 0).astype(F32)
    rep = lambda x, n: jnp.concatenate([x] * n, axis=1)
    dirs = ((qkf_ref, vf_ref, of_ref, False), (qkb_ref, vb_ref, ob_ref, True))
    for d, (qk_ref, v_ref, o_ref, reverse) in enumerate(dirs):
        rel = ((s_idx - t_idx) if reverse else (t_idx - s_idx)).astype(F32)
        for h in range(nh):
            j = d * nh + h
            lg = _log_sigmoid(logit_ref[j])[0:1, :]
            dmask = jnp.where(rel >= 0, jnp.exp(rep(lg, CHUNK // LANES) * jnp.maximum(rel, 0.0)), 0.0)
            if reverse:
                q_dec = jnp.exp(lg * (CHUNK - pos))
                k_dec = jnp.exp(lg * pos)
            else:
                q_dec = jnp.exp(lg * (pos + 1.0))
                k_dec = jnp.exp(lg * (CHUNK - 1.0 - pos))
            c_dec = jnp.exp(lg * CHUNK)
            q = qk_ref[0, :, h * dk:(h + 1) * dk]
            k = qk_ref[0, :, (nh + h) * dk:(nh + h + 1) * dk]
            v = v_ref[0, :, h * dv:(h + 1) * dv]
            s = lax.dot_general(q, k, (((1,), (1,)), ((), ())), preferred_element_type=F32) * dmask
            out = jnp.dot(s.astype(BF16), v, preferred_element_type=F32)
            out = out + jnp.dot((q.astype(F32) * rep(q_dec, dk // LANES)).astype(BF16), r_st[j].astype(BF16),
                                preferred_element_type=F32)
            o_ref[0, :, h * dv:(h + 1) * dv] = out.astype(o_ref.dtype)
            upd = lax.dot_general((k.astype(F32) * rep(k_dec, dk // LANES)).astype(BF16), v,
                                  (((0,), (0,)), ((), ())), preferred_element_type=F32)
            r_st[j] = rep(c_dec, dv // LANES) * r_st[j] + upd


def retention_scan(qk, v, logit_b, out_dtype):
    bsz, l, _ = qk.shape
    nc = l // CHUNK
    qw, vw = RET_HEADS * RET_HDK, RET_HEADS * RET_HDV
    in_specs = []
    for rev in (False, True):
        ci = (lambda c: nc - 1 - c) if rev else (lambda c: c)
        in_specs += [pl.BlockSpec((1, CHUNK, 2 * qw), lambda b, c, ci=ci: (b, ci(c), 0)),
                     pl.BlockSpec((1, CHUNK, vw), lambda b, c, ci=ci: (b, ci(c), 0))]
    in_specs.append(pl.BlockSpec((2 * RET_HEADS, 8, LANES), lambda b, c: (0, 0, 0)))
    shp = jax.ShapeDtypeStruct((bsz, l, vw), out_dtype)
    return pl.pallas_call(
        _retention_kernel,
        grid=(bsz, nc),
        in_specs=in_specs,
        out_specs=[pl.BlockSpec((1, CHUNK, vw), lambda b, c: (b, c, 0)),
                   pl.BlockSpec((1, CHUNK, vw), lambda b, c: (b, nc - 1 - c, 0))],
        out_shape=[shp, shp],
        scratch_shapes=[pltpu.VMEM((2 * RET_HEADS, RET_HDK, RET_HDV), F32)],
        compiler_params=_params("parallel", "arbitrary"),
        name="retention_scan",
    )(qk, v, qk, v, logit_b)


def _hyena_filter_kernel(cst_ref, w1_ref, w2_ref, w3a_ref, w3b_ref, delta_ref, kern_ref, asum_ref, *, l):
    i = pl.program_id(0)
    tr = kern_ref.shape[0]
    hr = tr // 2
    half = LANES // 2
    row = lax.broadcasted_iota(jnp.int32, (hr, LANES), 0) + i * tr
    n_a, n_b = row, row + hr
    lag = lambda n: jnp.where(n < l, n, 2 * l - n).astype(F32)
    tt_a, tt_b = lag(n_a) / (l - 1.0), lag(n_b) / (l - 1.0)
    lo = lax.broadcasted_iota(jnp.int32, (hr, LANES), 1) < half
    p2 = jnp.where(lo, lag(n_a), lag(n_b))
    tt2 = jnp.where(lo, tt_a, tt_b)
    bands, phase, w1t, b1, f1, b2, f2 = (cst_ref[k:k + 1, :] for k in range(7))
    ang = (2.0 * math.pi / l) * bands * p2 + phase
    pre = jnp.dot(jnp.cos(ang), w1_ref[...], preferred_element_type=F32, precision=HI) + tt2 * w1t
    z = jnp.sin(f1 * (pre + b1))
    z = jnp.sin(f2 * (jnp.dot(z, w2_ref[...], preferred_element_type=F32, precision=HI) + b2))
    rep = lambda x: jnp.concatenate([x] * (kern_ref.shape[1] // LANES), axis=1)
    dabs = jnp.abs(delta_ref[...])
    total = jnp.zeros((1, kern_ref.shape[1]), F32)
    z_hi = z.astype(BF16)
    z_lo = (z - z_hi.astype(F32)).astype(BF16)
    z3 = jnp.concatenate([z_hi, z_lo, z_hi], axis=1)
    for part, (w3_ref, n, tt) in enumerate(((w3a_ref, n_a, tt_a), (w3b_ref, n_b, tt_b))):
        hk = jnp.dot(z3, w3_ref[...], preferred_element_type=F32)
        hk = hk * (jnp.exp(-rep(tt) * dabs) + HYENA_SHIFT)
        hk = jnp.where(rep(n) == l, 0.0, hk)
        kern_ref[part * hr:(part + 1) * hr, :] = hk
        total = total + jnp.sum(jnp.abs(hk), axis=0, keepdims=True)

    @pl.when(i == 0)
    def _():
        asum_ref[...] = jnp.zeros_like(asum_ref)

    asum_ref[...] += jnp.broadcast_to(total, asum_ref.shape)


def hyena_filter(l, w1, b1, f1, w2, b2, f2, w3, delta, tr=1024):
    wd = HYENA_WIDTH
    hid = w1.shape[1]
    half = LANES // 2
    nb = HYENA_BANDS
    tr = min(tr, l)
    w1, w2, w3 = w1.astype(F32), w2.astype(F32), w3.astype(F32)
    two = lambda v: jnp.tile(jnp.pad(v.astype(F32).reshape(1, -1), ((0, 0), (0, half - v.shape[-1]))), (1, 2))
    bands = np.zeros((1, half), np.float32)
    bands[0, :nb] = bands[0, nb:2 * nb] = np.linspace(1e-4, nb - 1, nb, dtype=np.float32)
    phase = np.zeros((1, half), np.float32)
    phase[0, nb:2 * nb] = 0.5 * np.pi
    cst = jnp.concatenate([two(jnp.asarray(bands)), two(jnp.asarray(phase)), two(w1[0]), two(b1), two(f1),
                           two(b2), two(f2), jnp.zeros((1, LANES), F32)], axis=0)
    blk = lambda a: jnp.pad(a, ((0, half - a.shape[0]), (0, half - a.shape[1])))
    diag2 = lambda a: jnp.concatenate([jnp.pad(blk(a), ((0, 0), (0, half))),
                                       jnp.pad(blk(a), ((0, 0), (half, 0)))], axis=0)
    w3h = jnp.pad(w3, ((0, half - hid), (0, 0)))

    def split3(a):
        hi = a.astype(BF16)
        lo = (a - hi.astype(F32)).astype(BF16)
        return jnp.concatenate([hi, hi, lo], axis=0)

    w3a = split3(jnp.pad(w3h, ((0, half), (0, 0))))
    w3b = split3(jnp.pad(w3h, ((half, 0), (0, 0))))
    full = lambda shape: pl.BlockSpec(shape, lambda i: (0, 0))
    half_sel = lambda i: (0, (i * tr) // l)
    return pl.pallas_call(
        functools.partial(_hyena_filter_kernel, l=l),
        grid=(2 * l // tr,),
        in_specs=[full((8, LANES)), full((LANES, LANES)), full((LANES, LANES)),
                  pl.BlockSpec((3 * LANES, wd), half_sel), pl.BlockSpec((3 * LANES, wd), half_sel),
                  pl.BlockSpec((1, wd), half_sel)],
        out_specs=[pl.BlockSpec((tr, wd), lambda i: (i, 0)), pl.BlockSpec((8, wd), lambda i: (0, 0))],
        out_shape=[jax.ShapeDtypeStruct((2 * l, wd), F32), jax.ShapeDtypeStruct((8, wd), F32)],
        compiler_params=_params("arbitrary"),
        name="hyena_filter",
    )(cst, diag2(w1[1:]), diag2(w2), w3a, w3b, delta.reshape(1, 2 * wd).astype(F32))


def _dft_consts(n1, n2):
    n = n1 * n2
    k1 = np.arange(n1)
    f1 = np.exp(-2j * np.pi * np.outer(k1, k1) / n1)
    k2 = np.arange(n2)
    f2 = np.exp(-2j * np.pi * np.outer(k2, k2) / n2)
    tw = np.exp(-2j * np.pi * np.outer(k1, k2) / n)
    g = f2[None, :, :] * tw[:, None, :]
    ginv = np.conj(np.transpose(g, (0, 2, 1))) / n
    f1inv = np.conj(f1.T)
    return f1, g, ginv, f1inv


def _stack(c):
    return jnp.asarray(np.concatenate([c.real, c.imag], axis=-2).astype(np.float32)).astype(BF16)


def _cdot(fs, xr, xi, rows):
    p = jnp.dot(fs, xr, preferred_element_type=F32)
    if xi is None:
        return p[:rows], p[rows:]
    q = jnp.dot(fs, xi, preferred_element_type=F32)
    return p[:rows] - q[rows:], q[:rows] + p[rows:]


FFT_ROW_CHUNK = 512


def _kron_eye(f, s_blk):
    k = np.kron(f, np.eye(s_blk))
    to = lambda a: jnp.asarray(a.astype(np.float32)).astype(BF16)
    return to(k.real), to(k.imag)


def _strided_stage_kernel(*refs, has_imag_in, n_out):
    mr_ref, mi_ref, xr_ref = refs[:3]
    xi_ref = refs[3] if has_imag_in else None
    n_in = 4 if has_imag_in else 3
    if len(refs[n_in].shape) == 5:
        outs = [refs[n_in].at[part, 0] for part in range(n_out)]
    else:
        outs = [r.at[0] for r in refs[n_in:]]
    flat = lambda r: r.reshape(r.shape[0] * r.shape[1], r.shape[2])
    xr = flat(xr_ref[0]).astype(BF16)
    xi = flat(xi_ref[0]).astype(BF16) if has_imag_in else None
    rows = mr_ref.shape[0]
    rc = min(FFT_ROW_CHUNK, rows)
    s_blk = xr_ref.shape[2]
    for c0 in range(0, rows, rc):
        mr = mr_ref[c0:c0 + rc, :]
        mi = mi_ref[c0:c0 + rc, :]
        re = jnp.dot(mr, xr, preferred_element_type=F32)
        im = jnp.dot(mi, xr, preferred_element_type=F32) if n_out == 2 else None
        if has_imag_in:
            re = re - jnp.dot(mi, xi, preferred_element_type=F32)
            if n_out == 2:
                im = im + jnp.dot(mr, xi, preferred_element_type=F32)
        k0, k1 = c0 // s_blk, (c0 + rc) // s_blk
        for o_ref, val in zip(outs, (re, im)):
            o_ref[k0:k1] = val.reshape(rc // s_blk, s_blk, val.shape[1]).astype(o_ref.dtype)


def fft_stage1(x, n1, rows_in, pairs, complex_in, s_blk, out_dtype):
    c = x.shape[-1]
    n2 = FFT_N2
    xv = x.reshape(x.shape[0], rows_in, n2, c)
    mr, mi = _kron_eye(_dft_consts(n1, n2)[0][:, :rows_in], s_blk)
    mat = pl.BlockSpec(mr.shape, lambda p, j: (0, 0), pipeline_mode=pl.Buffered(1))
    in_specs = [mat, mat, pl.BlockSpec((1, rows_in, s_blk, c), lambda p, j: (p, 0, j, 0))]
    args = [mr, mi, xv]
    if complex_in:
        in_specs.append(pl.BlockSpec((1, rows_in, s_blk, c), lambda p, j: (p + pairs, 0, j, 0)))
        args.append(xv)
    out_spec = pl.BlockSpec((1, n1, s_blk, c), lambda p, j: (p, 0, j, 0))
    shp = jax.ShapeDtypeStruct((pairs, n1, n2, c), out_dtype)
    return pl.pallas_call(
        functools.partial(_strided_stage_kernel, has_imag_in=complex_in, n_out=2),
        grid=(pairs, n2 // s_blk),
        in_specs=in_specs,
        out_specs=[out_spec, out_spec],
        out_shape=[shp, shp],
        compiler_params=_params("parallel", "parallel"),
        name="fft_stage1",
    )(*args)


def _fft_mid_kernel(*refs, conv):
    n2 = FFT_N2
    if conv:
        gs_ref, gis_ref, ar_ref, ai_ref, kr_ref, ki_ref, br_ref, bi_ref = refs
    else:
        gs_ref, ar_ref, ai_ref, br_ref, bi_ref = refs
    for kk in range(ar_ref.shape[1]):
        xr, xi = _cdot(gs_ref[kk], ar_ref[0, kk].astype(BF16), ai_ref[0, kk].astype(BF16), n2)
        if conv:
            kr, ki = kr_ref[kk].astype(F32), ki_ref[kk].astype(F32)
            yr = xr * kr - xi * ki
            yi = xr * ki + xi * kr
            xr, xi = _cdot(gis_ref[kk], yr.astype(BF16), yi.astype(BF16), n2)
        br_ref[0, kk] = xr.astype(br_ref.dtype)
        bi_ref[0, kk] = xi.astype(bi_ref.dtype)


def fft_mid(ar, ai, n1, c, kf=None, ct=1024, kb=4):
    n2 = FFT_N2
    pairs = ar.shape[0]
    kb = min(kb, n1)
    _, g, ginv, _ = _dft_consts(n1, n2)
    a4 = lambda a: a
    mat = pl.BlockSpec((kb, 2 * n2, n2), lambda k, cb, p: (k, 0, 0))
    dat = pl.BlockSpec((1, kb, n2, ct), lambda k, cb, p: (p, k, 0, cb))
    in_specs = [mat]
    args = [_stack(g)]
    if kf is not None:
        in_specs.append(mat)
        args.append(_stack(ginv))
    in_specs += [dat, dat]
    args += [a4(ar), a4(ai)]
    if kf is not None:
        fil = pl.BlockSpec((kb, n2, ct), lambda k, cb, p: (k, 0, cb))
        in_specs += [fil, fil]
        args += [kf[0], kf[1]]
    shp = jax.ShapeDtypeStruct((pairs, n1, n2, c), BF16)
    br, bi = pl.pallas_call(
        functools.partial(_fft_mid_kernel, conv=kf is not None),
        grid=(n1 // kb, c // ct, pairs),
        in_specs=in_specs,
        out_specs=[dat, dat],
        out_shape=[shp, shp],
        compiler_params=_params("parallel", "parallel", "arbitrary"),
        name="fft_mid",
    )(*args)
    return br, bi


def fft_stage1_inv(br, bi, n1, rows_out, c, complex_out, s_blk):
    pairs = br.shape[0]
    n2 = FFT_N2
    mr, mi = _kron_eye(_dft_consts(n1, n2)[3][:rows_out, :], s_blk)
    parts = 2 if complex_out else 1
    mat = pl.BlockSpec(mr.shape, lambda p, j: (0, 0), pipeline_mode=pl.Buffered(1))
    dat = pl.BlockSpec((1, n1, s_blk, c), lambda p, j: (p, 0, j, 0))
    y = pl.pallas_call(
        functools.partial(_strided_stage_kernel, has_imag_in=True, n_out=parts),
        grid=(pairs, n2 // s_blk),
        in_specs=[mat, mat, dat, dat],
        out_specs=pl.BlockSpec((parts, 1, rows_out, s_blk, c), lambda p, j: (0, p, 0, j, 0)),
        out_shape=jax.ShapeDtypeStruct((parts, pairs, rows_out, n2, c), BF16),
        compiler_params=_params("parallel", "parallel"),
        name="fft_stage1_inv",
    )(mr, mi, br, bi)
    return y.reshape(parts * pairs, rows_out * n2, c)


def hyena_long_conv(t, kern):
    bsz, l, c = t.shape
    assert l % FFT_N2 == 0 and kern.shape == (2 * l, c)
    n1 = 2 * l // FFT_N2
    kr, ki = fft_stage1(kern[None], n1, n1, 1, False, 8, F32)
    kr, ki = fft_mid(kr, ki, n1, c)
    kf = (kr.reshape(n1, FFT_N2, c), ki.reshape(n1, FFT_N2, c))
    complex_in = bsz % 2 == 0
    pairs = bsz // 2 if complex_in else bsz
    ar, ai = fft_stage1(t, n1, n1 // 2, pairs, complex_in, BF16_SUBLANES, BF16)
    br, bi = fft_mid(ar, ai, n1, c, kf)
    return fft_stage1_inv(br, bi, n1, n1 // 2, c, complex_in, BF16_SUBLANES)


def _head_ln(x, nheads, g):
    hd = x.shape[-1] // nheads
    outs = []
    for h in range(nheads):
        seg = x[:, h * hd:(h + 1) * hd]
        mu = jnp.mean(seg, axis=-1, keepdims=True)
        cen = seg - mu
        var = jnp.mean(cen * cen, axis=-1, keepdims=True)
        outs.append(cen * lax.rsqrt(var + GN_EPS))
    return jnp.concatenate(outs, axis=-1) * g


def _cross_attn_tail(x, wqk_ref, vwo_ref, gq_ref, gf_ref, final_norm):
    xn = x * lax.rsqrt(jnp.mean(x * x, axis=-1, keepdims=True) + EPS) * gq_ref[...]
    s_all = jnp.dot(xn.astype(BF16), wqk_ref[0], preferred_element_type=F32)
    nm = s_all.shape[1] // CA_HEADS
    probs = []
    for h in range(CA_HEADS):
        s = s_all[:, h * nm:(h + 1) * nm]
        p = jnp.exp(s - jnp.max(s, axis=-1, keepdims=True))
        probs.append((p * (1.0 / jnp.sum(p, axis=-1, keepdims=True))).astype(BF16))
    y = x + jnp.dot(jnp.concatenate(probs, axis=-1), vwo_ref[0], preferred_element_type=F32)
    if final_norm:
        y = y * lax.rsqrt(jnp.mean(y * y, axis=-1, keepdims=True) + EPS) * gf_ref[...]
    return y


def _even_out_kernel(x_ref, hf_ref, hb_ref, ga_ref, xg_ref, y_ref, t_ref, g_ref, skip_ref, asum_ref, w_ref,
                     wqk_ref, vwo_ref, gq_ref, gf_ref, out_ref, *, final_norm):
    wd = D_MODEL
    ln = _head_ln(hf_ref[0].astype(F32) + hb_ref[0].astype(F32), MLSTM_HEADS, g_ref[...])
    ya = ln * ga_ref[0].astype(F32)
    conv = y_ref[0].astype(F32) * (1.0 / asum_ref[0:1, :])
    yb = xg_ref[0].astype(F32) * (conv + skip_ref[...] * t_ref[0].astype(F32))
    mix = jnp.dot(ya.astype(BF16), w_ref[0:wd, :], preferred_element_type=F32)
    mix = mix + jnp.dot(yb.astype(BF16), w_ref[wd:2 * wd, :], preferred_element_type=F32)
    out_ref[0] = _cross_attn_tail(x_ref[0] + mix, wqk_ref, vwo_ref, gq_ref, gf_ref, final_norm)


def _odd_out_kernel(x_ref, of_ref, ob_ref, gate_ref, g_ref, w_ref,
                    wqk_ref, vwo_ref, gq_ref, gf_ref, out_ref, *, final_norm):
    o = _head_ln(of_ref[0].astype(F32) + ob_ref[0].astype(F32), RET_HEADS, g_ref[...])
    y = gate_ref[0].astype(F32) * o
    mix = jnp.dot(y.astype(BF16), w_ref[...], preferred_element_type=F32)
    out_ref[0] = _cross_attn_tail(x_ref[0] + mix, wqk_ref, vwo_ref, gq_ref, gf_ref, final_norm)


def mixer_out(kernel, x, acts, vecs, w_out, ca, tm=512):
    bsz, l, d = x.shape
    tm = min(tm, l)
    wqk, vwo, gq, gf, final_norm = ca
    row = lambda a: pl.BlockSpec((1, tm, a.shape[-1]), lambda b, i: (b, i, 0))
    res = lambda a: pl.BlockSpec(a.shape, lambda b, i: (0, 0), pipeline_mode=pl.Buffered(1))
    per_b = lambda a: pl.BlockSpec((1,) + a.shape[1:], lambda b, i: (b, 0, 0))
    gq, gf = gq.reshape(1, d).astype(F32), gf.reshape(1, d).astype(F32)
    return pl.pallas_call(
        functools.partial(kernel, final_norm=final_norm),
        grid=(bsz, l // tm),
        in_specs=[row(x)] + [row(a) for a in acts] + [res(v) for v in vecs] + [res(w_out)]
                 + [per_b(wqk), per_b(vwo), res(gq), res(gf)],
        out_specs=row(x),
        out_shape=jax.ShapeDtypeStruct((bsz, l, d), F32),
        compiler_params=_params("parallel", "parallel"),
        name=kernel.__name__.strip("_").replace("_kernel", "_ca"),
    )(x, *acts, *vecs, w_out, wqk, vwo, gq, gf)


def _ca_fold_kernel(kv_ref, wq_ref, wo_ref, wqk_ref, vwo_ref):
    d = D_MODEL
    for h in range(CA_HEADS):
        hs = slice(h * CA_HD, (h + 1) * CA_HD)
        k = kv_ref[0, :, hs]
        v = kv_ref[0, :, d + h * CA_HD:d + (h + 1) * CA_HD]
        nm = k.shape[0]
        wqk = lax.dot_general(wq_ref[:, hs], k, (((1,), (1,)), ((), ())), preferred_element_type=F32)
        wqk_ref[0, :, h * nm:(h + 1) * nm] = (wqk * CA_HD ** -0.5).astype(wqk_ref.dtype)
        vwo_ref[0, h * nm:(h + 1) * nm, :] = jnp.dot(v, wo_ref[hs, :],
                                                     preferred_element_type=F32).astype(vwo_ref.dtype)


def ca_fold(kv, wq, wo):
    bsz, nm, d2 = kv.shape
    d = d2 // 2
    mat = pl.BlockSpec((d, d), lambda b: (0, 0))
    return pl.pallas_call(
        _ca_fold_kernel,
        grid=(bsz,),
        in_specs=[pl.BlockSpec((1, nm, d2), lambda b: (b, 0, 0)), mat, mat],
        out_specs=[pl.BlockSpec((1, d, CA_HEADS * nm), lambda b: (b, 0, 0)),
                   pl.BlockSpec((1, CA_HEADS * nm, d), lambda b: (b, 0, 0))],
        out_shape=[jax.ShapeDtypeStruct((bsz, d, CA_HEADS * nm), BF16),
                   jax.ShapeDtypeStruct((bsz, CA_HEADS * nm, d), BF16)],
        compiler_params=_params("parallel"),
        name="ca_fold",
    )(kv, wq, wo)


def _even_mixer(x, p, i, ca):
    bsz, l, d = x.shape
    w = D_MODEL
    nh = MLSTM_HEADS
    m = bsz * l
    g_mix = p['norm_mix_g_layer']
    w_in = p['even_w_in'][i]
    gate0 = 5 * w
    hy0 = gate0 + N_GATE_COLS
    w_conv = jnp.concatenate([w_in[:, :2 * w], w_in[:, hy0:hy0 + 3 * w]], axis=1).astype(BF16)
    w_rest = jnp.concatenate([w_in[:, 2 * w:gate0], w_in[:, hy0 + 3 * w:]], axis=1).astype(BF16)
    w_gate = jnp.pad(w_in[:, gate0:hy0], ((0, 0), (0, LANES - N_GATE_COLS))).astype(BF16)
    conv_w = jnp.concatenate([p['mlstm_conv_w'][i], p['hyena_conv_w'][i]], axis=1).astype(F32)
    conv_b = jnp.concatenate([p['mlstm_conv_b'][i], p['hyena_conv_b'][i]]).astype(F32).reshape(1, 5 * w)
    bias = jnp.pad(p['mlstm_gate_bias'][i].astype(F32).reshape(1, N_GATE_COLS),
                   ((0, 0), (0, LANES - N_GATE_COLS)))
    xf = x.reshape(m, d)
    qk, xg, t, v, ga, cols = even_proj(xf, g_mix, w_conv, w_rest, w_gate, conv_w, conv_b, bias, l)
    cols = cols.reshape(bsz, l, LANES)
    rows = jnp.swapaxes(cols[..., GATE_A_LANE:GATE_A_LANE + N_GATE_COLS], 1, 2)

    hf, hb = mlstm_scan(qk.reshape(bsz, l, 2 * w), v.reshape(bsz, l, w), cols, rows, BF16)

    t = t.reshape(bsz, l, w)
    kern, asum = hyena_filter(l, p['hyena_w1'][i], p['hyena_b1'][i], p['hyena_freq1'][i], p['hyena_w2'][i],
                              p['hyena_b2'][i], p['hyena_freq2'][i], p['hyena_w3'][i], p['hyena_delta'][i])
    y = hyena_long_conv(t, kern)

    b3 = lambda a: a.reshape(bsz, l, w)
    vec = lambda a: a.reshape(1, w).astype(F32)
    return mixer_out(_even_out_kernel, x, [hf, hb, b3(ga), b3(xg), y, t],
                     [vec(p['mlstm_norm_g'][i]), vec(p['hyena_skip'][i]), asum],
                     p['even_w_out'][i].astype(BF16), ca)


def _odd_mixer(x, p, i, cos, sin, ca):
    bsz, l, d = x.shape
    m = bsz * l
    xf = x.reshape(m, d)
    qk, v, gate = odd_proj(xf, p['norm_mix_g_layer'], p['odd_w_in'][i].astype(BF16), cos, sin, l)
    logit = p['ret_decay_logit'][i].astype(F32).reshape(2 * RET_HEADS, 1, 1)
    logit_b = jnp.broadcast_to(logit, (2 * RET_HEADS, 8, LANES))
    vw = RET_HEADS * RET_HDV
    of, ob = retention_scan(qk.reshape(bsz, l, -1), v.reshape(bsz, l, vw), logit_b, BF16)
    return mixer_out(_odd_out_kernel, x, [of, ob, gate.reshape(bsz, l, vw)],
                     [p['ret_norm_g'][i].reshape(1, vw).astype(F32)], p['odd_w_out'][i].astype(BF16), ca)


def _trunk(x, mem, p, cos, sin):
    depth = p['norm_mix_g'].shape[0]
    bsz, nm, d = mem.shape
    for layer in range(depth):
        i = layer // 2
        p['norm_mix_g_layer'] = p['norm_mix_g'][layer]
        kv, = rms_matmul(mem.reshape(bsz * nm, d), p['norm_mem_g'][layer], [p['ca_wkv'][layer].astype(BF16)],
                         [BF16], tm=nm)
        wqk, vwo = ca_fold(kv.reshape(bsz, nm, 2 * d), p['ca_wq'][layer].astype(BF16),
                           p['ca_wo'][layer].astype(BF16))
        ca = (wqk, vwo, p['norm_ca_g'][layer], p['norm_final_g'], layer == depth - 1)
        x = _even_mixer(x, p, i, ca) if layer % 2 == 0 else _odd_mixer(x, p, i, cos, sin, ca)
    return x


def kernel(x_prompt, x_sample, mem_prompt, mem_sample, norm_mix_g, norm_ca_g, norm_mem_g, norm_final_g, even_w_in, mlstm_conv_w, mlstm_conv_b, mlstm_gate_bias, mlstm_norm_g, hyena_conv_w, hyena_conv_b, hyena_w1, hyena_b1, hyena_freq1, hyena_w2, hyena_b2, hyena_freq2, hyena_w3, hyena_delta, hyena_skip, even_w_out, odd_w_in, ret_decay_logit, ret_norm_g, odd_w_out, ca_wq, ca_wkv, ca_wo):
    p = {'norm_mix_g': norm_mix_g, 'norm_ca_g': norm_ca_g, 'norm_mem_g': norm_mem_g, 'norm_final_g': norm_final_g,
         'even_w_in': even_w_in, 'mlstm_conv_w': mlstm_conv_w, 'mlstm_conv_b': mlstm_conv_b,
         'mlstm_gate_bias': mlstm_gate_bias, 'mlstm_norm_g': mlstm_norm_g,
         'hyena_conv_w': hyena_conv_w, 'hyena_conv_b': hyena_conv_b, 'hyena_w1': hyena_w1, 'hyena_b1': hyena_b1,
         'hyena_freq1': hyena_freq1, 'hyena_w2': hyena_w2, 'hyena_b2': hyena_b2, 'hyena_freq2': hyena_freq2,
         'hyena_w3': hyena_w3, 'hyena_delta': hyena_delta, 'hyena_skip': hyena_skip, 'even_w_out': even_w_out,
         'odd_w_in': odd_w_in, 'ret_decay_logit': ret_decay_logit, 'ret_norm_g': ret_norm_g, 'odd_w_out': odd_w_out,
         'ca_wq': ca_wq, 'ca_wkv': ca_wkv, 'ca_wo': ca_wo}
    l_max = max(x_prompt.shape[1], x_sample.shape[1])
    cos, sin = rope_tables(l_max)
    y_prompt = _trunk(x_prompt, mem_prompt, dict(p), cos, sin)
    y_sample = _trunk(x_sample, mem_sample, dict(p), cos, sin)
    return (y_prompt, y_sample)
```

```python
import functools
import math

import numpy as np
import jax
import jax.numpy as jnp
from jax import lax
from jax.experimental import pallas as pl
from jax.experimental.pallas import tpu as pltpu

F32 = jnp.float32
BF16 = jnp.bfloat16

D_MODEL = 1024
EPS = 1e-6
GN_EPS = 1e-5
CHUNK = 256

MLSTM_HEADS = 4
MLSTM_HD = D_MODEL // MLSTM_HEADS
N_GATE_COLS = 4 * MLSTM_HEADS

HYENA_WIDTH = D_MODEL
HYENA_EMB = 33
HYENA_BANDS = (HYENA_EMB - 1) // 2
HYENA_SHIFT = 0.05
FFT_N2 = 256

RET_HEADS = 4
RET_HDK = D_MODEL // RET_HEADS
RET_HDV = 2 * D_MODEL // RET_HEADS
ROPE_BASE = 10000.0

CA_HEADS = 4
CA_HD = D_MODEL // CA_HEADS

LANES = 128
BF16_SUBLANES = 16
VMEM_LIMIT = 56 * 1024 * 1024

HI = lax.Precision.HIGHEST


def _params(*sem):
    return pltpu.CompilerParams(dimension_semantics=sem, vmem_limit_bytes=VMEM_LIMIT)


def _silu(x):
    return x * (1.0 / (1.0 + jnp.exp(-x)))


def _sigmoid(x):
    return 1.0 / (1.0 + jnp.exp(-x))


def _log_sigmoid(x):
    return jnp.minimum(x, 0.0) - jnp.log(1.0 + jnp.exp(-jnp.abs(x)))


def _rms_matmul_kernel(x_ref, g_ref, *refs, tn):
    nw = len(refs) // 2
    x = x_ref[...]
    xn = (x * lax.rsqrt(jnp.mean(x * x, axis=-1, keepdims=True) + EPS) * g_ref[...]).astype(BF16)
    for w_ref, o_ref in zip(refs[:nw], refs[nw:]):
        n = o_ref.shape[1]
        step = min(tn, n)
        for j in range(n // step):
            cols = slice(j * step, (j + 1) * step)
            o_ref[:, cols] = jnp.dot(xn, w_ref[:, cols], preferred_element_type=F32).astype(o_ref.dtype)


def rms_matmul(x, g, ws, out_dtypes, tm, tn=1024):
    m, d = x.shape
    w_specs = [pl.BlockSpec(w.shape, lambda i: (0, 0), pipeline_mode=pl.Buffered(1)) for w in ws]
    return pl.pallas_call(
        functools.partial(_rms_matmul_kernel, tn=tn),
        grid=(m // tm,),
        in_specs=[pl.BlockSpec((tm, d), lambda i: (i, 0)), pl.BlockSpec((1, d), lambda i: (0, 0))] + w_specs,
        out_specs=[pl.BlockSpec((tm, w.shape[1]), lambda i: (i, 0)) for w in ws],
        out_shape=[jax.ShapeDtypeStruct((m, w.shape[1]), dt) for w, dt in zip(ws, out_dtypes)],
        compiler_params=_params("parallel"),
        name="rms_matmul",
    )(x, g.reshape(1, d).astype(F32), *ws)


F32_SUBLANES = 8


def _even_proj_kernel(x_ref, xp_ref, xn_ref, g_ref, wc_ref, wr_ref, wg_ref, cw_ref, cb_ref, gb_ref,
                      qk_ref, xg_ref, t_ref, v_ref, ga_ref, cols_ref, *, tiles_per_seq, tn):
    i = pl.program_id(0)
    tm, d = x_ref.shape
    w, h = D_MODEL, F32_SUBLANES
    first = (i % tiles_per_seq) == 0
    last = (i % tiles_per_seq) == tiles_per_seq - 1
    g = g_ref[...]

    def norm(x):
        return x * lax.rsqrt(jnp.mean(x * x, axis=-1, keepdims=True) + EPS) * g

    xm = norm(x_ref[...])
    xp = jnp.where(first, 0.0, norm(xp_ref[...]))
    xn = jnp.where(last, 0.0, norm(xn_ref[...]))
    x_ext = jnp.concatenate([xp, xm, xn], axis=0).astype(BF16)
    xb = xm.astype(BF16)

    def conv(c0):
        cols = slice(c0, c0 + tn)
        r = jnp.dot(x_ext, wc_ref[:, cols], preferred_element_type=F32)
        t = r.shape[0]
        y = (pltpu.roll(r, 1, axis=0) * cw_ref[0:1, cols] + r * cw_ref[1:2, cols]
             + pltpu.roll(r, t - 1, axis=0) * cw_ref[2:3, cols] + cb_ref[:, cols])
        return y[h:h + tm]

    for j in range(2 * w // tn):
        qk_ref[:, j * tn:(j + 1) * tn] = _silu(conv(j * tn)).astype(qk_ref.dtype)
    rest = lambda group, j: jnp.dot(xb, wr_ref[:, group * w + j * tn:group * w + (j + 1) * tn],
                                    preferred_element_type=F32)
    for j in range(w // tn):
        cols = slice(j * tn, (j + 1) * tn)
        xg_ref[:, cols] = (conv(2 * w + j * tn) * _silu(rest(3, j))).astype(xg_ref.dtype)
        t_ref[:, cols] = (conv(3 * w + j * tn) * conv(4 * w + j * tn)).astype(t_ref.dtype)
        v_ref[:, cols] = rest(0, j).astype(v_ref.dtype)
        ga_ref[:, cols] = (_sigmoid(rest(1, j)) * _silu(rest(2, j))).astype(ga_ref.dtype)
    cols_ref[...] = _gate_prep(jnp.dot(xb, wg_ref[...], preferred_element_type=F32) + gb_ref[...])


def even_proj(x, g, w_conv, w_rest, w_gate, conv_w, conv_b, gate_bias, seq_len, tm=512, tn=512):
    m, d = x.shape
    w, h = D_MODEL, F32_SUBLANES
    tm = min(tm, seq_len)
    assert m % seq_len == 0 and seq_len % tm == 0 and tm % CHUNK == 0
    r = tm // h
    last_blk = m // h - 1
    res = lambda a: pl.BlockSpec(a.shape, lambda i: (0, 0), pipeline_mode=pl.Buffered(1))
    row = lambda n: pl.BlockSpec((tm, n), lambda i: (i, 0))
    return pl.pallas_call(
        functools.partial(_even_proj_kernel, tiles_per_seq=seq_len // tm, tn=tn),
        grid=(m // tm,),
        in_specs=[row(d),
                  pl.BlockSpec((h, d), lambda i: (jnp.maximum(i * r - 1, 0), 0)),
                  pl.BlockSpec((h, d), lambda i: (jnp.minimum((i + 1) * r, last_blk), 0)),
                  pl.BlockSpec((1, d), lambda i: (0, 0)),
                  res(w_conv), res(w_rest), res(w_gate), res(conv_w), res(conv_b), res(gate_bias)],
        out_specs=[row(2 * w), row(w), row(w), row(w), row(w), row(LANES)],
        out_shape=[jax.ShapeDtypeStruct((m, 2 * w), BF16), jax.ShapeDtypeStruct((m, w), BF16),
                   jax.ShapeDtypeStruct((m, w), BF16), jax.ShapeDtypeStruct((m, w), BF16),
                   jax.ShapeDtypeStruct((m, w), BF16), jax.ShapeDtypeStruct((m, LANES), F32)],
        compiler_params=_params("parallel"),
        name="even_proj",
    )(x, x, x, g.reshape(1, d).astype(F32), w_conv, w_rest, w_gate, conv_w, conv_b, gate_bias)


def _seg_scan(x, op, identity, reverse):
    t = x.shape[0]
    r = lax.broadcasted_iota(jnp.int32, x.shape, 0) % CHUNK
    k = 1
    while k < CHUNK:
        if reverse:
            shifted = pltpu.roll(x, t - k, axis=0)
            valid = r < CHUNK - k
        else:
            shifted = pltpu.roll(x, k, axis=0)
            valid = r >= k
        x = op(x, jnp.where(valid, shifted, identity))
        k *= 2
    return x


GATE_A_LANE = 16
GATE_AMAX_LANE = 32


def _gate_lane(d, h):
    return d * 2 * MLSTM_HEADS + h


def _gate_prep(g):
    nh = MLSTM_HEADS
    lane = lax.broadcasted_iota(jnp.int32, g.shape, 1)
    live = (lane < 4 * nh) & (lane % (2 * nh) < nh)
    fwd = lane < 2 * nh
    lf = pltpu.roll(_log_sigmoid(g), LANES - nh, axis=1)
    bcum = jnp.where(fwd, _seg_scan(lf, jnp.add, 0.0, False), _seg_scan(lf, jnp.add, 0.0, True))
    a = g - bcum
    amax = jnp.where(fwd, _seg_scan(a, jnp.maximum, -jnp.inf, False),
                     _seg_scan(a, jnp.maximum, -jnp.inf, True))
    keep = lambda v: jnp.where(live, v, 0.0)
    return (keep(bcum) + pltpu.roll(keep(a), GATE_A_LANE, axis=1)
            + pltpu.roll(keep(amax), GATE_AMAX_LANE, axis=1))


def _mlstm_kernel(qkf_ref, vf_ref, cf_ref, rf_ref, qkb_ref, vb_ref, cb_ref, rb_ref,
                  hf_ref, hb_ref, c_st, m_st):
    @pl.when(pl.program_id(1) == 0)
    def _():
        c_st[...] = jnp.zeros_like(c_st)
        m_st[...] = jnp.zeros_like(m_st)

    nh, hd = MLSTM_HEADS, MLSTM_HD
    t_idx = lax.broadcasted_iota(jnp.int32, (CHUNK, CHUNK), 0)
    s_idx = lax.broadcasted_iota(jnp.int32, (CHUNK, CHUNK), 1)
    scale = hd ** -0.5
    ones_blk = jnp.ones((CHUNK, LANES), BF16)
    rep = lambda x, n: jnp.concatenate([x] * n, axis=1)
    lane_dense = lambda col: jnp.broadcast_to(col, (CHUNK, LANES))
    dirs = ((qkf_ref, vf_ref, cf_ref, rf_ref, hf_ref, False), (qkb_ref, vb_ref, cb_ref, rb_ref, hb_ref, True))
    for d, (qk_ref, v_ref, col_ref, row_ref, o_ref, reverse) in enumerate(dirs):
        mask = (s_idx >= t_idx) if reverse else (s_idx <= t_idx)
        last = 0 if reverse else CHUNK - 1
        for h in range(nh):
            j = d * nh + h
            hs = slice(h * hd, (h + 1) * hd)
            q = qk_ref[0, :, hs]
            k = qk_ref[0, :, (nh + h) * hd:(nh + h + 1) * hd] * scale
            v_aug = jnp.concatenate([v_ref[0, :, hs], ones_blk], axis=1)
            gl = _gate_lane(d, h)
            bc = lane_dense(col_ref[0, :, gl:gl + 1])
            a_c = lane_dense(col_ref[0, :, GATE_A_LANE + gl:GATE_A_LANE + gl + 1])
            amax = lane_dense(col_ref[0, :, GATE_AMAX_LANE + gl:GATE_AMAX_LANE + gl + 1])
            a_r = row_ref[0, gl:gl + 1, :]
            dmat = jnp.exp(jnp.where(mask, a_r - rep(amax, CHUNK // LANES), -jnp.inf))
            s = lax.dot_general(q, k, (((1,), (1,)), ((), ())), preferred_element_type=F32) * dmat
            nd_l = jnp.dot(s.astype(BF16), v_aug, preferred_element_type=F32)
            a_last = amax[last:last + 1, :]
            btot = bc[last:last + 1, :]
            m_prev = m_st[j, 0:1, :]
            m_last = jnp.maximum(a_last, m_prev)
            kw = rep(jnp.exp(a_c - m_last), hd // LANES) * k.astype(F32)
            upd = lax.dot_general(kw.astype(BF16), v_aug, (((0,), (0,)), ((), ())),
                                  preferred_element_type=F32)

            mt = jnp.maximum(amax, m_prev)
            f_l = jnp.exp(amax - mt)
            sc = jnp.exp(m_prev - mt)
            nd_c = jnp.dot(q, c_st[j].astype(BF16), preferred_element_type=F32)
            den = f_l * nd_l[:, hd:] + sc * nd_c[:, hd:]
            inv = 1.0 / jnp.maximum(jnp.abs(den), jnp.exp(-(bc + mt)))
            num = rep(f_l, hd // LANES) * nd_l[:, :hd] + rep(sc, hd // LANES) * nd_c[:, :hd]
            o_ref[0, :, hs] = (num * rep(inv, hd // LANES)).astype(o_ref.dtype)

            dec = jnp.exp(m_prev - m_last)
            c_st[j] = rep(dec, (hd + LANES) // LANES) * c_st[j] + upd
            m_st[j] = jnp.broadcast_to(btot + m_last, m_st.shape[1:])


def mlstm_scan(qk, v, cols, rows, out_dtype):
    bsz, l, _ = qk.shape
    w = D_MODEL
    assert l % CHUNK == 0
    nc = l // CHUNK
    fwd = lambda cb: (lambda b, c: (b, c, cb))
    bwd = lambda cb: (lambda b, c: (b, nc - 1 - c, cb))
    blk = lambda im: pl.BlockSpec((1, CHUNK, w), im)
    ncol = cols.shape[-1]
    in_specs = []
    for mk in (fwd, bwd):
        in_specs += [pl.BlockSpec((1, CHUNK, 2 * w), mk(0)), blk(mk(0)),
                     pl.BlockSpec((1, CHUNK, ncol), mk(0)),
                     pl.BlockSpec((1, rows.shape[1], CHUNK),
                                  (lambda b, c: (b, 0, c)) if mk is fwd else (lambda b, c: (b, 0, nc - 1 - c)))]
    nst = 2 * MLSTM_HEADS
    shp = jax.ShapeDtypeStruct((bsz, l, w), out_dtype)
    return pl.pallas_call(
        _mlstm_kernel,
        grid=(bsz, nc),
        in_specs=in_specs,
        out_specs=[blk(fwd(0)), blk(bwd(0))],
        out_shape=[shp, shp],
        scratch_shapes=[pltpu.VMEM((nst, MLSTM_HD, MLSTM_HD + LANES), F32),
                        pltpu.VMEM((nst, 8, LANES), F32)],
        compiler_params=_params("parallel", "arbitrary"),
        name="mlstm_scan",
    )(qk, v, cols, rows, qk, v, cols, rows)


def _rope_table_kernel(inv_ref, cos_ref, sin_ref):
    tl = cos_ref.shape[0]
    pos = (lax.broadcasted_iota(jnp.int32, cos_ref.shape, 0) + pl.program_id(0) * tl).astype(F32)
    ang = pos * inv_ref[...]
    cos_ref[...] = jnp.cos(ang)
    sin_ref[...] = jnp.sin(ang)


def rope_tables(l, tl=1024):
    half = RET_HDK // 2
    tl = min(tl, l)
    inv = (ROPE_BASE ** (-np.arange(0, RET_HDK, 2, dtype=np.float32) / RET_HDK)).astype(np.float32)
    spec = pl.BlockSpec((tl, half), lambda i: (i, 0))
    shp = jax.ShapeDtypeStruct((l, half), F32)
    return pl.pallas_call(
        _rope_table_kernel,
        grid=(l // tl,),
        in_specs=[pl.BlockSpec((1, half), lambda i: (0, 0))],
        out_specs=[spec, spec],
        out_shape=[shp, shp],
        compiler_params=_params("parallel"),
        name="rope_tables",
    )(jnp.asarray(inv).reshape(1, half))


def _rope(x, cos, sin):
    half = x.shape[-1] // 2
    x1, x2 = x[:, :half], x[:, half:]
    return jnp.concatenate([x1 * cos - x2 * sin, x1 * sin + x2 * cos], axis=-1)


def _odd_proj_kernel(x_ref, g_ref, w_ref, cos_ref, sin_ref, qk_ref, v_ref, gate_ref, *, tn):
    x = x_ref[...]
    xb = (x * lax.rsqrt(jnp.mean(x * x, axis=-1, keepdims=True) + EPS) * g_ref[...]).astype(BF16)
    cos, sin = cos_ref[...], sin_ref[...]
    nh, dk = RET_HEADS, RET_HDK
    for j in range(2 * nh):
        r = jnp.dot(xb, w_ref[:, j * dk:(j + 1) * dk], preferred_element_type=F32)
        r = _rope(r, cos, sin)
        if j < nh:
            r = r * dk ** -0.5
        qk_ref[:, j * dk:(j + 1) * dk] = r.astype(qk_ref.dtype)
    c0 = 2 * nh * dk
    vw = v_ref.shape[1]
    for j in range(vw // tn):
        cols = slice(j * tn, (j + 1) * tn)
        v_ref[:, cols] = jnp.dot(xb, w_ref[:, c0 + j * tn:c0 + (j + 1) * tn],
                                 preferred_element_type=F32).astype(v_ref.dtype)
        gate = jnp.dot(xb, w_ref[:, c0 + vw + j * tn:c0 + vw + (j + 1) * tn], preferred_element_type=F32)
        gate_ref[:, cols] = _silu(gate).astype(gate_ref.dtype)


def odd_proj(x, g, w, cos, sin, seq_len, tm=512, tn=512):
    m, d = x.shape
    qw, vw = RET_HEADS * RET_HDK, RET_HEADS * RET_HDV
    half = RET_HDK // 2
    tm = min(tm, seq_len)
    assert m % seq_len == 0 and seq_len % tm == 0
    tps = seq_len // tm
    row = lambda n: pl.BlockSpec((tm, n), lambda i: (i, 0))
    tab = pl.BlockSpec((tm, half), lambda i: (i % tps, 0))
    return pl.pallas_call(
        functools.partial(_odd_proj_kernel, tn=tn),
        grid=(m // tm,),
        in_specs=[row(d), pl.BlockSpec((1, d), lambda i: (0, 0)),
                  pl.BlockSpec(w.shape, lambda i: (0, 0), pipeline_mode=pl.Buffered(1)), tab, tab],
        out_specs=[row(2 * qw), row(vw), row(vw)],
        out_shape=[jax.ShapeDtypeStruct((m, 2 * qw), BF16), jax.ShapeDtypeStruct((m, vw), BF16),
                   jax.ShapeDtypeStruct((m, vw), BF16)],
        compiler_params=_params("parallel"),
        name="odd_proj",
    )(x, g.reshape(1, d).astype(F32), w, cos, sin)


def _retention_kernel(qk_ref, v_ref, logit_ref, o_ref, r_st, *, reverse):
    @pl.when(pl.program_id(1) == 0)
    def _():
        r_st[...] = jnp.zeros_like(r_st)

    nh, dk, dv = RET_HEADS, RET_HDK, RET_HDV
    t_idx = lax.broadcasted_iota(jnp.int32, (CHUNK, CHUNK), 0)
    s_idx = lax.broadcasted_iota(jnp.int32, (CHUNK, CHUNK), 1)
    pos = lax.broadcasted_iota(jnp.int32, (CHUNK, LANES), 0).astype(F32)
    rep = lambda x, n: jnp.concatenate([x] * n, axis=1)
    rel = ((s_idx - t_idx) if reverse else (t_idx - s_idx)).astype(F32)
    for h in range(nh):
        lg = _log_sigmoid(logit_ref[int(reverse) * nh + h])[0:1, :]
        dmask = jnp.where(rel >= 0, jnp.exp(rep(lg, CHUNK // LANES) * jnp.maximum(rel, 0.0)), 0.0)
        if reverse:
            q_dec = jnp.exp(lg * (CHUNK - pos))
            k_dec = jnp.exp(lg * pos)
        else:
            q_dec = jnp.exp(lg * (pos + 1.0))
            k_dec = jnp.exp(lg * (CHUNK - 1.0 - pos))
        c_dec = jnp.exp(lg * CHUNK)
        q = qk_ref[0, :, h * dk:(h + 1) * dk]
        k = qk_ref[0, :, (nh + h) * dk:(nh + h + 1) * dk]
        v = v_ref[0, :, h * dv:(h + 1) * dv]
        s = lax.dot_general(q, k, (((1,), (1,)), ((), ())), preferred_element_type=F32) * dmask
        out = jnp.dot(s.astype(BF16), v, preferred_element_type=F32)
        out = out + jnp.dot((q.astype(F32) * rep(q_dec, dk // LANES)).astype(BF16), r_st[h].astype(BF16),
                            preferred_element_type=F32)
        o_ref[0, :, h * dv:(h + 1) * dv] = out.astype(o_ref.dtype)
        upd = lax.dot_general((k.astype(F32) * rep(k_dec, dk // LANES)).astype(BF16), v,
                              (((0,), (0,)), ((), ())), preferred_element_type=F32)
        r_st[h] = rep(c_dec, dv // LANES) * r_st[h] + upd


def retention_scan(qk, v, logit_b, out_dtype):
    bsz, l, _ = qk.shape
    nc = l // CHUNK
    qw, vw = RET_HEADS * RET_HDK, RET_HEADS * RET_HDV
    shp = jax.ShapeDtypeStruct((bsz, l, vw), out_dtype)
    outs = []
    for rev in (False, True):
        ci = (lambda c: nc - 1 - c) if rev else (lambda c: c)
        outs.append(pl.pallas_call(
            functools.partial(_retention_kernel, reverse=rev),
            grid=(bsz, nc),
            in_specs=[pl.BlockSpec((1, CHUNK, 2 * qw), lambda b, c, ci=ci: (b, ci(c), 0)),
                      pl.BlockSpec((1, CHUNK, vw), lambda b, c, ci=ci: (b, ci(c), 0)),
                      pl.BlockSpec((2 * RET_HEADS, 8, LANES), lambda b, c: (0, 0, 0))],
            out_specs=pl.BlockSpec((1, CHUNK, vw), lambda b, c, ci=ci: (b, ci(c), 0)),
            out_shape=shp,
            scratch_shapes=[pltpu.VMEM((RET_HEADS, RET_HDK, RET_HDV), F32)],
            compiler_params=_params("parallel", "arbitrary"),
            name="retention_scan",
        )(qk, v, logit_b))
    return outs


def _hyena_filter_kernel(cst_ref, w1_ref, w2_ref, w3a_ref, w3b_ref, delta_ref, kern_ref, asum_ref, *, l):
    i = pl.program_id(0)
    tr = kern_ref.shape[0]
    hr = tr // 2
    half = LANES // 2
    row = lax.broadcasted_iota(jnp.int32, (hr, LANES), 0) + i * tr
    n_a, n_b = row, row + hr
    lag = lambda n: jnp.where(n < l, n, 2 * l - n).astype(F32)
    tt_a, tt_b = lag(n_a) / (l - 1.0), lag(n_b) / (l - 1.0)
    lo = lax.broadcasted_iota(jnp.int32, (hr, LANES), 1) < half
    p2 = jnp.where(lo, lag(n_a), lag(n_b))
    tt2 = jnp.where(lo, tt_a, tt_b)
    bands, phase, w1t, b1, f1, b2, f2 = (cst_ref[k:k + 1, :] for k in range(7))
    ang = (2.0 * math.pi / l) * bands * p2 + phase
    pre = jnp.dot(jnp.cos(ang), w1_ref[...], preferred_element_type=F32, precision=HI) + tt2 * w1t
    z = jnp.sin(f1 * (pre + b1))
    z = jnp.sin(f2 * (jnp.dot(z, w2_ref[...], preferred_element_type=F32, precision=HI) + b2))
    rep = lambda x: jnp.concatenate([x] * (kern_ref.shape[1] // LANES), axis=1)
    dabs = jnp.abs(delta_ref[...])
    total = jnp.zeros((1, kern_ref.shape[1]), F32)
    z_hi = z.astype(BF16)
    z_lo = (z - z_hi.astype(F32)).astype(BF16)
    z3 = jnp.concatenate([z_hi, z_lo, z_hi], axis=1)
    for part, (w3_ref, n, tt) in enumerate(((w3a_ref, n_a, tt_a), (w3b_ref, n_b, tt_b))):
        hk = jnp.dot(z3, w3_ref[...], preferred_element_type=F32)
        hk = hk * (jnp.exp(-rep(tt) * dabs) + HYENA_SHIFT)
        hk = jnp.where(rep(n) == l, 0.0, hk)
        kern_ref[part * hr:(part + 1) * hr, :] = hk
        total = total + jnp.sum(jnp.abs(hk), axis=0, keepdims=True)

    @pl.when(i == 0)
    def _():
        asum_ref[...] = jnp.zeros_like(asum_ref)

    asum_ref[...] += jnp.broadcast_to(total, asum_ref.shape)


def hyena_filter(l, w1, b1, f1, w2, b2, f2, w3, delta, tr=1024):
    wd = HYENA_WIDTH
    hid = w1.shape[1]
    half = LANES // 2
    nb = HYENA_BANDS
    tr = min(tr, l)
    w1, w2, w3 = w1.astype(F32), w2.astype(F32), w3.astype(F32)
    two = lambda v: jnp.tile(jnp.pad(v.astype(F32).reshape(1, -1), ((0, 0), (0, half - v.shape[-1]))), (1, 2))
    bands = np.zeros((1, half), np.float32)
    bands[0, :nb] = bands[0, nb:2 * nb] = np.linspace(1e-4, nb - 1, nb, dtype=np.float32)
    phase = np.zeros((1, half), np.float32)
    phase[0, nb:2 * nb] = 0.5 * np.pi
    cst = jnp.concatenate([two(jnp.asarray(bands)), two(jnp.asarray(phase)), two(w1[0]), two(b1), two(f1),
                           two(b2), two(f2), jnp.zeros((1, LANES), F32)], axis=0)
    blk = lambda a: jnp.pad(a, ((0, half - a.shape[0]), (0, half - a.shape[1])))
    diag2 = lambda a: jnp.concatenate([jnp.pad(blk(a), ((0, 0), (0, half))),
                                       jnp.pad(blk(a), ((0, 0), (half, 0)))], axis=0)
    w3h = jnp.pad(w3, ((0, half - hid), (0, 0)))

    def split3(a):
        hi = a.astype(BF16)
        lo = (a - hi.astype(F32)).astype(BF16)
        return jnp.concatenate([hi, hi, lo], axis=0)

    w3a = split3(jnp.pad(w3h, ((0, half), (0, 0))))
    w3b = split3(jnp.pad(w3h, ((half, 0), (0, 0))))
    full = lambda shape: pl.BlockSpec(shape, lambda i: (0, 0))
    half_sel = lambda i: (0, (i * tr) // l)
    return pl.pallas_call(
        functools.partial(_hyena_filter_kernel, l=l),
        grid=(2 * l // tr,),
        in_specs=[full((8, LANES)), full((LANES, LANES)), full((LANES, LANES)),
                  pl.BlockSpec((3 * LANES, wd), half_sel), pl.BlockSpec((3 * LANES, wd), half_sel),
                  pl.BlockSpec((1, wd), half_sel)],
        out_specs=[pl.BlockSpec((tr, wd), lambda i: (i, 0)), pl.BlockSpec((8, wd), lambda i: (0, 0))],
        out_shape=[jax.ShapeDtypeStruct((2 * l, wd), F32), jax.ShapeDtypeStruct((8, wd), F32)],
        compiler_params=_params("arbitrary"),
        name="hyena_filter",
    )(cst, diag2(w1[1:]), diag2(w2), w3a, w3b, delta.reshape(1, 2 * wd).astype(F32))


def _dft_consts(n1, n2):
    n = n1 * n2
    k1 = np.arange(n1)
    f1 = np.exp(-2j * np.pi * np.outer(k1, k1) / n1)
    k2 = np.arange(n2)
    f2 = np.exp(-2j * np.pi * np.outer(k2, k2) / n2)
    tw = np.exp(-2j * np.pi * np.outer(k1, k2) / n)
    g = f2[None, :, :] * tw[:, None, :]
    ginv = np.conj(np.transpose(g, (0, 2, 1))) / n
    f1inv = np.conj(f1.T)
    return f1, g, ginv, f1inv


def _stack(c):
    return jnp.asarray(np.concatenate([c.real, c.imag], axis=-2).astype(np.float32)).astype(BF16)


def _cdot(fs, xr, xi, rows):
    p = jnp.dot(fs, xr, preferred_element_type=F32)
    if xi is None:
        return p[:rows], p[rows:]
    q = jnp.dot(fs, xi, preferred_element_type=F32)
    return p[:rows] - q[rows:], q[:rows] + p[rows:]


FFT_ROW_CHUNK = 512


def _kron_eye(f, s_blk):
    k = np.kron(f, np.eye(s_blk))
    to = lambda a: jnp.asarray(a.astype(np.float32)).astype(BF16)
    return to(k.real), to(k.imag)


def _strided_stage_kernel(*refs, has_imag_in, n_out):
    mr_ref, mi_ref, xr_ref = refs[:3]
    xi_ref = refs[3] if has_imag_in else None
    n_in = 4 if has_imag_in else 3
    if len(refs[n_in].shape) == 5:
        outs = [refs[n_in].at[part, 0] for part in range(n_out)]
    else:
        outs = [r.at[0] for r in refs[n_in:]]
    flat = lambda r: r.reshape(r.shape[0] * r.shape[1], r.shape[2])
    xr = flat(xr_ref[0]).astype(BF16)
    xi = flat(xi_ref[0]).astype(BF16) if has_imag_in else None
    rows = mr_ref.shape[0]
    rc = min(FFT_ROW_CHUNK, rows)
    s_blk = xr_ref.shape[2]
    for c0 in range(0, rows, rc):
        mr = mr_ref[c0:c0 + rc, :]
        mi = mi_ref[c0:c0 + rc, :]
        re = jnp.dot(mr, xr, preferred_element_type=F32)
        im = jnp.dot(mi, xr, preferred_element_type=F32) if n_out == 2 else None
        if has_imag_in:
            re = re - jnp.dot(mi, xi, preferred_element_type=F32)
            if n_out == 2:
                im = im + jnp.dot(mr, xi, preferred_element_type=F32)
        k0, k1 = c0 // s_blk, (c0 + rc) // s_blk
        for o_ref, val in zip(outs, (re, im)):
            o_ref[k0:k1] = val.reshape(rc // s_blk, s_blk, val.shape[1]).astype(o_ref.dtype)


def fft_stage1(x, n1, rows_in, pairs, complex_in, s_blk, out_dtype):
    c = x.shape[-1]
    n2 = FFT_N2
    xv = x.reshape(x.shape[0], rows_in, n2, c)
    mr, mi = _kron_eye(_dft_consts(n1, n2)[0][:, :rows_in], s_blk)
    mat = pl.BlockSpec(mr.shape, lambda p, j: (0, 0), pipeline_mode=pl.Buffered(1))
    in_specs = [mat, mat, pl.BlockSpec((1, rows_in, s_blk, c), lambda p, j: (p, 0, j, 0))]
    args = [mr, mi, xv]
    if complex_in:
        in_specs.append(pl.BlockSpec((1, rows_in, s_blk, c), lambda p, j: (p + pairs, 0, j, 0)))
        args.append(xv)
    out_spec = pl.BlockSpec((1, n1, s_blk, c), lambda p, j: (p, 0, j, 0))
    shp = jax.ShapeDtypeStruct((pairs, n1, n2, c), out_dtype)
    return pl.pallas_call(
        functools.partial(_strided_stage_kernel, has_imag_in=complex_in, n_out=2),
        grid=(pairs, n2 // s_blk),
        in_specs=in_specs,
        out_specs=[out_spec, out_spec],
        out_shape=[shp, shp],
        compiler_params=_params("parallel", "parallel"),
        name="fft_stage1",
    )(*args)


def _fft_mid_kernel(*refs, conv):
    n2 = FFT_N2
    if conv:
        gs_ref, gis_ref, ar_ref, ai_ref, kr_ref, ki_ref, br_ref, bi_ref = refs
    else:
        gs_ref, ar_ref, ai_ref, br_ref, bi_ref = refs
    for kk in range(ar_ref.shape[1]):
        xr, xi = _cdot(gs_ref[kk], ar_ref[0, kk].astype(BF16), ai_ref[0, kk].astype(BF16), n2)
        if conv:
            kr, ki = kr_ref[kk].astype(F32), ki_ref[kk].astype(F32)
            yr = xr * kr - xi * ki
            yi = xr * ki + xi * kr
            xr, xi = _cdot(gis_ref[kk], yr.astype(BF16), yi.astype(BF16), n2)
        br_ref[0, kk] = xr.astype(br_ref.dtype)
        bi_ref[0, kk] = xi.astype(bi_ref.dtype)


def fft_mid(ar, ai, n1, c, kf=None, ct=1024, kb=4):
    n2 = FFT_N2
    pairs = ar.shape[0]
    kb = min(kb, n1)
    _, g, ginv, _ = _dft_consts(n1, n2)
    a4 = lambda a: a
    mat = pl.BlockSpec((kb, 2 * n2, n2), lambda k, cb, p: (k, 0, 0))
    dat = pl.BlockSpec((1, kb, n2, ct), lambda k, cb, p: (p, k, 0, cb))
    in_specs = [mat]
    args = [_stack(g)]
    if kf is not None:
        in_specs.append(mat)
        args.append(_stack(ginv))
    in_specs += [dat, dat]
    args += [a4(ar), a4(ai)]
    if kf is not None:
        fil = pl.BlockSpec((kb, n2, ct), lambda k, cb, p: (k, 0, cb))
        in_specs += [fil, fil]
        args += [kf[0], kf[1]]
    shp = jax.ShapeDtypeStruct((pairs, n1, n2, c), BF16)
    br, bi = pl.pallas_call(
        functools.partial(_fft_mid_kernel, conv=kf is not None),
        grid=(n1 // kb, c // ct, pairs),
        in_specs=in_specs,
        out_specs=[dat, dat],
        out_shape=[shp, shp],
        compiler_params=_params("parallel", "parallel", "arbitrary"),
        name="fft_mid",
    )(*args)
    return br, bi


def fft_stage1_inv(br, bi, n1, rows_out, c, complex_out, s_blk):
    pairs = br.shape[0]
    n2 = FFT_N2
    mr, mi = _kron_eye(_dft_consts(n1, n2)[3][:rows_out, :], s_blk)
    parts = 2 if complex_out else 1
    mat = pl.BlockSpec(mr.shape, lambda p, j: (0, 0), pipeline_mode=pl.Buffered(1))
    dat = pl.BlockSpec((1, n1, s_blk, c), lambda p, j: (p, 0, j, 0))
    y = pl.pallas_call(
        functools.partial(_strided_stage_kernel, has_imag_in=True, n_out=parts),
        grid=(pairs, n2 // s_blk),
        in_specs=[mat, mat, dat, dat],
        out_specs=pl.BlockSpec((parts, 1, rows_out, s_blk, c), lambda p, j: (0, p, 0, j, 0)),
        out_shape=jax.ShapeDtypeStruct((parts, pairs, rows_out, n2, c), BF16),
        compiler_params=_params("parallel", "parallel"),
        name="fft_stage1_inv",
    )(mr, mi, br, bi)
    return y.reshape(parts * pairs, rows_out * n2, c)


def hyena_long_conv(t, kern):
    bsz, l, c = t.shape
    assert l % FFT_N2 == 0 and kern.shape == (2 * l, c)
    n1 = 2 * l // FFT_N2
    kr, ki = fft_stage1(kern[None], n1, n1, 1, False, 8, F32)
    kr, ki = fft_mid(kr, ki, n1, c)
    kf = (kr.reshape(n1, FFT_N2, c), ki.reshape(n1, FFT_N2, c))
    complex_in = bsz % 2 == 0
    pairs = bsz // 2 if complex_in else bsz
    ar, ai = fft_stage1(t, n1, n1 // 2, pairs, complex_in, BF16_SUBLANES, BF16)
    br, bi = fft_mid(ar, ai, n1, c, kf)
    return fft_stage1_inv(br, bi, n1, n1 // 2, c, complex_in, BF16_SUBLANES)


def _head_ln(x, nheads, g):
    hd = x.shape[-1] // nheads
    outs = []
    for h in range(nheads):
        seg = x[:, h * hd:(h + 1) * hd]
        mu = jnp.mean(seg, axis=-1, keepdims=True)
        cen = seg - mu
        var = jnp.mean(cen * cen, axis=-1, keepdims=True)
        outs.append(cen * lax.rsqrt(var + GN_EPS))
    return jnp.concatenate(outs, axis=-1) * g


def _cross_attn_tail(x, wqk_ref, vwo_ref, gq_ref, gf_ref, final_norm):
    xn = x * lax.rsqrt(jnp.mean(x * x, axis=-1, keepdims=True) + EPS) * gq_ref[...]
    s_all = jnp.dot(xn.astype(BF16), wqk_ref[0], preferred_element_type=F32)
    nm = s_all.shape[1] // CA_HEADS
    probs = []
    for h in range(CA_HEADS):
        s = s_all[:, h * nm:(h + 1) * nm]
        p = jnp.exp(s - jnp.max(s, axis=-1, keepdims=True))
        probs.append((p * (1.0 / jnp.sum(p, axis=-1, keepdims=True))).astype(BF16))
    y = x + jnp.dot(jnp.concatenate(probs, axis=-1), vwo_ref[0], preferred_element_type=F32)
    if final_norm:
        y = y * lax.rsqrt(jnp.mean(y * y, axis=-1, keepdims=True) + EPS) * gf_ref[...]
    return y


def _even_out_kernel(x_ref, hf_ref, hb_ref, ga_ref, xg_ref, y_ref, t_ref, g_ref, skip_ref, asum_ref, w_ref,
                     wqk_ref, vwo_ref, gq_ref, gf_ref, out_ref, *, final_norm):
    wd = D_MODEL
    ln = _head_ln(hf_ref[0].astype(F32) + hb_ref[0].astype(F32), MLSTM_HEADS, g_ref[...])
    ya = ln * ga_ref[0].astype(F32)
    conv = y_ref[0].astype(F32) * (1.0 / asum_ref[0:1, :])
    yb = xg_ref[0].astype(F32) * (conv + skip_ref[...] * t_ref[0].astype(F32))
    mix = jnp.dot(ya.astype(BF16), w_ref[0:wd, :], preferred_element_type=F32)
    mix = mix + jnp.dot(yb.astype(BF16), w_ref[wd:2 * wd, :], preferred_element_type=F32)
    out_ref[0] = _cross_attn_tail(x_ref[0] + mix, wqk_ref, vwo_ref, gq_ref, gf_ref, final_norm)


def _odd_out_kernel(x_ref, of_ref, ob_ref, gate_ref, g_ref, w_ref,
                    wqk_ref, vwo_ref, gq_ref, gf_ref, out_ref, *, final_norm):
    o = _head_ln(of_ref[0].astype(F32) + ob_ref[0].astype(F32), RET_HEADS, g_ref[...])
    y = gate_ref[0].astype(F32) * o
    mix = jnp.dot(y.astype(BF16), w_ref[...], preferred_element_type=F32)
    out_ref[0] = _cross_attn_tail(x_ref[0] + mix, wqk_ref, vwo_ref, gq_ref, gf_ref, final_norm)


def mixer_out(kernel, x, acts, vecs, w_out, ca, tm=512):
    bsz, l, d = x.shape
    tm = min(tm, l)
    wqk, vwo, gq, gf, final_norm = ca
    row = lambda a: pl.BlockSpec((1, tm, a.shape[-1]), lambda b, i: (b, i, 0))
    res = lambda a: pl.BlockSpec(a.shape, lambda b, i: (0, 0), pipeline_mode=pl.Buffered(1))
    per_b = lambda a: pl.BlockSpec((1,) + a.shape[1:], lambda b, i: (b, 0, 0))
    gq, gf = gq.reshape(1, d).astype(F32), gf.reshape(1, d).astype(F32)
    return pl.pallas_call(
        functools.partial(kernel, final_norm=final_norm),
        grid=(bsz, l // tm),
        in_specs=[row(x)] + [row(a) for a in acts] + [res(v) for v in vecs] + [res(w_out)]
                 + [per_b(wqk), per_b(vwo), res(gq), res(gf)],
        out_specs=row(x),
        out_shape=jax.ShapeDtypeStruct((bsz, l, d), F32),
        compiler_params=_params("parallel", "parallel"),
        name=kernel.__name__.strip("_").replace("_kernel", "_ca"),
    )(x, *acts, *vecs, w_out, wqk, vwo, gq, gf)


def _ca_fold_kernel(kv_ref, wq_ref, wo_ref, wqk_ref, vwo_ref):
    d = D_MODEL
    for h in range(CA_HEADS):
        hs = slice(h * CA_HD, (h + 1) * CA_HD)
        k = kv_ref[0, :, hs]
        v = kv_ref[0, :, d + h * CA_HD:d + (h + 1) * CA_HD]
        nm = k.shape[0]
        wqk = lax.dot_general(wq_ref[:, hs], k, (((1,), (1,)), ((), ())), preferred_element_type=F32)
        wqk_ref[0, :, h * nm:(h + 1) * nm] = (wqk * CA_HD ** -0.5).astype(wqk_ref.dtype)
        vwo_ref[0, h * nm:(h + 1) * nm, :] = jnp.dot(v, wo_ref[hs, :],
                                                     preferred_element_type=F32).astype(vwo_ref.dtype)


def ca_fold(kv, wq, wo):
    bsz, nm, d2 = kv.shape
    d = d2 // 2
    mat = pl.BlockSpec((d, d), lambda b: (0, 0))
    return pl.pallas_call(
        _ca_fold_kernel,
        grid=(bsz,),
        in_specs=[pl.BlockSpec((1, nm, d2), lambda b: (b, 0, 0)), mat, mat],
        out_specs=[pl.BlockSpec((1, d, CA_HEADS * nm), lambda b: (b, 0, 0)),
                   pl.BlockSpec((1, CA_HEADS * nm, d), lambda b: (b, 0, 0))],
        out_shape=[jax.ShapeDtypeStruct((bsz, d, CA_HEADS * nm), BF16),
                   jax.ShapeDtypeStruct((bsz, CA_HEADS * nm, d), BF16)],
        compiler_params=_params("parallel"),
        name="ca_fold",
    )(kv, wq, wo)


def _even_mixer(x, p, i, ca):
    bsz, l, d = x.shape
    w = D_MODEL
    nh = MLSTM_HEADS
    m = bsz * l
    g_mix = p['norm_mix_g_layer']
    w_in = p['even_w_in'][i]
    gate0 = 5 * w
    hy0 = gate0 + N_GATE_COLS
    w_conv = jnp.concatenate([w_in[:, :2 * w], w_in[:, hy0:hy0 + 3 * w]], axis=1).astype(BF16)
    w_rest = jnp.concatenate([w_in[:, 2 * w:gate0], w_in[:, hy0 + 3 * w:]], axis=1).astype(BF16)
    w_gate = jnp.pad(w_in[:, gate0:hy0], ((0, 0), (0, LANES - N_GATE_COLS))).astype(BF16)
    conv_w = jnp.concatenate([p['mlstm_conv_w'][i], p['hyena_conv_w'][i]], axis=1).astype(F32)
    conv_b = jnp.concatenate([p['mlstm_conv_b'][i], p['hyena_conv_b'][i]]).astype(F32).reshape(1, 5 * w)
    bias = jnp.pad(p['mlstm_gate_bias'][i].astype(F32).reshape(1, N_GATE_COLS),
                   ((0, 0), (0, LANES - N_GATE_COLS)))
    xf = x.reshape(m, d)
    qk, xg, t, v, ga, cols = even_proj(xf, g_mix, w_conv, w_rest, w_gate, conv_w, conv_b, bias, l)
    cols = cols.reshape(bsz, l, LANES)
    rows = jnp.swapaxes(cols[..., GATE_A_LANE:GATE_A_LANE + N_GATE_COLS], 1, 2)

    hf, hb = mlstm_scan(qk.reshape(bsz, l, 2 * w), v.reshape(bsz, l, w), cols, rows, BF16)

    t = t.reshape(bsz, l, w)
    kern, asum = hyena_filter(l, p['hyena_w1'][i], p['hyena_b1'][i], p['hyena_freq1'][i], p['hyena_w2'][i],
                              p['hyena_b2'][i], p['hyena_freq2'][i], p['hyena_w3'][i], p['hyena_delta'][i])
    y = hyena_long_conv(t, kern)

    b3 = lambda a: a.reshape(bsz, l, w)
    vec = lambda a: a.reshape(1, w).astype(F32)
    return mixer_out(_even_out_kernel, x, [hf, hb, b3(ga), b3(xg), y, t],
                     [vec(p['mlstm_norm_g'][i]), vec(p['hyena_skip'][i]), asum],
                     p['even_w_out'][i].astype(BF16), ca)


def _odd_mixer(x, p, i, cos, sin, ca):
    bsz, l, d = x.shape
    m = bsz * l
    xf = x.reshape(m, d)
    qk, v, gate = odd_proj(xf, p['norm_mix_g_layer'], p['odd_w_in'][i].astype(BF16), cos, sin, l)
    logit = p['ret_decay_logit'][i].astype(F32).reshape(2 * RET_HEADS, 1, 1)
    logit_b = jnp.broadcast_to(logit, (2 * RET_HEADS, 8, LANES))
    vw = RET_HEADS * RET_HDV
    of, ob = retention_scan(qk.reshape(bsz, l, -1), v.reshape(bsz, l, vw), logit_b, BF16)
    return mixer_out(_odd_out_kernel, x, [of, ob, gate.reshape(bsz, l, vw)],
                     [p['ret_norm_g'][i].reshape(1, vw).astype(F32)], p['odd_w_out'][i].astype(BF16), ca)


def _trunk(x, mem, p, cos, sin):
    depth = p['norm_mix_g'].shape[0]
    bsz, nm, d = mem.shape
    for layer in range(depth):
        i = layer // 2
        p['norm_mix_g_layer'] = p['norm_mix_g'][layer]
        kv, = rms_matmul(mem.reshape(bsz * nm, d), p['norm_mem_g'][layer], [p['ca_wkv'][layer].astype(BF16)],
                         [BF16], tm=nm)
        wqk, vwo = ca_fold(kv.reshape(bsz, nm, 2 * d), p['ca_wq'][layer].astype(BF16),
                           p['ca_wo'][layer].astype(BF16))
        ca = (wqk, vwo, p['norm_ca_g'][layer], p['norm_final_g'], layer == depth - 1)
        x = _even_mixer(x, p, i, ca) if layer % 2 == 0 else _odd_mixer(x, p, i, cos, sin, ca)
    return x


def kernel(x_prompt, x_sample, mem_prompt, mem_sample, norm_mix_g, norm_ca_g, norm_mem_g, norm_final_g, even_w_in, mlstm_conv_w, mlstm_conv_b, mlstm_gate_bias, mlstm_norm_g, hyena_conv_w, hyena_conv_b, hyena_w1, hyena_b1, hyena_freq1, hyena_w2, hyena_b2, hyena_freq2, hyena_w3, hyena_delta, hyena_skip, even_w_out, odd_w_in, ret_decay_logit, ret_norm_g, odd_w_out, ca_wq, ca_wkv, ca_wo):
    p = {'norm_mix_g': norm_mix_g, 'norm_ca_g': norm_ca_g, 'norm_mem_g': norm_mem_g, 'norm_final_g': norm_final_g,
         'even_w_in': even_w_in, 'mlstm_conv_w': mlstm_conv_w, 'mlstm_conv_b': mlstm_conv_b,
         'mlstm_gate_bias': mlstm_gate_bias, 'mlstm_norm_g': mlstm_norm_g,
         'hyena_conv_w': hyena_conv_w, 'hyena_conv_b': hyena_conv_b, 'hyena_w1': hyena_w1, 'hyena_b1': hyena_b1,
         'hyena_freq1': hyena_freq1, 'hyena_w2': hyena_w2, 'hyena_b2': hyena_b2, 'hyena_freq2': hyena_freq2,
         'hyena_w3': hyena_w3, 'hyena_delta': hyena_delta, 'hyena_skip': hyena_skip, 'even_w_out': even_w_out,
         'odd_w_in': odd_w_in, 'ret_decay_logit': ret_decay_logit, 'ret_norm_g': ret_norm_g, 'odd_w_out': odd_w_out,
         'ca_wq': ca_wq, 'ca_wkv': ca_wkv, 'ca_wo': ca_wo}
    l_max = max(x_prompt.shape[1], x_sample.shape[1])
    cos, sin = rope_tables(l_max)
    y_prompt = _trunk(x_prompt, mem_prompt, dict(p), cos, sin)
    y_sample = _trunk(x_sample, mem_sample, dict(p), cos, sin)
    return (y_prompt, y_sample)
```
